```python
import jax, jax.numpy as jnp
from jax import lax
import numpy as np

D_MODEL = 1024
BATCH = 8
SEQ = 2048
DEPTH = 2
DEC_BATCH = 16
DEC_SEQ = 32
PAST_LEN = 2048

CHUNK = 64
HEAD_DIM = 64
N_EVEN = (DEPTH + 1) // 2
N_ODD = DEPTH // 2
NORM_EPS = 1e-6
ROPE_THETA = 500000.0
ROPE_DIM = HEAD_DIM // 4
FOX_HEADS = 8
FOX_DIM = FOX_HEADS * HEAD_DIM
FOX_COLS = 3 * FOX_DIM + FOX_HEADS
FOX_BLOCK = 128
RWKV_HEADS = 8
RWKV_DIM = RWKV_HEADS * HEAD_DIM
DECAY_LORA = 64
ICLR_LORA = 64
GATE_LORA = 128
RWKV_COLS = 3 * RWKV_DIM + DECAY_LORA + ICLR_LORA + GATE_LORA
RWKV_GN_EPS = 64e-5
SWA_HEADS = 8
SWA_KV_HEADS = 2
SWA_GROUP = SWA_HEADS // SWA_KV_HEADS
SWA_Q = SWA_HEADS * HEAD_DIM
SWA_KV = SWA_KV_HEADS * HEAD_DIM
SWA_COLS = SWA_Q + 2 * SWA_KV
WINDOW = 128
WINDOW_CHUNKS = WINDOW // CHUNK
SGU_GROUPS = 8
SGU_DIM = SGU_GROUPS * HEAD_DIM
SGU_CHUNK = 128
SGU_COLS = 2 * SGU_DIM
EVEN_COLS = FOX_COLS + RWKV_COLS
ODD_COLS = SWA_COLS + SGU_COLS
MIX_OUT = FOX_DIM + RWKV_DIM
MEM_TOKENS = 256
MEM_HEADS = 4
MEM_DIM = MEM_HEADS * HEAD_DIM
D_FF = 2816

kernel_name = "hybrid_streaming_encoder_step"


def rmsnorm(x, g):
    xf = x.astype(jnp.float32)
    y = xf * lax.rsqrt(jnp.mean(xf * xf, axis=-1, keepdims=True) + NORM_EPS)
    return (y * g.astype(jnp.float32)).astype(x.dtype)


def partial_rope(x, pos):
    half = ROPE_DIM // 2
    inv_freq = jnp.power(ROPE_THETA, -jnp.arange(half, dtype=jnp.float32) / half)
    ang = pos.astype(jnp.float32)[:, None] * inv_freq[None, :]
    cos, sin = jnp.cos(ang)[:, None, :], jnp.sin(ang)[:, None, :]
    xr = x[..., :ROPE_DIM].astype(jnp.float32)
    x1, x2 = xr[..., :half], xr[..., half:]
    rot = jnp.concatenate([x1 * cos - x2 * sin, x2 * cos + x1 * sin], axis=-1)
    return jnp.concatenate([rot.astype(x.dtype), x[..., ROPE_DIM:]], axis=-1)


def swiglu(x, w_gate, w_up, w_down):
    return (jax.nn.silu(x @ w_gate) * (x @ w_up)) @ w_down


def fox_project(h, b_f, q_norm, k_norm):
    B, T, _ = h.shape
    shp = (B, T, FOX_HEADS, HEAD_DIM)
    q = rmsnorm(h[..., :FOX_DIM].reshape(shp), q_norm)
    k = rmsnorm(h[..., FOX_DIM:2 * FOX_DIM].reshape(shp), k_norm)
    v = h[..., 2 * FOX_DIM:3 * FOX_DIM].reshape(shp)
    log_f = jax.nn.log_sigmoid((h[..., 3 * FOX_DIM:] + b_f).astype(jnp.float32))
    return q, k, v, log_f


def fox_attend(q, k, v, cum_q, cum_k, q_pos, k_pos):
    logits = jnp.einsum("bqhd,bkhd->bhqk", q, k).astype(jnp.float32) * HEAD_DIM ** -0.5
    bias = jnp.transpose(cum_q, (0, 2, 1))[..., :, None] - jnp.transpose(cum_k, (0, 2, 1))[..., None, :]
    causal = k_pos[None, :] <= q_pos[:, None]
    p = jax.nn.softmax(jnp.where(causal, logits + bias, -jnp.inf), axis=-1)
    return jnp.einsum("bhqk,bkhd->bqhd", p.astype(v.dtype), v)


def fox_prompt_attention(q, k, v, log_f):
    B, T, H, d = q.shape
    nb = T // FOX_BLOCK
    cum = jnp.cumsum(log_f, axis=1)
    pos = jnp.arange(T)
    q_blocks = jnp.moveaxis(q.reshape(B, nb, FOX_BLOCK, H, d), 1, 0)
    c_blocks = jnp.moveaxis(cum.reshape(B, nb, FOX_BLOCK, H), 1, 0)
    p_blocks = pos.reshape(nb, FOX_BLOCK)
    out = lax.map(lambda blk: fox_attend(blk[0], k, v, blk[1], cum, blk[2], pos),
                  (q_blocks, c_blocks, p_blocks))
    return jnp.moveaxis(out, 0, 1).reshape(B, T, H * d)


def fox_sample_attention(q, k, v, log_f, cache_k, cache_v, cache_log_f):
    B, T = q.shape[:2]
    past = cache_k.shape[1]
    k_all = jnp.concatenate([cache_k, k], axis=1)
    v_all = jnp.concatenate([cache_v, v], axis=1)
    cum = jnp.cumsum(jnp.concatenate([cache_log_f.astype(jnp.float32), log_f], axis=1), axis=1)
    o = fox_attend(q, k_all, v_all, cum[:, past:], cum, past + jnp.arange(T), jnp.arange(past + T))
    return o.reshape(B, T, FOX_DIM)


def rwkv_scan(r, decay, k, v, kk, a, state0):
    def step(S, inp):
        r_t, w_t, k_t, v_t, kk_t, a_t = inp
        removal = jnp.einsum("bhvk,bhk->bhv", S, kk_t)
        S = (S * w_t[:, :, None, :]
             - removal[..., None] * (kk_t * a_t)[:, :, None, :]
             + v_t[..., None] * k_t[:, :, None, :])
        return S, jnp.einsum("bhvk,bhk->bhv", S, r_t)
    xs = (jnp.moveaxis(r, 1, 0), jnp.moveaxis(decay, 1, 0), jnp.moveaxis(k, 1, 0),
          jnp.moveaxis(v, 1, 0), jnp.moveaxis(kk, 1, 0), jnp.moveaxis(a, 1, 0))
    state, out = lax.scan(step, state0, xs)
    return jnp.moveaxis(out, 0, 1), state


def rwkv_mixer(h, state0, shift0, mu, w0, w2, a0, a2, g2, k_k, k_a, r_k, ln_g, ln_b):
    B, T, _ = h.shape
    prev = jnp.concatenate([shift0.astype(h.dtype), h[:, :-1]], axis=1)
    hx = h + (prev - h) * mu
    cut = [RWKV_DIM, 2 * RWKV_DIM, 3 * RWKV_DIM, 3 * RWKV_DIM + DECAY_LORA,
           3 * RWKV_DIM + DECAY_LORA + ICLR_LORA]
    r, k, v, xw, xa, xg = jnp.split(hx, cut, axis=-1)
    w_logit = (w0 + jnp.tanh(xw) @ w2).astype(jnp.float32)
    decay = jnp.exp(-jnp.exp(-jax.nn.softplus(-w_logit) - 0.5))
    a = jax.nn.sigmoid(a0 + xa @ a2)
    g = jax.nn.sigmoid(xg) @ g2
    heads = lambda t: t.reshape(B, T, RWKV_HEADS, HEAD_DIM).astype(jnp.float32)
    kk = heads(k * k_k)
    kk = kk / jnp.maximum(jnp.linalg.norm(kk, axis=-1, keepdims=True), 1e-12)
    k = k * (1.0 + (a - 1.0) * k_a)
    r, k, v, a, decay = heads(r), heads(k), heads(v), heads(a), heads(decay)
    out, state = rwkv_scan(r, decay, k, v, kk, a, state0.astype(jnp.float32))
    mean = jnp.mean(out, axis=-1, keepdims=True)
    var = jnp.mean(jnp.square(out - mean), axis=-1, keepdims=True)
    out = ((out - mean) * lax.rsqrt(var + RWKV_GN_EPS)).reshape(B, T, RWKV_DIM) * ln_g + ln_b
    bonus = (jnp.sum(r * k * r_k, axis=-1, keepdims=True) * v).reshape(B, T, RWKV_DIM)
    return ((out + bonus) * g).astype(h.dtype), state, h[:, -1:]


def swa_project(h, q_norm, k_norm, pos):
    B, T, _ = h.shape
    q = h[..., :SWA_Q].reshape(B, T, SWA_HEADS, HEAD_DIM)
    k = h[..., SWA_Q:SWA_Q + SWA_KV].reshape(B, T, SWA_KV_HEADS, HEAD_DIM)
    v = h[..., SWA_Q + SWA_KV:].reshape(B, T, SWA_KV_HEADS, HEAD_DIM)
    q = partial_rope(rmsnorm(q, q_norm), pos).reshape(B, T, SWA_KV_HEADS, SWA_GROUP, HEAD_DIM)
    k = partial_rope(rmsnorm(k, k_norm), pos)
    return q, k, v


def sink_attend(q, k, v, mask, sinks):
    logits = jnp.einsum("...qngd,...knd->...ngqk", q, k).astype(jnp.float32) * HEAD_DIM ** -0.5
    logits = jnp.where(mask[..., None, None, :, :], logits, -jnp.inf)
    sink = jnp.broadcast_to(sinks.reshape(SWA_KV_HEADS, SWA_GROUP, 1, 1).astype(jnp.float32),
                            logits.shape[:-1] + (1,))
    p = jax.nn.softmax(jnp.concatenate([logits, sink], axis=-1), axis=-1)[..., :-1]
    return jnp.einsum("...ngqk,...knd->...qngd", p.astype(v.dtype), v)


def swa_prompt_attention(q, k, v, sinks):
    B, T = q.shape[:2]
    nc = T // CHUNK
    span = (WINDOW_CHUNKS + 1) * CHUNK

    def band(x):
        xc = jnp.pad(x.reshape(B, nc, CHUNK, SWA_KV_HEADS, HEAD_DIM),
                     ((0, 0), (WINDOW_CHUNKS, 0), (0, 0), (0, 0), (0, 0)))
        return jnp.concatenate([xc[:, j:j + nc] for j in range(WINDOW_CHUNKS + 1)], axis=2)

    key_chunk = jnp.arange(nc)[:, None] + jnp.arange(WINDOW_CHUNKS + 1)[None, :] - WINDOW_CHUNKS
    valid = jnp.repeat(key_chunk >= 0, CHUNK, axis=1)
    mask = jnp.broadcast_to(valid[:, None, :], (nc, CHUNK, span))
    o = sink_attend(q.reshape(B, nc, CHUNK, SWA_KV_HEADS, SWA_GROUP, HEAD_DIM),
                    band(k), band(v), mask, sinks)
    return o.reshape(B, T, SWA_Q)


def swa_sample_attention(q, k, v, cache_k, cache_v, sinks, past):
    B, T = q.shape[:2]
    rows = cache_k.shape[1]
    k_all = jnp.concatenate([cache_k, k], axis=1)
    v_all = jnp.concatenate([cache_v, v], axis=1)
    k_chunk = (past - rows + jnp.arange(rows + T)) // CHUNK
    q_chunk = (past + jnp.arange(T)) // CHUNK
    mask = (k_chunk[None, :] <= q_chunk[:, None]) & (k_chunk[None, :] >= q_chunk[:, None] - WINDOW_CHUNKS)
    o = sink_attend(q, k_all, v_all, mask, sinks)
    return o.reshape(B, T, SWA_Q), k_all[:, -rows:], v_all[:, -rows:]


def sgu_project(h, v_norm):
    u = jax.nn.gelu(h[..., :SGU_DIM])
    v = rmsnorm(jax.nn.gelu(h[..., SGU_DIM:]), v_norm)
    return u, v


def sgu_mix(u, v, w_s, b):
    L = v.shape[2]
    w = jnp.tril(w_s[:, :L, :L])
    vg = v.reshape(v.shape[:3] + (SGU_GROUPS, HEAD_DIM))
    mixed = jnp.einsum("gts,bcsgd->bctgd", w, vg) + jnp.transpose(b[:, :L])[:, :, None]
    return u * mixed.reshape(u.shape)


def mem_project_kv(mem, g, w_kv, k_norm):
    B, M, _ = mem.shape
    kv = rmsnorm(mem, g) @ w_kv
    k = rmsnorm(kv[..., :MEM_DIM].reshape(B, M, MEM_HEADS, HEAD_DIM), k_norm)
    v = kv[..., MEM_DIM:].reshape(B, M, MEM_HEADS, HEAD_DIM)
    return k, v


def mem_attend(xn, mem_k, mem_v, w_q, q_norm, w_o):
    B, T, _ = xn.shape
    q = rmsnorm((xn @ w_q).reshape(B, T, MEM_HEADS, HEAD_DIM), q_norm)
    logits = jnp.einsum("bqhd,bkhd->bhqk", q, mem_k).astype(jnp.float32) * HEAD_DIM ** -0.5
    p = jax.nn.softmax(logits, axis=-1).astype(mem_v.dtype)
    o = jnp.einsum("bhqk,bkhd->bqhd", p, mem_v).reshape(B, T, MEM_DIM)
    return o @ w_o


def setup_inputs(seed: int = 0) -> dict:
    key = jax.random.key(seed)
    keys = iter(jax.random.split(key, 64))

    def normal(shape, scale=1.0):
        return scale * jax.random.normal(next(keys), shape, jnp.float32)

    def dense(shape):
        return normal(shape, shape[-2] ** -0.5)

    def gain(shape):
        return 1.0 + normal(shape, 0.1)

    def uniform(shape, lo, hi):
        return jax.random.uniform(next(keys), shape, jnp.float32, lo, hi)

    swa_rows = min(WINDOW, PAST_LEN)
    return {
        "x_prompt": normal((BATCH, SEQ, D_MODEL)),
        "x_sample": normal((DEC_BATCH, DEC_SEQ, D_MODEL)),
        "cache_fox_k": normal((N_EVEN, DEC_BATCH, PAST_LEN, FOX_HEADS, HEAD_DIM)),
        "cache_fox_v": normal((N_EVEN, DEC_BATCH, PAST_LEN, FOX_HEADS, HEAD_DIM)),
        "cache_fox_logf": jax.nn.log_sigmoid(2.0 + normal((N_EVEN, DEC_BATCH, PAST_LEN, FOX_HEADS))),
        "state_rwkv": normal((N_EVEN, DEC_BATCH, RWKV_HEADS, HEAD_DIM, HEAD_DIM)),
        "state_rwkv_shift": normal((N_EVEN, DEC_BATCH, 1, RWKV_COLS)),
        "cache_swa_k": normal((N_ODD, DEC_BATCH, swa_rows, SWA_KV_HEADS, HEAD_DIM)),
        "cache_swa_v": normal((N_ODD, DEC_BATCH, swa_rows, SWA_KV_HEADS, HEAD_DIM)),
        "cache_mem_k": normal((DEPTH, DEC_BATCH, MEM_TOKENS, MEM_HEADS, HEAD_DIM)),
        "cache_mem_v": normal((DEPTH, DEC_BATCH, MEM_TOKENS, MEM_HEADS, HEAD_DIM)),
        "mem_prompt": normal((BATCH, MEM_TOKENS, D_MODEL)),
        "ffn1_norm": gain((DEPTH, D_MODEL)),
        "ffn1_w_gate": dense((DEPTH, D_MODEL, D_FF)),
        "ffn1_w_up": dense((DEPTH, D_MODEL, D_FF)),
        "ffn1_w_down": dense((DEPTH, D_FF, D_MODEL)),
        "mix_norm": gain((DEPTH, D_MODEL)),
        "ev_w_in": dense((N_EVEN, D_MODEL, EVEN_COLS)),
        "fox_b_f": 2.0 + normal((N_EVEN, FOX_HEADS), 0.1),
        "fox_q_norm": gain((N_EVEN, HEAD_DIM)),
        "fox_k_norm": gain((N_EVEN, HEAD_DIM)),
        "rwkv_mu": uniform((N_EVEN, RWKV_COLS), 0.0, 1.0),
        "rwkv_w0": uniform((N_EVEN, RWKV_DIM), -4.0, 0.0),
        "rwkv_w2": dense((N_EVEN, DECAY_LORA, RWKV_DIM)),
        "rwkv_a0": normal((N_EVEN, RWKV_DIM), 0.1),
        "rwkv_a2": dense((N_EVEN, ICLR_LORA, RWKV_DIM)),
        "rwkv_g2": dense((N_EVEN, GATE_LORA, RWKV_DIM)),
        "rwkv_k_k": uniform((N_EVEN, RWKV_DIM), 0.5, 1.0),
        "rwkv_k_a": gain((N_EVEN, RWKV_DIM)),
        "rwkv_r_k": normal((N_EVEN, RWKV_HEADS, HEAD_DIM), 0.1),
        "rwkv_ln_g": gain((N_EVEN, RWKV_DIM)),
        "rwkv_ln_b": normal((N_EVEN, RWKV_DIM), 0.02),
        "ev_w_out": dense((N_EVEN, MIX_OUT, D_MODEL)),
        "od_w_in": dense((N_ODD, D_MODEL, ODD_COLS)),
        "swa_q_norm": gain((N_ODD, HEAD_DIM)),
        "swa_k_norm": gain((N_ODD, HEAD_DIM)),
        "swa_sinks": normal((N_ODD, SWA_HEADS)),
        "sgu_v_norm": gain((N_ODD, SGU_DIM)),
        "sgu_w_s": dense((N_ODD, SGU_GROUPS, SGU_CHUNK, SGU_CHUNK)),
        "sgu_b": gain((N_ODD, SGU_GROUPS, SGU_CHUNK)),
        "od_w_out": dense((N_ODD, MIX_OUT, D_MODEL)),
        "xattn_norm": gain((DEPTH, D_MODEL)),
        "mem_norm": gain((DEPTH, D_MODEL)),
        "xattn_wq": dense((DEPTH, D_MODEL, MEM_DIM)),
        "xattn_wkv": dense((DEPTH, D_MODEL, 2 * MEM_DIM)),
        "xattn_q_norm": gain((DEPTH, HEAD_DIM)),
        "xattn_k_norm": gain((DEPTH, HEAD_DIM)),
        "xattn_wo": dense((DEPTH, MEM_DIM, D_MODEL)),
        "ffn2_norm": gain((DEPTH, D_MODEL)),
        "ffn2_w_gate": dense((DEPTH, D_MODEL, D_FF)),
        "ffn2_w_up": dense((DEPTH, D_MODEL, D_FF)),
        "ffn2_w_down": dense((DEPTH, D_FF, D_MODEL)),
    }


def reference(x_prompt, x_sample,
              cache_fox_k, cache_fox_v, cache_fox_logf, state_rwkv, state_rwkv_shift,
              cache_swa_k, cache_swa_v, cache_mem_k, cache_mem_v,
              mem_prompt,
              ffn1_norm, ffn1_w_gate, ffn1_w_up, ffn1_w_down,
              mix_norm,
              ev_w_in, fox_b_f, fox_q_norm, fox_k_norm,
              rwkv_mu, rwkv_w0, rwkv_w2, rwkv_a0, rwkv_a2, rwkv_g2, rwkv_k_k, rwkv_k_a,
              rwkv_r_k, rwkv_ln_g, rwkv_ln_b,
              ev_w_out,
              od_w_in, swa_q_norm, swa_k_norm, swa_sinks, sgu_v_norm, sgu_w_s, sgu_b, od_w_out,
              xattn_norm, mem_norm, xattn_wq, xattn_wkv, xattn_q_norm, xattn_k_norm, xattn_wo,
              ffn2_norm, ffn2_w_gate, ffn2_w_up, ffn2_w_down):
    Bp, Tp, _ = x_prompt.shape
    Bs, Ts, _ = x_sample.shape
    past = cache_fox_k.shape[2]
    pos_p = jnp.arange(Tp)
    pos_s = past + jnp.arange(Ts)
    xp, xs = x_prompt, x_sample
    p_fox_k, p_fox_v, p_fox_logf, p_rwkv_state, p_rwkv_shift = [], [], [], [], []
    p_swa_k, p_swa_v, p_mem_k, p_mem_v = [], [], [], []
    s_fox_k, s_fox_v, s_fox_logf, s_rwkv_state, s_rwkv_shift = [], [], [], [], []
    s_swa_k, s_swa_v, s_sgu_v = [], [], []

    for l in range(DEPTH):
        xp = xp + 0.5 * swiglu(rmsnorm(xp, ffn1_norm[l]), ffn1_w_gate[l], ffn1_w_up[l], ffn1_w_down[l])
        xs = xs + 0.5 * swiglu(rmsnorm(xs, ffn1_norm[l]), ffn1_w_gate[l], ffn1_w_up[l], ffn1_w_down[l])
        hp = rmsnorm(xp, mix_norm[l])
        hs = rmsnorm(xs, mix_norm[l])
        if l % 2 == 0:
            e = l // 2
            hp, hs = hp @ ev_w_in[e], hs @ ev_w_in[e]
            qp, kp, vp, fp = fox_project(hp[..., :FOX_COLS], fox_b_f[e], fox_q_norm[e], fox_k_norm[e])
            qs, ks, vs, fs = fox_project(hs[..., :FOX_COLS], fox_b_f[e], fox_q_norm[e], fox_k_norm[e])
            a_p = fox_prompt_attention(qp, kp, vp, fp)
            a_s = fox_sample_attention(qs, ks, vs, fs, cache_fox_k[e], cache_fox_v[e], cache_fox_logf[e])
            rw = (rwkv_mu[e], rwkv_w0[e], rwkv_w2[e], rwkv_a0[e], rwkv_a2[e], rwkv_g2[e],
                  rwkv_k_k[e], rwkv_k_a[e], rwkv_r_k[e], rwkv_ln_g[e], rwkv_ln_b[e])
            state0 = jnp.zeros((Bp, RWKV_HEADS, HEAD_DIM, HEAD_DIM), jnp.float32)
            shift0 = jnp.zeros((Bp, 1, RWKV_COLS), hp.dtype)
            b_p, st_p, sh_p = rwkv_mixer(hp[..., FOX_COLS:], state0, shift0, *rw)
            b_s, st_s, sh_s = rwkv_mixer(hs[..., FOX_COLS:], state_rwkv[e], state_rwkv_shift[e], *rw)
            xp = xp + jnp.concatenate([a_p, b_p], axis=-1) @ ev_w_out[e]
            xs = xs + jnp.concatenate([a_s, b_s], axis=-1) @ ev_w_out[e]
            p_fox_k.append(kp); p_fox_v.append(vp); p_fox_logf.append(fp)
            p_rwkv_state.append(st_p); p_rwkv_shift.append(sh_p)
            s_fox_k.append(ks); s_fox_v.append(vs); s_fox_logf.append(fs)
            s_rwkv_state.append(st_s); s_rwkv_shift.append(sh_s)
        else:
            j = l // 2
            hp, hs = hp @ od_w_in[j], hs @ od_w_in[j]
            qp, kp, vp = swa_project(hp[..., :SWA_COLS], swa_q_norm[j], swa_k_norm[j], pos_p)
            qs, ks, vs = swa_project(hs[..., :SWA_COLS], swa_q_norm[j], swa_k_norm[j], pos_s)
            c_p = swa_prompt_attention(qp, kp, vp, swa_sinks[j])
            c_s, nk_s, nv_s = swa_sample_attention(qs, ks, vs, cache_swa_k[j], cache_swa_v[j], swa_sinks[j], past)
            u_p, g_p = sgu_project(hp[..., SWA_COLS:], sgu_v_norm[j])
            u_s, g_s = sgu_project(hs[..., SWA_COLS:], sgu_v_norm[j])
            nc = Tp // SGU_CHUNK
            d_p = sgu_mix(u_p.reshape(Bp, nc, SGU_CHUNK, SGU_DIM), g_p.reshape(Bp, nc, SGU_CHUNK, SGU_DIM),
                          sgu_w_s[j], sgu_b[j]).reshape(Bp, Tp, SGU_DIM)
            d_s = sgu_mix(u_s[:, None], g_s[:, None], sgu_w_s[j], sgu_b[j])[:, 0]
            xp = xp + jnp.concatenate([c_p, d_p], axis=-1) @ od_w_out[j]
            xs = xs + jnp.concatenate([c_s, d_s], axis=-1) @ od_w_out[j]
            p_swa_k.append(kp[:, -WINDOW:]); p_swa_v.append(vp[:, -WINDOW:])
            s_swa_k.append(nk_s); s_swa_v.append(nv_s); s_sgu_v.append(g_s)
        mk_p, mv_p = mem_project_kv(mem_prompt, mem_norm[l], xattn_wkv[l], xattn_k_norm[l])
        xp = xp + mem_attend(rmsnorm(xp, xattn_norm[l]), mk_p, mv_p, xattn_wq[l], xattn_q_norm[l], xattn_wo[l])
        xs = xs + mem_attend(rmsnorm(xs, xattn_norm[l]), cache_mem_k[l], cache_mem_v[l],
                             xattn_wq[l], xattn_q_norm[l], xattn_wo[l])
        p_mem_k.append(mk_p); p_mem_v.append(mv_p)
        xp = xp + 0.5 * swiglu(rmsnorm(xp, ffn2_norm[l]), ffn2_w_gate[l], ffn2_w_up[l], ffn2_w_down[l])
        xs = xs + 0.5 * swiglu(rmsnorm(xs, ffn2_norm[l]), ffn2_w_gate[l], ffn2_w_up[l], ffn2_w_down[l])

    return (xp, xs,
            jnp.stack(p_fox_k), jnp.stack(p_fox_v), jnp.stack(p_fox_logf),
            jnp.stack(p_rwkv_state), jnp.stack(p_rwkv_shift),
            jnp.stack(p_swa_k), jnp.stack(p_swa_v),
            jnp.stack(p_mem_k), jnp.stack(p_mem_v),
            jnp.stack(s_fox_k), jnp.stack(s_fox_v), jnp.stack(s_fox_logf),
            jnp.stack(s_rwkv_state), jnp.stack(s_rwkv_shift),
            jnp.stack(s_swa_k), jnp.stack(s_swa_v),
            jnp.stack(s_sgu_v))
```

```python
import functools

import numpy as np
import jax
import jax.numpy as jnp
from jax import lax
from jax.experimental import pallas as pl
from jax.experimental.pallas import tpu as pltpu

F32 = jnp.float32
BF16 = jnp.bfloat16

D_MODEL = 1024
HEAD_DIM = 64
NORM_EPS = 1e-6
ROPE_THETA = 500000.0
ROPE_DIM = HEAD_DIM // 4
CHUNK = 64
FOX_HEADS = 8
FOX_DIM = FOX_HEADS * HEAD_DIM
RWKV_HEADS = 8
RWKV_DIM = RWKV_HEADS * HEAD_DIM
DECAY_LORA = 64
ICLR_LORA = 64
GATE_LORA = 128
RWKV_COLS = 3 * RWKV_DIM + DECAY_LORA + ICLR_LORA + GATE_LORA
RWKV_GN_EPS = 64e-5
SWA_HEADS = 8
SWA_KV_HEADS = 2
SWA_GROUP = SWA_HEADS // SWA_KV_HEADS
SWA_Q = SWA_HEADS * HEAD_DIM
SWA_KV = SWA_KV_HEADS * HEAD_DIM
WINDOW = 128
WINDOW_CHUNKS = WINDOW // CHUNK
SGU_GROUPS = 8
SGU_DIM = SGU_GROUPS * HEAD_DIM
SGU_CHUNK = 128
MEM_HEADS = 4
MEM_DIM = MEM_HEADS * HEAD_DIM
D_FF = 2816

LANES = 128
ROW_TILE = 512
FF_TILE = 256
ATT_TILE = 256
SWA_TILE = 256
SCAN_TOKENS = 64
VMEM_LIMIT = 56 * 1024 * 1024
ATT_SCALE = HEAD_DIM ** -0.5
NEG_BIG = -1e30


def _cparams(*sem):
    return pltpu.CompilerParams(dimension_semantics=sem, vmem_limit_bytes=VMEM_LIMIT)


def _dot(a, b):
    return jnp.dot(a, b, preferred_element_type=F32)


def _dot_nt(a, b):
    return lax.dot_general(a, b, (((1,), (1,)), ((), ())), preferred_element_type=F32)


def _rms(x, g):
    ms = jnp.mean(x * x, axis=-1, keepdims=True)
    return (x * lax.rsqrt(ms + NORM_EPS)) * g


def _group_ones(group):
    shift = int(np.log2(group))
    r = lax.broadcasted_iota(jnp.int32, (LANES, LANES), 0) >> shift
    c = lax.broadcasted_iota(jnp.int32, (LANES, LANES), 1) >> shift
    return jnp.where(r == c, 1.0, 0.0).astype(BF16)


def _group_sum(x, ones):
    parts = []
    for j in range(x.shape[1] // LANES):
        blk = x[:, j * LANES:(j + 1) * LANES]
        hi = blk.astype(BF16)
        lo = (blk - hi.astype(F32)).astype(BF16)
        parts.append(_dot(hi, ones) + _dot(lo, ones))
    return parts[0] if len(parts) == 1 else jnp.concatenate(parts, axis=1)


def _log_sigmoid(z):
    return jnp.minimum(z, 0.0) - jnp.log(1.0 + jnp.exp(-jnp.abs(z)))


def _gelu_tanh(x):
    return 0.5 * x * (1.0 + jnp.tanh(0.7978845608028654 * (x + 0.044715 * (x * x * x))))


def _lane_half(shape):
    return lax.broadcasted_iota(jnp.int32, shape, 1) < HEAD_DIM


def _row_spec(tm, cols):
    return pl.BlockSpec((tm, cols), lambda i: (i, 0))


def _full_spec(shape):
    nd = len(shape)
    return pl.BlockSpec(shape, lambda *_: (0,) * nd, pipeline_mode=pl.Buffered(1))


def _ffn_body(x_ref, g_ref, wg_ref, wu_ref, wd_ref, o_ref, acc_ref, *, n_chunks):
    x = x_ref[...]
    n = _rms(x, g_ref[...]).astype(BF16)
    acc_ref[...] = jnp.zeros_like(acc_ref)

    def chunk(c, carry):
        gate = _dot(n, wg_ref[c])
        up = _dot(n, wu_ref[c])
        act = (gate * jax.nn.sigmoid(gate) * up).astype(BF16)
        acc_ref[...] += _dot(act, wd_ref[c])
        return carry

    lax.fori_loop(0, n_chunks, chunk, 0)
    o_ref[...] = x + 0.5 * acc_ref[...]


def _ffn(x, g, wg, wu, wd):
    rows = x.shape[0]
    tm = min(ROW_TILE, rows)
    n_chunks = wg.shape[0]
    return pl.pallas_call(
        functools.partial(_ffn_body, n_chunks=n_chunks),
        grid=(rows // tm,),
        in_specs=[_row_spec(tm, D_MODEL), _full_spec(g.shape), _full_spec(wg.shape),
                  _full_spec(wu.shape), _full_spec(wd.shape)],
        out_specs=_row_spec(tm, D_MODEL),
        out_shape=jax.ShapeDtypeStruct(x.shape, F32),
        scratch_shapes=[pltpu.VMEM((tm, D_MODEL), F32)],
        compiler_params=_cparams("parallel"),
        name="ffn",
    )(x, g, wg, wu, wd)


def _prep_ffn(norm, w_gate, w_up, w_down):
    n_chunks = D_FF // FF_TILE
    wg = w_gate.astype(BF16).reshape(D_MODEL, n_chunks, FF_TILE).transpose(1, 0, 2)
    wu = w_up.astype(BF16).reshape(D_MODEL, n_chunks, FF_TILE).transpose(1, 0, 2)
    wd = w_down.astype(BF16).reshape(n_chunks, FF_TILE, D_MODEL)
    return norm.reshape(1, D_MODEL), wg, wu, wd


def _inproj_even_body(x_ref, g_ref, w_ref, bf_ref, qn_ref, kn_ref,
                      q_ref, k_ref, v_ref, lf_ref, hr_ref):
    n = _rms(x_ref[...], g_ref[...]).astype(BF16)
    ones = _group_ones(HEAD_DIM)
    hq = _dot(n, w_ref[:, 0:FOX_DIM])
    q = hq * lax.rsqrt(_group_sum(hq * hq, ones) * (1.0 / HEAD_DIM) + NORM_EPS) * qn_ref[...]
    q_ref[...] = (q * ATT_SCALE).astype(BF16)
    hk = _dot(n, w_ref[:, FOX_DIM:2 * FOX_DIM])
    k_ref[...] = hk * lax.rsqrt(_group_sum(hk * hk, ones) * (1.0 / HEAD_DIM) + NORM_EPS) * kn_ref[...]
    v_ref[...] = _dot(n, w_ref[:, 2 * FOX_DIM:3 * FOX_DIM])
    f0 = 3 * FOX_DIM
    lf_ref[...] = _log_sigmoid(_dot(n, w_ref[:, f0:f0 + LANES]) + bf_ref[...])
    hr_ref[...] = _dot(n, w_ref[:, f0 + LANES:])


def _inproj_even(x, g, w, bf, qn, kn):
    rows = x.shape[0]
    tm = min(ROW_TILE, rows)
    outs = [jax.ShapeDtypeStruct((rows, FOX_DIM), BF16), jax.ShapeDtypeStruct((rows, FOX_DIM), F32),
            jax.ShapeDtypeStruct((rows, FOX_DIM), F32), jax.ShapeDtypeStruct((rows, LANES), F32),
            jax.ShapeDtypeStruct((rows, RWKV_COLS), F32)]
    return pl.pallas_call(
        _inproj_even_body,
        grid=(rows // tm,),
        in_specs=[_row_spec(tm, D_MODEL), _full_spec(g.shape), _full_spec(w.shape), _full_spec(bf.shape),
                  _full_spec(qn.shape), _full_spec(kn.shape)],
        out_specs=[_row_spec(tm, FOX_DIM), _row_spec(tm, FOX_DIM), _row_spec(tm, FOX_DIM),
                   _row_spec(tm, LANES), _row_spec(tm, RWKV_COLS)],
        out_shape=outs,
        compiler_params=_cparams("parallel"),
        name="inproj_even",
    )(x, g, w, bf, qn, kn)


def _cumsum_body(lf_ref, col_ref, row_ref, *, n_chunks, tk):
    r = lax.broadcasted_iota(jnp.int32, (tk, tk), 0)
    c = lax.broadcasted_iota(jnp.int32, (tk, tk), 1)
    tri = jnp.where(c <= r, 1.0, 0.0).astype(BF16)
    carry = jnp.zeros((1, LANES), F32)
    for i in range(n_chunks):
        x = lf_ref[0, i * tk:(i + 1) * tk, :]
        hi = x.astype(BF16)
        r1 = x - hi.astype(F32)
        mid = r1.astype(BF16)
        lo = (r1 - mid.astype(F32)).astype(BF16)
        cs = _dot(tri, hi) + _dot(tri, mid) + _dot(tri, lo) + carry
        col_ref[0, i * tk:(i + 1) * tk, :] = cs
        row_ref[0, i] = cs.T[0:8, :]
        carry = cs[tk - 1:tk, :]


def _cumsum(lf, tk):
    b, length, _ = lf.shape
    n_chunks = length // tk
    return pl.pallas_call(
        functools.partial(_cumsum_body, n_chunks=n_chunks, tk=tk),
        grid=(b,),
        in_specs=[pl.BlockSpec((1, length, LANES), lambda i: (i, 0, 0))],
        out_specs=[pl.BlockSpec((1, length, LANES), lambda i: (i, 0, 0)),
                   pl.BlockSpec((1, n_chunks, 8, tk), lambda i: (i, 0, 0, 0))],
        out_shape=[jax.ShapeDtypeStruct((b, length, LANES), F32),
                   jax.ShapeDtypeStruct((b, n_chunks, 8, tk), F32)],
        compiler_params=_cparams("parallel"),
        name="cumsum",
    )(lf)


def _softmax_step(carry, qh, kb, vb, bias, mask):
    m, l, acc = carry
    s = _dot_nt(qh, kb) + bias
    if mask is not None:
        s = jnp.where(mask, s, NEG_BIG)
    m_new = jnp.maximum(m, jnp.max(s, axis=1, keepdims=True))
    alpha = jnp.exp(m - m_new)
    p = jnp.exp(s - m_new)
    l = alpha * l + jnp.sum(p, axis=1, keepdims=True)
    acc = alpha * acc + _dot(p.astype(BF16), vb)
    return m_new, l, acc


def _fox_heads(q, cq_ref, past_block, n_past, diag_block, tq):
    half = _lane_half((tq, LANES))
    outs = []
    for h in range(2):
        sel = half if h == 0 else jnp.logical_not(half)
        qh = jnp.where(sel, q, jnp.zeros_like(q))
        cq = cq_ref[0, 0, :, h:h + 1]
        init = (jnp.full((tq, 1), NEG_BIG, F32), jnp.zeros((tq, 1), F32), jnp.zeros((tq, LANES), F32))

        def past(j, carry, qh=qh, cq=cq, h=h):
            kb, vb, ck = past_block(j, h)
            return _softmax_step(carry, qh, kb, vb, cq - ck, None)

        carry = lax.fori_loop(0, n_past, past, init)
        kb, vb, ck = diag_block(h)
        td = kb.shape[0]
        causal = (lax.broadcasted_iota(jnp.int32, (tq, td), 1)
                  <= lax.broadcasted_iota(jnp.int32, (tq, td), 0))
        _, l, acc = _softmax_step(carry, qh, kb, vb, cq - ck, causal)
        outs.append(acc / l)
    return jnp.where(half, outs[0], outs[1])


def _fox_prompt_body(q_ref, cq_ref, k_ref, v_ref, ck_ref, o_ref, *, tq):
    i = pl.program_id(2)

    def block(j, h):
        off = pl.multiple_of(j * tq, tq)
        return (k_ref[0, pl.ds(off, tq), :].astype(BF16), v_ref[0, pl.ds(off, tq), :].astype(BF16),
                ck_ref[0, 0, j, h:h + 1, :])

    o_ref[0] = _fox_heads(q_ref[0], cq_ref, block, i, lambda h: block(i, h), tq)


def _fox_prompt(q, k, v, cq, ck, tq):
    b, t, _ = q.shape
    pairs = FOX_HEADS // 2
    nq = t // tq
    return pl.pallas_call(
        functools.partial(_fox_prompt_body, tq=tq),
        grid=(b, pairs, nq),
        in_specs=[pl.BlockSpec((1, tq, LANES), lambda bi, hp, i: (bi, i, hp)),
                  pl.BlockSpec((1, 1, tq, 2), lambda bi, hp, i: (bi, hp, i, 0)),
                  pl.BlockSpec((1, t, LANES), lambda bi, hp, i: (bi, 0, hp)),
                  pl.BlockSpec((1, t, LANES), lambda bi, hp, i: (bi, 0, hp)),
                  pl.BlockSpec((1, 1, nq, 2, tq), lambda bi, hp, i: (bi, hp, 0, 0, 0))],
        out_specs=pl.BlockSpec((1, tq, LANES), lambda bi, hp, i: (bi, i, hp)),
        out_shape=jax.ShapeDtypeStruct((b, t, FOX_DIM), F32),
        compiler_params=_cparams("parallel", "parallel", "parallel"),
        name="fox_prompt",
    )(q, cq, k, v, ck)


def _fox_sample_body(q_ref, cq_ref, kp_ref, vp_ref, ckp_ref, kd_ref, vd_ref, ckd_ref, o_ref, *, tq, tk, n_past):
    def past(j, h):
        off = pl.multiple_of(j * tk, tk)
        return (kp_ref[0, pl.ds(off, tk), :].astype(BF16), vp_ref[0, pl.ds(off, tk), :].astype(BF16),
                ckp_ref[0, 0, j, h:h + 1, :])

    def diag(h):
        return kd_ref[0].astype(BF16), vd_ref[0].astype(BF16), ckd_ref[0, 0, h:h + 1, :]

    o_ref[0] = _fox_heads(q_ref[0], cq_ref, past, n_past, diag, tq)


def _fox_sample(q, k_new, v_new, k_past, v_past, cq, ck_past, ck_new, tk):
    b, ts, _ = q.shape
    p = k_past.shape[1]
    pairs = FOX_HEADS // 2
    n_past = p // tk
    return pl.pallas_call(
        functools.partial(_fox_sample_body, tq=ts, tk=tk, n_past=n_past),
        grid=(b, pairs),
        in_specs=[pl.BlockSpec((1, ts, LANES), lambda bi, hp: (bi, 0, hp)),
                  pl.BlockSpec((1, 1, ts, 2), lambda bi, hp: (bi, hp, 0, 0)),
                  pl.BlockSpec((1, p, LANES), lambda bi, hp: (bi, 0, hp)),
                  pl.BlockSpec((1, p, LANES), lambda bi, hp: (bi, 0, hp)),
                  pl.BlockSpec((1, 1, n_past, 2, tk), lambda bi, hp: (bi, hp, 0, 0, 0)),
                  pl.BlockSpec((1, ts, LANES), lambda bi, hp: (bi, 0, hp)),
                  pl.BlockSpec((1, ts, LANES), lambda bi, hp: (bi, 0, hp)),
                  pl.BlockSpec((1, 1, 2, ts), lambda bi, hp: (bi, hp, 0, 0))],
        out_specs=pl.BlockSpec((1, ts, LANES), lambda bi, hp: (bi, 0, hp)),
        out_shape=jax.ShapeDtypeStruct((b, ts, FOX_DIM), F32),
        compiler_params=_cparams("parallel", "parallel"),
        name="fox_sample",
    )(q, cq, k_past, v_past, ck_past, k_new, v_new, ck_new)


def _rwkv_prep_body(h_ref, prev_ref, mu_ref, w0_ref, w2_ref, a0_ref, a2_ref, g2_ref, kk_ref, ka_ref, rk_ref,
                    r_out, w_out, k_out, v_out, kap_out, bet_out, g_out, bon_out, *, tiles_per_seq):
    h = h_ref[...]
    tm = h.shape[0]
    prev_row = prev_ref[0, 7:8, :]
    if tiles_per_seq:
        keep = jnp.where(pl.program_id(0) % tiles_per_seq != 0, 1.0, 0.0)
        prev_row = prev_row * keep
    first = lax.broadcasted_iota(jnp.int32, (tm, 1), 0) == 0
    prev = jnp.where(first, prev_row, pltpu.roll(h, 1, axis=0))
    hx = h + (prev - h) * mu_ref[...]
    r = hx[:, 0:RWKV_DIM]
    k = hx[:, RWKV_DIM:2 * RWKV_DIM]
    v = hx[:, 2 * RWKV_DIM:3 * RWKV_DIM]
    xwa = hx[:, 3 * RWKV_DIM:3 * RWKV_DIM + LANES]
    xg = hx[:, 3 * RWKV_DIM + LANES:]
    w_logit = w0_ref[...] + _dot(jnp.tanh(xwa).astype(BF16), w2_ref[...])
    decay = jnp.exp(-jnp.exp(_log_sigmoid(w_logit) - 0.5))
    a = jax.nn.sigmoid(a0_ref[...] + _dot(xwa.astype(BF16), a2_ref[...]))
    g = _dot(jax.nn.sigmoid(xg).astype(BF16), g2_ref[...])
    ones = _group_ones(HEAD_DIM)
    kk = k * kk_ref[...]
    kk = kk / jnp.maximum(jnp.sqrt(_group_sum(kk * kk, ones)), 1e-12)
    k2 = k * (1.0 + (a - 1.0) * ka_ref[...])
    r_out[...] = r
    w_out[...] = decay
    k_out[...] = k2
    v_out[...] = v
    kap_out[...] = kk
    bet_out[...] = kk * a
    g_out[...] = g
    bon_out[...] = _group_sum(r * k2 * rk_ref[...], ones) * v


def _rwkv_prep(hr, prev8, prev_map, tm, tiles_per_seq, params):
    rows = hr.shape[0]
    out = jax.ShapeDtypeStruct((rows, RWKV_DIM), F32)
    return pl.pallas_call(
        functools.partial(_rwkv_prep_body, tiles_per_seq=tiles_per_seq),
        grid=(rows // tm,),
        in_specs=[_row_spec(tm, RWKV_COLS), pl.BlockSpec((1, 8, RWKV_COLS), prev_map)]
        + [_full_spec(p.shape) for p in params],
        out_specs=[_row_spec(tm, RWKV_DIM)] * 8,
        out_shape=[out] * 8,
        compiler_params=_cparams("parallel"),
        name="rwkv_prep",
    )(hr, prev8, *params)


def _rwkv_scan_body(w_ref, kap_ref, bet_ref, k2_ref, r_ref, v_ref, s0_ref, o_ref, s_ref, *, tb, nv):
    @pl.when(pl.program_id(0) == 0)
    def _():
        s_ref[...] = s0_ref[...]

    def token(t, carry):
        w = w_ref[t]
        kap = kap_ref[t]
        bet = bet_ref[t]
        k2 = k2_ref[t]
        r = r_ref[t]
        for vp in range(nv):
            s = s_ref[vp]
            rho = jnp.sum(s * kap, axis=0, keepdims=True)
            sn = s * w - bet * rho + k2 * v_ref[t, vp:vp + 1, :]
            s_ref[vp] = sn
            o_ref[t, vp:vp + 1, :] = jnp.sum(sn * r, axis=0, keepdims=True)
        return carry

    lax.fori_loop(0, tb, token, 0)


def _rwkv_scan(w, kap, bet, k2, r, v, s0):
    t = w.shape[0]
    nv = v.shape[1]
    tb = min(SCAN_TOKENS, t)
    op_spec = pl.BlockSpec((tb, HEAD_DIM, LANES), lambda i: (i, 0, 0))
    v_spec = pl.BlockSpec((tb, nv, LANES), lambda i: (i, 0, 0))
    s_spec = pl.BlockSpec((nv, HEAD_DIM, LANES), lambda i: (0, 0, 0))
    return pl.pallas_call(
        functools.partial(_rwkv_scan_body, tb=tb, nv=nv),
        grid=(t // tb,),
        in_specs=[op_spec] * 5 + [v_spec, s_spec],
        out_specs=[v_spec, s_spec],
        out_shape=[jax.ShapeDtypeStruct((t, nv, LANES), F32), jax.ShapeDtypeStruct((nv, HEAD_DIM, LANES), F32)],
        compiler_params=_cparams("arbitrary"),
        name="rwkv_scan",
    )(w, kap, bet, k2, r, v, s0)


def _outproj_even_body(x_ref, a_ref, o_ref, bon_ref, g_ref, lng_ref, lnb_ref, w_ref, out_ref):
    ones = _group_ones(HEAD_DIM)
    o = o_ref[...]
    d = o - _group_sum(o, ones) * (1.0 / HEAD_DIM)
    var = _group_sum(d * d, ones) * (1.0 / HEAD_DIM)
    y = d * lax.rsqrt(var + RWKV_GN_EPS) * lng_ref[...] + lnb_ref[...]
    b = (y + bon_ref[...]) * g_ref[...]
    out_ref[...] = (x_ref[...] + _dot(a_ref[...].astype(BF16), w_ref[0:FOX_DIM, :])
                    + _dot(b.astype(BF16), w_ref[FOX_DIM:, :]))


def _outproj_even(x, a, o, bon, g, lng, lnb, w):
    rows = x.shape[0]
    tm = min(ROW_TILE, rows)
    return pl.pallas_call(
        _outproj_even_body,
        grid=(rows // tm,),
        in_specs=[_row_spec(tm, D_MODEL)] + [_row_spec(tm, FOX_DIM)] * 4
        + [_full_spec(lng.shape), _full_spec(lnb.shape), _full_spec(w.shape)],
        out_specs=_row_spec(tm, D_MODEL),
        out_shape=jax.ShapeDtypeStruct(x.shape, F32),
        compiler_params=_cparams("parallel"),
        name="outproj_even",
    )(x, a, o, bon, g, lng, lnb, w)


def _rope(x, cos, sin_up, sin_dn):
    return x * cos + pltpu.roll(x, ROPE_DIM // 2, axis=1) * sin_up + pltpu.roll(x, LANES - ROPE_DIM // 2, axis=1) * sin_dn


def _inproj_odd_body(x_ref, g_ref, w_ref, qn_ref, kn_ref, vn_ref, cos_ref, sup_ref, sdn_ref,
                     q_ref, k_ref, v_ref, u_ref, gv_ref):
    n = _rms(x_ref[...], g_ref[...]).astype(BF16)
    cos, sup, sdn = cos_ref[...], sup_ref[...], sdn_ref[...]
    ones128 = _group_ones(LANES)
    ones64 = _group_ones(HEAD_DIM)
    qw = SWA_HEADS * LANES
    for h in range(SWA_HEADS):
        hq = _dot(n, w_ref[:, h * LANES:(h + 1) * LANES])
        qh = hq * lax.rsqrt(_group_sum(hq * hq, ones128) * (1.0 / HEAD_DIM) + NORM_EPS) * qn_ref[:, h * LANES:(h + 1) * LANES]
        q_ref[:, h * LANES:(h + 1) * LANES] = (_rope(qh, cos, sup, sdn) * ATT_SCALE).astype(BF16)
    hk = _dot(n, w_ref[:, qw:qw + SWA_KV])
    kn = hk * lax.rsqrt(_group_sum(hk * hk, ones64) * (1.0 / HEAD_DIM) + NORM_EPS) * kn_ref[...]
    k_ref[...] = _rope(kn, cos, sup, sdn)
    v_ref[...] = _dot(n, w_ref[:, qw + SWA_KV:qw + 2 * SWA_KV])
    s0 = qw + 2 * SWA_KV
    u_ref[...] = _gelu_tanh(_dot(n, w_ref[:, s0:s0 + SGU_DIM]))
    gv = _gelu_tanh(_dot(n, w_ref[:, s0 + SGU_DIM:]))
    gv_ref[...] = _rms(gv, vn_ref[...])


def _inproj_odd(x, g, w, qn, kn, vn, cos, sup, sdn, table_blocks):
    rows = x.shape[0]
    tm = min(ROW_TILE, rows)
    tab = pl.BlockSpec((tm, LANES), lambda i: (i % table_blocks, 0))
    outs = [jax.ShapeDtypeStruct((rows, SWA_HEADS * LANES), BF16), jax.ShapeDtypeStruct((rows, SWA_KV), F32),
            jax.ShapeDtypeStruct((rows, SWA_KV), F32), jax.ShapeDtypeStruct((rows, SGU_DIM), F32),
            jax.ShapeDtypeStruct((rows, SGU_DIM), F32)]
    return pl.pallas_call(
        _inproj_odd_body,
        grid=(rows // tm,),
        in_specs=[_row_spec(tm, D_MODEL), _full_spec(g.shape), _full_spec(w.shape), _full_spec(qn.shape),
                  _full_spec(kn.shape), _full_spec(vn.shape), tab, tab, tab],
        out_specs=[_row_spec(tm, SWA_HEADS * LANES), _row_spec(tm, SWA_KV), _row_spec(tm, SWA_KV),
                   _row_spec(tm, SGU_DIM), _row_spec(tm, SGU_DIM)],
        out_shape=outs,
        compiler_params=_cparams("parallel"),
        name="inproj_odd",
    )(x, g, w, qn, kn, vn, cos, sup, sdn)


def _swa_body(sink_ref, q_ref, kp_ref, kc_ref, vp_ref, vc_ref, mask_ref, o_ref):
    kw = jnp.concatenate([kp_ref[0], kc_ref[0]], axis=0).astype(BF16)
    vw = jnp.concatenate([vp_ref[0], vc_ref[0]], axis=0).astype(BF16)
    visible = mask_ref[0] > 0.5
    for h in range(SWA_HEADS):
        s = jnp.where(visible, _dot_nt(q_ref[0, :, h * LANES:(h + 1) * LANES], kw), NEG_BIG)
        sink = sink_ref[h]
        m = jnp.maximum(jnp.max(s, axis=1, keepdims=True), sink)
        p = jnp.exp(s - m)
        l = jnp.sum(p, axis=1, keepdims=True) + jnp.exp(sink - m)
        o_ref[0, :, h * LANES:(h + 1) * LANES] = (_dot(p.astype(BF16), vw) / l).astype(BF16)


def _swa(sinks, q, k_prev, k_cur, v_prev, v_cur, mask, tq, prev_rows, prev_map, mask_map):
    b, t, qw = q.shape
    nk = prev_rows + tq
    cur = lambda bi, i: (bi, i, 0)
    return pl.pallas_call(
        _swa_body,
        grid=(b, t // tq),
        in_specs=[pl.BlockSpec(memory_space=pltpu.SMEM),
                  pl.BlockSpec((1, tq, qw), cur),
                  pl.BlockSpec((1, prev_rows, SWA_KV), prev_map), pl.BlockSpec((1, tq, SWA_KV), cur),
                  pl.BlockSpec((1, prev_rows, SWA_KV), prev_map), pl.BlockSpec((1, tq, SWA_KV), cur),
                  pl.BlockSpec((1, tq, nk), mask_map)],
        out_specs=pl.BlockSpec((1, tq, qw), cur),
        out_shape=jax.ShapeDtypeStruct((b, t, qw), BF16),
        compiler_params=_cparams("parallel", "parallel"),
        name="swa",
    )(sinks, q, k_prev, k_cur, v_prev, v_cur, mask)


def _sgu_body(u_ref, v_ref, w_ref, b_ref, o_ref, *, length, n_chunks):
    tril = (lax.broadcasted_iota(jnp.int32, (length, length), 1)
            <= lax.broadcasted_iota(jnp.int32, (length, length), 0))
    ws = [jnp.where(tril, w_ref[g], 0.0).astype(BF16) for g in range(SGU_GROUPS)]
    half = _lane_half((length, LANES))
    for c in range(n_chunks):
        rows = slice(c * length, (c + 1) * length)
        for j in range(SGU_GROUPS // 2):
            cols = slice(j * LANES, (j + 1) * LANES)
            vb = v_ref[rows, cols].astype(BF16)
            mixed = jnp.where(half, _dot(ws[2 * j], vb), _dot(ws[2 * j + 1], vb)) + b_ref[:, cols]
            o_ref[rows, cols] = u_ref[rows, cols] * mixed


def _sgu(u, v, w, bias, length, n_chunks):
    rows = u.shape[0]
    tm = length * n_chunks
    return pl.pallas_call(
        functools.partial(_sgu_body, length=length, n_chunks=n_chunks),
        grid=(rows // tm,),
        in_specs=[_row_spec(tm, SGU_DIM), _row_spec(tm, SGU_DIM), _full_spec(w.shape), _full_spec(bias.shape)],
        out_specs=_row_spec(tm, SGU_DIM),
        out_shape=jax.ShapeDtypeStruct(u.shape, F32),
        compiler_params=_cparams("parallel"),
        name="sgu",
    )(u, v, w, bias)


def _outproj_odd_body(x_ref, c_ref, d_ref, wc_ref, wd_ref, out_ref):
    out_ref[...] = x_ref[...] + _dot(c_ref[...], wc_ref[...]) + _dot(d_ref[...].astype(BF16), wd_ref[...])


def _outproj_odd(x, c, d, wc, wd):
    rows = x.shape[0]
    tm = min(ROW_TILE, rows)
    return pl.pallas_call(
        _outproj_odd_body,
        grid=(rows // tm,),
        in_specs=[_row_spec(tm, D_MODEL), _row_spec(tm, c.shape[1]), _row_spec(tm, SGU_DIM),
                  _full_spec(wc.shape), _full_spec(wd.shape)],
        out_specs=_row_spec(tm, D_MODEL),
        out_shape=jax.ShapeDtypeStruct(x.shape, F32),
        compiler_params=_cparams("parallel"),
        name="outproj_odd",
    )(x, c, d, wc, wd)


def _memkv_body(m_ref, g_ref, w_ref, kn_ref, k_ref, v_ref):
    n = _rms(m_ref[...], g_ref[...]).astype(BF16)
    ones = _group_ones(HEAD_DIM)
    hk = _dot(n, w_ref[:, 0:MEM_DIM])
    k_ref[...] = hk * lax.rsqrt(_group_sum(hk * hk, ones) * (1.0 / HEAD_DIM) + NORM_EPS) * kn_ref[...]
    v_ref[...] = _dot(n, w_ref[:, MEM_DIM:])


def _memkv(mem, g, w, kn):
    rows = mem.shape[0]
    tm = min(ROW_TILE, rows)
    out = jax.ShapeDtypeStruct((rows, MEM_DIM), F32)
    return pl.pallas_call(
        _memkv_body,
        grid=(rows // tm,),
        in_specs=[_row_spec(tm, D_MODEL), _full_spec(g.shape), _full_spec(w.shape), _full_spec(kn.shape)],
        out_specs=[_row_spec(tm, MEM_DIM)] * 2,
        out_shape=[out, out],
        compiler_params=_cparams("parallel"),
        name="memkv",
    )(mem, g, w, kn)


def _xattn_body(x_ref, g_ref, wq_ref, qn_ref, mk_ref, mv_ref, wo_ref, out_ref):
    x = x_ref[0]
    tq = x.shape[0]
    n = _rms(x, g_ref[...]).astype(BF16)
    hq = _dot(n, wq_ref[...])
    ones = _group_ones(HEAD_DIM)
    q = (hq * lax.rsqrt(_group_sum(hq * hq, ones) * (1.0 / HEAD_DIM) + NORM_EPS) * qn_ref[...] * ATT_SCALE).astype(BF16)
    half = _lane_half((tq, LANES))
    blocks = []
    for j in range(MEM_HEADS // 2):
        cols = slice(j * LANES, (j + 1) * LANES)
        kb = mk_ref[0, :, cols].astype(BF16)
        vb = mv_ref[0, :, cols].astype(BF16)
        qb = q[:, cols]
        outs = []
        for h in range(2):
            sel = half if h == 0 else jnp.logical_not(half)
            s = _dot_nt(jnp.where(sel, qb, jnp.zeros_like(qb)), kb)
            p = jnp.exp(s - jnp.max(s, axis=1, keepdims=True))
            outs.append(_dot(p.astype(BF16), vb) / jnp.sum(p, axis=1, keepdims=True))
        blocks.append(jnp.where(half, outs[0], outs[1]))
    o = jnp.concatenate(blocks, axis=1).astype(BF16)
    out_ref[0] = x + _dot(o, wo_ref[...])


def _xattn(x, g, wq, qn, mk, mv, wo, tq):
    b, t, _ = x.shape
    m = mk.shape[1]
    return pl.pallas_call(
        _xattn_body,
        grid=(b, t // tq),
        in_specs=[pl.BlockSpec((1, tq, D_MODEL), lambda bi, i: (bi, i, 0)), _full_spec(g.shape),
                  _full_spec(wq.shape), _full_spec(qn.shape),
                  pl.BlockSpec((1, m, MEM_DIM), lambda bi, i: (bi, 0, 0)),
                  pl.BlockSpec((1, m, MEM_DIM), lambda bi, i: (bi, 0, 0)), _full_spec(wo.shape)],
        out_specs=pl.BlockSpec((1, tq, D_MODEL), lambda bi, i: (bi, i, 0)),
        out_shape=jax.ShapeDtypeStruct(x.shape, F32),
        compiler_params=_cparams("parallel", "parallel"),
        name="xattn",
    )(x, g, wq, qn, mk, mv, wo)


def _pair_cols(cum_col, t):
    b = cum_col.shape[0]
    return cum_col[:, :t, :FOX_HEADS].reshape(b, t, FOX_HEADS // 2, 2).transpose(0, 2, 1, 3)


def _pair_rows(cum_row):
    b, n, _, tk = cum_row.shape
    return cum_row.reshape(b, n, FOX_HEADS // 2, 2, tk).transpose(0, 2, 1, 3, 4)


def _to_chains(x, b, t, dup):
    y = x.reshape(b, t, RWKV_HEADS, HEAD_DIM).transpose(1, 3, 0, 2).reshape(t, HEAD_DIM, b * RWKV_HEADS)
    return jnp.concatenate([y] * dup, axis=-1) if dup > 1 else y


def _rope_tables(pos):
    half = ROPE_DIM // 2
    inv_freq = jnp.power(ROPE_THETA, -jnp.arange(half, dtype=F32) / half)
    ang = pos.astype(F32)[:, None] * inv_freq[None, :]
    cos, sin = jnp.cos(ang), jnp.sin(ang)
    n = pos.shape[0]
    pad = jnp.zeros((n, HEAD_DIM - ROPE_DIM), F32)
    zero = jnp.zeros((n, half), F32)
    cos_t = jnp.concatenate([cos, cos, pad + 1.0], axis=1)
    up_t = jnp.concatenate([zero, sin, pad], axis=1)
    dn_t = jnp.concatenate([-sin, zero, pad], axis=1)
    two = lambda a: jnp.concatenate([a, a], axis=1)
    return two(cos_t), two(up_t), two(dn_t)


def _swa_prompt_mask(tq):
    span = WINDOW + tq
    qc = np.arange(tq)[:, None] // CHUNK
    kc = np.arange(span)[None, :] // CHUNK - WINDOW_CHUNKS
    band = (kc <= qc) & (kc >= qc - WINDOW_CHUNKS)
    first = band & (kc >= 0)
    return jnp.asarray(np.stack([first, band]).astype(np.float32))


def _swa_sample_mask(past, rows, t):
    kc = (past - rows + np.arange(rows + t)) // CHUNK
    qc = (past + np.arange(t)) // CHUNK
    m = (kc[None, :] <= qc[:, None]) & (kc[None, :] >= qc[:, None] - WINDOW_CHUNKS)
    return jnp.asarray(m[None].astype(np.float32))


def kernel(x_prompt, x_sample, cache_fox_k, cache_fox_v, cache_fox_logf, state_rwkv, state_rwkv_shift, cache_swa_k, cache_swa_v, cache_mem_k, cache_mem_v, mem_prompt, ffn1_norm, ffn1_w_gate, ffn1_w_up, ffn1_w_down, mix_norm, ev_w_in, fox_b_f, fox_q_norm, fox_k_norm, rwkv_mu, rwkv_w0, rwkv_w2, rwkv_a0, rwkv_a2, rwkv_g2, rwkv_k_k, rwkv_k_a, rwkv_r_k, rwkv_ln_g, rwkv_ln_b, ev_w_out, od_w_in, swa_q_norm, swa_k_norm, swa_sinks, sgu_v_norm, sgu_w_s, sgu_b, od_w_out, xattn_norm, mem_norm, xattn_wq, xattn_wkv, xattn_q_norm, xattn_k_norm, xattn_wo, ffn2_norm, ffn2_w_gate, ffn2_w_up, ffn2_w_down):
    bp, tp, _ = x_prompt.shape
    bs, ts, _ = x_sample.shape
    depth = ffn1_norm.shape[0]
    past = cache_fox_k.shape[2]
    mem_tokens = mem_prompt.shape[1]
    xp = x_prompt.reshape(bp * tp, D_MODEL)
    xs = x_sample.reshape(bs * ts, D_MODEL)
    mem_rows = mem_prompt.reshape(bp * mem_tokens, D_MODEL)
    row = lambda a: a.reshape(1, -1)
    tile_heads = lambda a, n: jnp.tile(a, n).reshape(1, -1)

    out = {k: [] for k in ("p_fox_k", "p_fox_v", "p_fox_logf", "p_rwkv_state", "p_rwkv_shift", "p_swa_k", "p_swa_v",
                           "p_mem_k", "p_mem_v", "s_fox_k", "s_fox_v", "s_fox_logf", "s_rwkv_state", "s_rwkv_shift",
                           "s_swa_k", "s_swa_v", "s_sgu_v")}

    for l in range(depth):
        f1 = _prep_ffn(ffn1_norm[l], ffn1_w_gate[l], ffn1_w_up[l], ffn1_w_down[l])
        xp = _ffn(xp, *f1)
        xs = _ffn(xs, *f1)
        g_mix = row(mix_norm[l])
        if l % 2 == 0:
            e = l // 2
            w_in = ev_w_in[e]
            f0 = 3 * FOX_DIM
            w_cat = jnp.concatenate([w_in[:, :f0], jnp.pad(w_in[:, f0:f0 + FOX_HEADS], ((0, 0), (0, LANES - FOX_HEADS))),
                                     w_in[:, f0 + FOX_HEADS:]], axis=1).astype(BF16)
            bf = jnp.pad(fox_b_f[e], (0, LANES - FOX_HEADS)).reshape(1, LANES)
            qn = tile_heads(fox_q_norm[e], FOX_HEADS)
            kn = tile_heads(fox_k_norm[e], FOX_HEADS)
            qp, kp, vp, lfp, hrp = _inproj_even(xp, g_mix, w_cat, bf, qn, kn)
            qs, ks, vs, lfs, hrs = _inproj_even(xs, g_mix, w_cat, bf, qn, kn)

            colp, rowp = _cumsum(lfp.reshape(bp, tp, LANES), ATT_TILE)
            a_p = _fox_prompt(qp.reshape(bp, tp, FOX_DIM), kp.reshape(bp, tp, FOX_DIM), vp.reshape(bp, tp, FOX_DIM),
                              _pair_cols(colp, tp), _pair_rows(rowp), ATT_TILE)
            lfs8 = lfs.reshape(bs, ts, LANES)[:, :, :FOX_HEADS]
            tot = past + ts
            padded = -(-tot // ATT_TILE) * ATT_TILE
            lf_all = jnp.concatenate([cache_fox_logf[e].astype(F32), lfs8], axis=1)
            lf_all = jnp.pad(lf_all, ((0, 0), (0, padded - tot), (0, LANES - FOX_HEADS)))
            cols, rows_ = _cumsum(lf_all, ATT_TILE)
            cq_s = cols[:, past:tot, :FOX_HEADS].reshape(bs, ts, FOX_HEADS // 2, 2).transpose(0, 2, 1, 3)
            ck_past = _pair_rows(rows_[:, :past // ATT_TILE])
            ck_new = cols[:, past:tot, :FOX_HEADS].reshape(bs, ts, FOX_HEADS // 2, 2).transpose(0, 2, 3, 1)
            a_s = _fox_sample(qs.reshape(bs, ts, FOX_DIM), ks.reshape(bs, ts, FOX_DIM), vs.reshape(bs, ts, FOX_DIM),
                              cache_fox_k[e].reshape(bs, past, FOX_DIM), cache_fox_v[e].reshape(bs, past, FOX_DIM),
                              cq_s, ck_past, ck_new, ATT_TILE)

            lora = 3 * RWKV_DIM
            w2p = jnp.pad(rwkv_w2[e], ((0, LANES - DECAY_LORA), (0, 0))).astype(BF16)
            a2p = jnp.pad(rwkv_a2[e], ((DECAY_LORA, 0), (0, 0))).astype(BF16)
            params = (row(rwkv_mu[e]), row(rwkv_w0[e]), w2p, row(rwkv_a0[e]), a2p, rwkv_g2[e].astype(BF16),
                      row(rwkv_k_k[e]), row(rwkv_k_a[e]), row(rwkv_r_k[e]))
            tiles_per_seq = tp // ROW_TILE
            prev_p = hrp.reshape(bp * tp // 8, 8, RWKV_COLS)
            prep_p = _rwkv_prep(hrp, prev_p, lambda i: (jnp.maximum(i * (ROW_TILE // 8) - 1, 0), 0, 0),
                                ROW_TILE, tiles_per_seq, params)
            prev_s = jnp.pad(state_rwkv_shift[e].astype(F32), ((0, 0), (7, 0), (0, 0)))
            prep_s = _rwkv_prep(hrs, prev_s, lambda i: (i, 0, 0), ts, 0, params)

            def scan(prep, b, t, state0):
                r, w, k2, v, kap, bet, g, bon = prep
                dup = LANES // (b * RWKV_HEADS)
                nv = HEAD_DIM // dup
                ops = [_to_chains(a, b, t, dup) for a in (w, kap, bet, k2, r)]
                vt = v.reshape(b, t, RWKV_HEADS, dup, nv).transpose(1, 4, 3, 0, 2).reshape(t, nv, LANES)
                s0 = state0.reshape(b, RWKV_HEADS, dup, nv, HEAD_DIM).transpose(3, 4, 2, 0, 1).reshape(nv, HEAD_DIM, LANES)
                o, sT = _rwkv_scan(*ops, vt, s0)
                o = o.reshape(t, nv, dup, b, RWKV_HEADS).transpose(3, 0, 4, 2, 1).reshape(b * t, RWKV_DIM)
                sT = sT.reshape(nv, HEAD_DIM, dup, b, RWKV_HEADS).transpose(3, 4, 2, 0, 1).reshape(b, RWKV_HEADS, HEAD_DIM, HEAD_DIM)
                return o, sT, g, bon

            o_p, st_p, g_p, bon_p = scan(prep_p, bp, tp, jnp.zeros((bp, RWKV_HEADS, HEAD_DIM, HEAD_DIM), F32))
            o_s, st_s, g_s, bon_s = scan(prep_s, bs, ts, state_rwkv[e].astype(F32))

            w_out = ev_w_out[e].astype(BF16)
            lng, lnb = row(rwkv_ln_g[e]), row(rwkv_ln_b[e])
            xp = _outproj_even(xp, a_p.reshape(bp * tp, FOX_DIM), o_p, bon_p, g_p, lng, lnb, w_out)
            xs = _outproj_even(xs, a_s.reshape(bs * ts, FOX_DIM), o_s, bon_s, g_s, lng, lnb, w_out)

            out["p_fox_k"].append(kp.reshape(bp, tp, FOX_HEADS, HEAD_DIM))
            out["p_fox_v"].append(vp.reshape(bp, tp, FOX_HEADS, HEAD_DIM))
            out["p_fox_logf"].append(lfp.reshape(bp, tp, LANES)[:, :, :FOX_HEADS])
            out["p_rwkv_state"].append(st_p)
            out["p_rwkv_shift"].append(hrp.reshape(bp, tp, RWKV_COLS)[:, -1:])
            out["s_fox_k"].append(ks.reshape(bs, ts, FOX_HEADS, HEAD_DIM))
            out["s_fox_v"].append(vs.reshape(bs, ts, FOX_HEADS, HEAD_DIM))
            out["s_fox_logf"].append(lfs8)
            out["s_rwkv_state"].append(st_s)
            out["s_rwkv_shift"].append(hrs.reshape(bs, ts, RWKV_COLS)[:, -1:])
        else:
            j = l // 2
            w_in = od_w_in[j]
            wq = w_in[:, :SWA_Q].reshape(D_MODEL, SWA_HEADS, HEAD_DIM)
            zeros = jnp.zeros_like(wq)
            kv_of = np.arange(SWA_HEADS) // SWA_GROUP
            lo = jnp.asarray(kv_of == 0)[None, :, None]
            wq_exp = jnp.concatenate([jnp.where(lo, wq, zeros), jnp.where(lo, zeros, wq)], axis=2)
            w_cat = jnp.concatenate([wq_exp.reshape(D_MODEL, SWA_HEADS * LANES), w_in[:, SWA_Q:]], axis=1).astype(BF16)
            qn = jnp.tile(swa_q_norm[j], 2 * SWA_HEADS).reshape(1, -1)
            kn = tile_heads(swa_k_norm[j], SWA_KV_HEADS)
            vn = row(sgu_v_norm[j])
            tabs_p = _rope_tables(jnp.arange(tp))
            tabs_s = _rope_tables(past + jnp.arange(bs * ts) % ts)
            qp, kp, vp, up, gp = _inproj_odd(xp, g_mix, w_cat, qn, kn, vn, *tabs_p, tp // ROW_TILE)
            qs, ks, vs, us, gs = _inproj_odd(xs, g_mix, w_cat, qn, kn, vn, *tabs_s, 1)

            qw = SWA_HEADS * LANES
            kp3, vp3 = kp.reshape(bp, tp, SWA_KV), vp.reshape(bp, tp, SWA_KV)
            ratio = SWA_TILE // WINDOW
            c_p = _swa(swa_sinks[j], qp.reshape(bp, tp, qw), kp3, kp3, vp3, vp3, _swa_prompt_mask(SWA_TILE),
                       SWA_TILE, WINDOW, lambda bi, i: (bi, jnp.maximum(i * ratio - 1, 0), 0),
                       lambda bi, i: (jnp.minimum(i, 1), 0, 0))
            rows_c = cache_swa_k.shape[2]
            ck3 = cache_swa_k[j].reshape(bs, rows_c, SWA_KV)
            cv3 = cache_swa_v[j].reshape(bs, rows_c, SWA_KV)
            ks3, vs3 = ks.reshape(bs, ts, SWA_KV), vs.reshape(bs, ts, SWA_KV)
            c_s = _swa(swa_sinks[j], qs.reshape(bs, ts, qw), ck3, ks3, cv3, vs3, _swa_sample_mask(past, rows_c, ts),
                       ts, rows_c, lambda bi, i: (bi, 0, 0), lambda bi, i: (0, 0, 0))

            bias = jnp.repeat(jnp.transpose(sgu_b[j]), HEAD_DIM, axis=1)
            d_p = _sgu(up, gp, sgu_w_s[j], bias, SGU_CHUNK, ROW_TILE // SGU_CHUNK)
            d_s = _sgu(us, gs, sgu_w_s[j][:, :ts, :ts], bias[:ts], ts, 1)

            w_out = od_w_out[j]
            wc = w_out[:SWA_Q].reshape(SWA_HEADS, HEAD_DIM, D_MODEL)
            zc = jnp.zeros_like(wc)
            lo_r = jnp.asarray(kv_of == 0)[:, None, None]
            wc_exp = jnp.concatenate([jnp.where(lo_r, wc, zc), jnp.where(lo_r, zc, wc)], axis=1)
            wc_exp = wc_exp.reshape(SWA_HEADS * LANES, D_MODEL).astype(BF16)
            wd = w_out[SWA_Q:].astype(BF16)
            xp = _outproj_odd(xp, c_p.reshape(bp * tp, qw), d_p, wc_exp, wd)
            xs = _outproj_odd(xs, c_s.reshape(bs * ts, qw), d_s, wc_exp, wd)

            out["p_swa_k"].append(kp3[:, -WINDOW:].reshape(bp, WINDOW, SWA_KV_HEADS, HEAD_DIM))
            out["p_swa_v"].append(vp3[:, -WINDOW:].reshape(bp, WINDOW, SWA_KV_HEADS, HEAD_DIM))
            out["s_swa_k"].append(jnp.concatenate([ck3, ks3], axis=1)[:, -rows_c:].reshape(bs, rows_c, SWA_KV_HEADS, HEAD_DIM))
            out["s_swa_v"].append(jnp.concatenate([cv3, vs3], axis=1)[:, -rows_c:].reshape(bs, rows_c, SWA_KV_HEADS, HEAD_DIM))
            out["s_sgu_v"].append(gs.reshape(bs, ts, SGU_DIM))

        mk, mv = _memkv(mem_rows, row(mem_norm[l]), xattn_wkv[l].astype(BF16), tile_heads(xattn_k_norm[l], MEM_HEADS))
        mk3, mv3 = mk.reshape(bp, mem_tokens, MEM_DIM), mv.reshape(bp, mem_tokens, MEM_DIM)
        xa = (row(xattn_norm[l]), xattn_wq[l].astype(BF16), tile_heads(xattn_q_norm[l], MEM_HEADS))
        wo = xattn_wo[l].astype(BF16)
        xp = _xattn(xp.reshape(bp, tp, D_MODEL), *xa, mk3, mv3, wo, ROW_TILE).reshape(bp * tp, D_MODEL)
        xs = _xattn(xs.reshape(bs, ts, D_MODEL), *xa, cache_mem_k[l].reshape(bs, mem_tokens, MEM_DIM),
                    cache_mem_v[l].reshape(bs, mem_tokens, MEM_DIM), wo, ts).reshape(bs * ts, D_MODEL)
        out["p_mem_k"].append(mk3.reshape(bp, mem_tokens, MEM_HEADS, HEAD_DIM))
        out["p_mem_v"].append(mv3.reshape(bp, mem_tokens, MEM_HEADS, HEAD_DIM))

        f2 = _prep_ffn(ffn2_norm[l], ffn2_w_gate[l], ffn2_w_up[l], ffn2_w_down[l])
        xp = _ffn(xp, *f2)
        xs = _ffn(xs, *f2)

    order = ("p_fox_k", "p_fox_v", "p_fox_logf", "p_rwkv_state", "p_rwkv_shift", "p_swa_k", "p_swa_v", "p_mem_k",
             "p_mem_v", "s_fox_k", "s_fox_v", "s_fox_logf", "s_rwkv_state", "s_rwkv_shift", "s_swa_k", "s_swa_v",
             "s_sgu_v")
    return (xp.reshape(bp, tp, D_MODEL), xs.reshape(bs, ts, D_MODEL)) + tuple(jnp.stack(out[k]) for k in order)
```

```python
import functools

import numpy as np
import jax
import jax.numpy as jnp
from jax import lax
from jax.experimental import pallas as pl
from jax.experimental.pallas import tpu as pltpu

F32 = jnp.float32
BF16 = jnp.bfloat16

D_MODEL = 1024
HEAD_DIM = 64
NORM_EPS = 1e-6
ROPE_THETA = 500000.0
ROPE_DIM = HEAD_DIM // 4
CHUNK = 64
FOX_HEADS = 8
FOX_DIM = FOX_HEADS * HEAD_DIM
RWKV_HEADS = 8
RWKV_DIM = RWKV_HEADS * HEAD_DIM
DECAY_LORA = 64
ICLR_LORA = 64
GATE_LORA = 128
RWKV_COLS = 3 * RWKV_DIM + DECAY_LORA + ICLR_LORA + GATE_LORA
RWKV_GN_EPS = 64e-5
SWA_HEADS = 8
SWA_KV_HEADS = 2
SWA_GROUP = SWA_HEADS // SWA_KV_HEADS
SWA_Q = SWA_HEADS * HEAD_DIM
SWA_KV = SWA_KV_HEADS * HEAD_DIM
WINDOW = 128
WINDOW_CHUNKS = WINDOW // CHUNK
SGU_GROUPS = 8
SGU_DIM = SGU_GROUPS * HEAD_DIM
SGU_CHUNK = 128
MEM_HEADS = 4
MEM_DIM = MEM_HEADS * HEAD_DIM
D_FF = 2816

LANES = 128
ROW_TILE = 512
FF_TILE = 256
ATT_TILE = 512
CUM_TILE = 256
SWA_TILE = 256
SCAN_TOKENS = 64
VMEM_LIMIT = 56 * 1024 * 1024
ATT_SCALE = HEAD_DIM ** -0.5
LOG2E = 1.4426950408889634
NEG_BIG = -1e30


def _cparams(*sem):
    return pltpu.CompilerParams(dimension_semantics=sem, vmem_limit_bytes=VMEM_LIMIT)


def _dot(a, b):
    return jnp.dot(a, b, preferred_element_type=F32)


def _dot_nt(a, b):
    return lax.dot_general(a, b, (((1,), (1,)), ((), ())), preferred_element_type=F32)


def _rms(x, g):
    ms = jnp.mean(x * x, axis=-1, keepdims=True)
    return (x * lax.rsqrt(ms + NORM_EPS)) * g


def _group_ones(group):
    shift = int(np.log2(group))
    r = lax.broadcasted_iota(jnp.int32, (LANES, LANES), 0) >> shift
    c = lax.broadcasted_iota(jnp.int32, (LANES, LANES), 1) >> shift
    return jnp.where(r == c, 1.0, 0.0).astype(BF16)


def _group_sum(x, ones):
    parts = []
    for j in range(x.shape[1] // LANES):
        blk = x[:, j * LANES:(j + 1) * LANES]
        hi = blk.astype(BF16)
        lo = (blk - hi.astype(F32)).astype(BF16)
        parts.append(_dot(hi, ones) + _dot(lo, ones))
    return parts[0] if len(parts) == 1 else jnp.concatenate(parts, axis=1)


def _log_sigmoid(z):
    return jnp.minimum(z, 0.0) - jnp.log(1.0 + jnp.exp(-jnp.abs(z)))


def _gelu_tanh(x):
    return 0.5 * x * (1.0 + jnp.tanh(0.7978845608028654 * (x + 0.044715 * (x * x * x))))


def _lane_half(shape):
    return lax.broadcasted_iota(jnp.int32, shape, 1) < HEAD_DIM


def _row_spec(tm, cols):
    return pl.BlockSpec((tm, cols), lambda i: (i, 0))


def _full_spec(shape):
    nd = len(shape)
    return pl.BlockSpec(shape, lambda *_: (0,) * nd, pipeline_mode=pl.Buffered(1))


def _ffn_body(x_ref, g_ref, wg_ref, wu_ref, wd_ref, o_ref, acc_ref, *, n_chunks):
    x = x_ref[...]
    n = _rms(x, g_ref[...]).astype(BF16)
    acc_ref[...] = jnp.zeros_like(acc_ref)

    def chunk(c, carry):
        gate = _dot(n, wg_ref[c])
        up = _dot(n, wu_ref[c])
        act = (gate * jax.nn.sigmoid(gate) * up).astype(BF16)
        acc_ref[...] += _dot(act, wd_ref[c])
        return carry

    lax.fori_loop(0, n_chunks, chunk, 0, unroll=True)
    o_ref[...] = x + 0.5 * acc_ref[...]


def _ffn(x, g, wg, wu, wd):
    rows = x.shape[0]
    tm = min(ROW_TILE, rows)
    n_chunks = wg.shape[0]
    return pl.pallas_call(
        functools.partial(_ffn_body, n_chunks=n_chunks),
        grid=(rows // tm,),
        in_specs=[_row_spec(tm, D_MODEL), _full_spec(g.shape), _full_spec(wg.shape),
                  _full_spec(wu.shape), _full_spec(wd.shape)],
        out_specs=_row_spec(tm, D_MODEL),
        out_shape=jax.ShapeDtypeStruct(x.shape, F32),
        scratch_shapes=[pltpu.VMEM((tm, D_MODEL), F32)],
        compiler_params=_cparams("parallel"),
        name="ffn",
    )(x, g, wg, wu, wd)


def _prep_ffn(norm, w_gate, w_up, w_down):
    n_chunks = D_FF // FF_TILE
    wg = w_gate.astype(BF16).reshape(D_MODEL, n_chunks, FF_TILE).transpose(1, 0, 2)
    wu = w_up.astype(BF16).reshape(D_MODEL, n_chunks, FF_TILE).transpose(1, 0, 2)
    wd = w_down.astype(BF16).reshape(n_chunks, FF_TILE, D_MODEL)
    return norm.reshape(1, D_MODEL), wg, wu, wd


def _inproj_even_body(x_ref, g_ref, w_ref, bf_ref, qn_ref, kn_ref,
                      q_ref, k_ref, v_ref, lf_ref, hr_ref):
    n = _rms(x_ref[...], g_ref[...]).astype(BF16)
    ones = _group_ones(HEAD_DIM)
    hq = _dot(n, w_ref[:, 0:FOX_DIM])
    q = hq * lax.rsqrt(_group_sum(hq * hq, ones) * (1.0 / HEAD_DIM) + NORM_EPS) * qn_ref[...]
    q_ref[...] = (q * (ATT_SCALE * LOG2E)).astype(BF16)
    hk = _dot(n, w_ref[:, FOX_DIM:2 * FOX_DIM])
    k_ref[...] = hk * lax.rsqrt(_group_sum(hk * hk, ones) * (1.0 / HEAD_DIM) + NORM_EPS) * kn_ref[...]
    v_ref[...] = _dot(n, w_ref[:, 2 * FOX_DIM:3 * FOX_DIM])
    f0 = 3 * FOX_DIM
    lf_ref[...] = _log_sigmoid(_dot(n, w_ref[:, f0:f0 + LANES]) + bf_ref[...])
    hr_ref[...] = _dot(n, w_ref[:, f0 + LANES:])


def _inproj_even(x, g, w, bf, qn, kn):
    rows = x.shape[0]
    tm = min(ROW_TILE, rows)
    outs = [jax.ShapeDtypeStruct((rows, FOX_DIM), BF16), jax.ShapeDtypeStruct((rows, FOX_DIM), F32),
            jax.ShapeDtypeStruct((rows, FOX_DIM), F32), jax.ShapeDtypeStruct((rows, LANES), F32),
            jax.ShapeDtypeStruct((rows, RWKV_COLS), F32)]
    return pl.pallas_call(
        _inproj_even_body,
        grid=(rows // tm,),
        in_specs=[_row_spec(tm, D_MODEL), _full_spec(g.shape), _full_spec(w.shape), _full_spec(bf.shape),
                  _full_spec(qn.shape), _full_spec(kn.shape)],
        out_specs=[_row_spec(tm, FOX_DIM), _row_spec(tm, FOX_DIM), _row_spec(tm, FOX_DIM),
                   _row_spec(tm, LANES), _row_spec(tm, RWKV_COLS)],
        out_shape=outs,
        compiler_params=_cparams("parallel"),
        name="inproj_even",
    )(x, g, w, bf, qn, kn)


def _split3(x):
    hi = x.astype(BF16)
    r1 = x - hi.astype(F32)
    mid = r1.astype(BF16)
    lo = (r1 - mid.astype(F32)).astype(BF16)
    return hi, mid, lo


def _cumsum_body(lf_ref, col_ref, *, n_chunks, tk):
    r = lax.broadcasted_iota(jnp.int32, (tk, tk), 0)
    c = lax.broadcasted_iota(jnp.int32, (tk, tk), 1)
    tri = jnp.where(c <= r, 1.0, 0.0).astype(BF16)
    carry = jnp.zeros((1, LANES), F32)
    for i in range(n_chunks):
        hi, mid, lo = _split3(lf_ref[0, i * tk:(i + 1) * tk, :])
        cs = _dot(tri, hi) + _dot(tri, mid) + _dot(tri, lo) + carry
        col_ref[0, i * tk:(i + 1) * tk, :] = cs
        carry = cs[tk - 1:tk, :]


def _cumsum(lf, tk):
    b, length, _ = lf.shape
    n_chunks = length // tk
    return pl.pallas_call(
        functools.partial(_cumsum_body, n_chunks=n_chunks, tk=tk),
        grid=(b,),
        in_specs=[pl.BlockSpec((1, length, LANES), lambda i: (i, 0, 0))],
        out_specs=pl.BlockSpec((1, length, LANES), lambda i: (i, 0, 0)),
        out_shape=jax.ShapeDtypeStruct((b, length, LANES), F32),
        compiler_params=_cparams("parallel"),
        name="cumsum",
    )(lf)


def _aug_base(h):
    return (1 - h) * HEAD_DIM


def _place3(terms, src_lane, dst_lane):
    r = lax.broadcasted_iota(jnp.int32, (LANES, LANES), 0)
    c = lax.broadcasted_iota(jnp.int32, (LANES, LANES), 1)
    out = None
    for i, t in enumerate(terms):
        sel = jnp.where((r == src_lane) & (c == dst_lane + i), 1.0, 0.0).astype(BF16)
        out = _dot(t, sel) if out is None else out + _dot(t, sel)
    return out


def _lane_ones(n, lo, hi):
    lane = lax.broadcasted_iota(jnp.int32, (n, LANES), 1)
    return jnp.where((lane >= lo) & (lane < hi), 1.0, 0.0)


def _data_mask(n, h):
    half = _lane_half((n, LANES))
    return half if h == 0 else jnp.logical_not(half)


def _fox_q(q, cum, head, h):
    n = q.shape[0]
    a0 = _aug_base(h)
    aug = _place3(_split3(cum * LOG2E), head, a0) + _lane_ones(n, a0 + 3, a0 + 6)
    return jnp.where(_data_mask(n, h), q.astype(F32), aug).astype(BF16)


def _fox_k(k, cum, head, h):
    n = k.shape[0]
    a0 = _aug_base(h)
    aug = _lane_ones(n, a0, a0 + 3) - _place3(_split3(cum * LOG2E), head, a0 + 3)
    return jnp.where(_data_mask(n, h), k, aug).astype(BF16)


def _fox_v(v, h):
    n = v.shape[0]
    a0 = _aug_base(h)
    return jnp.where(_data_mask(n, h), v, _lane_ones(n, a0, a0 + 1)).astype(BF16)


def _fox_finish(acc, h):
    a0 = _aug_base(h)
    return acc / acc[:, a0:a0 + 1]


def _fox_prompt_body(q_ref, cum_ref, k_ref, v_ref, o_ref, ka_sc, va_sc, m_sc, acc_sc, *, tq, t):
    hp = pl.program_id(1)
    i = pl.program_id(2)

    @pl.when(i == 0)
    def _():
        for h in range(2):
            for c in range(t // ROW_TILE):
                rows = slice(c * ROW_TILE, (c + 1) * ROW_TILE)
                ka_sc[h, rows, :] = _fox_k(k_ref[0, rows, :], cum_ref[0, rows, :], 2 * hp + h, h)
                va_sc[h, rows, :] = _fox_v(v_ref[0, rows, :], h)

    off_q = pl.multiple_of(i * tq, tq)
    cq = cum_ref[0, pl.ds(off_q, tq), :]
    qs = [_fox_q(q_ref[0], cq, 2 * hp + h, h) for h in range(2)]
    m_sc[...] = jnp.full(m_sc.shape, NEG_BIG, F32)
    acc_sc[...] = jnp.zeros(acc_sc.shape, F32)
    causal = (lax.broadcasted_iota(jnp.int32, (tq, tq), 1) <= lax.broadcasted_iota(jnp.int32, (tq, tq), 0))

    def step(j, masked):
        off = pl.multiple_of(j * tq, tq)
        for h in range(2):
            s = _dot_nt(qs[h], ka_sc[h, pl.ds(off, tq), :])
            if masked:
                s = jnp.where(causal, s, NEG_BIG)
            m_old = m_sc[h]
            m_new = jnp.maximum(m_old, jnp.max(s, axis=1, keepdims=True))
            p = jnp.concatenate([jnp.exp2(s[:, c * LANES:(c + 1) * LANES] - m_new) for c in range(tq // LANES)],
                                axis=1).astype(BF16)
            acc_sc[h] = jnp.exp2(m_old - m_new) * acc_sc[h] + _dot(p, va_sc[h, pl.ds(off, tq), :])
            m_sc[h] = m_new

    def past(j, carry):
        step(j, False)
        return carry

    lax.fori_loop(0, i, past, 0)
    step(i, True)
    o_ref[0] = jnp.where(_lane_half((tq, LANES)), _fox_finish(acc_sc[0], 0), _fox_finish(acc_sc[1], 1))


def _fox_prompt(q, k, v, cum, tq):
    b, t, _ = q.shape
    pairs = FOX_HEADS // 2
    return pl.pallas_call(
        functools.partial(_fox_prompt_body, tq=tq, t=t),
        grid=(b, pairs, t // tq),
        in_specs=[pl.BlockSpec((1, tq, LANES), lambda bi, hp, i: (bi, i, hp)),
                  pl.BlockSpec((1, t, LANES), lambda bi, hp, i: (bi, 0, 0)),
                  pl.BlockSpec((1, t, LANES), lambda bi, hp, i: (bi, 0, hp)),
                  pl.BlockSpec((1, t, LANES), lambda bi, hp, i: (bi, 0, hp))],
        out_specs=pl.BlockSpec((1, tq, LANES), lambda bi, hp, i: (bi, i, hp)),
        out_shape=jax.ShapeDtypeStruct((b, t, FOX_DIM), F32),
        scratch_shapes=[pltpu.VMEM((2, t, LANES), BF16), pltpu.VMEM((2, t, LANES), BF16),
                        pltpu.VMEM((2, tq, LANES), F32), pltpu.VMEM((2, tq, LANES), F32)],
        compiler_params=_cparams("parallel", "parallel", "arbitrary"),
        name="fox_prompt",
    )(q, cum, k, v)


def _fox_sample_body(q_ref, cum_ref, kp_ref, vp_ref, kn_ref, vn_ref, o_ref, *, ts, past):
    hp = pl.program_id(1)
    cum_p = cum_ref[0, 0:past, :]
    cum_n = cum_ref[0, past:past + ts, :]
    causal = (lax.broadcasted_iota(jnp.int32, (ts, ts), 1) <= lax.broadcasted_iota(jnp.int32, (ts, ts), 0))
    outs = []
    for h in range(2):
        head = 2 * hp + h
        q = _fox_q(q_ref[0], cum_n, head, h)
        s_p = _dot_nt(q, _fox_k(kp_ref[0], cum_p, head, h))
        s_n = jnp.where(causal, _dot_nt(q, _fox_k(kn_ref[0], cum_n, head, h)), NEG_BIG)
        m = jnp.maximum(jnp.max(s_p, axis=1, keepdims=True), jnp.max(s_n, axis=1, keepdims=True))
        acc = (_dot(jnp.exp2(s_p - m).astype(BF16), _fox_v(vp_ref[0], h))
               + _dot(jnp.exp2(s_n - m).astype(BF16), _fox_v(vn_ref[0], h)))
        outs.append(_fox_finish(acc, h))
    o_ref[0] = jnp.where(_lane_half((ts, LANES)), outs[0], outs[1])


def _fox_sample(q, k_new, v_new, k_past, v_past, cum):
    b, ts, _ = q.shape
    p = k_past.shape[1]
    length = cum.shape[1]
    pairs = FOX_HEADS // 2
    new = pl.BlockSpec((1, ts, LANES), lambda bi, hp: (bi, 0, hp))
    old = pl.BlockSpec((1, p, LANES), lambda bi, hp: (bi, 0, hp))
    return pl.pallas_call(
        functools.partial(_fox_sample_body, ts=ts, past=p),
        grid=(b, pairs),
        in_specs=[new, pl.BlockSpec((1, length, LANES), lambda bi, hp: (bi, 0, 0)), old, old, new, new],
        out_specs=new,
        out_shape=jax.ShapeDtypeStruct((b, ts, FOX_DIM), F32),
        compiler_params=_cparams("parallel", "parallel"),
        name="fox_sample",
    )(q, cum, k_past, v_past, k_new, v_new)


def _rwkv_prep_body(h_ref, prev_ref, mu_ref, w0_ref, w2_ref, a0_ref, a2_ref, g2_ref, kk_ref, ka_ref, rk_ref,
                    r_out, w_out, k_out, v_out, kap_out, bet_out, g_out, bon_out, *, tiles_per_seq):
    h = h_ref[...]
    tm = h.shape[0]
    prev_row = prev_ref[0, 7:8, :]
    if tiles_per_seq:
        keep = jnp.where(pl.program_id(0) % tiles_per_seq != 0, 1.0, 0.0)
        prev_row = prev_row * keep
    first = lax.broadcasted_iota(jnp.int32, (tm, 1), 0) == 0
    prev = jnp.where(first, prev_row, pltpu.roll(h, 1, axis=0))
    hx = h + (prev - h) * mu_ref[...]
    r = hx[:, 0:RWKV_DIM]
    k = hx[:, RWKV_DIM:2 * RWKV_DIM]
    v = hx[:, 2 * RWKV_DIM:3 * RWKV_DIM]
    xwa = hx[:, 3 * RWKV_DIM:3 * RWKV_DIM + LANES]
    xg = hx[:, 3 * RWKV_DIM + LANES:]
    w_logit = w0_ref[...] + _dot(jnp.tanh(xwa).astype(BF16), w2_ref[...])
    decay = jnp.exp(-jnp.exp(_log_sigmoid(w_logit) - 0.5))
    a = jax.nn.sigmoid(a0_ref[...] + _dot(xwa.astype(BF16), a2_ref[...]))
    g = _dot(jax.nn.sigmoid(xg).astype(BF16), g2_ref[...])
    ones = _group_ones(HEAD_DIM)
    kk = k * kk_ref[...]
    kk = kk / jnp.maximum(jnp.sqrt(_group_sum(kk * kk, ones)), 1e-12)
    k2 = k * (1.0 + (a - 1.0) * ka_ref[...])
    r_out[...] = r
    w_out[...] = decay
    k_out[...] = k2
    v_out[...] = v
    kap_out[...] = kk
    bet_out[...] = kk * a
    g_out[...] = g
    bon_out[...] = _group_sum(r * k2 * rk_ref[...], ones) * v


def _rwkv_prep(hr, prev8, prev_map, tm, tiles_per_seq, params):
    rows = hr.shape[0]
    out = jax.ShapeDtypeStruct((rows, RWKV_DIM), F32)
    return pl.pallas_call(
        functools.partial(_rwkv_prep_body, tiles_per_seq=tiles_per_seq),
        grid=(rows // tm,),
        in_specs=[_row_spec(tm, RWKV_COLS), pl.BlockSpec((1, 8, RWKV_COLS), prev_map)]
        + [_full_spec(p.shape) for p in params],
        out_specs=[_row_spec(tm, RWKV_DIM)] * 8,
        out_shape=[out] * 8,
        compiler_params=_cparams("parallel"),
        name="rwkv_prep",
    )(hr, prev8, *params)


def _to_chains_body(x_ref, o_ref, st_ref, *, nb, n_out, offsets):
    for b in range(nb):
        st_ref[b * RWKV_DIM:(b + 1) * RWKV_DIM, :] = x_ref[b].T
    nc = nb * RWKV_HEADS
    for j in range(n_out):
        parts = [st_ref[pl.ds(off, nc, stride=HEAD_DIM), :] for off in offsets(j)]
        o_ref[:, j * LANES:(j + 1) * LANES] = jnp.concatenate(parts, axis=0).T


def _to_chains_pallas(x, n_out, offsets):
    nb, t, _ = x.shape
    tt = LANES
    return pl.pallas_call(
        functools.partial(_to_chains_body, nb=nb, n_out=n_out, offsets=offsets),
        grid=(t // tt,),
        in_specs=[pl.BlockSpec((nb, tt, RWKV_DIM), lambda i: (0, i, 0))],
        out_specs=pl.BlockSpec((tt, n_out * LANES), lambda i: (i, 0)),
        out_shape=jax.ShapeDtypeStruct((t, n_out * LANES), F32),
        scratch_shapes=[pltpu.VMEM((nb * RWKV_DIM, tt), F32)],
        compiler_params=_cparams("parallel"),
        name="to_chains",
    )(x).reshape(t, n_out, LANES)


def _from_chains_body(o_ref, x_ref, st_ref, *, nb, nv, dup):
    nc = nb * RWKV_HEADS
    for vp in range(nv):
        tile = o_ref[:, vp * LANES:(vp + 1) * LANES].T
        for vh in range(dup):
            st_ref[pl.ds(vh * nv + vp, nc, stride=HEAD_DIM), :] = tile[vh * nc:(vh + 1) * nc, :]
    for b in range(nb):
        x_ref[b] = st_ref[b * RWKV_DIM:(b + 1) * RWKV_DIM, :].T


def _from_chains_pallas(o, nb):
    t, nv, _ = o.shape
    dup = HEAD_DIM // nv
    tt = LANES
    return pl.pallas_call(
        functools.partial(_from_chains_body, nb=nb, nv=nv, dup=dup),
        grid=(t // tt,),
        in_specs=[pl.BlockSpec((tt, nv * LANES), lambda i: (i, 0))],
        out_specs=pl.BlockSpec((nb, tt, RWKV_DIM), lambda i: (0, i, 0)),
        out_shape=jax.ShapeDtypeStruct((nb, t, RWKV_DIM), F32),
        scratch_shapes=[pltpu.VMEM((nb * RWKV_DIM, tt), F32)],
        compiler_params=_cparams("parallel"),
        name="from_chains",
    )(o.reshape(t, nv * LANES))


def _rwkv_scan_body(w_ref, kap_ref, bet_ref, k2_ref, r_ref, v_ref, s0_ref, o_ref, s_ref, *, tb, nv):
    @pl.when(pl.program_id(0) == 0)
    def _():
        s_ref[...] = s0_ref[...]

    def token(t, carry):
        w = w_ref[t]
        kap = kap_ref[t]
        bet = bet_ref[t]
        k2 = k2_ref[t]
        r = r_ref[t]
        for vp in range(nv):
            s = s_ref[vp]
            rho = jnp.sum(s * kap, axis=0, keepdims=True)
            sn = s * w - bet * rho + k2 * v_ref[t, vp:vp + 1, :]
            s_ref[vp] = sn
            o_ref[t, vp:vp + 1, :] = jnp.sum(sn * r, axis=0, keepdims=True)
        return carry

    lax.fori_loop(0, tb, token, 0)


def _rwkv_scan(w, kap, bet, k2, r, v, s0):
    t = w.shape[0]
    nv = v.shape[1]
    tb = min(SCAN_TOKENS, t)
    op_spec = pl.BlockSpec((tb, HEAD_DIM, LANES), lambda i: (i, 0, 0))
    v_spec = pl.BlockSpec((tb, nv, LANES), lambda i: (i, 0, 0))
    s_spec = pl.BlockSpec((nv, HEAD_DIM, LANES), lambda i: (0, 0, 0))
    return pl.pallas_call(
        functools.partial(_rwkv_scan_body, tb=tb, nv=nv),
        grid=(t // tb,),
        in_specs=[op_spec] * 5 + [v_spec, s_spec],
        out_specs=[v_spec, s_spec],
        out_shape=[jax.ShapeDtypeStruct((t, nv, LANES), F32), jax.ShapeDtypeStruct((nv, HEAD_DIM, LANES), F32)],
        compiler_params=_cparams("arbitrary"),
        name="rwkv_scan",
    )(w, kap, bet, k2, r, v, s0)


def _outproj_even_body(x_ref, a_ref, o_ref, bon_ref, g_ref, lng_ref, lnb_ref, w_ref, out_ref):
    ones = _group_ones(HEAD_DIM)
    o = o_ref[...]
    d = o - _group_sum(o, ones) * (1.0 / HEAD_DIM)
    var = _group_sum(d * d, ones) * (1.0 / HEAD_DIM)
    y = d * lax.rsqrt(var + RWKV_GN_EPS) * lng_ref[...] + lnb_ref[...]
    b = (y + bon_ref[...]) * g_ref[...]
    out_ref[...] = (x_ref[...] + _dot(a_ref[...].astype(BF16), w_ref[0:FOX_DIM, :])
                    + _dot(b.astype(BF16), w_ref[FOX_DIM:, :]))


def _outproj_even(x, a, o, bon, g, lng, lnb, w):
    rows = x.shape[0]
    tm = min(ROW_TILE, rows)
    return pl.pallas_call(
        _outproj_even_body,
        grid=(rows // tm,),
        in_specs=[_row_spec(tm, D_MODEL)] + [_row_spec(tm, FOX_DIM)] * 4
        + [_full_spec(lng.shape), _full_spec(lnb.shape), _full_spec(w.shape)],
        out_specs=_row_spec(tm, D_MODEL),
        out_shape=jax.ShapeDtypeStruct(x.shape, F32),
        compiler_params=_cparams("parallel"),
        name="outproj_even",
    )(x, a, o, bon, g, lng, lnb, w)


def _rope(x, cos, sin_up, sin_dn):
    return x * cos + pltpu.roll(x, ROPE_DIM // 2, axis=1) * sin_up + pltpu.roll(x, LANES - ROPE_DIM // 2, axis=1) * sin_dn


def _inproj_odd_body(x_ref, g_ref, w_ref, qn_ref, kn_ref, vn_ref, cos_ref, sup_ref, sdn_ref,
                     q_ref, k_ref, v_ref, u_ref, gv_ref):
    n = _rms(x_ref[...], g_ref[...]).astype(BF16)
    cos, sup, sdn = cos_ref[...], sup_ref[...], sdn_ref[...]
    ones128 = _group_ones(LANES)
    ones64 = _group_ones(HEAD_DIM)
    qw = SWA_HEADS * LANES
    for h in range(SWA_HEADS):
        hq = _dot(n, w_ref[:, h * LANES:(h + 1) * LANES])
        qh = hq * lax.rsqrt(_group_sum(hq * hq, ones128) * (1.0 / HEAD_DIM) + NORM_EPS) * qn_ref[:, h * LANES:(h + 1) * LANES]
        q_ref[:, h * LANES:(h + 1) * LANES] = (_rope(qh, cos, sup, sdn) * ATT_SCALE).astype(BF16)
    hk = _dot(n, w_ref[:, qw:qw + SWA_KV])
    kn = hk * lax.rsqrt(_group_sum(hk * hk, ones64) * (1.0 / HEAD_DIM) + NORM_EPS) * kn_ref[...]
    k_ref[...] = _rope(kn, cos, sup, sdn)
    v_ref[...] = _dot(n, w_ref[:, qw + SWA_KV:qw + 2 * SWA_KV])
    s0 = qw + 2 * SWA_KV
    u_ref[...] = _gelu_tanh(_dot(n, w_ref[:, s0:s0 + SGU_DIM]))
    gv = _gelu_tanh(_dot(n, w_ref[:, s0 + SGU_DIM:]))
    gv_ref[...] = _rms(gv, vn_ref[...])


def _inproj_odd(x, g, w, qn, kn, vn, cos, sup, sdn, table_blocks):
    rows = x.shape[0]
    tm = min(ROW_TILE, rows)
    tab = pl.BlockSpec((tm, LANES), lambda i: (i % table_blocks, 0))
    outs = [jax.ShapeDtypeStruct((rows, SWA_HEADS * LANES), BF16), jax.ShapeDtypeStruct((rows, SWA_KV), F32),
            jax.ShapeDtypeStruct((rows, SWA_KV), F32), jax.ShapeDtypeStruct((rows, SGU_DIM), F32),
            jax.ShapeDtypeStruct((rows, SGU_DIM), F32)]
    return pl.pallas_call(
        _inproj_odd_body,
        grid=(rows // tm,),
        in_specs=[_row_spec(tm, D_MODEL), _full_spec(g.shape), _full_spec(w.shape), _full_spec(qn.shape),
                  _full_spec(kn.shape), _full_spec(vn.shape), tab, tab, tab],
        out_specs=[_row_spec(tm, SWA_HEADS * LANES), _row_spec(tm, SWA_KV), _row_spec(tm, SWA_KV),
                   _row_spec(tm, SGU_DIM), _row_spec(tm, SGU_DIM)],
        out_shape=outs,
        compiler_params=_cparams("parallel"),
        name="inproj_odd",
    )(x, g, w, qn, kn, vn, cos, sup, sdn)


def _swa_body(sink_ref, q_ref, kp_ref, kc_ref, vp_ref, vc_ref, mask_ref, o_ref):
    kw = jnp.concatenate([kp_ref[0], kc_ref[0]], axis=0).astype(BF16)
    vw = jnp.concatenate([vp_ref[0], vc_ref[0]], axis=0).astype(BF16)
    visible = mask_ref[0] > 0.5
    for h in range(SWA_HEADS):
        s = jnp.where(visible, _dot_nt(q_ref[0, :, h * LANES:(h + 1) * LANES], kw), NEG_BIG)
        sink = sink_ref[h]
        m = jnp.maximum(jnp.max(s, axis=1, keepdims=True), sink)
        p = jnp.exp(s - m)
        l = jnp.sum(p, axis=1, keepdims=True) + jnp.exp(sink - m)
        o_ref[0, :, h * LANES:(h + 1) * LANES] = (_dot(p.astype(BF16), vw) / l).astype(BF16)


def _swa(sinks, q, k_prev, k_cur, v_prev, v_cur, mask, tq, prev_rows, prev_map, mask_map):
    b, t, qw = q.shape
    nk = prev_rows + tq
    cur = lambda bi, i: (bi, i, 0)
    return pl.pallas_call(
        _swa_body,
        grid=(b, t // tq),
        in_specs=[pl.BlockSpec(memory_space=pltpu.SMEM),
                  pl.BlockSpec((1, tq, qw), cur),
                  pl.BlockSpec((1, prev_rows, SWA_KV), prev_map), pl.BlockSpec((1, tq, SWA_KV), cur),
                  pl.BlockSpec((1, prev_rows, SWA_KV), prev_map), pl.BlockSpec((1, tq, SWA_KV), cur),
                  pl.BlockSpec((1, tq, nk), mask_map)],
        out_specs=pl.BlockSpec((1, tq, qw), cur),
        out_shape=jax.ShapeDtypeStruct((b, t, qw), BF16),
        compiler_params=_cparams("parallel", "parallel"),
        name="swa",
    )(sinks, q, k_prev, k_cur, v_prev, v_cur, mask)


def _sgu_body(u_ref, v_ref, w_ref, b_ref, o_ref, *, length, n_chunks):
    tril = (lax.broadcasted_iota(jnp.int32, (length, length), 1)
            <= lax.broadcasted_iota(jnp.int32, (length, length), 0))
    ws = [jnp.where(tril, w_ref[g], 0.0).astype(BF16) for g in range(SGU_GROUPS)]
    half = _lane_half((length, LANES))
    for c in range(n_chunks):
        rows = slice(c * length, (c + 1) * length)
        for j in range(SGU_GROUPS // 2):
            cols = slice(j * LANES, (j + 1) * LANES)
            vb = v_ref[rows, cols].astype(BF16)
            mixed = jnp.where(half, _dot(ws[2 * j], vb), _dot(ws[2 * j + 1], vb)) + b_ref[:, cols]
            o_ref[rows, cols] = u_ref[rows, cols] * mixed


def _sgu(u, v, w, bias, length, n_chunks):
    rows = u.shape[0]
    tm = length * n_chunks
    return pl.pallas_call(
        functools.partial(_sgu_body, length=length, n_chunks=n_chunks),
        grid=(rows // tm,),
        in_specs=[_row_spec(tm, SGU_DIM), _row_spec(tm, SGU_DIM), _full_spec(w.shape), _full_spec(bias.shape)],
        out_specs=_row_spec(tm, SGU_DIM),
        out_shape=jax.ShapeDtypeStruct(u.shape, F32),
        compiler_params=_cparams("parallel"),
        name="sgu",
    )(u, v, w, bias)


def _outproj_odd_body(x_ref, c_ref, d_ref, wc_ref, wd_ref, out_ref):
    out_ref[...] = x_ref[...] + _dot(c_ref[...], wc_ref[...]) + _dot(d_ref[...].astype(BF16), wd_ref[...])


def _outproj_odd(x, c, d, wc, wd):
    rows = x.shape[0]
    tm = min(ROW_TILE, rows)
    return pl.pallas_call(
        _outproj_odd_body,
        grid=(rows // tm,),
        in_specs=[_row_spec(tm, D_MODEL), _row_spec(tm, c.shape[1]), _row_spec(tm, SGU_DIM),
                  _full_spec(wc.shape), _full_spec(wd.shape)],
        out_specs=_row_spec(tm, D_MODEL),
        out_shape=jax.ShapeDtypeStruct(x.shape, F32),
        compiler_params=_cparams("parallel"),
        name="outproj_odd",
    )(x, c, d, wc, wd)


def _memkv_body(m_ref, g_ref, w_ref, kn_ref, k_ref, v_ref):
    n = _rms(m_ref[...], g_ref[...]).astype(BF16)
    ones = _group_ones(HEAD_DIM)
    hk = _dot(n, w_ref[:, 0:MEM_DIM])
    k_ref[...] = hk * lax.rsqrt(_group_sum(hk * hk, ones) * (1.0 / HEAD_DIM) + NORM_EPS) * kn_ref[...]
    v_ref[...] = _dot(n, w_ref[:, MEM_DIM:])


def _memkv(mem, g, w, kn):
    rows = mem.shape[0]
    tm = min(ROW_TILE, rows)
    out = jax.ShapeDtypeStruct((rows, MEM_DIM), F32)
    return pl.pallas_call(
        _memkv_body,
        grid=(rows // tm,),
        in_specs=[_row_spec(tm, D_MODEL), _full_spec(g.shape), _full_spec(w.shape), _full_spec(kn.shape)],
        out_specs=[_row_spec(tm, MEM_DIM)] * 2,
        out_shape=[out, out],
        compiler_params=_cparams("parallel"),
        name="memkv",
    )(mem, g, w, kn)


def _xattn_body(x_ref, g_ref, wq_ref, qn_ref, mk_ref, mv_ref, wo_ref, out_ref):
    x = x_ref[0]
    tq = x.shape[0]
    n = _rms(x, g_ref[...]).astype(BF16)
    hq = _dot(n, wq_ref[...])
    ones = _group_ones(HEAD_DIM)
    q = (hq * lax.rsqrt(_group_sum(hq * hq, ones) * (1.0 / HEAD_DIM) + NORM_EPS) * qn_ref[...] * ATT_SCALE).astype(BF16)
    half = _lane_half((tq, LANES))
    blocks = []
    for j in range(MEM_HEADS // 2):
        cols = slice(j * LANES, (j + 1) * LANES)
        kb = mk_ref[0, :, cols].astype(BF16)
        vb = mv_ref[0, :, cols].astype(BF16)
        qb = q[:, cols]
        outs = []
        for h in range(2):
            sel = half if h == 0 else jnp.logical_not(half)
            s = _dot_nt(jnp.where(sel, qb, jnp.zeros_like(qb)), kb)
            p = jnp.exp(s - jnp.max(s, axis=1, keepdims=True))
            outs.append(_dot(p.astype(BF16), vb) / jnp.sum(p, axis=1, keepdims=True))
        blocks.append(jnp.where(half, outs[0], outs[1]))
    o = jnp.concatenate(blocks, axis=1).astype(BF16)
    out_ref[0] = x + _dot(o, wo_ref[...])


def _xattn(x, g, wq, qn, mk, mv, wo, tq):
    b, t, _ = x.shape
    m = mk.shape[1]
    return pl.pallas_call(
        _xattn_body,
        grid=(b, t // tq),
        in_specs=[pl.BlockSpec((1, tq, D_MODEL), lambda bi, i: (bi, i, 0)), _full_spec(g.shape),
                  _full_spec(wq.shape), _full_spec(qn.shape),
                  pl.BlockSpec((1, m, MEM_DIM), lambda bi, i: (bi, 0, 0)),
                  pl.BlockSpec((1, m, MEM_DIM), lambda bi, i: (bi, 0, 0)), _full_spec(wo.shape)],
        out_specs=pl.BlockSpec((1, tq, D_MODEL), lambda bi, i: (bi, i, 0)),
        out_shape=jax.ShapeDtypeStruct(x.shape, F32),
        compiler_params=_cparams("parallel", "parallel"),
        name="xattn",
    )(x, g, wq, qn, mk, mv, wo)


def _to_chains(x, b, t, dup):
    y = x.reshape(b, t, RWKV_HEADS, HEAD_DIM).transpose(1, 3, 0, 2).reshape(t, HEAD_DIM, b * RWKV_HEADS)
    return jnp.concatenate([y] * dup, axis=-1) if dup > 1 else y


def _rope_tables(pos):
    half = ROPE_DIM // 2
    inv_freq = jnp.power(ROPE_THETA, -jnp.arange(half, dtype=F32) / half)
    ang = pos.astype(F32)[:, None] * inv_freq[None, :]
    cos, sin = jnp.cos(ang), jnp.sin(ang)
    n = pos.shape[0]
    pad = jnp.zeros((n, HEAD_DIM - ROPE_DIM), F32)
    zero = jnp.zeros((n, half), F32)
    cos_t = jnp.concatenate([cos, cos, pad + 1.0], axis=1)
    up_t = jnp.concatenate([zero, sin, pad], axis=1)
    dn_t = jnp.concatenate([-sin, zero, pad], axis=1)
    two = lambda a: jnp.concatenate([a, a], axis=1)
    return two(cos_t), two(up_t), two(dn_t)


def _swa_prompt_mask(tq):
    span = WINDOW + tq
    qc = np.arange(tq)[:, None] // CHUNK
    kc = np.arange(span)[None, :] // CHUNK - WINDOW_CHUNKS
    band = (kc <= qc) & (kc >= qc - WINDOW_CHUNKS)
    first = band & (kc >= 0)
    return jnp.asarray(np.stack([first, band]).astype(np.float32))


def _swa_sample_mask(past, rows, t):
    kc = (past - rows + np.arange(rows + t)) // CHUNK
    qc = (past + np.arange(t)) // CHUNK
    m = (kc[None, :] <= qc[:, None]) & (kc[None, :] >= qc[:, None] - WINDOW_CHUNKS)
    return jnp.asarray(m[None].astype(np.float32))


def kernel(x_prompt, x_sample, cache_fox_k, cache_fox_v, cache_fox_logf, state_rwkv, state_rwkv_shift, cache_swa_k, cache_swa_v, cache_mem_k, cache_mem_v, mem_prompt, ffn1_norm, ffn1_w_gate, ffn1_w_up, ffn1_w_down, mix_norm, ev_w_in, fox_b_f, fox_q_norm, fox_k_norm, rwkv_mu, rwkv_w0, rwkv_w2, rwkv_a0, rwkv_a2, rwkv_g2, rwkv_k_k, rwkv_k_a, rwkv_r_k, rwkv_ln_g, rwkv_ln_b, ev_w_out, od_w_in, swa_q_norm, swa_k_norm, swa_sinks, sgu_v_norm, sgu_w_s, sgu_b, od_w_out, xattn_norm, mem_norm, xattn_wq, xattn_wkv, xattn_q_norm, xattn_k_norm, xattn_wo, ffn2_norm, ffn2_w_gate, ffn2_w_up, ffn2_w_down):
    bp, tp, _ = x_prompt.shape
    bs, ts, _ = x_sample.shape
    depth = ffn1_norm.shape[0]
    past = cache_fox_k.shape[2]
    mem_tokens = mem_prompt.shape[1]
    xp = x_prompt.reshape(bp * tp, D_MODEL)
    xs = x_sample.reshape(bs * ts, D_MODEL)
    mem_rows = mem_prompt.reshape(bp * mem_tokens, D_MODEL)
    row = lambda a: a.reshape(1, -1)
    tile_heads = lambda a, n: jnp.tile(a, n).reshape(1, -1)

    out = {k: [] for k in ("p_fox_k", "p_fox_v", "p_fox_logf", "p_rwkv_state", "p_rwkv_shift", "p_swa_k", "p_swa_v",
                           "p_mem_k", "p_mem_v", "s_fox_k", "s_fox_v", "s_fox_logf", "s_rwkv_state", "s_rwkv_shift",
                           "s_swa_k", "s_swa_v", "s_sgu_v")}

    for l in range(depth):
        f1 = _prep_ffn(ffn1_norm[l], ffn1_w_gate[l], ffn1_w_up[l], ffn1_w_down[l])
        xp = _ffn(xp, *f1)
        xs = _ffn(xs, *f1)
        g_mix = row(mix_norm[l])
        if l % 2 == 0:
            e = l // 2
            w_in = ev_w_in[e]
            f0 = 3 * FOX_DIM
            w_cat = jnp.concatenate([w_in[:, :f0], jnp.pad(w_in[:, f0:f0 + FOX_HEADS], ((0, 0), (0, LANES - FOX_HEADS))),
                                     w_in[:, f0 + FOX_HEADS:]], axis=1).astype(BF16)
            bf = jnp.pad(fox_b_f[e], (0, LANES - FOX_HEADS)).reshape(1, LANES)
            qn = tile_heads(fox_q_norm[e], FOX_HEADS)
            kn = tile_heads(fox_k_norm[e], FOX_HEADS)
            qp, kp, vp, lfp, hrp = _inproj_even(xp, g_mix, w_cat, bf, qn, kn)
            qs, ks, vs, lfs, hrs = _inproj_even(xs, g_mix, w_cat, bf, qn, kn)

            cum_p = _cumsum(lfp.reshape(bp, tp, LANES), CUM_TILE)
            a_p = _fox_prompt(qp.reshape(bp, tp, FOX_DIM), kp.reshape(bp, tp, FOX_DIM), vp.reshape(bp, tp, FOX_DIM),
                              cum_p, ATT_TILE)
            lfs8 = lfs.reshape(bs, ts, LANES)[:, :, :FOX_HEADS]
            tot = past + ts
            padded = -(-tot // CUM_TILE) * CUM_TILE
            lf_all = jnp.concatenate([cache_fox_logf[e].astype(F32), lfs8], axis=1)
            lf_all = jnp.pad(lf_all, ((0, 0), (0, padded - tot), (0, LANES - FOX_HEADS)))
            cum_s = _cumsum(lf_all, CUM_TILE)
            a_s = _fox_sample(qs.reshape(bs, ts, FOX_DIM), ks.reshape(bs, ts, FOX_DIM), vs.reshape(bs, ts, FOX_DIM),
                              cache_fox_k[e].reshape(bs, past, FOX_DIM), cache_fox_v[e].reshape(bs, past, FOX_DIM),
                              cum_s)

            lora = 3 * RWKV_DIM
            w2p = jnp.pad(rwkv_w2[e], ((0, LANES - DECAY_LORA), (0, 0))).astype(BF16)
            a2p = jnp.pad(rwkv_a2[e], ((DECAY_LORA, 0), (0, 0))).astype(BF16)
            params = (row(rwkv_mu[e]), row(rwkv_w0[e]), w2p, row(rwkv_a0[e]), a2p, rwkv_g2[e].astype(BF16),
                      row(rwkv_k_k[e]), row(rwkv_k_a[e]), row(rwkv_r_k[e]))
            tiles_per_seq = tp // ROW_TILE
            prev_p = hrp.reshape(bp * tp // 8, 8, RWKV_COLS)
            prep_p = _rwkv_prep(hrp, prev_p, lambda i: (jnp.maximum(i * (ROW_TILE // 8) - 1, 0), 0, 0),
                                ROW_TILE, tiles_per_seq, params)
            prev_s = jnp.pad(state_rwkv_shift[e].astype(F32), ((0, 0), (7, 0), (0, 0)))
            prep_s = _rwkv_prep(hrs, prev_s, lambda i: (i, 0, 0), ts, 0, params)

            def scan(prep, b, t, state0):
                r, w, k2, v, kap, bet, g, bon = prep
                dup = LANES // (b * RWKV_HEADS)
                nv = HEAD_DIM // dup
                in_kernel = t % LANES == 0
                if in_kernel:
                    ops = [_to_chains_pallas(a.reshape(b, t, RWKV_DIM), HEAD_DIM, lambda k: (k,) * dup)
                           for a in (w, kap, bet, k2, r)]
                    vt = _to_chains_pallas(v.reshape(b, t, RWKV_DIM), nv,
                                           lambda vp: tuple(vh * nv + vp for vh in range(dup)))
                else:
                    ops = [_to_chains(a, b, t, dup) for a in (w, kap, bet, k2, r)]
                    vt = v.reshape(b, t, RWKV_HEADS, dup, nv).transpose(1, 4, 3, 0, 2).reshape(t, nv, LANES)
                s0 = state0.reshape(b, RWKV_HEADS, dup, nv, HEAD_DIM).transpose(3, 4, 2, 0, 1).reshape(nv, HEAD_DIM, LANES)
                o, sT = _rwkv_scan(*ops, vt, s0)
                if in_kernel:
                    o = _from_chains_pallas(o, b).reshape(b * t, RWKV_DIM)
                else:
                    o = o.reshape(t, nv, dup, b, RWKV_HEADS).transpose(3, 0, 4, 2, 1).reshape(b * t, RWKV_DIM)
                sT = sT.reshape(nv, HEAD_DIM, dup, b, RWKV_HEADS).transpose(3, 4, 2, 0, 1).reshape(b, RWKV_HEADS, HEAD_DIM, HEAD_DIM)
                return o, sT, g, bon

            o_p, st_p, g_p, bon_p = scan(prep_p, bp, tp, jnp.zeros((bp, RWKV_HEADS, HEAD_DIM, HEAD_DIM), F32))
            o_s, st_s, g_s, bon_s = scan(prep_s, bs, ts, state_rwkv[e].astype(F32))

            w_out = ev_w_out[e].astype(BF16)
            lng, lnb = row(rwkv_ln_g[e]), row(rwkv_ln_b[e])
            xp = _outproj_even(xp, a_p.reshape(bp * tp, FOX_DIM), o_p, bon_p, g_p, lng, lnb, w_out)
            xs = _outproj_even(xs, a_s.reshape(bs * ts, FOX_DIM), o_s, bon_s, g_s, lng, lnb, w_out)

            out["p_fox_k"].append(kp.reshape(bp, tp, FOX_HEADS, HEAD_DIM))
            out["p_fox_v"].append(vp.reshape(bp, tp, FOX_HEADS, HEAD_DIM))
            out["p_fox_logf"].append(lfp.reshape(bp, tp, LANES)[:, :, :FOX_HEADS])
            out["p_rwkv_state"].append(st_p)
            out["p_rwkv_shift"].append(hrp.reshape(bp, tp, RWKV_COLS)[:, -1:])
            out["s_fox_k"].append(ks.reshape(bs, ts, FOX_HEADS, HEAD_DIM))
            out["s_fox_v"].append(vs.reshape(bs, ts, FOX_HEADS, HEAD_DIM))
            out["s_fox_logf"].append(lfs8)
            out["s_rwkv_state"].append(st_s)
            out["s_rwkv_shift"].append(hrs.reshape(bs, ts, RWKV_COLS)[:, -1:])
        else:
            j = l // 2
            w_in = od_w_in[j]
            wq = w_in[:, :SWA_Q].reshape(D_MODEL, SWA_HEADS, HEAD_DIM)
            zeros = jnp.zeros_like(wq)
            kv_of = np.arange(SWA_HEADS) // SWA_GROUP
            lo = jnp.asarray(kv_of == 0)[None, :, None]
            wq_exp = jnp.concatenate([jnp.where(lo, wq, zeros), jnp.where(lo, zeros, wq)], axis=2)
            w_cat = jnp.concatenate([wq_exp.reshape(D_MODEL, SWA_HEADS * LANES), w_in[:, SWA_Q:]], axis=1).astype(BF16)
            qn = jnp.tile(swa_q_norm[j], 2 * SWA_HEADS).reshape(1, -1)
            kn = tile_heads(swa_k_norm[j], SWA_KV_HEADS)
            vn = row(sgu_v_norm[j])
            tabs_p = _rope_tables(jnp.arange(tp))
            tabs_s = _rope_tables(past + jnp.arange(bs * ts) % ts)
            qp, kp, vp, up, gp = _inproj_odd(xp, g_mix, w_cat, qn, kn, vn, *tabs_p, tp // ROW_TILE)
            qs, ks, vs, us, gs = _inproj_odd(xs, g_mix, w_cat, qn, kn, vn, *tabs_s, 1)

            qw = SWA_HEADS * LANES
            kp3, vp3 = kp.reshape(bp, tp, SWA_KV), vp.reshape(bp, tp, SWA_KV)
            ratio = SWA_TILE // WINDOW
            c_p = _swa(swa_sinks[j], qp.reshape(bp, tp, qw), kp3, kp3, vp3, vp3, _swa_prompt_mask(SWA_TILE),
                       SWA_TILE, WINDOW, lambda bi, i: (bi, jnp.maximum(i * ratio - 1, 0), 0),
                       lambda bi, i: (jnp.minimum(i, 1), 0, 0))
            rows_c = cache_swa_k.shape[2]
            ck3 = cache_swa_k[j].reshape(bs, rows_c, SWA_KV)
            cv3 = cache_swa_v[j].reshape(bs, rows_c, SWA_KV)
            ks3, vs3 = ks.reshape(bs, ts, SWA_KV), vs.reshape(bs, ts, SWA_KV)
            c_s = _swa(swa_sinks[j], qs.reshape(bs, ts, qw), ck3, ks3, cv3, vs3, _swa_sample_mask(past, rows_c, ts),
                       ts, rows_c, lambda bi, i: (bi, 0, 0), lambda bi, i: (0, 0, 0))

            bias = jnp.repeat(jnp.transpose(sgu_b[j]), HEAD_DIM, axis=1)
            d_p = _sgu(up, gp, sgu_w_s[j], bias, SGU_CHUNK, ROW_TILE // SGU_CHUNK)
            d_s = _sgu(us, gs, sgu_w_s[j][:, :ts, :ts], bias[:ts], ts, 1)

            w_out = od_w_out[j]
            wc = w_out[:SWA_Q].reshape(SWA_HEADS, HEAD_DIM, D_MODEL)
            zc = jnp.zeros_like(wc)
            lo_r = jnp.asarray(kv_of == 0)[:, None, None]
            wc_exp = jnp.concatenate([jnp.where(lo_r, wc, zc), jnp.where(lo_r, zc, wc)], axis=1)
            wc_exp = wc_exp.reshape(SWA_HEADS * LANES, D_MODEL).astype(BF16)
            wd = w_out[SWA_Q:].astype(BF16)
            xp = _outproj_odd(xp, c_p.reshape(bp * tp, qw), d_p, wc_exp, wd)
            xs = _outproj_odd(xs, c_s.reshape(bs * ts, qw), d_s, wc_exp, wd)

            out["p_swa_k"].append(kp3[:, -WINDOW:].reshape(bp, WINDOW, SWA_KV_HEADS, HEAD_DIM))
            out["p_swa_v"].append(vp3[:, -WINDOW:].reshape(bp, WINDOW, SWA_KV_HEADS, HEAD_DIM))
            out["s_swa_k"].append(jnp.concatenate([ck3, ks3], axis=1)[:, -rows_c:].reshape(bs, rows_c, SWA_KV_HEADS, HEAD_DIM))
            out["s_swa_v"].append(jnp.concatenate([cv3, vs3], axis=1)[:, -rows_c:].reshape(bs, rows_c, SWA_KV_HEADS, HEAD_DIM))
            out["s_sgu_v"].append(gs.reshape(bs, ts, SGU_DIM))

        mk, mv = _memkv(mem_rows, row(mem_norm[l]), xattn_wkv[l].astype(BF16), tile_heads(xattn_k_norm[l], MEM_HEADS))
        mk3, mv3 = mk.reshape(bp, mem_tokens, MEM_DIM), mv.reshape(bp, mem_tokens, MEM_DIM)
        xa = (row(xattn_norm[l]), xattn_wq[l].astype(BF16), tile_heads(xattn_q_norm[l], MEM_HEADS))
        wo = xattn_wo[l].astype(BF16)
        xp = _xattn(xp.reshape(bp, tp, D_MODEL), *xa, mk3, mv3, wo, ROW_TILE).reshape(bp * tp, D_MODEL)
        xs = _xattn(xs.reshape(bs, ts, D_MODEL), *xa, cache_mem_k[l].reshape(bs, mem_tokens, MEM_DIM),
                    cache_mem_v[l].reshape(bs, mem_tokens, MEM_DIM), wo, ts).reshape(bs * ts, D_MODEL)
        out["p_mem_k"].append(mk3.reshape(bp, mem_tokens, MEM_HEADS, HEAD_DIM))
        out["p_mem_v"].append(mv3.reshape(bp, mem_tokens, MEM_HEADS, HEAD_DIM))

        f2 = _prep_ffn(ffn2_norm[l], ffn2_w_gate[l], ffn2_w_up[l], ffn2_w_down[l])
        xp = _ffn(xp, *f2)
        xs = _ffn(xs, *f2)

    order = ("p_fox_k", "p_fox_v", "p_fox_logf", "p_rwkv_state", "p_rwkv_shift", "p_swa_k", "p_swa_v", "p_mem_k",
             "p_mem_v", "s_fox_k", "s_fox_v", "s_fox_logf", "s_rwkv_state", "s_rwkv_shift", "s_swa_k", "s_swa_v",
             "s_sgu_v")
    return (xp.reshape(bp, tp, D_MODEL), xs.reshape(bs, ts, D_MODEL)) + tuple(jnp.stack(out[k]) for k in order)
```

```python
import functools

import numpy as np
import jax
import jax.numpy as jnp
from jax import lax
from jax.experimental import pallas as pl
from jax.experimental.pallas import tpu as pltpu

F32 = jnp.float32
BF16 = jnp.bfloat16

D_MODEL = 1024
HEAD_DIM = 64
NORM_EPS = 1e-6
ROPE_THETA = 500000.0
ROPE_DIM = HEAD_DIM // 4
CHUNK = 64
FOX_HEADS = 8
FOX_DIM = FOX_HEADS * HEAD_DIM
RWKV_HEADS = 8
RWKV_DIM = RWKV_HEADS * HEAD_DIM
DECAY_LORA = 64
ICLR_LORA = 64
GATE_LORA = 128
RWKV_COLS = 3 * RWKV_DIM + DECAY_LORA + ICLR_LORA + GATE_LORA
RWKV_GN_EPS = 64e-5
SWA_HEADS = 8
SWA_KV_HEADS = 2
SWA_GROUP = SWA_HEADS // SWA_KV_HEADS
SWA_Q = SWA_HEADS * HEAD_DIM
SWA_KV = SWA_KV_HEADS * HEAD_DIM
WINDOW = 128
WINDOW_CHUNKS = WINDOW // CHUNK
SGU_GROUPS = 8
SGU_DIM = SGU_GROUPS * HEAD_DIM
SGU_CHUNK = 128
MEM_HEADS = 4
MEM_DIM = MEM_HEADS * HEAD_DIM
D_FF = 2816

LANES = 128
ROW_TILE = 512
FF_TILE = 256
ATT_TILE = 512
CUM_TILE = 256
SWA_TILE = 256
SCAN_TOKENS = 64
VMEM_LIMIT = 56 * 1024 * 1024
RWKV_HEAD_PERM = np.arange(RWKV_DIM).reshape(RWKV_HEADS, HEAD_DIM).T.reshape(-1)
RWKV_COL_PERM = np.concatenate([RWKV_HEAD_PERM, RWKV_DIM + RWKV_HEAD_PERM, 2 * RWKV_DIM + RWKV_HEAD_PERM,
                                np.arange(3 * RWKV_DIM, RWKV_COLS)])
RWKV_COL_UNPERM = np.argsort(RWKV_COL_PERM)
ATT_SCALE = HEAD_DIM ** -0.5
LOG2E = 1.4426950408889634
NEG_BIG = -1e30


def _cparams(*sem):
    return pltpu.CompilerParams(dimension_semantics=sem, vmem_limit_bytes=VMEM_LIMIT)


def _dot(a, b):
    return jnp.dot(a, b, preferred_element_type=F32)


def _dot_nt(a, b):
    return lax.dot_general(a, b, (((1,), (1,)), ((), ())), preferred_element_type=F32)


def _rms(x, g):
    ms = jnp.mean(x * x, axis=-1, keepdims=True)
    return (x * lax.rsqrt(ms + NORM_EPS)) * g


def _group_ones(group):
    shift = int(np.log2(group))
    r = lax.broadcasted_iota(jnp.int32, (LANES, LANES), 0) >> shift
    c = lax.broadcasted_iota(jnp.int32, (LANES, LANES), 1) >> shift
    return jnp.where(r == c, 1.0, 0.0).astype(BF16)


def _group_sum(x, ones):
    parts = []
    for j in range(x.shape[1] // LANES):
        blk = x[:, j * LANES:(j + 1) * LANES]
        hi = blk.astype(BF16)
        lo = (blk - hi.astype(F32)).astype(BF16)
        parts.append(_dot(hi, ones) + _dot(lo, ones))
    return parts[0] if len(parts) == 1 else jnp.concatenate(parts, axis=1)


def _head_ones():
    r = lax.broadcasted_iota(jnp.int32, (LANES, LANES), 0) & (RWKV_HEADS - 1)
    c = lax.broadcasted_iota(jnp.int32, (LANES, LANES), 1) & (RWKV_HEADS - 1)
    return jnp.where(r == c, 1.0, 0.0).astype(BF16)


def _head_sum(x, ones):
    part = x[:, 0:LANES]
    for j in range(1, x.shape[1] // LANES):
        part = part + x[:, j * LANES:(j + 1) * LANES]
    hi = part.astype(BF16)
    lo = (part - hi.astype(F32)).astype(BF16)
    tot = _dot(hi, ones) + _dot(lo, ones)
    return jnp.concatenate([tot] * (x.shape[1] // LANES), axis=1)


def _log_sigmoid(z):
    return jnp.minimum(z, 0.0) - jnp.log(1.0 + jnp.exp(-jnp.abs(z)))


def _gelu_tanh(x):
    return 0.5 * x * (1.0 + jnp.tanh(0.7978845608028654 * (x + 0.044715 * (x * x * x))))


def _lane_half(shape):
    return lax.broadcasted_iota(jnp.int32, shape, 1) < HEAD_DIM


def _row_spec(tm, cols):
    return pl.BlockSpec((tm, cols), lambda i: (i, 0))


def _full_spec(shape):
    nd = len(shape)
    return pl.BlockSpec(shape, lambda *_: (0,) * nd, pipeline_mode=pl.Buffered(1))


def _ffn_body(x_ref, g_ref, wg_ref, wu_ref, wd_ref, o_ref, acc_ref, *, n_chunks):
    x = x_ref[...]
    n = _rms(x, g_ref[...]).astype(BF16)
    for c in range(n_chunks):
        cols = slice(c * FF_TILE, (c + 1) * FF_TILE)
        gate = _dot(n, wg_ref[:, cols])
        up = _dot(n, wu_ref[:, cols])
        act = (gate * jax.nn.sigmoid(gate) * up).astype(BF16)
        part = _dot(act, wd_ref[cols, :])
        if c == 0:
            acc_ref[...] = part
        else:
            acc_ref[...] += part
    o_ref[...] = x + 0.5 * acc_ref[...]


def _ffn(x, g, wg, wu, wd):
    rows = x.shape[0]
    tm = min(ROW_TILE, rows)
    return pl.pallas_call(
        functools.partial(_ffn_body, n_chunks=wg.shape[1] // FF_TILE),
        grid=(rows // tm,),
        in_specs=[_row_spec(tm, D_MODEL), _full_spec(g.shape), _full_spec(wg.shape),
                  _full_spec(wu.shape), _full_spec(wd.shape)],
        out_specs=_row_spec(tm, D_MODEL),
        out_shape=jax.ShapeDtypeStruct(x.shape, F32),
        scratch_shapes=[pltpu.VMEM((tm, D_MODEL), F32)],
        compiler_params=_cparams("parallel"),
        name="ffn",
    )(x, g, wg, wu, wd)


def _prep_ffn(norm, w_gate, w_up, w_down):
    return norm.reshape(1, D_MODEL), w_gate.astype(BF16), w_up.astype(BF16), w_down.astype(BF16)


def _inproj_even_body(x_ref, g_ref, w_ref, bf_ref, qn_ref, kn_ref,
                      q_ref, k_ref, v_ref, lf_ref, hr_ref):
    n = _rms(x_ref[...], g_ref[...]).astype(BF16)
    ones = _group_ones(HEAD_DIM)
    hq = _dot(n, w_ref[:, 0:FOX_DIM])
    q = hq * lax.rsqrt(_group_sum(hq * hq, ones) * (1.0 / HEAD_DIM) + NORM_EPS) * qn_ref[...]
    q_ref[...] = (q * (ATT_SCALE * LOG2E)).astype(BF16)
    hk = _dot(n, w_ref[:, FOX_DIM:2 * FOX_DIM])
    k_ref[...] = hk * lax.rsqrt(_group_sum(hk * hk, ones) * (1.0 / HEAD_DIM) + NORM_EPS) * kn_ref[...]
    v_ref[...] = _dot(n, w_ref[:, 2 * FOX_DIM:3 * FOX_DIM])
    f0 = 3 * FOX_DIM
    lf_ref[...] = _log_sigmoid(_dot(n, w_ref[:, f0:f0 + LANES]) + bf_ref[...])
    hr_ref[...] = _dot(n, w_ref[:, f0 + LANES:])


def _inproj_even(x, g, w, bf, qn, kn):
    rows = x.shape[0]
    tm = min(ROW_TILE, rows)
    outs = [jax.ShapeDtypeStruct((rows, FOX_DIM), BF16), jax.ShapeDtypeStruct((rows, FOX_DIM), F32),
            jax.ShapeDtypeStruct((rows, FOX_DIM), F32), jax.ShapeDtypeStruct((rows, LANES), F32),
            jax.ShapeDtypeStruct((rows, RWKV_COLS), F32)]
    return pl.pallas_call(
        _inproj_even_body,
        grid=(rows // tm,),
        in_specs=[_row_spec(tm, D_MODEL), _full_spec(g.shape), _full_spec(w.shape), _full_spec(bf.shape),
                  _full_spec(qn.shape), _full_spec(kn.shape)],
        out_specs=[_row_spec(tm, FOX_DIM), _row_spec(tm, FOX_DIM), _row_spec(tm, FOX_DIM),
                   _row_spec(tm, LANES), _row_spec(tm, RWKV_COLS)],
        out_shape=outs,
        compiler_params=_cparams("parallel"),
        name="inproj_even",
    )(x, g, w, bf, qn, kn)


def _split3(x):
    hi = x.astype(BF16)
    r1 = x - hi.astype(F32)
    mid = r1.astype(BF16)
    lo = (r1 - mid.astype(F32)).astype(BF16)
    return hi, mid, lo


def _cumsum_body(lf_ref, col_ref, *, n_chunks, tk):
    r = lax.broadcasted_iota(jnp.int32, (tk, tk), 0)
    c = lax.broadcasted_iota(jnp.int32, (tk, tk), 1)
    tri = jnp.where(c <= r, 1.0, 0.0).astype(BF16)
    carry = jnp.zeros((1, LANES), F32)
    for i in range(n_chunks):
        hi, mid, lo = _split3(lf_ref[0, i * tk:(i + 1) * tk, :])
        cs = _dot(tri, hi) + _dot(tri, mid) + _dot(tri, lo) + carry
        col_ref[0, i * tk:(i + 1) * tk, :] = cs
        carry = cs[tk - 1:tk, :]


def _cumsum(lf, tk):
    b, length, _ = lf.shape
    n_chunks = length // tk
    return pl.pallas_call(
        functools.partial(_cumsum_body, n_chunks=n_chunks, tk=tk),
        grid=(b,),
        in_specs=[pl.BlockSpec((1, length, LANES), lambda i: (i, 0, 0))],
        out_specs=pl.BlockSpec((1, length, LANES), lambda i: (i, 0, 0)),
        out_shape=jax.ShapeDtypeStruct((b, length, LANES), F32),
        compiler_params=_cparams("parallel"),
        name="cumsum",
    )(lf)


def _aug_base(h):
    return (1 - h) * HEAD_DIM


def _place3(terms, src_lane, dst_lane):
    r = lax.broadcasted_iota(jnp.int32, (LANES, LANES), 0)
    c = lax.broadcasted_iota(jnp.int32, (LANES, LANES), 1)
    out = None
    for i, t in enumerate(terms):
        sel = jnp.where((r == src_lane) & (c == dst_lane + i), 1.0, 0.0).astype(BF16)
        out = _dot(t, sel) if out is None else out + _dot(t, sel)
    return out


def _lane_ones(n, lo, hi):
    lane = lax.broadcasted_iota(jnp.int32, (n, LANES), 1)
    return jnp.where((lane >= lo) & (lane < hi), 1.0, 0.0)


def _data_mask(n, h):
    half = _lane_half((n, LANES))
    return half if h == 0 else jnp.logical_not(half)


def _fox_q(q, cum, head, h):
    n = q.shape[0]
    a0 = _aug_base(h)
    aug = _place3(_split3(cum * LOG2E), head, a0) + _lane_ones(n, a0 + 3, a0 + 6)
    return jnp.where(_data_mask(n, h), q.astype(F32), aug).astype(BF16)


def _fox_k(k, cum, head, h):
    n = k.shape[0]
    a0 = _aug_base(h)
    aug = _lane_ones(n, a0, a0 + 3) - _place3(_split3(cum * LOG2E), head, a0 + 3)
    return jnp.where(_data_mask(n, h), k, aug).astype(BF16)


def _fox_v(v, h):
    n = v.shape[0]
    a0 = _aug_base(h)
    return jnp.where(_data_mask(n, h), v, _lane_ones(n, a0, a0 + 1)).astype(BF16)


def _fox_finish(acc, h):
    a0 = _aug_base(h)
    return acc / acc[:, a0:a0 + 1]


def _fox_prompt_body(q_ref, cum_ref, k_ref, v_ref, o_ref, ka_sc, va_sc, m_sc, acc_sc, *, tq, t):
    hp = pl.program_id(1)
    i = pl.program_id(2)

    @pl.when(i == 0)
    def _():
        for h in range(2):
            for c in range(t // ROW_TILE):
                rows = slice(c * ROW_TILE, (c + 1) * ROW_TILE)
                ka_sc[h, rows, :] = _fox_k(k_ref[0, rows, :], cum_ref[0, rows, :], 2 * hp + h, h)
                va_sc[h, rows, :] = _fox_v(v_ref[0, rows, :], h)

    off_q = pl.multiple_of(i * tq, tq)
    cq = cum_ref[0, pl.ds(off_q, tq), :]
    qs = [_fox_q(q_ref[0], cq, 2 * hp + h, h) for h in range(2)]
    m_sc[...] = jnp.full(m_sc.shape, NEG_BIG, F32)
    acc_sc[...] = jnp.zeros(acc_sc.shape, F32)
    causal = (lax.broadcasted_iota(jnp.int32, (tq, tq), 1) <= lax.broadcasted_iota(jnp.int32, (tq, tq), 0))

    def step(j, masked):
        off = pl.multiple_of(j * tq, tq)
        for h in range(2):
            s = _dot_nt(qs[h], ka_sc[h, pl.ds(off, tq), :])
            if masked:
                s = jnp.where(causal, s, NEG_BIG)
            m_old = m_sc[h]
            m_new = jnp.maximum(m_old, jnp.max(s, axis=1, keepdims=True))
            p = jnp.concatenate([jnp.exp2(s[:, c * LANES:(c + 1) * LANES] - m_new) for c in range(tq // LANES)],
                                axis=1).astype(BF16)
            acc_sc[h] = jnp.exp2(m_old - m_new) * acc_sc[h] + _dot(p, va_sc[h, pl.ds(off, tq), :])
            m_sc[h] = m_new

    def past(j, carry):
        step(j, False)
        return carry

    lax.fori_loop(0, i, past, 0)
    step(i, True)
    o_ref[0] = jnp.where(_lane_half((tq, LANES)), _fox_finish(acc_sc[0], 0), _fox_finish(acc_sc[1], 1))


def _fox_prompt(q, k, v, cum, tq):
    b, t, _ = q.shape
    pairs = FOX_HEADS // 2
    return pl.pallas_call(
        functools.partial(_fox_prompt_body, tq=tq, t=t),
        grid=(b, pairs, t // tq),
        in_specs=[pl.BlockSpec((1, tq, LANES), lambda bi, hp, i: (bi, i, hp)),
                  pl.BlockSpec((1, t, LANES), lambda bi, hp, i: (bi, 0, 0)),
                  pl.BlockSpec((1, t, LANES), lambda bi, hp, i: (bi, 0, hp)),
                  pl.BlockSpec((1, t, LANES), lambda bi, hp, i: (bi, 0, hp))],
        out_specs=pl.BlockSpec((1, tq, LANES), lambda bi, hp, i: (bi, i, hp)),
        out_shape=jax.ShapeDtypeStruct((b, t, FOX_DIM), F32),
        scratch_shapes=[pltpu.VMEM((2, t, LANES), BF16), pltpu.VMEM((2, t, LANES), BF16),
                        pltpu.VMEM((2, tq, LANES), F32), pltpu.VMEM((2, tq, LANES), F32)],
        compiler_params=_cparams("parallel", "parallel", "arbitrary"),
        name="fox_prompt",
    )(q, cum, k, v)


def _fox_sample_body(q_ref, cum_ref, kp_ref, vp_ref, kn_ref, vn_ref, o_ref, *, ts, past):
    hp = pl.program_id(1)
    cum_p = cum_ref[0, 0:past, :] * LOG2E
    cum_n = cum_ref[0, past:past + ts, :] * LOG2E
    causal = (lax.broadcasted_iota(jnp.int32, (ts, ts), 1) <= lax.broadcasted_iota(jnp.int32, (ts, ts), 0))
    lane = lax.broadcasted_iota(jnp.int32, (ts, LANES), 1)
    picks = [lane == 2 * hp + h for h in range(2)]
    onehot = jnp.concatenate([jnp.where(p, 1.0, 0.0) for p in picks], axis=0).astype(BF16)
    ck_p = sum(_dot_nt(onehot, t) for t in _split3(cum_p))
    ck_n = sum(_dot_nt(onehot, t) for t in _split3(cum_n))
    q = q_ref[0]
    kp = kp_ref[0].astype(BF16)
    kn = kn_ref[0].astype(BF16)
    outs = []
    for h in range(2):
        rows = slice(h * ts, (h + 1) * ts)
        cq = jnp.sum(jnp.where(picks[h], cum_n, 0.0), axis=1, keepdims=True)
        qh = jnp.where(_data_mask(ts, h), q, jnp.zeros_like(q))
        s_p = _dot_nt(qh, kp) + (cq - ck_p[rows])
        s_n = jnp.where(causal, _dot_nt(qh, kn) + (cq - ck_n[rows]), NEG_BIG)
        m = jnp.maximum(jnp.max(s_p, axis=1, keepdims=True), jnp.max(s_n, axis=1, keepdims=True))
        acc = (_dot(jnp.exp2(s_p - m).astype(BF16), _fox_v(vp_ref[0], h))
               + _dot(jnp.exp2(s_n - m).astype(BF16), _fox_v(vn_ref[0], h)))
        outs.append(_fox_finish(acc, h))
    o_ref[0] = jnp.where(_lane_half((ts, LANES)), outs[0], outs[1])


def _fox_sample(q, k_new, v_new, k_past, v_past, cum):
    b, ts, _ = q.shape
    p = k_past.shape[1]
    length = cum.shape[1]
    pairs = FOX_HEADS // 2
    new = pl.BlockSpec((1, ts, LANES), lambda bi, hp: (bi, 0, hp))
    old = pl.BlockSpec((1, p, LANES), lambda bi, hp: (bi, 0, hp))
    return pl.pallas_call(
        functools.partial(_fox_sample_body, ts=ts, past=p),
        grid=(b, pairs),
        in_specs=[new, pl.BlockSpec((1, length, LANES), lambda bi, hp: (bi, 0, 0)), old, old, new, new],
        out_specs=new,
        out_shape=jax.ShapeDtypeStruct((b, ts, FOX_DIM), F32),
        compiler_params=_cparams("parallel", "parallel"),
        name="fox_sample",
    )(q, cum, k_past, v_past, k_new, v_new)


def _rwkv_prep_body(h_ref, prev_ref, mu_ref, w0_ref, w2_ref, a0_ref, a2_ref, g2_ref, kk_ref, ka_ref, rk_ref,
                    r_out, w_out, k_out, v_out, kap_out, bet_out, g_out, bon_out, *, tiles_per_seq):
    h = h_ref[...]
    tm = h.shape[0]
    prev_row = prev_ref[0, 7:8, :]
    if tiles_per_seq:
        keep = jnp.where(pl.program_id(0) % tiles_per_seq != 0, 1.0, 0.0)
        prev_row = prev_row * keep
    first = lax.broadcasted_iota(jnp.int32, (tm, 1), 0) == 0
    prev = jnp.where(first, prev_row, pltpu.roll(h, 1, axis=0))
    hx = h + (prev - h) * mu_ref[...]
    r = hx[:, 0:RWKV_DIM]
    k = hx[:, RWKV_DIM:2 * RWKV_DIM]
    v = hx[:, 2 * RWKV_DIM:3 * RWKV_DIM]
    xwa = hx[:, 3 * RWKV_DIM:3 * RWKV_DIM + LANES]
    xg = hx[:, 3 * RWKV_DIM + LANES:]
    w_logit = w0_ref[...] + _dot(jnp.tanh(xwa).astype(BF16), w2_ref[...])
    decay = jnp.exp(-jnp.exp(_log_sigmoid(w_logit) - 0.5))
    a = jax.nn.sigmoid(a0_ref[...] + _dot(xwa.astype(BF16), a2_ref[...]))
    g = _dot(jax.nn.sigmoid(xg).astype(BF16), g2_ref[...])
    ones = _head_ones()
    kk = k * kk_ref[...]
    kk = kk / jnp.maximum(jnp.sqrt(_head_sum(kk * kk, ones)), 1e-12)
    k2 = k * (1.0 + (a - 1.0) * ka_ref[...])
    r_out[...] = r
    w_out[...] = decay
    k_out[...] = k2
    v_out[...] = v
    kap_out[...] = kk
    bet_out[...] = kk * a
    g_out[...] = g
    bon_out[...] = _head_sum(r * k2 * rk_ref[...], ones) * v


def _rwkv_prep(hr, prev8, prev_map, tm, tiles_per_seq, params):
    rows = hr.shape[0]
    out = jax.ShapeDtypeStruct((rows, RWKV_DIM), F32)
    return pl.pallas_call(
        functools.partial(_rwkv_prep_body, tiles_per_seq=tiles_per_seq),
        grid=(rows // tm,),
        in_specs=[_row_spec(tm, RWKV_COLS), pl.BlockSpec((1, 8, RWKV_COLS), prev_map)]
        + [_full_spec(p.shape) for p in params],
        out_specs=[_row_spec(tm, RWKV_DIM)] * 8,
        out_shape=[out] * 8,
        compiler_params=_cparams("parallel"),
        name="rwkv_prep",
    )(hr, prev8, *params)


def _to_chains_body(x_ref, o_ref, st_ref, *, nb, n_out, offsets):
    for b in range(nb):
        st_ref[b] = x_ref[b].T
    for j in range(n_out):
        groups = [st_ref[b, off * RWKV_HEADS:(off + 1) * RWKV_HEADS, :] for off in offsets(j) for b in range(nb)]
        o_ref[:, j * LANES:(j + 1) * LANES] = jnp.concatenate(groups, axis=0).T


def _to_chains_pallas(x, n_out, offsets):
    nb, t, _ = x.shape
    tt = LANES
    return pl.pallas_call(
        functools.partial(_to_chains_body, nb=nb, n_out=n_out, offsets=offsets),
        grid=(t // tt,),
        in_specs=[pl.BlockSpec((nb, tt, RWKV_DIM), lambda i: (0, i, 0))],
        out_specs=pl.BlockSpec((tt, n_out * LANES), lambda i: (i, 0)),
        out_shape=jax.ShapeDtypeStruct((t, n_out * LANES), F32),
        scratch_shapes=[pltpu.VMEM((nb, RWKV_DIM, tt), F32)],
        compiler_params=_cparams("parallel"),
        name="to_chains",
    )(x).reshape(t, n_out, LANES)


def _from_chains_body(o_ref, x_ref, st_ref, *, nb, nv, dup):
    nh = RWKV_HEADS
    for vp in range(nv):
        tile = o_ref[:, vp * LANES:(vp + 1) * LANES].T
        for vh in range(dup):
            ch = vh * nv + vp
            for b in range(nb):
                r0 = (vh * nb + b) * nh
                st_ref[b, ch * nh:(ch + 1) * nh, :] = tile[r0:r0 + nh, :]
    for b in range(nb):
        x_ref[b] = st_ref[b].T


def _from_chains_pallas(o, nb):
    t, nv, _ = o.shape
    dup = HEAD_DIM // nv
    tt = LANES
    return pl.pallas_call(
        functools.partial(_from_chains_body, nb=nb, nv=nv, dup=dup),
        grid=(t // tt,),
        in_specs=[pl.BlockSpec((tt, nv * LANES), lambda i: (i, 0))],
        out_specs=pl.BlockSpec((nb, tt, RWKV_DIM), lambda i: (0, i, 0)),
        out_shape=jax.ShapeDtypeStruct((nb, t, RWKV_DIM), F32),
        scratch_shapes=[pltpu.VMEM((nb, RWKV_DIM, tt), F32)],
        compiler_params=_cparams("parallel"),
        name="from_chains",
    )(o.reshape(t, nv * LANES))


def _rwkv_scan_body(w_ref, kap_ref, bet_ref, k2_ref, r_ref, v_ref, s0_ref, o_ref, s_ref, *, tb, nv):
    @pl.when(pl.program_id(0) == 0)
    def _():
        s_ref[...] = s0_ref[...]

    def token(t, carry):
        w = w_ref[t]
        kap = kap_ref[t]
        bet = bet_ref[t]
        k2 = k2_ref[t]
        r = r_ref[t]
        for vp in range(nv):
            s = s_ref[vp]
            rho = jnp.sum(s * kap, axis=0, keepdims=True)
            sn = s * w - bet * rho + k2 * v_ref[t, vp:vp + 1, :]
            s_ref[vp] = sn
            o_ref[t, vp:vp + 1, :] = jnp.sum(sn * r, axis=0, keepdims=True)
        return carry

    lax.fori_loop(0, tb, token, 0)


def _rwkv_scan(w, kap, bet, k2, r, v, s0):
    t = w.shape[0]
    nv = v.shape[1]
    tb = min(SCAN_TOKENS, t)
    op_spec = pl.BlockSpec((tb, HEAD_DIM, LANES), lambda i: (i, 0, 0))
    v_spec = pl.BlockSpec((tb, nv, LANES), lambda i: (i, 0, 0))
    s_spec = pl.BlockSpec((nv, HEAD_DIM, LANES), lambda i: (0, 0, 0))
    return pl.pallas_call(
        functools.partial(_rwkv_scan_body, tb=tb, nv=nv),
        grid=(t // tb,),
        in_specs=[op_spec] * 5 + [v_spec, s_spec],
        out_specs=[v_spec, s_spec],
        out_shape=[jax.ShapeDtypeStruct((t, nv, LANES), F32), jax.ShapeDtypeStruct((nv, HEAD_DIM, LANES), F32)],
        compiler_params=_cparams("arbitrary"),
        name="rwkv_scan",
    )(w, kap, bet, k2, r, v, s0)


def _outproj_even_body(x_ref, a_ref, o_ref, bon_ref, g_ref, lng_ref, lnb_ref, w_ref, out_ref):
    ones = _head_ones()
    o = o_ref[...]
    d = o - _head_sum(o, ones) * (1.0 / HEAD_DIM)
    var = _head_sum(d * d, ones) * (1.0 / HEAD_DIM)
    y = d * lax.rsqrt(var + RWKV_GN_EPS) * lng_ref[...] + lnb_ref[...]
    b = (y + bon_ref[...]) * g_ref[...]
    out_ref[...] = (x_ref[...] + _dot(a_ref[...].astype(BF16), w_ref[0:FOX_DIM, :])
                    + _dot(b.astype(BF16), w_ref[FOX_DIM:, :]))


def _outproj_even(x, a, o, bon, g, lng, lnb, w):
    rows = x.shape[0]
    tm = min(ROW_TILE, rows)
    return pl.pallas_call(
        _outproj_even_body,
        grid=(rows // tm,),
        in_specs=[_row_spec(tm, D_MODEL)] + [_row_spec(tm, FOX_DIM)] * 4
        + [_full_spec(lng.shape), _full_spec(lnb.shape), _full_spec(w.shape)],
        out_specs=_row_spec(tm, D_MODEL),
        out_shape=jax.ShapeDtypeStruct(x.shape, F32),
        compiler_params=_cparams("parallel"),
        name="outproj_even",
    )(x, a, o, bon, g, lng, lnb, w)


def _rope(x, cos, sin_up, sin_dn):
    return x * cos + pltpu.roll(x, ROPE_DIM // 2, axis=1) * sin_up + pltpu.roll(x, LANES - ROPE_DIM // 2, axis=1) * sin_dn


def _inproj_odd_body(x_ref, g_ref, w_ref, qn_ref, kn_ref, vn_ref, cos_ref, sup_ref, sdn_ref,
                     q_ref, k_ref, v_ref, u_ref, gv_ref):
    n = _rms(x_ref[...], g_ref[...]).astype(BF16)
    cos, sup, sdn = cos_ref[...], sup_ref[...], sdn_ref[...]
    ones64 = _group_ones(HEAD_DIM)
    qw = SWA_Q
    for j in range(SWA_Q // LANES):
        cols = slice(j * LANES, (j + 1) * LANES)
        hq = _dot(n, w_ref[:, cols])
        qh = hq * lax.rsqrt(_group_sum(hq * hq, ones64) * (1.0 / HEAD_DIM) + NORM_EPS) * qn_ref[:, cols]
        q_ref[:, cols] = (_rope(qh, cos, sup, sdn) * ATT_SCALE).astype(BF16)
    hk = _dot(n, w_ref[:, qw:qw + SWA_KV])
    kn = hk * lax.rsqrt(_group_sum(hk * hk, ones64) * (1.0 / HEAD_DIM) + NORM_EPS) * kn_ref[...]
    k_ref[...] = _rope(kn, cos, sup, sdn)
    v_ref[...] = _dot(n, w_ref[:, qw + SWA_KV:qw + 2 * SWA_KV])
    s0 = qw + 2 * SWA_KV
    u_ref[...] = _gelu_tanh(_dot(n, w_ref[:, s0:s0 + SGU_DIM]))
    gv = _gelu_tanh(_dot(n, w_ref[:, s0 + SGU_DIM:]))
    gv_ref[...] = _rms(gv, vn_ref[...])


def _inproj_odd(x, g, w, qn, kn, vn, cos, sup, sdn, table_blocks):
    rows = x.shape[0]
    tm = min(ROW_TILE, rows)
    tab = pl.BlockSpec((tm, LANES), lambda i: (i % table_blocks, 0))
    outs = [jax.ShapeDtypeStruct((rows, SWA_Q), BF16), jax.ShapeDtypeStruct((rows, SWA_KV), F32),
            jax.ShapeDtypeStruct((rows, SWA_KV), F32), jax.ShapeDtypeStruct((rows, SGU_DIM), F32),
            jax.ShapeDtypeStruct((rows, SGU_DIM), F32)]
    return pl.pallas_call(
        _inproj_odd_body,
        grid=(rows // tm,),
        in_specs=[_row_spec(tm, D_MODEL), _full_spec(g.shape), _full_spec(w.shape), _full_spec(qn.shape),
                  _full_spec(kn.shape), _full_spec(vn.shape), tab, tab, tab],
        out_specs=[_row_spec(tm, SWA_Q), _row_spec(tm, SWA_KV), _row_spec(tm, SWA_KV),
                   _row_spec(tm, SGU_DIM), _row_spec(tm, SGU_DIM)],
        out_shape=outs,
        compiler_params=_cparams("parallel"),
        name="inproj_odd",
    )(x, g, w, qn, kn, vn, cos, sup, sdn)


def _sink_attend(q, kb, vb, sink, visible):
    s = jnp.where(visible, _dot_nt(q, kb), NEG_BIG)
    m = jnp.maximum(jnp.max(s, axis=1, keepdims=True), sink)
    p = jnp.exp(s - m)
    l = jnp.sum(p, axis=1, keepdims=True) + jnp.exp(sink - m)
    return _dot(p.astype(BF16), vb) / l


def _swa_body(sink_ref, q_ref, kp_ref, kc_ref, vp_ref, vc_ref, mask_ref, o_ref):
    kw = jnp.concatenate([kp_ref[0], kc_ref[0]], axis=0)
    vw = jnp.concatenate([vp_ref[0], vc_ref[0]], axis=0)
    ks = (kw.astype(BF16), pltpu.roll(kw, HEAD_DIM, axis=1).astype(BF16))
    vs = (vw.astype(BF16), pltpu.roll(vw, HEAD_DIM, axis=1).astype(BF16))
    visible = mask_ref[0] > 0.5
    half = _lane_half((q_ref.shape[1], LANES))
    for j in range(SWA_Q // LANES):
        cols = slice(j * LANES, (j + 1) * LANES)
        qb = q_ref[0, :, cols]
        outs = []
        for e in range(2):
            head = 2 * j + e
            swapped = int(head // SWA_GROUP != e)
            qh = jnp.where(half if e == 0 else jnp.logical_not(half), qb, jnp.zeros_like(qb))
            outs.append(_sink_attend(qh, ks[swapped], vs[swapped], sink_ref[head], visible))
        o_ref[0, :, cols] = jnp.where(half, outs[0], outs[1]).astype(BF16)


def _swa(sinks, q, k_prev, k_cur, v_prev, v_cur, mask, tq, prev_rows, prev_map, mask_map):
    b, t, qw = q.shape
    nk = prev_rows + tq
    cur = lambda bi, i: (bi, i, 0)
    return pl.pallas_call(
        _swa_body,
        grid=(b, t // tq),
        in_specs=[pl.BlockSpec(memory_space=pltpu.SMEM),
                  pl.BlockSpec((1, tq, qw), cur),
                  pl.BlockSpec((1, prev_rows, SWA_KV), prev_map), pl.BlockSpec((1, tq, SWA_KV), cur),
                  pl.BlockSpec((1, prev_rows, SWA_KV), prev_map), pl.BlockSpec((1, tq, SWA_KV), cur),
                  pl.BlockSpec((1, tq, nk), mask_map)],
        out_specs=pl.BlockSpec((1, tq, qw), cur),
        out_shape=jax.ShapeDtypeStruct((b, t, qw), BF16),
        compiler_params=_cparams("parallel", "parallel"),
        name="swa",
    )(sinks, q, k_prev, k_cur, v_prev, v_cur, mask)


def _sgu_body(u_ref, v_ref, w_ref, b_ref, o_ref, *, length, n_chunks):
    tril = (lax.broadcasted_iota(jnp.int32, (length, length), 1)
            <= lax.broadcasted_iota(jnp.int32, (length, length), 0))
    ws = [jnp.where(tril, w_ref[g], 0.0).astype(BF16) for g in range(SGU_GROUPS)]
    half = _lane_half((length, LANES))
    for c in range(n_chunks):
        rows = slice(c * length, (c + 1) * length)
        for j in range(SGU_GROUPS // 2):
            cols = slice(j * LANES, (j + 1) * LANES)
            vb = v_ref[rows, cols].astype(BF16)
            mixed = jnp.where(half, _dot(ws[2 * j], vb), _dot(ws[2 * j + 1], vb)) + b_ref[:, cols]
            o_ref[rows, cols] = u_ref[rows, cols] * mixed


def _sgu(u, v, w, bias, length, n_chunks):
    rows = u.shape[0]
    tm = length * n_chunks
    return pl.pallas_call(
        functools.partial(_sgu_body, length=length, n_chunks=n_chunks),
        grid=(rows // tm,),
        in_specs=[_row_spec(tm, SGU_DIM), _row_spec(tm, SGU_DIM), _full_spec(w.shape), _full_spec(bias.shape)],
        out_specs=_row_spec(tm, SGU_DIM),
        out_shape=jax.ShapeDtypeStruct(u.shape, F32),
        compiler_params=_cparams("parallel"),
        name="sgu",
    )(u, v, w, bias)


def _outproj_odd_body(x_ref, c_ref, d_ref, w_ref, out_ref):
    out_ref[...] = (x_ref[...] + _dot(c_ref[...], w_ref[0:SWA_Q, :])
                    + _dot(d_ref[...].astype(BF16), w_ref[SWA_Q:, :]))


def _outproj_odd(x, c, d, w):
    rows = x.shape[0]
    tm = min(ROW_TILE, rows)
    return pl.pallas_call(
        _outproj_odd_body,
        grid=(rows // tm,),
        in_specs=[_row_spec(tm, D_MODEL), _row_spec(tm, SWA_Q), _row_spec(tm, SGU_DIM), _full_spec(w.shape)],
        out_specs=_row_spec(tm, D_MODEL),
        out_shape=jax.ShapeDtypeStruct(x.shape, F32),
        compiler_params=_cparams("parallel"),
        name="outproj_odd",
    )(x, c, d, w)


def _memkv_body(m_ref, g_ref, w_ref, kn_ref, k_ref, v_ref):
    n = _rms(m_ref[...], g_ref[...]).astype(BF16)
    ones = _group_ones(HEAD_DIM)
    hk = _dot(n, w_ref[:, 0:MEM_DIM])
    k_ref[...] = hk * lax.rsqrt(_group_sum(hk * hk, ones) * (1.0 / HEAD_DIM) + NORM_EPS) * kn_ref[...]
    v_ref[...] = _dot(n, w_ref[:, MEM_DIM:])


def _memkv(mem, g, w, kn):
    rows = mem.shape[0]
    tm = min(ROW_TILE, rows)
    out = jax.ShapeDtypeStruct((rows, MEM_DIM), F32)
    return pl.pallas_call(
        _memkv_body,
        grid=(rows // tm,),
        in_specs=[_row_spec(tm, D_MODEL), _full_spec(g.shape), _full_spec(w.shape), _full_spec(kn.shape)],
        out_specs=[_row_spec(tm, MEM_DIM)] * 2,
        out_shape=[out, out],
        compiler_params=_cparams("parallel"),
        name="memkv",
    )(mem, g, w, kn)


def _xattn_body(x_ref, g_ref, wq_ref, qn_ref, mk_ref, mv_ref, wo_ref, out_ref):
    x = x_ref[0]
    tq = x.shape[0]
    n = _rms(x, g_ref[...]).astype(BF16)
    hq = _dot(n, wq_ref[...])
    ones = _group_ones(HEAD_DIM)
    q = (hq * lax.rsqrt(_group_sum(hq * hq, ones) * (1.0 / HEAD_DIM) + NORM_EPS) * qn_ref[...] * ATT_SCALE).astype(BF16)
    half = _lane_half((tq, LANES))
    blocks = []
    for j in range(MEM_HEADS // 2):
        cols = slice(j * LANES, (j + 1) * LANES)
        kb = mk_ref[0, :, cols].astype(BF16)
        vb = mv_ref[0, :, cols].astype(BF16)
        qb = q[:, cols]
        outs = []
        for h in range(2):
            sel = half if h == 0 else jnp.logical_not(half)
            s = _dot_nt(jnp.where(sel, qb, jnp.zeros_like(qb)), kb)
            p = jnp.exp(s - jnp.max(s, axis=1, keepdims=True))
            outs.append(_dot(p.astype(BF16), vb) / jnp.sum(p, axis=1, keepdims=True))
        blocks.append(jnp.where(half, outs[0], outs[1]))
    o = jnp.concatenate(blocks, axis=1).astype(BF16)
    out_ref[0] = x + _dot(o, wo_ref[...])


def _xattn(x, g, wq, qn, mk, mv, wo, tq):
    b, t, _ = x.shape
    m = mk.shape[1]
    return pl.pallas_call(
        _xattn_body,
        grid=(b, t // tq),
        in_specs=[pl.BlockSpec((1, tq, D_MODEL), lambda bi, i: (bi, i, 0)), _full_spec(g.shape),
                  _full_spec(wq.shape), _full_spec(qn.shape),
                  pl.BlockSpec((1, m, MEM_DIM), lambda bi, i: (bi, 0, 0)),
                  pl.BlockSpec((1, m, MEM_DIM), lambda bi, i: (bi, 0, 0)), _full_spec(wo.shape)],
        out_specs=pl.BlockSpec((1, tq, D_MODEL), lambda bi, i: (bi, i, 0)),
        out_shape=jax.ShapeDtypeStruct(x.shape, F32),
        compiler_params=_cparams("parallel", "parallel"),
        name="xattn",
    )(x, g, wq, qn, mk, mv, wo)


def _to_chains(x, b, t, dup):
    y = x.reshape(b, t, HEAD_DIM, RWKV_HEADS).transpose(1, 2, 0, 3).reshape(t, HEAD_DIM, b * RWKV_HEADS)
    return jnp.concatenate([y] * dup, axis=-1) if dup > 1 else y


def _rope_tables(pos):
    half = ROPE_DIM // 2
    inv_freq = jnp.power(ROPE_THETA, -jnp.arange(half, dtype=F32) / half)
    ang = pos.astype(F32)[:, None] * inv_freq[None, :]
    cos, sin = jnp.cos(ang), jnp.sin(ang)
    n = pos.shape[0]
    pad = jnp.zeros((n, HEAD_DIM - ROPE_DIM), F32)
    zero = jnp.zeros((n, half), F32)
    cos_t = jnp.concatenate([cos, cos, pad + 1.0], axis=1)
    up_t = jnp.concatenate([zero, sin, pad], axis=1)
    dn_t = jnp.concatenate([-sin, zero, pad], axis=1)
    two = lambda a: jnp.concatenate([a, a], axis=1)
    return two(cos_t), two(up_t), two(dn_t)


def _swa_prompt_mask(tq):
    span = WINDOW + tq
    qc = np.arange(tq)[:, None] // CHUNK
    kc = np.arange(span)[None, :] // CHUNK - WINDOW_CHUNKS
    band = (kc <= qc) & (kc >= qc - WINDOW_CHUNKS)
    first = band & (kc >= 0)
    return jnp.asarray(np.stack([first, band]).astype(np.float32))


def _swa_sample_mask(past, rows, t):
    kc = (past - rows + np.arange(rows + t)) // CHUNK
    qc = (past + np.arange(t)) // CHUNK
    m = (kc[None, :] <= qc[:, None]) & (kc[None, :] >= qc[:, None] - WINDOW_CHUNKS)
    return jnp.asarray(m[None].astype(np.float32))


def kernel(x_prompt, x_sample, cache_fox_k, cache_fox_v, cache_fox_logf, state_rwkv, state_rwkv_shift, cache_swa_k, cache_swa_v, cache_mem_k, cache_mem_v, mem_prompt, ffn1_norm, ffn1_w_gate, ffn1_w_up, ffn1_w_down, mix_norm, ev_w_in, fox_b_f, fox_q_norm, fox_k_norm, rwkv_mu, rwkv_w0, rwkv_w2, rwkv_a0, rwkv_a2, rwkv_g2, rwkv_k_k, rwkv_k_a, rwkv_r_k, rwkv_ln_g, rwkv_ln_b, ev_w_out, od_w_in, swa_q_norm, swa_k_norm, swa_sinks, sgu_v_norm, sgu_w_s, sgu_b, od_w_out, xattn_norm, mem_norm, xattn_wq, xattn_wkv, xattn_q_norm, xattn_k_norm, xattn_wo, ffn2_norm, ffn2_w_gate, ffn2_w_up, ffn2_w_down):
    bp, tp, _ = x_prompt.shape
    bs, ts, _ = x_sample.shape
    depth = ffn1_norm.shape[0]
    past = cache_fox_k.shape[2]
    mem_tokens = mem_prompt.shape[1]
    xp = x_prompt.reshape(bp * tp, D_MODEL)
    xs = x_sample.reshape(bs * ts, D_MODEL)
    mem_rows = mem_prompt.reshape(bp * mem_tokens, D_MODEL)
    row = lambda a: a.reshape(1, -1)
    tile_heads = lambda a, n: jnp.tile(a, n).reshape(1, -1)

    out = {k: [] for k in ("p_fox_k", "p_fox_v", "p_fox_logf", "p_rwkv_state", "p_rwkv_shift", "p_swa_k", "p_swa_v",
                           "p_mem_k", "p_mem_v", "s_fox_k", "s_fox_v", "s_fox_logf", "s_rwkv_state", "s_rwkv_shift",
                           "s_swa_k", "s_swa_v", "s_sgu_v")}

    for l in range(depth):
        f1 = _prep_ffn(ffn1_norm[l], ffn1_w_gate[l], ffn1_w_up[l], ffn1_w_down[l])
        xp = _ffn(xp, *f1)
        xs = _ffn(xs, *f1)
        g_mix = row(mix_norm[l])
        if l % 2 == 0:
            e = l // 2
            w_in = ev_w_in[e]
            f0 = 3 * FOX_DIM
            w_cat = jnp.concatenate([w_in[:, :f0], jnp.pad(w_in[:, f0:f0 + FOX_HEADS], ((0, 0), (0, LANES - FOX_HEADS))),
                                     w_in[:, f0 + FOX_HEADS:][:, RWKV_COL_PERM]], axis=1).astype(BF16)
            bf = jnp.pad(fox_b_f[e], (0, LANES - FOX_HEADS)).reshape(1, LANES)
            qn = tile_heads(fox_q_norm[e], FOX_HEADS)
            kn = tile_heads(fox_k_norm[e], FOX_HEADS)
            qp, kp, vp, lfp, hrp = _inproj_even(xp, g_mix, w_cat, bf, qn, kn)
            qs, ks, vs, lfs, hrs = _inproj_even(xs, g_mix, w_cat, bf, qn, kn)

            cum_p = _cumsum(lfp.reshape(bp, tp, LANES), CUM_TILE)
            a_p = _fox_prompt(qp.reshape(bp, tp, FOX_DIM), kp.reshape(bp, tp, FOX_DIM), vp.reshape(bp, tp, FOX_DIM),
                              cum_p, ATT_TILE)
            lfs8 = lfs.reshape(bs, ts, LANES)[:, :, :FOX_HEADS]
            tot = past + ts
            padded = -(-tot // CUM_TILE) * CUM_TILE
            lf_all = jnp.concatenate([cache_fox_logf[e].astype(F32), lfs8], axis=1)
            lf_all = jnp.pad(lf_all, ((0, 0), (0, padded - tot), (0, LANES - FOX_HEADS)))
            cum_s = _cumsum(lf_all, CUM_TILE)
            a_s = _fox_sample(qs.reshape(bs, ts, FOX_DIM), ks.reshape(bs, ts, FOX_DIM), vs.reshape(bs, ts, FOX_DIM),
                              cache_fox_k[e].reshape(bs, past, FOX_DIM), cache_fox_v[e].reshape(bs, past, FOX_DIM),
                              cum_s)

            perm = RWKV_HEAD_PERM
            w2p = jnp.pad(rwkv_w2[e][:, perm], ((0, LANES - DECAY_LORA), (0, 0))).astype(BF16)
            a2p = jnp.pad(rwkv_a2[e][:, perm], ((DECAY_LORA, 0), (0, 0))).astype(BF16)
            params = (row(rwkv_mu[e][RWKV_COL_PERM]), row(rwkv_w0[e][perm]), w2p, row(rwkv_a0[e][perm]), a2p,
                      rwkv_g2[e][:, perm].astype(BF16), row(rwkv_k_k[e][perm]), row(rwkv_k_a[e][perm]),
                      row(rwkv_r_k[e].reshape(-1)[perm]))
            tiles_per_seq = tp // ROW_TILE
            prev_p = hrp.reshape(bp * tp // 8, 8, RWKV_COLS)
            prep_p = _rwkv_prep(hrp, prev_p, lambda i: (jnp.maximum(i * (ROW_TILE // 8) - 1, 0), 0, 0),
                                ROW_TILE, tiles_per_seq, params)
            prev_s = jnp.pad(state_rwkv_shift[e].astype(F32)[:, :, RWKV_COL_PERM], ((0, 0), (7, 0), (0, 0)))
            prep_s = _rwkv_prep(hrs, prev_s, lambda i: (i, 0, 0), ts, 0, params)

            def scan(prep, b, t, state0):
                r, w, k2, v, kap, bet, g, bon = prep
                dup = LANES // (b * RWKV_HEADS)
                nv = HEAD_DIM // dup
                in_kernel = t % LANES == 0
                if in_kernel:
                    ops = [_to_chains_pallas(a.reshape(b, t, RWKV_DIM), HEAD_DIM, lambda k: (k,) * dup)
                           for a in (w, kap, bet, k2, r)]
                    vt = _to_chains_pallas(v.reshape(b, t, RWKV_DIM), nv,
                                           lambda vp: tuple(vh * nv + vp for vh in range(dup)))
                else:
                    ops = [_to_chains(a, b, t, dup) for a in (w, kap, bet, k2, r)]
                    vt = v.reshape(b, t, dup, nv, RWKV_HEADS).transpose(1, 3, 2, 0, 4).reshape(t, nv, LANES)
                s0 = state0.reshape(b, RWKV_HEADS, dup, nv, HEAD_DIM).transpose(3, 4, 2, 0, 1).reshape(nv, HEAD_DIM, LANES)
                o, sT = _rwkv_scan(*ops, vt, s0)
                if in_kernel:
                    o = _from_chains_pallas(o, b).reshape(b * t, RWKV_DIM)
                else:
                    o = o.reshape(t, nv, dup, b, RWKV_HEADS).transpose(3, 0, 2, 1, 4).reshape(b * t, RWKV_DIM)
                sT = sT.reshape(nv, HEAD_DIM, dup, b, RWKV_HEADS).transpose(3, 4, 2, 0, 1).reshape(b, RWKV_HEADS, HEAD_DIM, HEAD_DIM)
                return o, sT, g, bon

            o_p, st_p, g_p, bon_p = scan(prep_p, bp, tp, jnp.zeros((bp, RWKV_HEADS, HEAD_DIM, HEAD_DIM), F32))
            o_s, st_s, g_s, bon_s = scan(prep_s, bs, ts, state_rwkv[e].astype(F32))

            w_out = jnp.concatenate([ev_w_out[e][:FOX_DIM], ev_w_out[e][FOX_DIM:][perm]], axis=0).astype(BF16)
            lng, lnb = row(rwkv_ln_g[e][perm]), row(rwkv_ln_b[e][perm])
            xp = _outproj_even(xp, a_p.reshape(bp * tp, FOX_DIM), o_p, bon_p, g_p, lng, lnb, w_out)
            xs = _outproj_even(xs, a_s.reshape(bs * ts, FOX_DIM), o_s, bon_s, g_s, lng, lnb, w_out)

            out["p_fox_k"].append(kp.reshape(bp, tp, FOX_HEADS, HEAD_DIM))
            out["p_fox_v"].append(vp.reshape(bp, tp, FOX_HEADS, HEAD_DIM))
            out["p_fox_logf"].append(lfp.reshape(bp, tp, LANES)[:, :, :FOX_HEADS])
            out["p_rwkv_state"].append(st_p)
            out["p_rwkv_shift"].append(hrp.reshape(bp, tp, RWKV_COLS)[:, -1:][:, :, RWKV_COL_UNPERM])
            out["s_fox_k"].append(ks.reshape(bs, ts, FOX_HEADS, HEAD_DIM))
            out["s_fox_v"].append(vs.reshape(bs, ts, FOX_HEADS, HEAD_DIM))
            out["s_fox_logf"].append(lfs8)
            out["s_rwkv_state"].append(st_s)
            out["s_rwkv_shift"].append(hrs.reshape(bs, ts, RWKV_COLS)[:, -1:][:, :, RWKV_COL_UNPERM])
        else:
            j = l // 2
            w_cat = od_w_in[j].astype(BF16)
            qn = tile_heads(swa_q_norm[j], SWA_HEADS)
            kn = tile_heads(swa_k_norm[j], SWA_KV_HEADS)
            vn = row(sgu_v_norm[j])
            tabs_p = _rope_tables(jnp.arange(tp))
            tabs_s = _rope_tables(past + jnp.arange(bs * ts) % ts)
            qp, kp, vp, up, gp = _inproj_odd(xp, g_mix, w_cat, qn, kn, vn, *tabs_p, tp // ROW_TILE)
            qs, ks, vs, us, gs = _inproj_odd(xs, g_mix, w_cat, qn, kn, vn, *tabs_s, 1)

            qw = SWA_Q
            kp3, vp3 = kp.reshape(bp, tp, SWA_KV), vp.reshape(bp, tp, SWA_KV)
            ratio = SWA_TILE // WINDOW
            c_p = _swa(swa_sinks[j], qp.reshape(bp, tp, qw), kp3, kp3, vp3, vp3, _swa_prompt_mask(SWA_TILE),
                       SWA_TILE, WINDOW, lambda bi, i: (bi, jnp.maximum(i * ratio - 1, 0), 0),
                       lambda bi, i: (jnp.minimum(i, 1), 0, 0))
            rows_c = cache_swa_k.shape[2]
            ck3 = cache_swa_k[j].reshape(bs, rows_c, SWA_KV)
            cv3 = cache_swa_v[j].reshape(bs, rows_c, SWA_KV)
            ks3, vs3 = ks.reshape(bs, ts, SWA_KV), vs.reshape(bs, ts, SWA_KV)
            c_s = _swa(swa_sinks[j], qs.reshape(bs, ts, qw), ck3, ks3, cv3, vs3, _swa_sample_mask(past, rows_c, ts),
                       ts, rows_c, lambda bi, i: (bi, 0, 0), lambda bi, i: (0, 0, 0))

            bias = jnp.repeat(jnp.transpose(sgu_b[j]), HEAD_DIM, axis=1)
            d_p = _sgu(up, gp, sgu_w_s[j], bias, SGU_CHUNK, ROW_TILE // SGU_CHUNK)
            d_s = _sgu(us, gs, sgu_w_s[j][:, :ts, :ts], bias[:ts], ts, 1)

            w_out = od_w_out[j].astype(BF16)
            xp = _outproj_odd(xp, c_p.reshape(bp * tp, qw), d_p, w_out)
            xs = _outproj_odd(xs, c_s.reshape(bs * ts, qw), d_s, w_out)

            out["p_swa_k"].append(kp3[:, -WINDOW:].reshape(bp, WINDOW, SWA_KV_HEADS, HEAD_DIM))
            out["p_swa_v"].append(vp3[:, -WINDOW:].reshape(bp, WINDOW, SWA_KV_HEADS, HEAD_DIM))
            out["s_swa_k"].append(jnp.concatenate([ck3, ks3], axis=1)[:, -rows_c:].reshape(bs, rows_c, SWA_KV_HEADS, HEAD_DIM))
            out["s_swa_v"].append(jnp.concatenate([cv3, vs3], axis=1)[:, -rows_c:].reshape(bs, rows_c, SWA_KV_HEADS, HEAD_DIM))
            out["s_sgu_v"].append(gs.reshape(bs, ts, SGU_DIM))

        mk, mv = _memkv(mem_rows, row(mem_norm[l]), xattn_wkv[l].astype(BF16), tile_heads(xattn_k_norm[l], MEM_HEADS))
        mk3, mv3 = mk.reshape(bp, mem_tokens, MEM_DIM), mv.reshape(bp, mem_tokens, MEM_DIM)
        xa = (row(xattn_norm[l]), xattn_wq[l].astype(BF16), tile_heads(xattn_q_norm[l], MEM_HEADS))
        wo = xattn_wo[l].astype(BF16)
        xp = _xattn(xp.reshape(bp, tp, D_MODEL), *xa, mk3, mv3, wo, ROW_TILE).reshape(bp * tp, D_MODEL)
        xs = _xattn(xs.reshape(bs, ts, D_MODEL), *xa, cache_mem_k[l].reshape(bs, mem_tokens, MEM_DIM),
                    cache_mem_v[l].reshape(bs, mem_tokens, MEM_DIM), wo, ts).reshape(bs * ts, D_MODEL)
        out["p_mem_k"].append(mk3.reshape(bp, mem_tokens, MEM_HEADS, HEAD_DIM))
        out["p_mem_v"].append(mv3.reshape(bp, mem_tokens, MEM_HEADS, HEAD_DIM))

        f2 = _prep_ffn(ffn2_norm[l], ffn2_w_gate[l], ffn2_w_up[l], ffn2_w_down[l])
        xp = _ffn(xp, *f2)
        xs = _ffn(xs, *f2)

    order = ("p_fox_k", "p_fox_v", "p_fox_logf", "p_rwkv_state", "p_rwkv_shift", "p_swa_k", "p_swa_v", "p_mem_k",
             "p_mem_v", "s_fox_k", "s_fox_v", "s_fox_logf", "s_rwkv_state", "s_rwkv_shift", "s_swa_k", "s_swa_v",
             "s_sgu_v")
    return (xp.reshape(bp, tp, D_MODEL), xs.reshape(bs, ts, D_MODEL)) + tuple(jnp.stack(out[k]) for k in order)
```

```python
import functools

import numpy as np
import jax
import jax.numpy as jnp
from jax import lax
from jax.experimental import pallas as pl
from jax.experimental.pallas import tpu as pltpu

F32 = jnp.float32
BF16 = jnp.bfloat16

D_MODEL = 1024
HEAD_DIM = 64
NORM_EPS = 1e-6
ROPE_THETA = 500000.0
ROPE_DIM = HEAD_DIM // 4
CHUNK = 64
FOX_HEADS = 8
FOX_DIM = FOX_HEADS * HEAD_DIM
RWKV_HEADS = 8
RWKV_DIM = RWKV_HEADS * HEAD_DIM
DECAY_LORA = 64
ICLR_LORA = 64
GATE_LORA = 128
RWKV_COLS = 3 * RWKV_DIM + DECAY_LORA + ICLR_LORA + GATE_LORA
RWKV_GN_EPS = 64e-5
SWA_HEADS = 8
SWA_KV_HEADS = 2
SWA_GROUP = SWA_HEADS // SWA_KV_HEADS
SWA_Q = SWA_HEADS * HEAD_DIM
SWA_KV = SWA_KV_HEADS * HEAD_DIM
WINDOW = 128
WINDOW_CHUNKS = WINDOW // CHUNK
SGU_GROUPS = 8
SGU_DIM = SGU_GROUPS * HEAD_DIM
SGU_CHUNK = 128
MEM_HEADS = 4
MEM_DIM = MEM_HEADS * HEAD_DIM
D_FF = 2816

LANES = 128
ROW_TILE = 512
FF_TILE = 256
ATT_TILE = 512
CUM_TILE = 256
SWA_TILE = 256
SCAN_TOKENS = 64
VMEM_LIMIT = 56 * 1024 * 1024
RWKV_HEAD_PERM = np.arange(RWKV_DIM).reshape(RWKV_HEADS, HEAD_DIM).T.reshape(-1)
RWKV_COL_PERM = np.concatenate([RWKV_HEAD_PERM, RWKV_DIM + RWKV_HEAD_PERM, 2 * RWKV_DIM + RWKV_HEAD_PERM,
                                np.arange(3 * RWKV_DIM, RWKV_COLS)])
RWKV_COL_UNPERM = np.argsort(RWKV_COL_PERM)
ATT_SCALE = HEAD_DIM ** -0.5
LOG2E = 1.4426950408889634
NEG_BIG = -1e30


def _cparams(*sem):
    return pltpu.CompilerParams(dimension_semantics=sem, vmem_limit_bytes=VMEM_LIMIT)


def _dot(a, b):
    return jnp.dot(a, b, preferred_element_type=F32)


def _dot_nt(a, b):
    return lax.dot_general(a, b, (((1,), (1,)), ((), ())), preferred_element_type=F32)


def _rms(x, g):
    ms = jnp.mean(x * x, axis=-1, keepdims=True)
    return (x * lax.rsqrt(ms + NORM_EPS)) * g


def _group_ones(group):
    shift = int(np.log2(group))
    r = lax.broadcasted_iota(jnp.int32, (LANES, LANES), 0) >> shift
    c = lax.broadcasted_iota(jnp.int32, (LANES, LANES), 1) >> shift
    return jnp.where(r == c, 1.0, 0.0).astype(BF16)


def _group_sum(x, ones):
    parts = []
    for j in range(x.shape[1] // LANES):
        blk = x[:, j * LANES:(j + 1) * LANES]
        hi = blk.astype(BF16)
        lo = (blk - hi.astype(F32)).astype(BF16)
        parts.append(_dot(hi, ones) + _dot(lo, ones))
    return parts[0] if len(parts) == 1 else jnp.concatenate(parts, axis=1)


def _head_ones():
    r = lax.broadcasted_iota(jnp.int32, (LANES, LANES), 0) & (RWKV_HEADS - 1)
    c = lax.broadcasted_iota(jnp.int32, (LANES, LANES), 1) & (RWKV_HEADS - 1)
    return jnp.where(r == c, 1.0, 0.0).astype(BF16)


def _head_sum(x, ones):
    part = x[:, 0:LANES]
    for j in range(1, x.shape[1] // LANES):
        part = part + x[:, j * LANES:(j + 1) * LANES]
    hi = part.astype(BF16)
    lo = (part - hi.astype(F32)).astype(BF16)
    tot = _dot(hi, ones) + _dot(lo, ones)
    return jnp.concatenate([tot] * (x.shape[1] // LANES), axis=1)


def _log_sigmoid(z):
    return jnp.minimum(z, 0.0) - jnp.log(1.0 + jnp.exp(-jnp.abs(z)))


def _gelu_tanh(x):
    return 0.5 * x * (1.0 + jnp.tanh(0.7978845608028654 * (x + 0.044715 * (x * x * x))))


def _lane_half(shape):
    return lax.broadcasted_iota(jnp.int32, shape, 1) < HEAD_DIM


def _row_spec(tm, cols):
    return pl.BlockSpec((tm, cols), lambda i: (i, 0))


def _full_spec(shape):
    nd = len(shape)
    return pl.BlockSpec(shape, lambda *_: (0,) * nd, pipeline_mode=pl.Buffered(1))


def _ffn_body(x_ref, g_ref, wg_ref, wu_ref, wd_ref, o_ref, acc_ref, *, n_chunks):
    x = x_ref[...]
    n = _rms(x, g_ref[...]).astype(BF16)
    for c in range(n_chunks):
        cols = slice(c * FF_TILE, (c + 1) * FF_TILE)
        gate = _dot(n, wg_ref[:, cols])
        up = _dot(n, wu_ref[:, cols])
        act = (gate * jax.nn.sigmoid(gate) * up).astype(BF16)
        part = _dot(act, wd_ref[cols, :])
        if c == 0:
            acc_ref[...] = part
        else:
            acc_ref[...] += part
    o_ref[...] = x + 0.5 * acc_ref[...]


def _ffn(x, g, wg, wu, wd):
    rows = x.shape[0]
    tm = min(ROW_TILE, rows)
    return pl.pallas_call(
        functools.partial(_ffn_body, n_chunks=wg.shape[1] // FF_TILE),
        grid=(rows // tm,),
        in_specs=[_row_spec(tm, D_MODEL), _full_spec(g.shape), _full_spec(wg.shape),
                  _full_spec(wu.shape), _full_spec(wd.shape)],
        out_specs=_row_spec(tm, D_MODEL),
        out_shape=jax.ShapeDtypeStruct(x.shape, F32),
        scratch_shapes=[pltpu.VMEM((tm, D_MODEL), F32)],
        compiler_params=_cparams("parallel"),
        name="ffn",
    )(x, g, wg, wu, wd)


def _prep_ffn(norm, w_gate, w_up, w_down):
    return norm.reshape(1, D_MODEL), w_gate.astype(BF16), w_up.astype(BF16), w_down.astype(BF16)


def _inproj_even_body(x_ref, g_ref, w_ref, bf_ref, qn_ref, kn_ref,
                      q_ref, k_ref, v_ref, lf_ref, hr_ref, *, token_minor):
    n = _rms(x_ref[...], g_ref[...]).astype(BF16)
    ones = _group_ones(HEAD_DIM)
    hq = _dot(n, w_ref[:, 0:FOX_DIM])
    q = hq * lax.rsqrt(_group_sum(hq * hq, ones) * (1.0 / HEAD_DIM) + NORM_EPS) * qn_ref[...]
    q_ref[...] = (q * (ATT_SCALE * LOG2E)).astype(BF16)
    hk = _dot(n, w_ref[:, FOX_DIM:2 * FOX_DIM])
    k = hk * lax.rsqrt(_group_sum(hk * hk, ones) * (1.0 / HEAD_DIM) + NORM_EPS) * kn_ref[...]
    v = _dot(n, w_ref[:, 2 * FOX_DIM:3 * FOX_DIM])
    if token_minor:
        k_ref[0] = k.T
        v_ref[0] = v.T
    else:
        k_ref[...] = k
        v_ref[...] = v
    f0 = 3 * FOX_DIM
    lf_ref[...] = _log_sigmoid(_dot(n, w_ref[:, f0:f0 + LANES]) + bf_ref[...])
    hr_ref[...] = _dot(n, w_ref[:, f0 + LANES:])


def _inproj_even(x, g, w, bf, qn, kn, seq_len=None):
    rows = x.shape[0]
    tm = min(ROW_TILE, rows)
    if seq_len is None:
        kv_shape = jax.ShapeDtypeStruct((rows, FOX_DIM), F32)
        kv_spec = _row_spec(tm, FOX_DIM)
    else:
        per_seq = seq_len // tm
        kv_shape = jax.ShapeDtypeStruct((rows // seq_len, FOX_DIM, seq_len), F32)
        kv_spec = pl.BlockSpec((1, FOX_DIM, tm), lambda i: (i // per_seq, 0, i % per_seq))
    outs = [jax.ShapeDtypeStruct((rows, FOX_DIM), BF16), kv_shape, kv_shape,
            jax.ShapeDtypeStruct((rows, LANES), F32), jax.ShapeDtypeStruct((rows, RWKV_COLS), F32)]
    return pl.pallas_call(
        functools.partial(_inproj_even_body, token_minor=seq_len is not None),
        grid=(rows // tm,),
        in_specs=[_row_spec(tm, D_MODEL), _full_spec(g.shape), _full_spec(w.shape), _full_spec(bf.shape),
                  _full_spec(qn.shape), _full_spec(kn.shape)],
        out_specs=[_row_spec(tm, FOX_DIM), kv_spec, kv_spec, _row_spec(tm, LANES), _row_spec(tm, RWKV_COLS)],
        out_shape=outs,
        compiler_params=_cparams("parallel"),
        name="inproj_even",
    )(x, g, w, bf, qn, kn)


def _split3(x):
    hi = x.astype(BF16)
    r1 = x - hi.astype(F32)
    mid = r1.astype(BF16)
    lo = (r1 - mid.astype(F32)).astype(BF16)
    return hi, mid, lo


def _cumsum_body(lf_ref, col_ref, row_ref, *, n_chunks, tk):
    r = lax.broadcasted_iota(jnp.int32, (tk, tk), 0)
    c = lax.broadcasted_iota(jnp.int32, (tk, tk), 1)
    tri = jnp.where(c <= r, 1.0, 0.0).astype(BF16)
    carry = jnp.zeros((1, LANES), F32)
    for i in range(n_chunks):
        hi, mid, lo = _split3(lf_ref[0, i * tk:(i + 1) * tk, :])
        cs = _dot(tri, hi) + _dot(tri, mid) + _dot(tri, lo) + carry
        col_ref[0, i * tk:(i + 1) * tk, :] = cs
        row_ref[0, :, i * tk:(i + 1) * tk] = cs.T[0:FOX_HEADS, :]
        carry = cs[tk - 1:tk, :]


def _cumsum(lf, tk):
    b, length, _ = lf.shape
    n_chunks = length // tk
    return pl.pallas_call(
        functools.partial(_cumsum_body, n_chunks=n_chunks, tk=tk),
        grid=(b,),
        in_specs=[pl.BlockSpec((1, length, LANES), lambda i: (i, 0, 0))],
        out_specs=[pl.BlockSpec((1, length, LANES), lambda i: (i, 0, 0)),
                   pl.BlockSpec((1, FOX_HEADS, length), lambda i: (i, 0, 0))],
        out_shape=[jax.ShapeDtypeStruct((b, length, LANES), F32),
                   jax.ShapeDtypeStruct((b, FOX_HEADS, length), F32)],
        compiler_params=_cparams("parallel"),
        name="cumsum",
    )(lf)


def _place3(terms, src_lane, dst_lane):
    r = lax.broadcasted_iota(jnp.int32, (LANES, LANES), 0)
    c = lax.broadcasted_iota(jnp.int32, (LANES, LANES), 1)
    out = None
    for i, t in enumerate(terms):
        sel = jnp.where((r == src_lane) & (c == dst_lane + i), 1.0, 0.0).astype(BF16)
        out = _dot(t, sel) if out is None else out + _dot(t, sel)
    return out


def _fox_q(q_pair, cum_col, head, h):
    n = q_pair.shape[0]
    qf = q_pair.astype(F32)
    if h == 1:
        qf = pltpu.roll(qf, HEAD_DIM, axis=1)
    lane = lax.broadcasted_iota(jnp.int32, (n, LANES), 1)
    ones = jnp.where((lane >= HEAD_DIM + 3) & (lane < HEAD_DIM + 6), 1.0, 0.0)
    aug = _place3(_split3(cum_col * LOG2E), head, HEAD_DIM) + ones
    return jnp.where(lane < HEAD_DIM, qf, aug).astype(BF16)


def _fox_kt(kt, cum_row):
    n = kt.shape[1]
    hi, mid, lo = (x.astype(F32) for x in _split3(cum_row * LOG2E))
    row = lax.broadcasted_iota(jnp.int32, (8, n), 0)
    aug = jnp.where(row < 3, 1.0, jnp.where(row == 3, -hi, jnp.where(row == 4, -mid, jnp.where(row == 5, -lo, 0.0))))
    return jnp.concatenate([kt, aug, jnp.zeros((HEAD_DIM - 8, n), F32)], axis=0).astype(BF16)


def _fox_vt(vt):
    n = vt.shape[1]
    row = lax.broadcasted_iota(jnp.int32, (HEAD_DIM, n), 0)
    return jnp.concatenate([vt, jnp.where(row == 0, 1.0, 0.0)], axis=0).astype(BF16)


def _fox_finish(acc):
    return acc / acc[:, HEAD_DIM:HEAD_DIM + 1]


def _fox_prompt_body(q_ref, cum_ref, cumt_ref, kt_ref, vt_ref, o_ref, ka_sc, va_sc, m_sc, acc_sc, *, tq, t):
    hp = pl.program_id(1)
    i = pl.program_id(2)

    @pl.when(i == 0)
    def _():
        for h in range(2):
            rows = slice(h * HEAD_DIM, (h + 1) * HEAD_DIM)
            cum_row = cumt_ref[0, pl.ds(2 * hp + h, 1), :]
            for c in range(t // tq):
                cols = slice(c * tq, (c + 1) * tq)
                ka_sc[h, c] = _fox_kt(kt_ref[0, rows, cols], cum_row[:, cols])
                va_sc[h, c] = _fox_vt(vt_ref[0, rows, cols])

    off_q = pl.multiple_of(i * tq, tq)
    cq = cum_ref[0, pl.ds(off_q, tq), :]
    qs = [_fox_q(q_ref[0], cq, 2 * hp + h, h) for h in range(2)]
    m_sc[...] = jnp.full(m_sc.shape, NEG_BIG, F32)
    acc_sc[...] = jnp.zeros(acc_sc.shape, F32)
    causal = (lax.broadcasted_iota(jnp.int32, (tq, tq), 1) <= lax.broadcasted_iota(jnp.int32, (tq, tq), 0))

    def step(j, masked):
        for h in range(2):
            s = _dot(qs[h], ka_sc[h, j])
            if masked:
                s = jnp.where(causal, s, NEG_BIG)
            m_old = m_sc[h]
            m_new = jnp.maximum(m_old, jnp.max(s, axis=1, keepdims=True))
            p = jnp.concatenate([jnp.exp2(s[:, c * LANES:(c + 1) * LANES] - m_new) for c in range(tq // LANES)],
                                axis=1).astype(BF16)
            acc_sc[h] = jnp.exp2(m_old - m_new) * acc_sc[h] + _dot_nt(p, va_sc[h, j])
            m_sc[h] = m_new

    def past(j, carry):
        step(j, False)
        return carry

    lax.fori_loop(0, i, past, 0)
    step(i, True)
    second = pltpu.roll(_fox_finish(acc_sc[1]), HEAD_DIM, axis=1)
    o_ref[0] = jnp.where(_lane_half((tq, LANES)), _fox_finish(acc_sc[0]), second)


def _fox_prompt(q, kt, vt, cum, cumt, tq):
    b, t, _ = q.shape
    pairs = FOX_HEADS // 2
    nblk = t // tq
    return pl.pallas_call(
        functools.partial(_fox_prompt_body, tq=tq, t=t),
        grid=(b, pairs, nblk),
        in_specs=[pl.BlockSpec((1, tq, LANES), lambda bi, hp, i: (bi, i, hp)),
                  pl.BlockSpec((1, t, LANES), lambda bi, hp, i: (bi, 0, 0)),
                  pl.BlockSpec((1, FOX_HEADS, t), lambda bi, hp, i: (bi, 0, 0)),
                  pl.BlockSpec((1, LANES, t), lambda bi, hp, i: (bi, hp, 0)),
                  pl.BlockSpec((1, LANES, t), lambda bi, hp, i: (bi, hp, 0))],
        out_specs=pl.BlockSpec((1, tq, LANES), lambda bi, hp, i: (bi, i, hp)),
        out_shape=jax.ShapeDtypeStruct((b, t, FOX_DIM), F32),
        scratch_shapes=[pltpu.VMEM((2, nblk, LANES, tq), BF16), pltpu.VMEM((2, nblk, LANES, tq), BF16),
                        pltpu.VMEM((2, tq, LANES), F32), pltpu.VMEM((2, tq, LANES), F32)],
        compiler_params=_cparams("parallel", "parallel", "arbitrary"),
        name="fox_prompt",
    )(q, cum, cumt, kt, vt)


def _fox_sample_body(q_ref, cum_ref, cumt_ref, kt_ref, vt_ref, kn_ref, vn_ref, o_ref, *, ts, past):
    cum_n = cum_ref[0, 0:ts, :] * LOG2E
    causal = (lax.broadcasted_iota(jnp.int32, (ts, ts), 1) <= lax.broadcasted_iota(jnp.int32, (ts, ts), 0))
    lane = lax.broadcasted_iota(jnp.int32, (ts, LANES), 1)
    q = q_ref[0].astype(F32)
    kn = kn_ref[0]
    vn = vn_ref[0]
    outs = []
    for h in range(FOX_HEADS):
        cols = slice(h * HEAD_DIM, (h + 1) * HEAD_DIM)
        cq = jnp.sum(jnp.where(lane == h, cum_n, 0.0), axis=1, keepdims=True)
        ck_p = cumt_ref[0, h:h + 1, 0:past] * LOG2E
        ck_n = cumt_ref[0, h:h + 1, past:past + ts] * LOG2E
        qh = q[:, cols].astype(BF16)
        s_p = _dot(qh, kt_ref[0, cols, :].astype(BF16)) + (cq - ck_p)
        s_n = jnp.where(causal, _dot_nt(qh, kn[:, cols].astype(BF16)) + (cq - ck_n), NEG_BIG)
        m = jnp.maximum(jnp.max(s_p, axis=1, keepdims=True), jnp.max(s_n, axis=1, keepdims=True))
        p_p = jnp.exp2(s_p - m)
        p_n = jnp.exp2(s_n - m)
        l = jnp.sum(p_p, axis=1, keepdims=True) + jnp.sum(p_n, axis=1, keepdims=True)
        acc = (_dot_nt(p_p.astype(BF16), vt_ref[0, cols, :].astype(BF16))
               + _dot(p_n.astype(BF16), vn[:, cols].astype(BF16)))
        outs.append(acc / l)
    o_ref[0] = jnp.concatenate(outs, axis=1)


def _fox_sample(q, k_new, v_new, kt_past, vt_past, cum, cumt):
    b, ts, _ = q.shape
    p = kt_past.shape[2]
    length = cum.shape[1]
    assert p % CUM_TILE == 0 and ts <= CUM_TILE
    new = pl.BlockSpec((1, ts, FOX_DIM), lambda bi: (bi, 0, 0))
    old = pl.BlockSpec((1, FOX_DIM, p), lambda bi: (bi, 0, 0))
    return pl.pallas_call(
        functools.partial(_fox_sample_body, ts=ts, past=p),
        grid=(b,),
        in_specs=[new, pl.BlockSpec((1, CUM_TILE, LANES), lambda bi: (bi, p // CUM_TILE, 0)),
                  pl.BlockSpec((1, FOX_HEADS, length), lambda bi: (bi, 0, 0)), old, old, new, new],
        out_specs=new,
        out_shape=jax.ShapeDtypeStruct((b, ts, FOX_DIM), F32),
        compiler_params=_cparams("parallel"),
        name="fox_sample",
    )(q, cum, cumt, kt_past, vt_past, k_new, v_new)


def _rwkv_prep_body(h_ref, prev_ref, mu_ref, w0_ref, w2_ref, a0_ref, a2_ref, g2_ref, kk_ref, ka_ref, rk_ref,
                    r_out, w_out, k_out, v_out, kap_out, bet_out, g_out, bon_out, *, tiles_per_seq):
    h = h_ref[...]
    tm = h.shape[0]
    prev_row = prev_ref[0, 7:8, :]
    if tiles_per_seq:
        keep = jnp.where(pl.program_id(0) % tiles_per_seq != 0, 1.0, 0.0)
        prev_row = prev_row * keep
    first = lax.broadcasted_iota(jnp.int32, (tm, 1), 0) == 0
    prev = jnp.where(first, prev_row, pltpu.roll(h, 1, axis=0))
    hx = h + (prev - h) * mu_ref[...]
    r = hx[:, 0:RWKV_DIM]
    k = hx[:, RWKV_DIM:2 * RWKV_DIM]
    v = hx[:, 2 * RWKV_DIM:3 * RWKV_DIM]
    xwa = hx[:, 3 * RWKV_DIM:3 * RWKV_DIM + LANES]
    xg = hx[:, 3 * RWKV_DIM + LANES:]
    w_logit = w0_ref[...] + _dot(jnp.tanh(xwa).astype(BF16), w2_ref[...])
    decay = jnp.exp(-jnp.exp(_log_sigmoid(w_logit) - 0.5))
    a = jax.nn.sigmoid(a0_ref[...] + _dot(xwa.astype(BF16), a2_ref[...]))
    g = _dot(jax.nn.sigmoid(xg).astype(BF16), g2_ref[...])
    ones = _head_ones()
    kk = k * kk_ref[...]
    kk = kk / jnp.maximum(jnp.sqrt(_head_sum(kk * kk, ones)), 1e-12)
    k2 = k * (1.0 + (a - 1.0) * ka_ref[...])
    r_out[...] = r
    w_out[...] = decay
    k_out[...] = k2
    v_out[...] = v
    kap_out[...] = kk
    bet_out[...] = kk * a
    g_out[...] = g
    bon_out[...] = _head_sum(r * k2 * rk_ref[...], ones) * v


def _rwkv_prep(hr, prev8, prev_map, tm, tiles_per_seq, params):
    rows = hr.shape[0]
    out = jax.ShapeDtypeStruct((rows, RWKV_DIM), F32)
    return pl.pallas_call(
        functools.partial(_rwkv_prep_body, tiles_per_seq=tiles_per_seq),
        grid=(rows // tm,),
        in_specs=[_row_spec(tm, RWKV_COLS), pl.BlockSpec((1, 8, RWKV_COLS), prev_map)]
        + [_full_spec(p.shape) for p in params],
        out_specs=[_row_spec(tm, RWKV_DIM)] * 8,
        out_shape=[out] * 8,
        compiler_params=_cparams("parallel"),
        name="rwkv_prep",
    )(hr, prev8, *params)


def _to_chains_body(x_ref, o_ref, st_ref, *, nb, n_out, offsets):
    for b in range(nb):
        st_ref[b] = x_ref[b].T
    for j in range(n_out):
        groups = [st_ref[b, off * RWKV_HEADS:(off + 1) * RWKV_HEADS, :] for off in offsets(j) for b in range(nb)]
        o_ref[:, j * LANES:(j + 1) * LANES] = jnp.concatenate(groups, axis=0).T


def _to_chains_pallas(x, n_out, offsets):
    nb, t, _ = x.shape
    tt = LANES
    return pl.pallas_call(
        functools.partial(_to_chains_body, nb=nb, n_out=n_out, offsets=offsets),
        grid=(t // tt,),
        in_specs=[pl.BlockSpec((nb, tt, RWKV_DIM), lambda i: (0, i, 0))],
        out_specs=pl.BlockSpec((tt, n_out * LANES), lambda i: (i, 0)),
        out_shape=jax.ShapeDtypeStruct((t, n_out * LANES), F32),
        scratch_shapes=[pltpu.VMEM((nb, RWKV_DIM, tt), F32)],
        compiler_params=_cparams("parallel"),
        name="to_chains",
    )(x).reshape(t, n_out, LANES)


def _from_chains_body(o_ref, x_ref, st_ref, *, nb, nv, dup):
    nh = RWKV_HEADS
    for vp in range(nv):
        tile = o_ref[:, vp * LANES:(vp + 1) * LANES].T
        for vh in range(dup):
            ch = vh * nv + vp
            for b in range(nb):
                r0 = (vh * nb + b) * nh
                st_ref[b, ch * nh:(ch + 1) * nh, :] = tile[r0:r0 + nh, :]
    for b in range(nb):
        x_ref[b] = st_ref[b].T


def _from_chains_pallas(o, nb):
    t, nv, _ = o.shape
    dup = HEAD_DIM // nv
    tt = LANES
    return pl.pallas_call(
        functools.partial(_from_chains_body, nb=nb, nv=nv, dup=dup),
        grid=(t // tt,),
        in_specs=[pl.BlockSpec((tt, nv * LANES), lambda i: (i, 0))],
        out_specs=pl.BlockSpec((nb, tt, RWKV_DIM), lambda i: (0, i, 0)),
        out_shape=jax.ShapeDtypeStruct((nb, t, RWKV_DIM), F32),
        scratch_shapes=[pltpu.VMEM((nb, RWKV_DIM, tt), F32)],
        compiler_params=_cparams("parallel"),
        name="from_chains",
    )(o.reshape(t, nv * LANES))


def _rwkv_scan_body(w_ref, kap_ref, bet_ref, k2_ref, r_ref, v_ref, s0_ref, o_ref, s_ref, *, tb, nv):
    @pl.when(pl.program_id(0) == 0)
    def _():
        s_ref[...] = s0_ref[...]

    packed = w_ref.shape[1] < HEAD_DIM
    low = lax.broadcasted_iota(jnp.int32, (w_ref.shape[1], LANES), 1) < LANES // 2

    def operand(ref, t):
        x = ref[t]
        if not packed:
            return x
        swapped = pltpu.roll(x, LANES // 2, axis=1)
        return jnp.concatenate([jnp.where(low, x, swapped), jnp.where(low, swapped, x)], axis=0)

    def operands(t):
        return tuple(operand(ref, t) for ref in (w_ref, kap_ref, bet_ref, k2_ref, r_ref))

    def token(t, current):
        following = operands(jnp.minimum(t + 1, tb - 1))
        w, kap, bet, k2, r = current
        for vp in range(nv):
            s = s_ref[vp]
            rho = jnp.sum(s * kap, axis=0, keepdims=True)
            sn = s * w - bet * rho + k2 * v_ref[t, vp:vp + 1, :]
            s_ref[vp] = sn
            o_ref[t, vp:vp + 1, :] = jnp.sum(sn * r, axis=0, keepdims=True)
        return following

    lax.fori_loop(0, tb, token, operands(0))


def _rwkv_scan(w, kap, bet, k2, r, v, s0):
    t = w.shape[0]
    nv = v.shape[1]
    tb = min(SCAN_TOKENS, t)
    op_spec = pl.BlockSpec((tb, w.shape[1], LANES), lambda i: (i, 0, 0))
    v_spec = pl.BlockSpec((tb, nv, LANES), lambda i: (i, 0, 0))
    s_spec = pl.BlockSpec((nv, HEAD_DIM, LANES), lambda i: (0, 0, 0))
    return pl.pallas_call(
        functools.partial(_rwkv_scan_body, tb=tb, nv=nv),
        grid=(t // tb,),
        in_specs=[op_spec] * 5 + [v_spec, s_spec],
        out_specs=[v_spec, s_spec],
        out_shape=[jax.ShapeDtypeStruct((t, nv, LANES), F32), jax.ShapeDtypeStruct((nv, HEAD_DIM, LANES), F32)],
        compiler_params=_cparams("arbitrary"),
        name="rwkv_scan",
    )(w, kap, bet, k2, r, v, s0)


def _outproj_even_body(x_ref, a_ref, o_ref, bon_ref, g_ref, lng_ref, lnb_ref, w_ref, out_ref):
    ones = _head_ones()
    o = o_ref[...]
    d = o - _head_sum(o, ones) * (1.0 / HEAD_DIM)
    var = _head_sum(d * d, ones) * (1.0 / HEAD_DIM)
    y = d * lax.rsqrt(var + RWKV_GN_EPS) * lng_ref[...] + lnb_ref[...]
    b = (y + bon_ref[...]) * g_ref[...]
    out_ref[...] = (x_ref[...] + _dot(a_ref[...].astype(BF16), w_ref[0:FOX_DIM, :])
                    + _dot(b.astype(BF16), w_ref[FOX_DIM:, :]))


def _outproj_even(x, a, o, bon, g, lng, lnb, w):
    rows = x.shape[0]
    tm = min(ROW_TILE, rows)
    return pl.pallas_call(
        _outproj_even_body,
        grid=(rows // tm,),
        in_specs=[_row_spec(tm, D_MODEL)] + [_row_spec(tm, FOX_DIM)] * 4
        + [_full_spec(lng.shape), _full_spec(lnb.shape), _full_spec(w.shape)],
        out_specs=_row_spec(tm, D_MODEL),
        out_shape=jax.ShapeDtypeStruct(x.shape, F32),
        compiler_params=_cparams("parallel"),
        name="outproj_even",
    )(x, a, o, bon, g, lng, lnb, w)


def _rope(x, cos, sin_up, sin_dn):
    return x * cos + pltpu.roll(x, ROPE_DIM // 2, axis=1) * sin_up + pltpu.roll(x, LANES - ROPE_DIM // 2, axis=1) * sin_dn


def _inproj_odd_body(x_ref, g_ref, w_ref, qn_ref, kn_ref, vn_ref, cos_ref, sup_ref, sdn_ref,
                     q_ref, k_ref, v_ref, u_ref, gv_ref):
    n = _rms(x_ref[...], g_ref[...]).astype(BF16)
    cos, sup, sdn = cos_ref[...], sup_ref[...], sdn_ref[...]
    ones64 = _group_ones(HEAD_DIM)
    qw = SWA_Q
    for j in range(SWA_Q // LANES):
        cols = slice(j * LANES, (j + 1) * LANES)
        hq = _dot(n, w_ref[:, cols])
        qh = hq * lax.rsqrt(_group_sum(hq * hq, ones64) * (1.0 / HEAD_DIM) + NORM_EPS) * qn_ref[:, cols]
        q_ref[:, cols] = (_rope(qh, cos, sup, sdn) * ATT_SCALE).astype(BF16)
    hk = _dot(n, w_ref[:, qw:qw + SWA_KV])
    kn = hk * lax.rsqrt(_group_sum(hk * hk, ones64) * (1.0 / HEAD_DIM) + NORM_EPS) * kn_ref[...]
    k_ref[...] = _rope(kn, cos, sup, sdn)
    v_ref[...] = _dot(n, w_ref[:, qw + SWA_KV:qw + 2 * SWA_KV])
    s0 = qw + 2 * SWA_KV
    u_ref[...] = _gelu_tanh(_dot(n, w_ref[:, s0:s0 + SGU_DIM]))
    gv = _gelu_tanh(_dot(n, w_ref[:, s0 + SGU_DIM:]))
    gv_ref[...] = _rms(gv, vn_ref[...])


def _inproj_odd(x, g, w, qn, kn, vn, cos, sup, sdn, table_blocks):
    rows = x.shape[0]
    tm = min(ROW_TILE, rows)
    tab = pl.BlockSpec((tm, LANES), lambda i: (i % table_blocks, 0))
    outs = [jax.ShapeDtypeStruct((rows, SWA_Q), BF16), jax.ShapeDtypeStruct((rows, SWA_KV), F32),
            jax.ShapeDtypeStruct((rows, SWA_KV), F32), jax.ShapeDtypeStruct((rows, SGU_DIM), F32),
            jax.ShapeDtypeStruct((rows, SGU_DIM), F32)]
    return pl.pallas_call(
        _inproj_odd_body,
        grid=(rows // tm,),
        in_specs=[_row_spec(tm, D_MODEL), _full_spec(g.shape), _full_spec(w.shape), _full_spec(qn.shape),
                  _full_spec(kn.shape), _full_spec(vn.shape), tab, tab, tab],
        out_specs=[_row_spec(tm, SWA_Q), _row_spec(tm, SWA_KV), _row_spec(tm, SWA_KV),
                   _row_spec(tm, SGU_DIM), _row_spec(tm, SGU_DIM)],
        out_shape=outs,
        compiler_params=_cparams("parallel"),
        name="inproj_odd",
    )(x, g, w, qn, kn, vn, cos, sup, sdn)


def _sink_attend(q, kb, vb, sink, visible):
    s = jnp.where(visible, _dot_nt(q, kb), NEG_BIG)
    m = jnp.maximum(jnp.max(s, axis=1, keepdims=True), sink)
    p = jnp.exp(s - m)
    l = jnp.sum(p, axis=1, keepdims=True) + jnp.exp(sink - m)
    return _dot(p.astype(BF16), vb) / l


def _swa_body(sink_ref, q_ref, kp_ref, kc_ref, vp_ref, vc_ref, mask_ref, o_ref):
    kw = jnp.concatenate([kp_ref[0], kc_ref[0]], axis=0)
    vw = jnp.concatenate([vp_ref[0], vc_ref[0]], axis=0)
    ks = (kw.astype(BF16), pltpu.roll(kw, HEAD_DIM, axis=1).astype(BF16))
    vs = (vw.astype(BF16), pltpu.roll(vw, HEAD_DIM, axis=1).astype(BF16))
    visible = mask_ref[0] > 0.5
    half = _lane_half((q_ref.shape[1], LANES))
    for j in range(SWA_Q // LANES):
        cols = slice(j * LANES, (j + 1) * LANES)
        qb = q_ref[0, :, cols]
        outs = []
        for e in range(2):
            head = 2 * j + e
            swapped = int(head // SWA_GROUP != e)
            qh = jnp.where(half if e == 0 else jnp.logical_not(half), qb, jnp.zeros_like(qb))
            outs.append(_sink_attend(qh, ks[swapped], vs[swapped], sink_ref[head], visible))
        o_ref[0, :, cols] = jnp.where(half, outs[0], outs[1]).astype(BF16)


def _swa(sinks, q, k_prev, k_cur, v_prev, v_cur, mask, tq, prev_rows, prev_map, mask_map):
    b, t, qw = q.shape
    nk = prev_rows + tq
    cur = lambda bi, i: (bi, i, 0)
    return pl.pallas_call(
        _swa_body,
        grid=(b, t // tq),
        in_specs=[pl.BlockSpec(memory_space=pltpu.SMEM),
                  pl.BlockSpec((1, tq, qw), cur),
                  pl.BlockSpec((1, prev_rows, SWA_KV), prev_map), pl.BlockSpec((1, tq, SWA_KV), cur),
                  pl.BlockSpec((1, prev_rows, SWA_KV), prev_map), pl.BlockSpec((1, tq, SWA_KV), cur),
                  pl.BlockSpec((1, tq, nk), mask_map)],
        out_specs=pl.BlockSpec((1, tq, qw), cur),
        out_shape=jax.ShapeDtypeStruct((b, t, qw), BF16),
        compiler_params=_cparams("parallel", "parallel"),
        name="swa",
    )(sinks, q, k_prev, k_cur, v_prev, v_cur, mask)


def _sgu_body(u_ref, v_ref, w_ref, b_ref, o_ref, *, length, n_chunks):
    tril = (lax.broadcasted_iota(jnp.int32, (length, length), 1)
            <= lax.broadcasted_iota(jnp.int32, (length, length), 0))
    ws = [jnp.where(tril, w_ref[g], 0.0).astype(BF16) for g in range(SGU_GROUPS)]
    half = _lane_half((length, LANES))
    for c in range(n_chunks):
        rows = slice(c * length, (c + 1) * length)
        for j in range(SGU_GROUPS // 2):
            cols = slice(j * LANES, (j + 1) * LANES)
            vb = v_ref[rows, cols].astype(BF16)
            mixed = jnp.where(half, _dot(ws[2 * j], vb), _dot(ws[2 * j + 1], vb)) + b_ref[:, cols]
            o_ref[rows, cols] = u_ref[rows, cols] * mixed


def _sgu(u, v, w, bias, length, n_chunks):
    rows = u.shape[0]
    tm = length * n_chunks
    return pl.pallas_call(
        functools.partial(_sgu_body, length=length, n_chunks=n_chunks),
        grid=(rows // tm,),
        in_specs=[_row_spec(tm, SGU_DIM), _row_spec(tm, SGU_DIM), _full_spec(w.shape), _full_spec(bias.shape)],
        out_specs=_row_spec(tm, SGU_DIM),
        out_shape=jax.ShapeDtypeStruct(u.shape, F32),
        compiler_params=_cparams("parallel"),
        name="sgu",
    )(u, v, w, bias)


def _outproj_odd_body(x_ref, c_ref, d_ref, w_ref, out_ref):
    out_ref[...] = (x_ref[...] + _dot(c_ref[...], w_ref[0:SWA_Q, :])
                    + _dot(d_ref[...].astype(BF16), w_ref[SWA_Q:, :]))


def _outproj_odd(x, c, d, w):
    rows = x.shape[0]
    tm = min(ROW_TILE, rows)
    return pl.pallas_call(
        _outproj_odd_body,
        grid=(rows // tm,),
        in_specs=[_row_spec(tm, D_MODEL), _row_spec(tm, SWA_Q), _row_spec(tm, SGU_DIM), _full_spec(w.shape)],
        out_specs=_row_spec(tm, D_MODEL),
        out_shape=jax.ShapeDtypeStruct(x.shape, F32),
        compiler_params=_cparams("parallel"),
        name="outproj_odd",
    )(x, c, d, w)


def _memkv_body(m_ref, g_ref, w_ref, kn_ref, k_ref, v_ref):
    n = _rms(m_ref[...], g_ref[...]).astype(BF16)
    ones = _group_ones(HEAD_DIM)
    hk = _dot(n, w_ref[:, 0:MEM_DIM])
    k_ref[...] = hk * lax.rsqrt(_group_sum(hk * hk, ones) * (1.0 / HEAD_DIM) + NORM_EPS) * kn_ref[...]
    v_ref[...] = _dot(n, w_ref[:, MEM_DIM:])


def _memkv(mem, g, w, kn):
    rows = mem.shape[0]
    tm = min(ROW_TILE, rows)
    out = jax.ShapeDtypeStruct((rows, MEM_DIM), F32)
    return pl.pallas_call(
        _memkv_body,
        grid=(rows // tm,),
        in_specs=[_row_spec(tm, D_MODEL), _full_spec(g.shape), _full_spec(w.shape), _full_spec(kn.shape)],
        out_specs=[_row_spec(tm, MEM_DIM)] * 2,
        out_shape=[out, out],
        compiler_params=_cparams("parallel"),
        name="memkv",
    )(mem, g, w, kn)


def _xattn_body(x_ref, g_ref, wq_ref, qn_ref, mk_ref, mv_ref, wo_ref, out_ref):
    x = x_ref[0]
    tq = x.shape[0]
    n = _rms(x, g_ref[...]).astype(BF16)
    hq = _dot(n, wq_ref[...])
    ones = _group_ones(HEAD_DIM)
    q = (hq * lax.rsqrt(_group_sum(hq * hq, ones) * (1.0 / HEAD_DIM) + NORM_EPS) * qn_ref[...] * ATT_SCALE).astype(BF16)
    half = _lane_half((tq, LANES))
    blocks = []
    for j in range(MEM_HEADS // 2):
        cols = slice(j * LANES, (j + 1) * LANES)
        kb = mk_ref[0, :, cols].astype(BF16)
        vb = mv_ref[0, :, cols].astype(BF16)
        qb = q[:, cols]
        outs = []
        for h in range(2):
            sel = half if h == 0 else jnp.logical_not(half)
            s = _dot_nt(jnp.where(sel, qb, jnp.zeros_like(qb)), kb)
            p = jnp.exp(s - jnp.max(s, axis=1, keepdims=True))
            outs.append(_dot(p.astype(BF16), vb) / jnp.sum(p, axis=1, keepdims=True))
        blocks.append(jnp.where(half, outs[0], outs[1]))
    o = jnp.concatenate(blocks, axis=1).astype(BF16)
    out_ref[0] = x + _dot(o, wo_ref[...])


def _xattn(x, g, wq, qn, mk, mv, wo, tq):
    b, t, _ = x.shape
    m = mk.shape[1]
    return pl.pallas_call(
        _xattn_body,
        grid=(b, t // tq),
        in_specs=[pl.BlockSpec((1, tq, D_MODEL), lambda bi, i: (bi, i, 0)), _full_spec(g.shape),
                  _full_spec(wq.shape), _full_spec(qn.shape),
                  pl.BlockSpec((1, m, MEM_DIM), lambda bi, i: (bi, 0, 0)),
                  pl.BlockSpec((1, m, MEM_DIM), lambda bi, i: (bi, 0, 0)), _full_spec(wo.shape)],
        out_specs=pl.BlockSpec((1, tq, D_MODEL), lambda bi, i: (bi, i, 0)),
        out_shape=jax.ShapeDtypeStruct(x.shape, F32),
        compiler_params=_cparams("parallel", "parallel"),
        name="xattn",
    )(x, g, wq, qn, mk, mv, wo)


def _to_chains(x, b, t, dup):
    y = x.reshape(b, t, HEAD_DIM, RWKV_HEADS).transpose(1, 2, 0, 3).reshape(t, HEAD_DIM, b * RWKV_HEADS)
    return jnp.concatenate([y] * dup, axis=-1) if dup > 1 else y


def _rope_tables(pos):
    half = ROPE_DIM // 2
    inv_freq = jnp.power(ROPE_THETA, -jnp.arange(half, dtype=F32) / half)
    ang = pos.astype(F32)[:, None] * inv_freq[None, :]
    cos, sin = jnp.cos(ang), jnp.sin(ang)
    n = pos.shape[0]
    pad = jnp.zeros((n, HEAD_DIM - ROPE_DIM), F32)
    zero = jnp.zeros((n, half), F32)
    cos_t = jnp.concatenate([cos, cos, pad + 1.0], axis=1)
    up_t = jnp.concatenate([zero, sin, pad], axis=1)
    dn_t = jnp.concatenate([-sin, zero, pad], axis=1)
    two = lambda a: jnp.concatenate([a, a], axis=1)
    return two(cos_t), two(up_t), two(dn_t)


def _swa_prompt_mask(tq):
    span = WINDOW + tq
    qc = np.arange(tq)[:, None] // CHUNK
    kc = np.arange(span)[None, :] // CHUNK - WINDOW_CHUNKS
    band = (kc <= qc) & (kc >= qc - WINDOW_CHUNKS)
    first = band & (kc >= 0)
    return jnp.asarray(np.stack([first, band]).astype(np.float32))


def _swa_sample_mask(past, rows, t):
    kc = (past - rows + np.arange(rows + t)) // CHUNK
    qc = (past + np.arange(t)) // CHUNK
    m = (kc[None, :] <= qc[:, None]) & (kc[None, :] >= qc[:, None] - WINDOW_CHUNKS)
    return jnp.asarray(m[None].astype(np.float32))


def kernel(x_prompt, x_sample, cache_fox_k, cache_fox_v, cache_fox_logf, state_rwkv, state_rwkv_shift, cache_swa_k, cache_swa_v, cache_mem_k, cache_mem_v, mem_prompt, ffn1_norm, ffn1_w_gate, ffn1_w_up, ffn1_w_down, mix_norm, ev_w_in, fox_b_f, fox_q_norm, fox_k_norm, rwkv_mu, rwkv_w0, rwkv_w2, rwkv_a0, rwkv_a2, rwkv_g2, rwkv_k_k, rwkv_k_a, rwkv_r_k, rwkv_ln_g, rwkv_ln_b, ev_w_out, od_w_in, swa_q_norm, swa_k_norm, swa_sinks, sgu_v_norm, sgu_w_s, sgu_b, od_w_out, xattn_norm, mem_norm, xattn_wq, xattn_wkv, xattn_q_norm, xattn_k_norm, xattn_wo, ffn2_norm, ffn2_w_gate, ffn2_w_up, ffn2_w_down):
    bp, tp, _ = x_prompt.shape
    bs, ts, _ = x_sample.shape
    depth = ffn1_norm.shape[0]
    past = cache_fox_k.shape[2]
    mem_tokens = mem_prompt.shape[1]
    xp = x_prompt.reshape(bp * tp, D_MODEL)
    xs = x_sample.reshape(bs * ts, D_MODEL)
    mem_rows = mem_prompt.reshape(bp * mem_tokens, D_MODEL)
    row = lambda a: a.reshape(1, -1)
    tile_heads = lambda a, n: jnp.tile(a, n).reshape(1, -1)

    out = {k: [] for k in ("p_fox_k", "p_fox_v", "p_fox_logf", "p_rwkv_state", "p_rwkv_shift", "p_swa_k", "p_swa_v",
                           "p_mem_k", "p_mem_v", "s_fox_k", "s_fox_v", "s_fox_logf", "s_rwkv_state", "s_rwkv_shift",
                           "s_swa_k", "s_swa_v", "s_sgu_v")}

    for l in range(depth):
        f1 = _prep_ffn(ffn1_norm[l], ffn1_w_gate[l], ffn1_w_up[l], ffn1_w_down[l])
        xp = _ffn(xp, *f1)
        xs = _ffn(xs, *f1)
        g_mix = row(mix_norm[l])
        if l % 2 == 0:
            e = l // 2
            w_in = ev_w_in[e]
            f0 = 3 * FOX_DIM
            w_cat = jnp.concatenate([w_in[:, :f0], jnp.pad(w_in[:, f0:f0 + FOX_HEADS], ((0, 0), (0, LANES - FOX_HEADS))),
                                     w_in[:, f0 + FOX_HEADS:][:, RWKV_COL_PERM]], axis=1).astype(BF16)
            bf = jnp.pad(fox_b_f[e], (0, LANES - FOX_HEADS)).reshape(1, LANES)
            qn = tile_heads(fox_q_norm[e], FOX_HEADS)
            kn = tile_heads(fox_k_norm[e], FOX_HEADS)
            qp, ktp, vtp, lfp, hrp = _inproj_even(xp, g_mix, w_cat, bf, qn, kn, seq_len=tp)
            qs, ks, vs, lfs, hrs = _inproj_even(xs, g_mix, w_cat, bf, qn, kn)

            cum_p, cumt_p = _cumsum(lfp.reshape(bp, tp, LANES), CUM_TILE)
            a_p = _fox_prompt(qp.reshape(bp, tp, FOX_DIM), ktp, vtp, cum_p, cumt_p, ATT_TILE)
            lfs8 = lfs.reshape(bs, ts, LANES)[:, :, :FOX_HEADS]
            tot = past + ts
            padded = -(-tot // CUM_TILE) * CUM_TILE
            lf_all = jnp.concatenate([cache_fox_logf[e].astype(F32), lfs8], axis=1)
            lf_all = jnp.pad(lf_all, ((0, 0), (0, padded - tot), (0, LANES - FOX_HEADS)))
            cum_s, cumt_s = _cumsum(lf_all, CUM_TILE)
            token_minor = lambda c: jnp.transpose(c, (0, 2, 3, 1)).reshape(bs, FOX_DIM, past)
            a_s = _fox_sample(qs.reshape(bs, ts, FOX_DIM), ks.reshape(bs, ts, FOX_DIM), vs.reshape(bs, ts, FOX_DIM),
                              token_minor(cache_fox_k[e]), token_minor(cache_fox_v[e]), cum_s, cumt_s)

            perm = RWKV_HEAD_PERM
            w2p = jnp.pad(rwkv_w2[e][:, perm], ((0, LANES - DECAY_LORA), (0, 0))).astype(BF16)
            a2p = jnp.pad(rwkv_a2[e][:, perm], ((DECAY_LORA, 0), (0, 0))).astype(BF16)
            params = (row(rwkv_mu[e][RWKV_COL_PERM]), row(rwkv_w0[e][perm]), w2p, row(rwkv_a0[e][perm]), a2p,
                      rwkv_g2[e][:, perm].astype(BF16), row(rwkv_k_k[e][perm]), row(rwkv_k_a[e][perm]),
                      row(rwkv_r_k[e].reshape(-1)[perm]))
            tiles_per_seq = tp // ROW_TILE
            prev_p = hrp.reshape(bp * tp // 8, 8, RWKV_COLS)
            prep_p = _rwkv_prep(hrp, prev_p, lambda i: (jnp.maximum(i * (ROW_TILE // 8) - 1, 0), 0, 0),
                                ROW_TILE, tiles_per_seq, params)
            prev_s = jnp.pad(state_rwkv_shift[e].astype(F32)[:, :, RWKV_COL_PERM], ((0, 0), (7, 0), (0, 0)))
            prep_s = _rwkv_prep(hrs, prev_s, lambda i: (i, 0, 0), ts, 0, params)

            def scan(prep, b, t, state0):
                r, w, k2, v, kap, bet, g, bon = prep
                dup = LANES // (b * RWKV_HEADS)
                nv = HEAD_DIM // dup
                in_kernel = t % LANES == 0
                if in_kernel:
                    halves = lambda j: tuple(half * nv + j for half in range(dup))
                    ops = [_to_chains_pallas(a.reshape(b, t, RWKV_DIM), nv, halves) for a in (w, kap, bet, k2, r)]
                    vt = _to_chains_pallas(v.reshape(b, t, RWKV_DIM), nv, halves)
                else:
                    ops = [_to_chains(a, b, t, dup) for a in (w, kap, bet, k2, r)]
                    vt = v.reshape(b, t, dup, nv, RWKV_HEADS).transpose(1, 3, 2, 0, 4).reshape(t, nv, LANES)
                s0 = state0.reshape(b, RWKV_HEADS, dup, nv, HEAD_DIM).transpose(3, 4, 2, 0, 1).reshape(nv, HEAD_DIM, LANES)
                o, sT = _rwkv_scan(*ops, vt, s0)
                if in_kernel:
                    o = _from_chains_pallas(o, b).reshape(b * t, RWKV_DIM)
                else:
                    o = o.reshape(t, nv, dup, b, RWKV_HEADS).transpose(3, 0, 2, 1, 4).reshape(b * t, RWKV_DIM)
                sT = sT.reshape(nv, HEAD_DIM, dup, b, RWKV_HEADS).transpose(3, 4, 2, 0, 1).reshape(b, RWKV_HEADS, HEAD_DIM, HEAD_DIM)
                return o, sT, g, bon

            o_p, st_p, g_p, bon_p = scan(prep_p, bp, tp, jnp.zeros((bp, RWKV_HEADS, HEAD_DIM, HEAD_DIM), F32))
            o_s, st_s, g_s, bon_s = scan(prep_s, bs, ts, state_rwkv[e].astype(F32))

            w_out = jnp.concatenate([ev_w_out[e][:FOX_DIM], ev_w_out[e][FOX_DIM:][perm]], axis=0).astype(BF16)
            lng, lnb = row(rwkv_ln_g[e][perm]), row(rwkv_ln_b[e][perm])
            xp = _outproj_even(xp, a_p.reshape(bp * tp, FOX_DIM), o_p, bon_p, g_p, lng, lnb, w_out)
            xs = _outproj_even(xs, a_s.reshape(bs * ts, FOX_DIM), o_s, bon_s, g_s, lng, lnb, w_out)

            rows_of = lambda a: jnp.transpose(a.reshape(bp, FOX_HEADS, HEAD_DIM, tp), (0, 3, 1, 2))
            out["p_fox_k"].append(rows_of(ktp))
            out["p_fox_v"].append(rows_of(vtp))
            out["p_fox_logf"].append(lfp.reshape(bp, tp, LANES)[:, :, :FOX_HEADS])
            out["p_rwkv_state"].append(st_p)
            out["p_rwkv_shift"].append(hrp.reshape(bp, tp, RWKV_COLS)[:, -1:][:, :, RWKV_COL_UNPERM])
            out["s_fox_k"].append(ks.reshape(bs, ts, FOX_HEADS, HEAD_DIM))
            out["s_fox_v"].append(vs.reshape(bs, ts, FOX_HEADS, HEAD_DIM))
            out["s_fox_logf"].append(lfs8)
            out["s_rwkv_state"].append(st_s)
            out["s_rwkv_shift"].append(hrs.reshape(bs, ts, RWKV_COLS)[:, -1:][:, :, RWKV_COL_UNPERM])
        else:
            j = l // 2
            w_cat = od_w_in[j].astype(BF16)
            qn = tile_heads(swa_q_norm[j], SWA_HEADS)
            kn = tile_heads(swa_k_norm[j], SWA_KV_HEADS)
            vn = row(sgu_v_norm[j])
            tabs_p = _rope_tables(jnp.arange(tp))
            tabs_s = _rope_tables(past + jnp.arange(bs * ts) % ts)
            qp, kp, vp, up, gp = _inproj_odd(xp, g_mix, w_cat, qn, kn, vn, *tabs_p, tp // ROW_TILE)
            qs, ks, vs, us, gs = _inproj_odd(xs, g_mix, w_cat, qn, kn, vn, *tabs_s, 1)

            qw = SWA_Q
            kp3, vp3 = kp.reshape(bp, tp, SWA_KV), vp.reshape(bp, tp, SWA_KV)
            ratio = SWA_TILE // WINDOW
            c_p = _swa(swa_sinks[j], qp.reshape(bp, tp, qw), kp3, kp3, vp3, vp3, _swa_prompt_mask(SWA_TILE),
                       SWA_TILE, WINDOW, lambda bi, i: (bi, jnp.maximum(i * ratio - 1, 0), 0),
                       lambda bi, i: (jnp.minimum(i, 1), 0, 0))
            rows_c = cache_swa_k.shape[2]
            ck3 = cache_swa_k[j].reshape(bs, rows_c, SWA_KV)
            cv3 = cache_swa_v[j].reshape(bs, rows_c, SWA_KV)
            ks3, vs3 = ks.reshape(bs, ts, SWA_KV), vs.reshape(bs, ts, SWA_KV)
            c_s = _swa(swa_sinks[j], qs.reshape(bs, ts, qw), ck3, ks3, cv3, vs3, _swa_sample_mask(past, rows_c, ts),
                       ts, rows_c, lambda bi, i: (bi, 0, 0), lambda bi, i: (0, 0, 0))

            bias = jnp.repeat(jnp.transpose(sgu_b[j]), HEAD_DIM, axis=1)
            d_p = _sgu(up, gp, sgu_w_s[j], bias, SGU_CHUNK, ROW_TILE // SGU_CHUNK)
            d_s = _sgu(us, gs, sgu_w_s[j][:, :ts, :ts], bias[:ts], ts, 1)

            w_out = od_w_out[j].astype(BF16)
            xp = _outproj_odd(xp, c_p.reshape(bp * tp, qw), d_p, w_out)
            xs = _outproj_odd(xs, c_s.reshape(bs * ts, qw), d_s, w_out)

            out["p_swa_k"].append(kp3[:, -WINDOW:].reshape(bp, WINDOW, SWA_KV_HEADS, HEAD_DIM))
            out["p_swa_v"].append(vp3[:, -WINDOW:].reshape(bp, WINDOW, SWA_KV_HEADS, HEAD_DIM))
            out["s_swa_k"].append(jnp.concatenate([ck3, ks3], axis=1)[:, -rows_c:].reshape(bs, rows_c, SWA_KV_HEADS, HEAD_DIM))
            out["s_swa_v"].append(jnp.concatenate([cv3, vs3], axis=1)[:, -rows_c:].reshape(bs, rows_c, SWA_KV_HEADS, HEAD_DIM))
            out["s_sgu_v"].append(gs.reshape(bs, ts, SGU_DIM))

        mk, mv = _memkv(mem_rows, row(mem_norm[l]), xattn_wkv[l].astype(BF16), tile_heads(xattn_k_norm[l], MEM_HEADS))
        mk3, mv3 = mk.reshape(bp, mem_tokens, MEM_DIM), mv.reshape(bp, mem_tokens, MEM_DIM)
        xa = (row(xattn_norm[l]), xattn_wq[l].astype(BF16), tile_heads(xattn_q_norm[l], MEM_HEADS))
        wo = xattn_wo[l].astype(BF16)
        xp = _xattn(xp.reshape(bp, tp, D_MODEL), *xa, mk3, mv3, wo, ROW_TILE).reshape(bp * tp, D_MODEL)
        xs = _xattn(xs.reshape(bs, ts, D_MODEL), *xa, cache_mem_k[l].reshape(bs, mem_tokens, MEM_DIM),
                    cache_mem_v[l].reshape(bs, mem_tokens, MEM_DIM), wo, ts).reshape(bs * ts, D_MODEL)
        out["p_mem_k"].append(mk3.reshape(bp, mem_tokens, MEM_HEADS, HEAD_DIM))
        out["p_mem_v"].append(mv3.reshape(bp, mem_tokens, MEM_HEADS, HEAD_DIM))

        f2 = _prep_ffn(ffn2_norm[l], ffn2_w_gate[l], ffn2_w_up[l], ffn2_w_down[l])
        xp = _ffn(xp, *f2)
        xs = _ffn(xs, *f2)

    order = ("p_fox_k", "p_fox_v", "p_fox_logf", "p_rwkv_state", "p_rwkv_shift", "p_swa_k", "p_swa_v", "p_mem_k",
             "p_mem_v", "s_fox_k", "s_fox_v", "s_fox_logf", "s_rwkv_state", "s_rwkv_shift", "s_swa_k", "s_swa_v",
             "s_sgu_v")
    return (xp.reshape(bp, tp, D_MODEL), xs.reshape(bs, ts, D_MODEL)) + tuple(jnp.stack(out[k]) for k in order)
```

```python
import functools

import numpy as np
import jax
import jax.numpy as jnp
from jax import lax
from jax.experimental import pallas as pl
from jax.experimental.pallas import tpu as pltpu

F32 = jnp.float32
BF16 = jnp.bfloat16

D_MODEL = 1024
HEAD_DIM = 64
NORM_EPS = 1e-6
ROPE_THETA = 500000.0
ROPE_DIM = HEAD_DIM // 4
CHUNK = 64
FOX_HEADS = 8
FOX_DIM = FOX_HEADS * HEAD_DIM
RWKV_HEADS = 8
RWKV_DIM = RWKV_HEADS * HEAD_DIM
DECAY_LORA = 64
ICLR_LORA = 64
GATE_LORA = 128
RWKV_COLS = 3 * RWKV_DIM + DECAY_LORA + ICLR_LORA + GATE_LORA
RWKV_GN_EPS = 64e-5
SWA_HEADS = 8
SWA_KV_HEADS = 2
SWA_GROUP = SWA_HEADS // SWA_KV_HEADS
SWA_Q = SWA_HEADS * HEAD_DIM
SWA_KV = SWA_KV_HEADS * HEAD_DIM
WINDOW = 128
WINDOW_CHUNKS = WINDOW // CHUNK
SGU_GROUPS = 8
SGU_DIM = SGU_GROUPS * HEAD_DIM
SGU_CHUNK = 128
MEM_HEADS = 4
MEM_DIM = MEM_HEADS * HEAD_DIM
D_FF = 2816

LANES = 128
ROW_TILE = 512
FF_TILE = 256
ATT_TILE = 512
CUM_TILE = 256
SWA_TILE = 256
SCAN_TOKENS = 64
VMEM_LIMIT = 56 * 1024 * 1024
RWKV_HEAD_PERM = np.arange(RWKV_DIM).reshape(RWKV_HEADS, HEAD_DIM).T.reshape(-1)
RWKV_COL_PERM = np.concatenate([RWKV_HEAD_PERM, RWKV_DIM + RWKV_HEAD_PERM, 2 * RWKV_DIM + RWKV_HEAD_PERM,
                                np.arange(3 * RWKV_DIM, RWKV_COLS)])
RWKV_COL_UNPERM = np.argsort(RWKV_COL_PERM)
ATT_SCALE = HEAD_DIM ** -0.5
LOG2E = 1.4426950408889634
NEG_BIG = -1e30


def _cparams(*sem):
    return pltpu.CompilerParams(dimension_semantics=sem, vmem_limit_bytes=VMEM_LIMIT)


def _dot(a, b):
    return jnp.dot(a, b, preferred_element_type=F32)


def _dot_nt(a, b):
    return lax.dot_general(a, b, (((1,), (1,)), ((), ())), preferred_element_type=F32)


def _rms(x, g):
    ms = jnp.mean(x * x, axis=-1, keepdims=True)
    return (x * lax.rsqrt(ms + NORM_EPS)) * g


def _group_ones(group):
    shift = int(np.log2(group))
    r = lax.broadcasted_iota(jnp.int32, (LANES, LANES), 0) >> shift
    c = lax.broadcasted_iota(jnp.int32, (LANES, LANES), 1) >> shift
    return jnp.where(r == c, 1.0, 0.0).astype(BF16)


def _group_sum(x, ones):
    parts = []
    for j in range(x.shape[1] // LANES):
        blk = x[:, j * LANES:(j + 1) * LANES]
        hi = blk.astype(BF16)
        lo = (blk - hi.astype(F32)).astype(BF16)
        parts.append(_dot(hi, ones) + _dot(lo, ones))
    return parts[0] if len(parts) == 1 else jnp.concatenate(parts, axis=1)


def _head_ones():
    r = lax.broadcasted_iota(jnp.int32, (LANES, LANES), 0) & (RWKV_HEADS - 1)
    c = lax.broadcasted_iota(jnp.int32, (LANES, LANES), 1) & (RWKV_HEADS - 1)
    return jnp.where(r == c, 1.0, 0.0).astype(BF16)


def _head_sum(x, ones):
    part = x[:, 0:LANES]
    for j in range(1, x.shape[1] // LANES):
        part = part + x[:, j * LANES:(j + 1) * LANES]
    hi = part.astype(BF16)
    lo = (part - hi.astype(F32)).astype(BF16)
    tot = _dot(hi, ones) + _dot(lo, ones)
    return jnp.concatenate([tot] * (x.shape[1] // LANES), axis=1)


def _log_sigmoid(z):
    return jnp.minimum(z, 0.0) - jnp.log(1.0 + jnp.exp(-jnp.abs(z)))


def _gelu_tanh(x):
    return 0.5 * x * (1.0 + jnp.tanh(0.7978845608028654 * (x + 0.044715 * (x * x * x))))


def _lane_half(shape):
    return lax.broadcasted_iota(jnp.int32, shape, 1) < HEAD_DIM


def _row_spec(tm, cols):
    return pl.BlockSpec((tm, cols), lambda i: (i, 0))


def _full_spec(shape):
    nd = len(shape)
    return pl.BlockSpec(shape, lambda *_: (0,) * nd, pipeline_mode=pl.Buffered(1))


def _ffn_core(x, g, wg_ref, wu_ref, wd_ref, acc_ref):
    n = _rms(x, g).astype(BF16)
    for c in range(wg_ref.shape[1] // FF_TILE):
        cols = slice(c * FF_TILE, (c + 1) * FF_TILE)
        gate = _dot(n, wg_ref[:, cols])
        up = _dot(n, wu_ref[:, cols])
        act = (gate * jax.nn.sigmoid(gate) * up).astype(BF16)
        part = _dot(act, wd_ref[cols, :])
        if c == 0:
            acc_ref[...] = part
        else:
            acc_ref[...] += part
    return x + 0.5 * acc_ref[...]


def _ffn_body(x_ref, g_ref, wg_ref, wu_ref, wd_ref, o_ref, acc_ref):
    o_ref[...] = _ffn_core(x_ref[...], g_ref[...], wg_ref, wu_ref, wd_ref, acc_ref)


def _ffn(x, g, wg, wu, wd):
    rows = x.shape[0]
    tm = min(ROW_TILE, rows)
    return pl.pallas_call(
        _ffn_body,
        grid=(rows // tm,),
        in_specs=[_row_spec(tm, D_MODEL), _full_spec(g.shape), _full_spec(wg.shape),
                  _full_spec(wu.shape), _full_spec(wd.shape)],
        out_specs=_row_spec(tm, D_MODEL),
        out_shape=jax.ShapeDtypeStruct(x.shape, F32),
        scratch_shapes=[pltpu.VMEM((tm, D_MODEL), F32)],
        compiler_params=_cparams("parallel"),
        name="ffn",
    )(x, g, wg, wu, wd)


def _prep_ffn(norm, w_gate, w_up, w_down):
    return norm.reshape(1, D_MODEL), w_gate.astype(BF16), w_up.astype(BF16), w_down.astype(BF16)


def _inproj_even_body(x_ref, g_ref, w_ref, bf_ref, qn_ref, kn_ref,
                      q_ref, k_ref, v_ref, lf_ref, hr_ref, *, token_minor):
    n = _rms(x_ref[...], g_ref[...]).astype(BF16)
    ones = _group_ones(HEAD_DIM)
    hq = _dot(n, w_ref[:, 0:FOX_DIM])
    q = hq * lax.rsqrt(_group_sum(hq * hq, ones) * (1.0 / HEAD_DIM) + NORM_EPS) * qn_ref[...]
    q_ref[...] = (q * (ATT_SCALE * LOG2E)).astype(BF16)
    hk = _dot(n, w_ref[:, FOX_DIM:2 * FOX_DIM])
    k = hk * lax.rsqrt(_group_sum(hk * hk, ones) * (1.0 / HEAD_DIM) + NORM_EPS) * kn_ref[...]
    v = _dot(n, w_ref[:, 2 * FOX_DIM:3 * FOX_DIM])
    if token_minor:
        k_ref[0] = k.T
        v_ref[0] = v.T
    else:
        k_ref[...] = k
        v_ref[...] = v
    f0 = 3 * FOX_DIM
    lf_ref[...] = _log_sigmoid(_dot(n, w_ref[:, f0:f0 + LANES]) + bf_ref[...])
    hr_ref[...] = _dot(n, w_ref[:, f0 + LANES:])


def _inproj_even(x, g, w, bf, qn, kn, seq_len=None):
    rows = x.shape[0]
    tm = min(ROW_TILE, rows)
    if seq_len is None:
        kv_shape = jax.ShapeDtypeStruct((rows, FOX_DIM), F32)
        kv_spec = _row_spec(tm, FOX_DIM)
    else:
        per_seq = seq_len // tm
        kv_shape = jax.ShapeDtypeStruct((rows // seq_len, FOX_DIM, seq_len), F32)
        kv_spec = pl.BlockSpec((1, FOX_DIM, tm), lambda i: (i // per_seq, 0, i % per_seq))
    outs = [jax.ShapeDtypeStruct((rows, FOX_DIM), BF16), kv_shape, kv_shape,
            jax.ShapeDtypeStruct((rows, LANES), F32), jax.ShapeDtypeStruct((rows, RWKV_COLS), F32)]
    return pl.pallas_call(
        functools.partial(_inproj_even_body, token_minor=seq_len is not None),
        grid=(rows // tm,),
        in_specs=[_row_spec(tm, D_MODEL), _full_spec(g.shape), _full_spec(w.shape), _full_spec(bf.shape),
                  _full_spec(qn.shape), _full_spec(kn.shape)],
        out_specs=[_row_spec(tm, FOX_DIM), kv_spec, kv_spec, _row_spec(tm, LANES), _row_spec(tm, RWKV_COLS)],
        out_shape=outs,
        compiler_params=_cparams("parallel"),
        name="inproj_even",
    )(x, g, w, bf, qn, kn)


def _split3(x):
    hi = x.astype(BF16)
    r1 = x - hi.astype(F32)
    mid = r1.astype(BF16)
    lo = (r1 - mid.astype(F32)).astype(BF16)
    return hi, mid, lo


def _cumsum_body(lf_ref, col_ref, row_ref, *, n_chunks, tk):
    r = lax.broadcasted_iota(jnp.int32, (tk, tk), 0)
    c = lax.broadcasted_iota(jnp.int32, (tk, tk), 1)
    tri = jnp.where(c <= r, 1.0, 0.0).astype(BF16)
    carry = jnp.zeros((1, LANES), F32)
    for i in range(n_chunks):
        hi, mid, lo = _split3(lf_ref[0, i * tk:(i + 1) * tk, :])
        cs = _dot(tri, hi) + _dot(tri, mid) + _dot(tri, lo) + carry
        col_ref[0, i * tk:(i + 1) * tk, :] = cs
        row_ref[0, :, i * tk:(i + 1) * tk] = cs.T[0:FOX_HEADS, :]
        carry = cs[tk - 1:tk, :]


def _cumsum(lf, tk):
    b, length, _ = lf.shape
    n_chunks = length // tk
    return pl.pallas_call(
        functools.partial(_cumsum_body, n_chunks=n_chunks, tk=tk),
        grid=(b,),
        in_specs=[pl.BlockSpec((1, length, LANES), lambda i: (i, 0, 0))],
        out_specs=[pl.BlockSpec((1, length, LANES), lambda i: (i, 0, 0)),
                   pl.BlockSpec((1, FOX_HEADS, length), lambda i: (i, 0, 0))],
        out_shape=[jax.ShapeDtypeStruct((b, length, LANES), F32),
                   jax.ShapeDtypeStruct((b, FOX_HEADS, length), F32)],
        compiler_params=_cparams("parallel"),
        name="cumsum",
    )(lf)


def _place3(terms, src_lane, dst_lane):
    r = lax.broadcasted_iota(jnp.int32, (LANES, LANES), 0)
    c = lax.broadcasted_iota(jnp.int32, (LANES, LANES), 1)
    out = None
    for i, t in enumerate(terms):
        sel = jnp.where((r == src_lane) & (c == dst_lane + i), 1.0, 0.0).astype(BF16)
        out = _dot(t, sel) if out is None else out + _dot(t, sel)
    return out


def _fox_q(q_pair, cum_col, head, h):
    n = q_pair.shape[0]
    qf = q_pair.astype(F32)
    if h == 1:
        qf = pltpu.roll(qf, HEAD_DIM, axis=1)
    lane = lax.broadcasted_iota(jnp.int32, (n, LANES), 1)
    ones = jnp.where((lane >= HEAD_DIM + 3) & (lane < HEAD_DIM + 6), 1.0, 0.0)
    aug = _place3(_split3(cum_col * LOG2E), head, HEAD_DIM) + ones
    return jnp.where(lane < HEAD_DIM, qf, aug).astype(BF16)


def _fox_kt(kt, cum_row):
    n = kt.shape[1]
    hi, mid, lo = (x.astype(F32) for x in _split3(cum_row * LOG2E))
    row = lax.broadcasted_iota(jnp.int32, (8, n), 0)
    aug = jnp.where(row < 3, 1.0, jnp.where(row == 3, -hi, jnp.where(row == 4, -mid, jnp.where(row == 5, -lo, 0.0))))
    return jnp.concatenate([kt, aug, jnp.zeros((HEAD_DIM - 8, n), F32)], axis=0).astype(BF16)


def _fox_vt(vt):
    n = vt.shape[1]
    row = lax.broadcasted_iota(jnp.int32, (HEAD_DIM, n), 0)
    return jnp.concatenate([vt, jnp.where(row == 0, 1.0, 0.0)], axis=0).astype(BF16)


def _fox_finish(acc):
    return acc / acc[:, HEAD_DIM:HEAD_DIM + 1]


def _fox_prompt_body(q_ref, cum_ref, cumt_ref, kt_ref, vt_ref, o_ref, ka_sc, va_sc, m_sc, acc_sc, *, tq, t):
    hp = pl.program_id(1)
    i = pl.program_id(2)

    @pl.when(i == 0)
    def _():
        for h in range(2):
            rows = slice(h * HEAD_DIM, (h + 1) * HEAD_DIM)
            cum_row = cumt_ref[0, pl.ds(2 * hp + h, 1), :]
            for c in range(t // tq):
                cols = slice(c * tq, (c + 1) * tq)
                ka_sc[h, c] = _fox_kt(kt_ref[0, rows, cols], cum_row[:, cols])
                va_sc[h, c] = _fox_vt(vt_ref[0, rows, cols])

    off_q = pl.multiple_of(i * tq, tq)
    cq = cum_ref[0, pl.ds(off_q, tq), :]
    qs = [_fox_q(q_ref[0], cq, 2 * hp + h, h) for h in range(2)]
    m_sc[...] = jnp.full(m_sc.shape, NEG_BIG, F32)
    acc_sc[...] = jnp.zeros(acc_sc.shape, F32)
    causal = (lax.broadcasted_iota(jnp.int32, (tq, tq), 1) <= lax.broadcasted_iota(jnp.int32, (tq, tq), 0))

    def step(j, masked):
        for h in range(2):
            s = _dot(qs[h], ka_sc[h, j])
            if masked:
                s = jnp.where(causal, s, NEG_BIG)
            m_old = m_sc[h]
            m_new = jnp.maximum(m_old, jnp.max(s, axis=1, keepdims=True))
            p = jnp.concatenate([jnp.exp2(s[:, c * LANES:(c + 1) * LANES] - m_new) for c in range(tq // LANES)],
                                axis=1).astype(BF16)
            acc_sc[h] = jnp.exp2(m_old - m_new) * acc_sc[h] + _dot_nt(p, va_sc[h, j])
            m_sc[h] = m_new

    def past(j, carry):
        step(j, False)
        return carry

    lax.fori_loop(0, i, past, 0)
    step(i, True)
    second = pltpu.roll(_fox_finish(acc_sc[1]), HEAD_DIM, axis=1)
    o_ref[0] = jnp.where(_lane_half((tq, LANES)), _fox_finish(acc_sc[0]), second)


def _fox_prompt(q, kt, vt, cum, cumt, tq):
    b, t, _ = q.shape
    pairs = FOX_HEADS // 2
    nblk = t // tq
    return pl.pallas_call(
        functools.partial(_fox_prompt_body, tq=tq, t=t),
        grid=(b, pairs, nblk),
        in_specs=[pl.BlockSpec((1, tq, LANES), lambda bi, hp, i: (bi, i, hp)),
                  pl.BlockSpec((1, t, LANES), lambda bi, hp, i: (bi, 0, 0)),
                  pl.BlockSpec((1, FOX_HEADS, t), lambda bi, hp, i: (bi, 0, 0)),
                  pl.BlockSpec((1, LANES, t), lambda bi, hp, i: (bi, hp, 0)),
                  pl.BlockSpec((1, LANES, t), lambda bi, hp, i: (bi, hp, 0))],
        out_specs=pl.BlockSpec((1, tq, LANES), lambda bi, hp, i: (bi, i, hp)),
        out_shape=jax.ShapeDtypeStruct((b, t, FOX_DIM), F32),
        scratch_shapes=[pltpu.VMEM((2, nblk, LANES, tq), BF16), pltpu.VMEM((2, nblk, LANES, tq), BF16),
                        pltpu.VMEM((2, tq, LANES), F32), pltpu.VMEM((2, tq, LANES), F32)],
        compiler_params=_cparams("parallel", "parallel", "arbitrary"),
        name="fox_prompt",
    )(q, cum, cumt, kt, vt)


def _fox_sample_body(q_ref, cum_ref, cumt_ref, kt_ref, vt_ref, kn_ref, vn_ref, o_ref, *, ts, past):
    cum_n = cum_ref[0, 0:ts, :] * LOG2E
    causal = (lax.broadcasted_iota(jnp.int32, (ts, ts), 1) <= lax.broadcasted_iota(jnp.int32, (ts, ts), 0))
    lane = lax.broadcasted_iota(jnp.int32, (ts, LANES), 1)
    q = q_ref[0].astype(F32)
    kn = kn_ref[0]
    vn = vn_ref[0]
    outs = []
    for h in range(FOX_HEADS):
        cols = slice(h * HEAD_DIM, (h + 1) * HEAD_DIM)
        cq = jnp.sum(jnp.where(lane == h, cum_n, 0.0), axis=1, keepdims=True)
        ck_p = cumt_ref[0, h:h + 1, 0:past] * LOG2E
        ck_n = cumt_ref[0, h:h + 1, past:past + ts] * LOG2E
        qh = q[:, cols].astype(BF16)
        s_p = _dot(qh, kt_ref[0, cols, :].astype(BF16)) + (cq - ck_p)
        s_n = jnp.where(causal, _dot_nt(qh, kn[:, cols].astype(BF16)) + (cq - ck_n), NEG_BIG)
        m = jnp.maximum(jnp.max(s_p, axis=1, keepdims=True), jnp.max(s_n, axis=1, keepdims=True))
        p_p = jnp.exp2(s_p - m)
        p_n = jnp.exp2(s_n - m)
        l = jnp.sum(p_p, axis=1, keepdims=True) + jnp.sum(p_n, axis=1, keepdims=True)
        acc = (_dot_nt(p_p.astype(BF16), vt_ref[0, cols, :].astype(BF16))
               + _dot(p_n.astype(BF16), vn[:, cols].astype(BF16)))
        outs.append(acc / l)
    o_ref[0] = jnp.concatenate(outs, axis=1)


def _fox_sample(q, k_new, v_new, kt_past, vt_past, cum, cumt):
    b, ts, _ = q.shape
    p = kt_past.shape[2]
    length = cum.shape[1]
    assert p % CUM_TILE == 0 and ts <= CUM_TILE
    new = pl.BlockSpec((1, ts, FOX_DIM), lambda bi: (bi, 0, 0))
    old = pl.BlockSpec((1, FOX_DIM, p), lambda bi: (bi, 0, 0))
    return pl.pallas_call(
        functools.partial(_fox_sample_body, ts=ts, past=p),
        grid=(b,),
        in_specs=[new, pl.BlockSpec((1, CUM_TILE, LANES), lambda bi: (bi, p // CUM_TILE, 0)),
                  pl.BlockSpec((1, FOX_HEADS, length), lambda bi: (bi, 0, 0)), old, old, new, new],
        out_specs=new,
        out_shape=jax.ShapeDtypeStruct((b, ts, FOX_DIM), F32),
        compiler_params=_cparams("parallel"),
        name="fox_sample",
    )(q, cum, cumt, kt_past, vt_past, k_new, v_new)


def _rwkv_prep_body(h_ref, prev_ref, mu_ref, w0_ref, w2_ref, a0_ref, a2_ref, g2_ref, kk_ref, ka_ref, rk_ref,
                    r_out, w_out, k_out, v_out, kap_out, bet_out, g_out, bon_out, *, tiles_per_seq):
    h = h_ref[...]
    tm = h.shape[0]
    prev_row = prev_ref[0, 7:8, :]
    if tiles_per_seq:
        keep = jnp.where(pl.program_id(0) % tiles_per_seq != 0, 1.0, 0.0)
        prev_row = prev_row * keep
    first = lax.broadcasted_iota(jnp.int32, (tm, 1), 0) == 0
    prev = jnp.where(first, prev_row, pltpu.roll(h, 1, axis=0))
    hx = h + (prev - h) * mu_ref[...]
    r = hx[:, 0:RWKV_DIM]
    k = hx[:, RWKV_DIM:2 * RWKV_DIM]
    v = hx[:, 2 * RWKV_DIM:3 * RWKV_DIM]
    xwa = hx[:, 3 * RWKV_DIM:3 * RWKV_DIM + LANES]
    xg = hx[:, 3 * RWKV_DIM + LANES:]
    w_logit = w0_ref[...] + _dot(jnp.tanh(xwa).astype(BF16), w2_ref[...])
    decay = jnp.exp(-jnp.exp(_log_sigmoid(w_logit) - 0.5))
    a = jax.nn.sigmoid(a0_ref[...] + _dot(xwa.astype(BF16), a2_ref[...]))
    g = _dot(jax.nn.sigmoid(xg).astype(BF16), g2_ref[...])
    ones = _head_ones()
    kk = k * kk_ref[...]
    kk = kk / jnp.maximum(jnp.sqrt(_head_sum(kk * kk, ones)), 1e-12)
    k2 = k * (1.0 + (a - 1.0) * ka_ref[...])
    r_out[...] = r
    w_out[...] = decay
    k_out[...] = k2
    v_out[...] = v
    kap_out[...] = kk
    bet_out[...] = kk * a
    g_out[...] = g
    bon_out[...] = _head_sum(r * k2 * rk_ref[...], ones) * v


def _rwkv_prep(hr, prev8, prev_map, tm, tiles_per_seq, params):
    rows = hr.shape[0]
    out = jax.ShapeDtypeStruct((rows, RWKV_DIM), F32)
    return pl.pallas_call(
        functools.partial(_rwkv_prep_body, tiles_per_seq=tiles_per_seq),
        grid=(rows // tm,),
        in_specs=[_row_spec(tm, RWKV_COLS), pl.BlockSpec((1, 8, RWKV_COLS), prev_map)]
        + [_full_spec(p.shape) for p in params],
        out_specs=[_row_spec(tm, RWKV_DIM)] * 8,
        out_shape=[out] * 8,
        compiler_params=_cparams("parallel"),
        name="rwkv_prep",
    )(hr, prev8, *params)


def _to_chains_body(x_ref, o_ref, st_ref, *, nb, n_out, offsets):
    for b in range(nb):
        st_ref[b] = x_ref[b].T
    for j in range(n_out):
        groups = [st_ref[b, off * RWKV_HEADS:(off + 1) * RWKV_HEADS, :] for off in offsets(j) for b in range(nb)]
        o_ref[:, j * LANES:(j + 1) * LANES] = jnp.concatenate(groups, axis=0).T


def _to_chains_pallas(x, n_out, offsets):
    nb, t, _ = x.shape
    tt = LANES
    return pl.pallas_call(
        functools.partial(_to_chains_body, nb=nb, n_out=n_out, offsets=offsets),
        grid=(t // tt,),
        in_specs=[pl.BlockSpec((nb, tt, RWKV_DIM), lambda i: (0, i, 0))],
        out_specs=pl.BlockSpec((tt, n_out * LANES), lambda i: (i, 0)),
        out_shape=jax.ShapeDtypeStruct((t, n_out * LANES), F32),
        scratch_shapes=[pltpu.VMEM((nb, RWKV_DIM, tt), F32)],
        compiler_params=_cparams("parallel"),
        name="to_chains",
    )(x).reshape(t, n_out, LANES)


def _from_chains_body(o_ref, x_ref, st_ref, *, nb, nv, dup):
    nh = RWKV_HEADS
    for vp in range(nv):
        tile = o_ref[:, vp * LANES:(vp + 1) * LANES].T
        for vh in range(dup):
            ch = vh * nv + vp
            for b in range(nb):
                r0 = (vh * nb + b) * nh
                st_ref[b, ch * nh:(ch + 1) * nh, :] = tile[r0:r0 + nh, :]
    for b in range(nb):
        x_ref[b] = st_ref[b].T


def _from_chains_pallas(o, nb):
    t, nv, _ = o.shape
    dup = HEAD_DIM // nv
    tt = LANES
    return pl.pallas_call(
        functools.partial(_from_chains_body, nb=nb, nv=nv, dup=dup),
        grid=(t // tt,),
        in_specs=[pl.BlockSpec((tt, nv * LANES), lambda i: (i, 0))],
        out_specs=pl.BlockSpec((nb, tt, RWKV_DIM), lambda i: (0, i, 0)),
        out_shape=jax.ShapeDtypeStruct((nb, t, RWKV_DIM), F32),
        scratch_shapes=[pltpu.VMEM((nb, RWKV_DIM, tt), F32)],
        compiler_params=_cparams("parallel"),
        name="from_chains",
    )(o.reshape(t, nv * LANES))


def _rwkv_scan_body(w_ref, kap_ref, bet_ref, k2_ref, r_ref, v_ref, s0_ref, o_ref, s_ref, *, tb, nv):
    @pl.when(pl.program_id(0) == 0)
    def _():
        s_ref[...] = s0_ref[...]

    packed = w_ref.shape[1] < HEAD_DIM
    low = lax.broadcasted_iota(jnp.int32, (w_ref.shape[1], LANES), 1) < LANES // 2

    def operand(ref, t):
        x = ref[t]
        if not packed:
            return x
        swapped = pltpu.roll(x, LANES // 2, axis=1)
        return jnp.concatenate([jnp.where(low, x, swapped), jnp.where(low, swapped, x)], axis=0)

    def operands(t):
        return tuple(operand(ref, t) for ref in (w_ref, kap_ref, bet_ref, k2_ref, r_ref))

    def token(t, current):
        following = operands(jnp.minimum(t + 1, tb - 1))
        w, kap, bet, k2, r = current
        for vp in range(nv):
            s = s_ref[vp]
            rho = jnp.sum(s * kap, axis=0, keepdims=True)
            sn = s * w - bet * rho + k2 * v_ref[t, vp:vp + 1, :]
            s_ref[vp] = sn
            o_ref[t, vp:vp + 1, :] = jnp.sum(sn * r, axis=0, keepdims=True)
        return following

    lax.fori_loop(0, tb, token, operands(0))


def _rwkv_scan(w, kap, bet, k2, r, v, s0):
    t = w.shape[0]
    nv = v.shape[1]
    tb = min(SCAN_TOKENS, t)
    op_spec = pl.BlockSpec((tb, w.shape[1], LANES), lambda i: (i, 0, 0))
    v_spec = pl.BlockSpec((tb, nv, LANES), lambda i: (i, 0, 0))
    s_spec = pl.BlockSpec((nv, HEAD_DIM, LANES), lambda i: (0, 0, 0))
    return pl.pallas_call(
        functools.partial(_rwkv_scan_body, tb=tb, nv=nv),
        grid=(t // tb,),
        in_specs=[op_spec] * 5 + [v_spec, s_spec],
        out_specs=[v_spec, s_spec],
        out_shape=[jax.ShapeDtypeStruct((t, nv, LANES), F32), jax.ShapeDtypeStruct((nv, HEAD_DIM, LANES), F32)],
        compiler_params=_cparams("arbitrary"),
        name="rwkv_scan",
    )(w, kap, bet, k2, r, v, s0)


def _outproj_even_core(x, a, o, bon, gate, lng, lnb, w_ref):
    ones = _head_ones()
    d = o - _head_sum(o, ones) * (1.0 / HEAD_DIM)
    var = _head_sum(d * d, ones) * (1.0 / HEAD_DIM)
    y = d * lax.rsqrt(var + RWKV_GN_EPS) * lng + lnb
    b = (y + bon) * gate
    return x + _dot(a.astype(BF16), w_ref[0:FOX_DIM, :]) + _dot(b.astype(BF16), w_ref[FOX_DIM:, :])


def _outproj_even_body(x_ref, a_ref, o_ref, bon_ref, g_ref, lng_ref, lnb_ref, w_ref, out_ref):
    out_ref[...] = _outproj_even_core(x_ref[...], a_ref[...], o_ref[...], bon_ref[...], g_ref[...],
                                      lng_ref[...], lnb_ref[...], w_ref)


def _outproj_even(x, a, o, bon, g, lng, lnb, w):
    rows = x.shape[0]
    tm = min(ROW_TILE, rows)
    return pl.pallas_call(
        _outproj_even_body,
        grid=(rows // tm,),
        in_specs=[_row_spec(tm, D_MODEL)] + [_row_spec(tm, FOX_DIM)] * 4
        + [_full_spec(lng.shape), _full_spec(lnb.shape), _full_spec(w.shape)],
        out_specs=_row_spec(tm, D_MODEL),
        out_shape=jax.ShapeDtypeStruct(x.shape, F32),
        compiler_params=_cparams("parallel"),
        name="outproj_even",
    )(x, a, o, bon, g, lng, lnb, w)


def _rope(x, cos, sin_up, sin_dn):
    return x * cos + pltpu.roll(x, ROPE_DIM // 2, axis=1) * sin_up + pltpu.roll(x, LANES - ROPE_DIM // 2, axis=1) * sin_dn


def _inproj_odd_body(x_ref, g_ref, w_ref, qn_ref, kn_ref, vn_ref, cos_ref, sup_ref, sdn_ref,
                     q_ref, k_ref, v_ref, u_ref, gv_ref):
    n = _rms(x_ref[...], g_ref[...]).astype(BF16)
    cos, sup, sdn = cos_ref[...], sup_ref[...], sdn_ref[...]
    ones64 = _group_ones(HEAD_DIM)
    qw = SWA_Q
    for j in range(SWA_Q // LANES):
        cols = slice(j * LANES, (j + 1) * LANES)
        hq = _dot(n, w_ref[:, cols])
        qh = hq * lax.rsqrt(_group_sum(hq * hq, ones64) * (1.0 / HEAD_DIM) + NORM_EPS) * qn_ref[:, cols]
        q_ref[:, cols] = (_rope(qh, cos, sup, sdn) * ATT_SCALE).astype(BF16)
    hk = _dot(n, w_ref[:, qw:qw + SWA_KV])
    kn = hk * lax.rsqrt(_group_sum(hk * hk, ones64) * (1.0 / HEAD_DIM) + NORM_EPS) * kn_ref[...]
    k_ref[...] = _rope(kn, cos, sup, sdn)
    v_ref[...] = _dot(n, w_ref[:, qw + SWA_KV:qw + 2 * SWA_KV])
    s0 = qw + 2 * SWA_KV
    u_ref[...] = _gelu_tanh(_dot(n, w_ref[:, s0:s0 + SGU_DIM]))
    gv = _gelu_tanh(_dot(n, w_ref[:, s0 + SGU_DIM:]))
    gv_ref[...] = _rms(gv, vn_ref[...])


def _inproj_odd(x, g, w, qn, kn, vn, cos, sup, sdn, table_blocks):
    rows = x.shape[0]
    tm = min(ROW_TILE, rows)
    tab = pl.BlockSpec((tm, LANES), lambda i: (i % table_blocks, 0))
    outs = [jax.ShapeDtypeStruct((rows, SWA_Q), BF16), jax.ShapeDtypeStruct((rows, SWA_KV), F32),
            jax.ShapeDtypeStruct((rows, SWA_KV), F32), jax.ShapeDtypeStruct((rows, SGU_DIM), F32),
            jax.ShapeDtypeStruct((rows, SGU_DIM), F32)]
    return pl.pallas_call(
        _inproj_odd_body,
        grid=(rows // tm,),
        in_specs=[_row_spec(tm, D_MODEL), _full_spec(g.shape), _full_spec(w.shape), _full_spec(qn.shape),
                  _full_spec(kn.shape), _full_spec(vn.shape), tab, tab, tab],
        out_specs=[_row_spec(tm, SWA_Q), _row_spec(tm, SWA_KV), _row_spec(tm, SWA_KV),
                   _row_spec(tm, SGU_DIM), _row_spec(tm, SGU_DIM)],
        out_shape=outs,
        compiler_params=_cparams("parallel"),
        name="inproj_odd",
    )(x, g, w, qn, kn, vn, cos, sup, sdn)


def _sink_attend(q, kb, vb, sink, visible):
    s = jnp.where(visible, _dot_nt(q, kb), NEG_BIG)
    m = jnp.maximum(jnp.max(s, axis=1, keepdims=True), sink)
    p = jnp.exp(s - m)
    l = jnp.sum(p, axis=1, keepdims=True) + jnp.exp(sink - m)
    return _dot(p.astype(BF16), vb) / l


def _swa_body(sink_ref, q_ref, kp_ref, kc_ref, vp_ref, vc_ref, mask_ref, o_ref):
    kw = jnp.concatenate([kp_ref[0], kc_ref[0]], axis=0)
    vw = jnp.concatenate([vp_ref[0], vc_ref[0]], axis=0)
    ks = (kw.astype(BF16), pltpu.roll(kw, HEAD_DIM, axis=1).astype(BF16))
    vs = (vw.astype(BF16), pltpu.roll(vw, HEAD_DIM, axis=1).astype(BF16))
    visible = mask_ref[0] > 0.5
    half = _lane_half((q_ref.shape[1], LANES))
    for j in range(SWA_Q // LANES):
        cols = slice(j * LANES, (j + 1) * LANES)
        qb = q_ref[0, :, cols]
        outs = []
        for e in range(2):
            head = 2 * j + e
            swapped = int(head // SWA_GROUP != e)
            qh = jnp.where(half if e == 0 else jnp.logical_not(half), qb, jnp.zeros_like(qb))
            outs.append(_sink_attend(qh, ks[swapped], vs[swapped], sink_ref[head], visible))
        o_ref[0, :, cols] = jnp.where(half, outs[0], outs[1]).astype(BF16)


def _swa(sinks, q, k_prev, k_cur, v_prev, v_cur, mask, tq, prev_rows, prev_map, mask_map):
    b, t, qw = q.shape
    nk = prev_rows + tq
    cur = lambda bi, i: (bi, i, 0)
    return pl.pallas_call(
        _swa_body,
        grid=(b, t // tq),
        in_specs=[pl.BlockSpec(memory_space=pltpu.SMEM),
                  pl.BlockSpec((1, tq, qw), cur),
                  pl.BlockSpec((1, prev_rows, SWA_KV), prev_map), pl.BlockSpec((1, tq, SWA_KV), cur),
                  pl.BlockSpec((1, prev_rows, SWA_KV), prev_map), pl.BlockSpec((1, tq, SWA_KV), cur),
                  pl.BlockSpec((1, tq, nk), mask_map)],
        out_specs=pl.BlockSpec((1, tq, qw), cur),
        out_shape=jax.ShapeDtypeStruct((b, t, qw), BF16),
        compiler_params=_cparams("parallel", "parallel"),
        name="swa",
    )(sinks, q, k_prev, k_cur, v_prev, v_cur, mask)


def _sgu_body(u_ref, v_ref, w_ref, b_ref, o_ref, *, length, n_chunks):
    tril = (lax.broadcasted_iota(jnp.int32, (length, length), 1)
            <= lax.broadcasted_iota(jnp.int32, (length, length), 0))
    ws = [jnp.where(tril, w_ref[g], 0.0).astype(BF16) for g in range(SGU_GROUPS)]
    half = _lane_half((length, LANES))
    for c in range(n_chunks):
        rows = slice(c * length, (c + 1) * length)
        for j in range(SGU_GROUPS // 2):
            cols = slice(j * LANES, (j + 1) * LANES)
            vb = v_ref[rows, cols].astype(BF16)
            mixed = jnp.where(half, _dot(ws[2 * j], vb), _dot(ws[2 * j + 1], vb)) + b_ref[:, cols]
            o_ref[rows, cols] = u_ref[rows, cols] * mixed


def _sgu(u, v, w, bias, length, n_chunks):
    rows = u.shape[0]
    tm = length * n_chunks
    return pl.pallas_call(
        functools.partial(_sgu_body, length=length, n_chunks=n_chunks),
        grid=(rows // tm,),
        in_specs=[_row_spec(tm, SGU_DIM), _row_spec(tm, SGU_DIM), _full_spec(w.shape), _full_spec(bias.shape)],
        out_specs=_row_spec(tm, SGU_DIM),
        out_shape=jax.ShapeDtypeStruct(u.shape, F32),
        compiler_params=_cparams("parallel"),
        name="sgu",
    )(u, v, w, bias)


def _outproj_odd_core(x, c, d, w_ref):
    return x + _dot(c, w_ref[0:SWA_Q, :]) + _dot(d.astype(BF16), w_ref[SWA_Q:, :])


def _outproj_odd_body(x_ref, c_ref, d_ref, w_ref, out_ref):
    out_ref[...] = _outproj_odd_core(x_ref[...], c_ref[...], d_ref[...], w_ref)


def _outproj_odd(x, c, d, w):
    rows = x.shape[0]
    tm = min(ROW_TILE, rows)
    return pl.pallas_call(
        _outproj_odd_body,
        grid=(rows // tm,),
        in_specs=[_row_spec(tm, D_MODEL), _row_spec(tm, SWA_Q), _row_spec(tm, SGU_DIM), _full_spec(w.shape)],
        out_specs=_row_spec(tm, D_MODEL),
        out_shape=jax.ShapeDtypeStruct(x.shape, F32),
        compiler_params=_cparams("parallel"),
        name="outproj_odd",
    )(x, c, d, w)


def _memkv_body(m_ref, g_ref, w_ref, kn_ref, k_ref, v_ref):
    n = _rms(m_ref[...], g_ref[...]).astype(BF16)
    ones = _group_ones(HEAD_DIM)
    hk = _dot(n, w_ref[:, 0:MEM_DIM])
    k_ref[...] = hk * lax.rsqrt(_group_sum(hk * hk, ones) * (1.0 / HEAD_DIM) + NORM_EPS) * kn_ref[...]
    v_ref[...] = _dot(n, w_ref[:, MEM_DIM:])


def _memkv(mem, g, w, kn):
    rows = mem.shape[0]
    tm = min(ROW_TILE, rows)
    out = jax.ShapeDtypeStruct((rows, MEM_DIM), F32)
    return pl.pallas_call(
        _memkv_body,
        grid=(rows // tm,),
        in_specs=[_row_spec(tm, D_MODEL), _full_spec(g.shape), _full_spec(w.shape), _full_spec(kn.shape)],
        out_specs=[_row_spec(tm, MEM_DIM)] * 2,
        out_shape=[out, out],
        compiler_params=_cparams("parallel"),
        name="memkv",
    )(mem, g, w, kn)


def _xattn_core(x, g, wq_ref, qn, mk_ref, mv_ref, wo_ref):
    tq = x.shape[0]
    n = _rms(x, g).astype(BF16)
    hq = _dot(n, wq_ref[...])
    ones = _group_ones(HEAD_DIM)
    q = (hq * lax.rsqrt(_group_sum(hq * hq, ones) * (1.0 / HEAD_DIM) + NORM_EPS) * qn * ATT_SCALE).astype(BF16)
    half = _lane_half((tq, LANES))
    blocks = []
    for j in range(MEM_HEADS // 2):
        cols = slice(j * LANES, (j + 1) * LANES)
        kb = mk_ref[0, :, cols].astype(BF16)
        vb = mv_ref[0, :, cols].astype(BF16)
        qb = q[:, cols]
        outs = []
        for h in range(2):
            sel = half if h == 0 else jnp.logical_not(half)
            s = _dot_nt(jnp.where(sel, qb, jnp.zeros_like(qb)), kb)
            p = jnp.exp(s - jnp.max(s, axis=1, keepdims=True))
            outs.append(_dot(p.astype(BF16), vb) / jnp.sum(p, axis=1, keepdims=True))
        blocks.append(jnp.where(half, outs[0], outs[1]))
    o = jnp.concatenate(blocks, axis=1).astype(BF16)
    return x + _dot(o, wo_ref[...])


def _xattn_body(x_ref, g_ref, wq_ref, qn_ref, mk_ref, mv_ref, wo_ref, out_ref):
    out_ref[0] = _xattn_core(x_ref[0], g_ref[...], wq_ref, qn_ref[...], mk_ref, mv_ref, wo_ref)


def _tail_body(*refs, even):
    n_mix = 8 if even else 4
    mix, rest = refs[:n_mix], refs[n_mix:]
    gx_ref, wq_ref, qn_ref, mk_ref, mv_ref, wo_ref, gf_ref, wg_ref, wu_ref, wd_ref, out_ref, acc_ref = rest
    if even:
        x_ref, a_ref, o_ref, bon_ref, gate_ref, lng_ref, lnb_ref, w_ref = mix
        x = _outproj_even_core(x_ref[0], a_ref[0], o_ref[0], bon_ref[0], gate_ref[0], lng_ref[...], lnb_ref[...], w_ref)
    else:
        x_ref, c_ref, d_ref, w_ref = mix
        x = _outproj_odd_core(x_ref[0], c_ref[0], d_ref[0], w_ref)
    x = _xattn_core(x, gx_ref[...], wq_ref, qn_ref[...], mk_ref, mv_ref, wo_ref)
    out_ref[0] = _ffn_core(x, gf_ref[...], wg_ref, wu_ref, wd_ref, acc_ref)


def _tail(even, x, mix_rows, mix_params, xattn_params, mk, mv, wo, ffn_params, tm):
    b, t, _ = x.shape
    m = mk.shape[1]
    rows3 = lambda c: pl.BlockSpec((1, tm, c), lambda bi, i: (bi, i, 0))
    mem = pl.BlockSpec((1, m, MEM_DIM), lambda bi, i: (bi, 0, 0))
    args = [x, *mix_rows, *mix_params, *xattn_params, mk, mv, wo, *ffn_params]
    in_specs = ([rows3(a.shape[2]) for a in (x, *mix_rows)] + [_full_spec(p.shape) for p in mix_params]
                + [_full_spec(p.shape) for p in xattn_params] + [mem, mem, _full_spec(wo.shape)]
                + [_full_spec(p.shape) for p in ffn_params])
    return pl.pallas_call(
        functools.partial(_tail_body, even=even),
        grid=(b, t // tm),
        in_specs=in_specs,
        out_specs=rows3(D_MODEL),
        out_shape=jax.ShapeDtypeStruct(x.shape, F32),
        scratch_shapes=[pltpu.VMEM((tm, D_MODEL), F32)],
        compiler_params=_cparams("parallel", "parallel"),
        name="tail_even" if even else "tail_odd",
    )(*args)


def _xattn(x, g, wq, qn, mk, mv, wo, tq):
    b, t, _ = x.shape
    m = mk.shape[1]
    return pl.pallas_call(
        _xattn_body,
        grid=(b, t // tq),
        in_specs=[pl.BlockSpec((1, tq, D_MODEL), lambda bi, i: (bi, i, 0)), _full_spec(g.shape),
                  _full_spec(wq.shape), _full_spec(qn.shape),
                  pl.BlockSpec((1, m, MEM_DIM), lambda bi, i: (bi, 0, 0)),
                  pl.BlockSpec((1, m, MEM_DIM), lambda bi, i: (bi, 0, 0)), _full_spec(wo.shape)],
        out_specs=pl.BlockSpec((1, tq, D_MODEL), lambda bi, i: (bi, i, 0)),
        out_shape=jax.ShapeDtypeStruct(x.shape, F32),
        compiler_params=_cparams("parallel", "parallel"),
        name="xattn",
    )(x, g, wq, qn, mk, mv, wo)


def _to_chains(x, b, t, dup):
    y = x.reshape(b, t, HEAD_DIM, RWKV_HEADS).transpose(1, 2, 0, 3).reshape(t, HEAD_DIM, b * RWKV_HEADS)
    return jnp.concatenate([y] * dup, axis=-1) if dup > 1 else y


def _rope_tables(pos):
    half = ROPE_DIM // 2
    inv_freq = jnp.power(ROPE_THETA, -jnp.arange(half, dtype=F32) / half)
    ang = pos.astype(F32)[:, None] * inv_freq[None, :]
    cos, sin = jnp.cos(ang), jnp.sin(ang)
    n = pos.shape[0]
    pad = jnp.zeros((n, HEAD_DIM - ROPE_DIM), F32)
    zero = jnp.zeros((n, half), F32)
    cos_t = jnp.concatenate([cos, cos, pad + 1.0], axis=1)
    up_t = jnp.concatenate([zero, sin, pad], axis=1)
    dn_t = jnp.concatenate([-sin, zero, pad], axis=1)
    two = lambda a: jnp.concatenate([a, a], axis=1)
    return two(cos_t), two(up_t), two(dn_t)


def _swa_prompt_mask(tq):
    span = WINDOW + tq
    qc = np.arange(tq)[:, None] // CHUNK
    kc = np.arange(span)[None, :] // CHUNK - WINDOW_CHUNKS
    band = (kc <= qc) & (kc >= qc - WINDOW_CHUNKS)
    first = band & (kc >= 0)
    return jnp.asarray(np.stack([first, band]).astype(np.float32))


def _swa_sample_mask(past, rows, t):
    kc = (past - rows + np.arange(rows + t)) // CHUNK
    qc = (past + np.arange(t)) // CHUNK
    m = (kc[None, :] <= qc[:, None]) & (kc[None, :] >= qc[:, None] - WINDOW_CHUNKS)
    return jnp.asarray(m[None].astype(np.float32))


def kernel(x_prompt, x_sample, cache_fox_k, cache_fox_v, cache_fox_logf, state_rwkv, state_rwkv_shift, cache_swa_k, cache_swa_v, cache_mem_k, cache_mem_v, mem_prompt, ffn1_norm, ffn1_w_gate, ffn1_w_up, ffn1_w_down, mix_norm, ev_w_in, fox_b_f, fox_q_norm, fox_k_norm, rwkv_mu, rwkv_w0, rwkv_w2, rwkv_a0, rwkv_a2, rwkv_g2, rwkv_k_k, rwkv_k_a, rwkv_r_k, rwkv_ln_g, rwkv_ln_b, ev_w_out, od_w_in, swa_q_norm, swa_k_norm, swa_sinks, sgu_v_norm, sgu_w_s, sgu_b, od_w_out, xattn_norm, mem_norm, xattn_wq, xattn_wkv, xattn_q_norm, xattn_k_norm, xattn_wo, ffn2_norm, ffn2_w_gate, ffn2_w_up, ffn2_w_down):
    bp, tp, _ = x_prompt.shape
    bs, ts, _ = x_sample.shape
    depth = ffn1_norm.shape[0]
    past = cache_fox_k.shape[2]
    mem_tokens = mem_prompt.shape[1]
    xp = x_prompt.reshape(bp * tp, D_MODEL)
    xs = x_sample.reshape(bs * ts, D_MODEL)
    mem_rows = mem_prompt.reshape(bp * mem_tokens, D_MODEL)
    row = lambda a: a.reshape(1, -1)
    tile_heads = lambda a, n: jnp.tile(a, n).reshape(1, -1)

    out = {k: [] for k in ("p_fox_k", "p_fox_v", "p_fox_logf", "p_rwkv_state", "p_rwkv_shift", "p_swa_k", "p_swa_v",
                           "p_mem_k", "p_mem_v", "s_fox_k", "s_fox_v", "s_fox_logf", "s_rwkv_state", "s_rwkv_shift",
                           "s_swa_k", "s_swa_v", "s_sgu_v")}

    for l in range(depth):
        f1 = _prep_ffn(ffn1_norm[l], ffn1_w_gate[l], ffn1_w_up[l], ffn1_w_down[l])
        xp = _ffn(xp, *f1)
        xs = _ffn(xs, *f1)
        g_mix = row(mix_norm[l])
        if l % 2 == 0:
            e = l // 2
            w_in = ev_w_in[e]
            f0 = 3 * FOX_DIM
            w_cat = jnp.concatenate([w_in[:, :f0], jnp.pad(w_in[:, f0:f0 + FOX_HEADS], ((0, 0), (0, LANES - FOX_HEADS))),
                                     w_in[:, f0 + FOX_HEADS:][:, RWKV_COL_PERM]], axis=1).astype(BF16)
            bf = jnp.pad(fox_b_f[e], (0, LANES - FOX_HEADS)).reshape(1, LANES)
            qn = tile_heads(fox_q_norm[e], FOX_HEADS)
            kn = tile_heads(fox_k_norm[e], FOX_HEADS)
            qp, ktp, vtp, lfp, hrp = _inproj_even(xp, g_mix, w_cat, bf, qn, kn, seq_len=tp)
            qs, ks, vs, lfs, hrs = _inproj_even(xs, g_mix, w_cat, bf, qn, kn)

            cum_p, cumt_p = _cumsum(lfp.reshape(bp, tp, LANES), CUM_TILE)
            a_p = _fox_prompt(qp.reshape(bp, tp, FOX_DIM), ktp, vtp, cum_p, cumt_p, ATT_TILE)
            lfs8 = lfs.reshape(bs, ts, LANES)[:, :, :FOX_HEADS]
            tot = past + ts
            padded = -(-tot // CUM_TILE) * CUM_TILE
            lf_all = jnp.concatenate([cache_fox_logf[e].astype(F32), lfs8], axis=1)
            lf_all = jnp.pad(lf_all, ((0, 0), (0, padded - tot), (0, LANES - FOX_HEADS)))
            cum_s, cumt_s = _cumsum(lf_all, CUM_TILE)
            token_minor = lambda c: jnp.transpose(c, (0, 2, 3, 1)).reshape(bs, FOX_DIM, past)
            a_s = _fox_sample(qs.reshape(bs, ts, FOX_DIM), ks.reshape(bs, ts, FOX_DIM), vs.reshape(bs, ts, FOX_DIM),
                              token_minor(cache_fox_k[e]), token_minor(cache_fox_v[e]), cum_s, cumt_s)

            perm = RWKV_HEAD_PERM
            w2p = jnp.pad(rwkv_w2[e][:, perm], ((0, LANES - DECAY_LORA), (0, 0))).astype(BF16)
            a2p = jnp.pad(rwkv_a2[e][:, perm], ((DECAY_LORA, 0), (0, 0))).astype(BF16)
            params = (row(rwkv_mu[e][RWKV_COL_PERM]), row(rwkv_w0[e][perm]), w2p, row(rwkv_a0[e][perm]), a2p,
                      rwkv_g2[e][:, perm].astype(BF16), row(rwkv_k_k[e][perm]), row(rwkv_k_a[e][perm]),
                      row(rwkv_r_k[e].reshape(-1)[perm]))
            tiles_per_seq = tp // ROW_TILE
            prev_p = hrp.reshape(bp * tp // 8, 8, RWKV_COLS)
            prep_p = _rwkv_prep(hrp, prev_p, lambda i: (jnp.maximum(i * (ROW_TILE // 8) - 1, 0), 0, 0),
                                ROW_TILE, tiles_per_seq, params)
            prev_s = jnp.pad(state_rwkv_shift[e].astype(F32)[:, :, RWKV_COL_PERM], ((0, 0), (7, 0), (0, 0)))
            prep_s = _rwkv_prep(hrs, prev_s, lambda i: (i, 0, 0), ts, 0, params)

            def scan(prep, b, t, state0):
                r, w, k2, v, kap, bet, g, bon = prep
                dup = LANES // (b * RWKV_HEADS)
                nv = HEAD_DIM // dup
                in_kernel = t % LANES == 0
                if in_kernel:
                    halves = lambda j: tuple(half * nv + j for half in range(dup))
                    ops = [_to_chains_pallas(a.reshape(b, t, RWKV_DIM), nv, halves) for a in (w, kap, bet, k2, r)]
                    vt = _to_chains_pallas(v.reshape(b, t, RWKV_DIM), nv, halves)
                else:
                    ops = [_to_chains(a, b, t, dup) for a in (w, kap, bet, k2, r)]
                    vt = v.reshape(b, t, dup, nv, RWKV_HEADS).transpose(1, 3, 2, 0, 4).reshape(t, nv, LANES)
                s0 = state0.reshape(b, RWKV_HEADS, dup, nv, HEAD_DIM).transpose(3, 4, 2, 0, 1).reshape(nv, HEAD_DIM, LANES)
                o, sT = _rwkv_scan(*ops, vt, s0)
                if in_kernel:
                    o = _from_chains_pallas(o, b).reshape(b * t, RWKV_DIM)
                else:
                    o = o.reshape(t, nv, dup, b, RWKV_HEADS).transpose(3, 0, 2, 1, 4).reshape(b * t, RWKV_DIM)
                sT = sT.reshape(nv, HEAD_DIM, dup, b, RWKV_HEADS).transpose(3, 4, 2, 0, 1).reshape(b, RWKV_HEADS, HEAD_DIM, HEAD_DIM)
                return o, sT, g, bon

            o_p, st_p, g_p, bon_p = scan(prep_p, bp, tp, jnp.zeros((bp, RWKV_HEADS, HEAD_DIM, HEAD_DIM), F32))
            o_s, st_s, g_s, bon_s = scan(prep_s, bs, ts, state_rwkv[e].astype(F32))

            w_out = jnp.concatenate([ev_w_out[e][:FOX_DIM], ev_w_out[e][FOX_DIM:][perm]], axis=0).astype(BF16)
            lng, lnb = row(rwkv_ln_g[e][perm]), row(rwkv_ln_b[e][perm])
            seq = lambda a: a.reshape(bp, tp, -1)
            mix_p = (True, (a_p, seq(o_p), seq(bon_p), seq(g_p)), (lng, lnb, w_out))
            xs = _outproj_even(xs, a_s.reshape(bs * ts, FOX_DIM), o_s, bon_s, g_s, lng, lnb, w_out)

            rows_of = lambda a: jnp.transpose(a.reshape(bp, FOX_HEADS, HEAD_DIM, tp), (0, 3, 1, 2))
            out["p_fox_k"].append(rows_of(ktp))
            out["p_fox_v"].append(rows_of(vtp))
            out["p_fox_logf"].append(lfp.reshape(bp, tp, LANES)[:, :, :FOX_HEADS])
            out["p_rwkv_state"].append(st_p)
            out["p_rwkv_shift"].append(hrp.reshape(bp, tp, RWKV_COLS)[:, -1:][:, :, RWKV_COL_UNPERM])
            out["s_fox_k"].append(ks.reshape(bs, ts, FOX_HEADS, HEAD_DIM))
            out["s_fox_v"].append(vs.reshape(bs, ts, FOX_HEADS, HEAD_DIM))
            out["s_fox_logf"].append(lfs8)
            out["s_rwkv_state"].append(st_s)
            out["s_rwkv_shift"].append(hrs.reshape(bs, ts, RWKV_COLS)[:, -1:][:, :, RWKV_COL_UNPERM])
        else:
            j = l // 2
            w_cat = od_w_in[j].astype(BF16)
            qn = tile_heads(swa_q_norm[j], SWA_HEADS)
            kn = tile_heads(swa_k_norm[j], SWA_KV_HEADS)
            vn = row(sgu_v_norm[j])
            tabs_p = _rope_tables(jnp.arange(tp))
            tabs_s = _rope_tables(past + jnp.arange(bs * ts) % ts)
            qp, kp, vp, up, gp = _inproj_odd(xp, g_mix, w_cat, qn, kn, vn, *tabs_p, tp // ROW_TILE)
            qs, ks, vs, us, gs = _inproj_odd(xs, g_mix, w_cat, qn, kn, vn, *tabs_s, 1)

            qw = SWA_Q
            kp3, vp3 = kp.reshape(bp, tp, SWA_KV), vp.reshape(bp, tp, SWA_KV)
            ratio = SWA_TILE // WINDOW
            c_p = _swa(swa_sinks[j], qp.reshape(bp, tp, qw), kp3, kp3, vp3, vp3, _swa_prompt_mask(SWA_TILE),
                       SWA_TILE, WINDOW, lambda bi, i: (bi, jnp.maximum(i * ratio - 1, 0), 0),
                       lambda bi, i: (jnp.minimum(i, 1), 0, 0))
            rows_c = cache_swa_k.shape[2]
            ck3 = cache_swa_k[j].reshape(bs, rows_c, SWA_KV)
            cv3 = cache_swa_v[j].reshape(bs, rows_c, SWA_KV)
            ks3, vs3 = ks.reshape(bs, ts, SWA_KV), vs.reshape(bs, ts, SWA_KV)
            c_s = _swa(swa_sinks[j], qs.reshape(bs, ts, qw), ck3, ks3, cv3, vs3, _swa_sample_mask(past, rows_c, ts),
                       ts, rows_c, lambda bi, i: (bi, 0, 0), lambda bi, i: (0, 0, 0))

            bias = jnp.repeat(jnp.transpose(sgu_b[j]), HEAD_DIM, axis=1)
            d_p = _sgu(up, gp, sgu_w_s[j], bias, SGU_CHUNK, ROW_TILE // SGU_CHUNK)
            d_s = _sgu(us, gs, sgu_w_s[j][:, :ts, :ts], bias[:ts], ts, 1)

            w_out = od_w_out[j].astype(BF16)
            mix_p = (False, (c_p, d_p.reshape(bp, tp, SGU_DIM)), (w_out,))
            xs = _outproj_odd(xs, c_s.reshape(bs * ts, qw), d_s, w_out)

            out["p_swa_k"].append(kp3[:, -WINDOW:].reshape(bp, WINDOW, SWA_KV_HEADS, HEAD_DIM))
            out["p_swa_v"].append(vp3[:, -WINDOW:].reshape(bp, WINDOW, SWA_KV_HEADS, HEAD_DIM))
            out["s_swa_k"].append(jnp.concatenate([ck3, ks3], axis=1)[:, -rows_c:].reshape(bs, rows_c, SWA_KV_HEADS, HEAD_DIM))
            out["s_swa_v"].append(jnp.concatenate([cv3, vs3], axis=1)[:, -rows_c:].reshape(bs, rows_c, SWA_KV_HEADS, HEAD_DIM))
            out["s_sgu_v"].append(gs.reshape(bs, ts, SGU_DIM))

        mk, mv = _memkv(mem_rows, row(mem_norm[l]), xattn_wkv[l].astype(BF16), tile_heads(xattn_k_norm[l], MEM_HEADS))
        mk3, mv3 = mk.reshape(bp, mem_tokens, MEM_DIM), mv.reshape(bp, mem_tokens, MEM_DIM)
        xa = (row(xattn_norm[l]), xattn_wq[l].astype(BF16), tile_heads(xattn_q_norm[l], MEM_HEADS))
        wo = xattn_wo[l].astype(BF16)
        f2 = _prep_ffn(ffn2_norm[l], ffn2_w_gate[l], ffn2_w_up[l], ffn2_w_down[l])
        xp = _tail(mix_p[0], xp.reshape(bp, tp, D_MODEL), mix_p[1], mix_p[2], xa, mk3, mv3, wo, f2,
                   ROW_TILE).reshape(bp * tp, D_MODEL)
        xs = _xattn(xs.reshape(bs, ts, D_MODEL), *xa, cache_mem_k[l].reshape(bs, mem_tokens, MEM_DIM),
                    cache_mem_v[l].reshape(bs, mem_tokens, MEM_DIM), wo, ts).reshape(bs * ts, D_MODEL)
        out["p_mem_k"].append(mk3.reshape(bp, mem_tokens, MEM_HEADS, HEAD_DIM))
        out["p_mem_v"].append(mv3.reshape(bp, mem_tokens, MEM_HEADS, HEAD_DIM))

        xs = _ffn(xs, *f2)

    order = ("p_fox_k", "p_fox_v", "p_fox_logf", "p_rwkv_state", "p_rwkv_shift", "p_swa_k", "p_swa_v", "p_mem_k",
             "p_mem_v", "s_fox_k", "s_fox_v", "s_fox_logf", "s_rwkv_state", "s_rwkv_shift", "s_swa_k", "s_swa_v",
             "s_sgu_v")
    return (xp.reshape(bp, tp, D_MODEL), xs.reshape(bs, ts, D_MODEL)) + tuple(jnp.stack(out[k]) for k in order)
```

```python
import functools

import numpy as np
import jax
import jax.numpy as jnp
from jax import lax
from jax.experimental import pallas as pl
from jax.experimental.pallas import tpu as pltpu

F32 = jnp.float32
BF16 = jnp.bfloat16

D_MODEL = 1024
HEAD_DIM = 64
NORM_EPS = 1e-6
ROPE_THETA = 500000.0
ROPE_DIM = HEAD_DIM // 4
CHUNK = 64
FOX_HEADS = 8
FOX_DIM = FOX_HEADS * HEAD_DIM
RWKV_HEADS = 8
RWKV_DIM = RWKV_HEADS * HEAD_DIM
DECAY_LORA = 64
ICLR_LORA = 64
GATE_LORA = 128
RWKV_COLS = 3 * RWKV_DIM + DECAY_LORA + ICLR_LORA + GATE_LORA
RWKV_GN_EPS = 64e-5
SWA_HEADS = 8
SWA_KV_HEADS = 2
SWA_GROUP = SWA_HEADS // SWA_KV_HEADS
SWA_Q = SWA_HEADS * HEAD_DIM
SWA_KV = SWA_KV_HEADS * HEAD_DIM
WINDOW = 128
WINDOW_CHUNKS = WINDOW // CHUNK
SGU_GROUPS = 8
SGU_DIM = SGU_GROUPS * HEAD_DIM
SGU_CHUNK = 128
MEM_HEADS = 4
MEM_DIM = MEM_HEADS * HEAD_DIM
D_FF = 2816

LANES = 128
ROW_TILE = 512
FF_TILE = 256
ATT_TILE = 512
CUM_TILE = 256
SWA_TILE = 256
SCAN_TOKENS = 64
VMEM_LIMIT = 56 * 1024 * 1024
ATT_SCALE = HEAD_DIM ** -0.5
LOG2E = 1.4426950408889634
NEG_BIG = -1e30


def _cparams(*sem):
    return pltpu.CompilerParams(dimension_semantics=sem, vmem_limit_bytes=VMEM_LIMIT)


def _dot(a, b):
    return jnp.dot(a, b, preferred_element_type=F32)


def _dot_nt(a, b):
    return lax.dot_general(a, b, (((1,), (1,)), ((), ())), preferred_element_type=F32)


def _rms(x, g):
    ms = jnp.mean(x * x, axis=-1, keepdims=True)
    return (x * lax.rsqrt(ms + NORM_EPS)) * g


def _group_ones(group):
    shift = int(np.log2(group))
    r = lax.broadcasted_iota(jnp.int32, (LANES, LANES), 0) >> shift
    c = lax.broadcasted_iota(jnp.int32, (LANES, LANES), 1) >> shift
    return jnp.where(r == c, 1.0, 0.0).astype(BF16)


def _group_sum(x, ones):
    parts = []
    for j in range(x.shape[1] // LANES):
        blk = x[:, j * LANES:(j + 1) * LANES]
        hi = blk.astype(BF16)
        lo = (blk - hi.astype(F32)).astype(BF16)
        parts.append(_dot(hi, ones) + _dot(lo, ones))
    return parts[0] if len(parts) == 1 else jnp.concatenate(parts, axis=1)


def _head_ones():
    r = lax.broadcasted_iota(jnp.int32, (LANES, LANES), 0) & (RWKV_HEADS - 1)
    c = lax.broadcasted_iota(jnp.int32, (LANES, LANES), 1) & (RWKV_HEADS - 1)
    return jnp.where(r == c, 1.0, 0.0).astype(BF16)


def _head_sum(x, ones):
    part = x[:, 0:LANES]
    for j in range(1, x.shape[1] // LANES):
        part = part + x[:, j * LANES:(j + 1) * LANES]
    hi = part.astype(BF16)
    lo = (part - hi.astype(F32)).astype(BF16)
    tot = _dot(hi, ones) + _dot(lo, ones)
    return jnp.concatenate([tot] * (x.shape[1] // LANES), axis=1)


def _log_sigmoid(z):
    return jnp.minimum(z, 0.0) - jnp.log(1.0 + jnp.exp(-jnp.abs(z)))


def _gelu_tanh(x):
    return 0.5 * x * (1.0 + jnp.tanh(0.7978845608028654 * (x + 0.044715 * (x * x * x))))


def _lane_half(shape):
    return lax.broadcasted_iota(jnp.int32, shape, 1) < HEAD_DIM


def _row_spec(tm, cols):
    return pl.BlockSpec((tm, cols), lambda i: (i, 0))


def _full_spec(shape):
    nd = len(shape)
    return pl.BlockSpec(shape, lambda *_: (0,) * nd, pipeline_mode=pl.Buffered(1))


def _ffn_core(x, g, wg_ref, wu_ref, wd_ref, acc_ref):
    n = _rms(x, g).astype(BF16)
    for c in range(wg_ref.shape[2] // FF_TILE):
        cols = slice(c * FF_TILE, (c + 1) * FF_TILE)
        gate = _dot(n, wg_ref[0, :, cols])
        up = _dot(n, wu_ref[0, :, cols])
        act = (gate * jax.nn.sigmoid(gate) * up).astype(BF16)
        part = _dot(act, wd_ref[0, cols, :])
        if c == 0:
            acc_ref[...] = part
        else:
            acc_ref[...] += part
    return x + 0.5 * acc_ref[...]


def _ffn_body(x_ref, g_ref, wg_ref, wu_ref, wd_ref, o_ref, acc_ref):
    o_ref[...] = _ffn_core(x_ref[...], g_ref[...], wg_ref, wu_ref, wd_ref, acc_ref)


def _layer_spec(w, layer):
    nd = w.ndim
    return pl.BlockSpec((1,) + w.shape[1:], lambda *_: (layer,) + (0,) * (nd - 1), pipeline_mode=pl.Buffered(1))


def _ffn(x, layer, g, wg, wu, wd):
    rows = x.shape[0]
    tm = min(ROW_TILE, rows)
    return pl.pallas_call(
        _ffn_body,
        grid=(rows // tm,),
        in_specs=[_row_spec(tm, D_MODEL), _full_spec(g.shape), _layer_spec(wg, layer),
                  _layer_spec(wu, layer), _layer_spec(wd, layer)],
        out_specs=_row_spec(tm, D_MODEL),
        out_shape=jax.ShapeDtypeStruct(x.shape, F32),
        scratch_shapes=[pltpu.VMEM((tm, D_MODEL), F32)],
        compiler_params=_cparams("parallel"),
        name="ffn",
    )(x, g, wg, wu, wd)


def _inproj_even_body(x_ref, g_ref, w_ref, bf_ref, qn_ref, kn_ref,
                      q_ref, k_ref, v_ref, lf_ref, hr_ref, *, token_minor):
    n = _rms(x_ref[...], g_ref[...]).astype(BF16)
    ones = _group_ones(HEAD_DIM)
    hq = _dot(n, w_ref[:, 0:FOX_DIM])
    q = hq * lax.rsqrt(_group_sum(hq * hq, ones) * (1.0 / HEAD_DIM) + NORM_EPS) * qn_ref[...]
    q_ref[...] = (q * (ATT_SCALE * LOG2E)).astype(BF16)
    hk = _dot(n, w_ref[:, FOX_DIM:2 * FOX_DIM])
    k = hk * lax.rsqrt(_group_sum(hk * hk, ones) * (1.0 / HEAD_DIM) + NORM_EPS) * kn_ref[...]
    v = _dot(n, w_ref[:, 2 * FOX_DIM:3 * FOX_DIM])
    if token_minor:
        k_ref[0] = k.T
        v_ref[0] = v.T
    else:
        k_ref[...] = k
        v_ref[...] = v
    f0 = 3 * FOX_DIM
    lf_ref[...] = _log_sigmoid(_dot(n, w_ref[:, f0:f0 + LANES]) + bf_ref[...])
    hr_ref[...] = _dot(n, w_ref[:, f0 + LANES:])


def _inproj_even(x, g, w, bf, qn, kn, seq_len=None):
    rows = x.shape[0]
    tm = min(ROW_TILE, rows)
    if seq_len is None:
        kv_shape = jax.ShapeDtypeStruct((rows, FOX_DIM), F32)
        kv_spec = _row_spec(tm, FOX_DIM)
    else:
        per_seq = seq_len // tm
        kv_shape = jax.ShapeDtypeStruct((rows // seq_len, FOX_DIM, seq_len), F32)
        kv_spec = pl.BlockSpec((1, FOX_DIM, tm), lambda i: (i // per_seq, 0, i % per_seq))
    outs = [jax.ShapeDtypeStruct((rows, FOX_DIM), BF16), kv_shape, kv_shape,
            jax.ShapeDtypeStruct((rows, LANES), F32), jax.ShapeDtypeStruct((rows, RWKV_COLS), F32)]
    return pl.pallas_call(
        functools.partial(_inproj_even_body, token_minor=seq_len is not None),
        grid=(rows // tm,),
        in_specs=[_row_spec(tm, D_MODEL), _full_spec(g.shape), _full_spec(w.shape), _full_spec(bf.shape),
                  _full_spec(qn.shape), _full_spec(kn.shape)],
        out_specs=[_row_spec(tm, FOX_DIM), kv_spec, kv_spec, _row_spec(tm, LANES), _row_spec(tm, RWKV_COLS)],
        out_shape=outs,
        compiler_params=_cparams("parallel"),
        name="inproj_even",
    )(x, g, w, bf, qn, kn)


def _split3(x):
    hi = x.astype(BF16)
    r1 = x - hi.astype(F32)
    mid = r1.astype(BF16)
    lo = (r1 - mid.astype(F32)).astype(BF16)
    return hi, mid, lo


def _cumsum_body(lf_ref, col_ref, row_ref, *, n_chunks, tk):
    r = lax.broadcasted_iota(jnp.int32, (tk, tk), 0)
    c = lax.broadcasted_iota(jnp.int32, (tk, tk), 1)
    tri = jnp.where(c <= r, 1.0, 0.0).astype(BF16)
    carry = jnp.zeros((1, LANES), F32)
    for i in range(n_chunks):
        hi, mid, lo = _split3(lf_ref[0, i * tk:(i + 1) * tk, :])
        cs = _dot(tri, hi) + _dot(tri, mid) + _dot(tri, lo) + carry
        col_ref[0, i * tk:(i + 1) * tk, :] = cs
        row_ref[0, :, i * tk:(i + 1) * tk] = cs.T[0:FOX_HEADS, :]
        carry = cs[tk - 1:tk, :]


def _cumsum(lf, tk):
    b, length, _ = lf.shape
    n_chunks = length // tk
    return pl.pallas_call(
        functools.partial(_cumsum_body, n_chunks=n_chunks, tk=tk),
        grid=(b,),
        in_specs=[pl.BlockSpec((1, length, LANES), lambda i: (i, 0, 0))],
        out_specs=[pl.BlockSpec((1, length, LANES), lambda i: (i, 0, 0)),
                   pl.BlockSpec((1, FOX_HEADS, length), lambda i: (i, 0, 0))],
        out_shape=[jax.ShapeDtypeStruct((b, length, LANES), F32),
                   jax.ShapeDtypeStruct((b, FOX_HEADS, length), F32)],
        compiler_params=_cparams("parallel"),
        name="cumsum",
    )(lf)


def _place3(terms, src_lane, dst_lane):
    r = lax.broadcasted_iota(jnp.int32, (LANES, LANES), 0)
    c = lax.broadcasted_iota(jnp.int32, (LANES, LANES), 1)
    out = None
    for i, t in enumerate(terms):
        sel = jnp.where((r == src_lane) & (c == dst_lane + i), 1.0, 0.0).astype(BF16)
        out = _dot(t, sel) if out is None else out + _dot(t, sel)
    return out


def _fox_q(q_pair, cum_col, head, h):
    n = q_pair.shape[0]
    qf = q_pair.astype(F32)
    if h == 1:
        qf = pltpu.roll(qf, HEAD_DIM, axis=1)
    lane = lax.broadcasted_iota(jnp.int32, (n, LANES), 1)
    ones = jnp.where((lane >= HEAD_DIM + 3) & (lane < HEAD_DIM + 6), 1.0, 0.0)
    aug = _place3(_split3(cum_col * LOG2E), head, HEAD_DIM) + ones
    return jnp.where(lane < HEAD_DIM, qf, aug).astype(BF16)


def _fox_kt(kt, cum_row):
    n = kt.shape[1]
    hi, mid, lo = (x.astype(F32) for x in _split3(cum_row * LOG2E))
    row = lax.broadcasted_iota(jnp.int32, (8, n), 0)
    aug = jnp.where(row < 3, 1.0, jnp.where(row == 3, -hi, jnp.where(row == 4, -mid, jnp.where(row == 5, -lo, 0.0))))
    return jnp.concatenate([kt, aug, jnp.zeros((HEAD_DIM - 8, n), F32)], axis=0).astype(BF16)


def _fox_vt(vt):
    n = vt.shape[1]
    row = lax.broadcasted_iota(jnp.int32, (HEAD_DIM, n), 0)
    return jnp.concatenate([vt, jnp.where(row == 0, 1.0, 0.0)], axis=0).astype(BF16)


def _fox_finish(acc):
    return acc / acc[:, HEAD_DIM:HEAD_DIM + 1]


def _fox_prompt_body(q_ref, cum_ref, cumt_ref, kt_ref, vt_ref, o_ref, ka_sc, va_sc, m_sc, acc_sc, *, tq, t):
    hp = pl.program_id(1)
    i = pl.program_id(2)

    @pl.when(i == 0)
    def _():
        for h in range(2):
            rows = slice(h * HEAD_DIM, (h + 1) * HEAD_DIM)
            cum_row = cumt_ref[0, pl.ds(2 * hp + h, 1), :]
            for c in range(t // tq):
                cols = slice(c * tq, (c + 1) * tq)
                ka_sc[h, c] = _fox_kt(kt_ref[0, rows, cols], cum_row[:, cols])
                va_sc[h, c] = _fox_vt(vt_ref[0, rows, cols])

    off_q = pl.multiple_of(i * tq, tq)
    cq = cum_ref[0, pl.ds(off_q, tq), :]
    qs = [_fox_q(q_ref[0], cq, 2 * hp + h, h) for h in range(2)]
    m_sc[...] = jnp.full(m_sc.shape, NEG_BIG, F32)
    acc_sc[...] = jnp.zeros(acc_sc.shape, F32)
    causal = (lax.broadcasted_iota(jnp.int32, (tq, tq), 1) <= lax.broadcasted_iota(jnp.int32, (tq, tq), 0))

    def step(j, masked):
        for h in range(2):
            s = _dot(qs[h], ka_sc[h, j])
            if masked:
                s = jnp.where(causal, s, NEG_BIG)
            m_old = m_sc[h]
            m_new = jnp.maximum(m_old, jnp.max(s, axis=1, keepdims=True))
            p = jnp.concatenate([jnp.exp2(s[:, c * LANES:(c + 1) * LANES] - m_new) for c in range(tq // LANES)],
                                axis=1).astype(BF16)
            acc_sc[h] = jnp.exp2(m_old - m_new) * acc_sc[h] + _dot_nt(p, va_sc[h, j])
            m_sc[h] = m_new

    def past(j, carry):
        step(j, False)
        return carry

    lax.fori_loop(0, i, past, 0)
    step(i, True)
    second = pltpu.roll(_fox_finish(acc_sc[1]), HEAD_DIM, axis=1)
    o_ref[0] = jnp.where(_lane_half((tq, LANES)), _fox_finish(acc_sc[0]), second)


def _fox_prompt(q, kt, vt, cum, cumt, tq):
    b, t, _ = q.shape
    pairs = FOX_HEADS // 2
    nblk = t // tq
    return pl.pallas_call(
        functools.partial(_fox_prompt_body, tq=tq, t=t),
        grid=(b, pairs, nblk),
        in_specs=[pl.BlockSpec((1, tq, LANES), lambda bi, hp, i: (bi, i, hp)),
                  pl.BlockSpec((1, t, LANES), lambda bi, hp, i: (bi, 0, 0)),
                  pl.BlockSpec((1, FOX_HEADS, t), lambda bi, hp, i: (bi, 0, 0)),
                  pl.BlockSpec((1, LANES, t), lambda bi, hp, i: (bi, hp, 0)),
                  pl.BlockSpec((1, LANES, t), lambda bi, hp, i: (bi, hp, 0))],
        out_specs=pl.BlockSpec((1, tq, LANES), lambda bi, hp, i: (bi, i, hp)),
        out_shape=jax.ShapeDtypeStruct((b, t, FOX_DIM), F32),
        scratch_shapes=[pltpu.VMEM((2, nblk, LANES, tq), BF16), pltpu.VMEM((2, nblk, LANES, tq), BF16),
                        pltpu.VMEM((2, tq, LANES), F32), pltpu.VMEM((2, tq, LANES), F32)],
        compiler_params=_cparams("parallel", "parallel", "arbitrary"),
        name="fox_prompt",
    )(q, cum, cumt, kt, vt)


def _fox_sample_body(q_ref, cum_ref, cumt_ref, kt_ref, vt_ref, kn_ref, vn_ref, o_ref, *, ts, past):
    cum_n = cum_ref[0, 0:ts, :] * LOG2E
    causal = (lax.broadcasted_iota(jnp.int32, (ts, ts), 1) <= lax.broadcasted_iota(jnp.int32, (ts, ts), 0))
    lane = lax.broadcasted_iota(jnp.int32, (ts, LANES), 1)
    q = q_ref[0].astype(F32)
    kn = kn_ref[0]
    vn = vn_ref[0]
    outs = []
    for h in range(FOX_HEADS):
        cols = slice(h * HEAD_DIM, (h + 1) * HEAD_DIM)
        cq = jnp.sum(jnp.where(lane == h, cum_n, 0.0), axis=1, keepdims=True)
        ck_p = cumt_ref[0, h:h + 1, 0:past] * LOG2E
        ck_n = cumt_ref[0, h:h + 1, past:past + ts] * LOG2E
        qh = q[:, cols].astype(BF16)
        s_p = _dot(qh, kt_ref[0, cols, :].astype(BF16)) + (cq - ck_p)
        s_n = jnp.where(causal, _dot_nt(qh, kn[:, cols].astype(BF16)) + (cq - ck_n), NEG_BIG)
        m = jnp.maximum(jnp.max(s_p, axis=1, keepdims=True), jnp.max(s_n, axis=1, keepdims=True))
        p_p = jnp.exp2(s_p - m)
        p_n = jnp.exp2(s_n - m)
        l = jnp.sum(p_p, axis=1, keepdims=True) + jnp.sum(p_n, axis=1, keepdims=True)
        acc = (_dot_nt(p_p.astype(BF16), vt_ref[0, cols, :].astype(BF16))
               + _dot(p_n.astype(BF16), vn[:, cols].astype(BF16)))
        outs.append(acc / l)
    o_ref[0] = jnp.concatenate(outs, axis=1)


def _fox_sample(q, k_new, v_new, kt_past, vt_past, cum, cumt):
    b, ts, _ = q.shape
    p = kt_past.shape[2]
    length = cum.shape[1]
    assert p % CUM_TILE == 0 and ts <= CUM_TILE
    new = pl.BlockSpec((1, ts, FOX_DIM), lambda bi: (bi, 0, 0))
    old = pl.BlockSpec((1, FOX_DIM, p), lambda bi: (bi, 0, 0))
    return pl.pallas_call(
        functools.partial(_fox_sample_body, ts=ts, past=p),
        grid=(b,),
        in_specs=[new, pl.BlockSpec((1, CUM_TILE, LANES), lambda bi: (bi, p // CUM_TILE, 0)),
                  pl.BlockSpec((1, FOX_HEADS, length), lambda bi: (bi, 0, 0)), old, old, new, new],
        out_specs=new,
        out_shape=jax.ShapeDtypeStruct((b, ts, FOX_DIM), F32),
        compiler_params=_cparams("parallel"),
        name="fox_sample",
    )(q, cum, cumt, kt_past, vt_past, k_new, v_new)


def _rwkv_prep_body(h_ref, prev_ref, mu_ref, w0_ref, w2_ref, a0_ref, a2_ref, g2_ref, kk_ref, ka_ref, rk_ref,
                    r_out, w_out, k_out, v_out, kap_out, bet_out, g_out, bon_out, *, tiles_per_seq):
    h = h_ref[...]
    tm = h.shape[0]
    prev_row = prev_ref[0, 7:8, :]
    if tiles_per_seq:
        keep = jnp.where(pl.program_id(0) % tiles_per_seq != 0, 1.0, 0.0)
        prev_row = prev_row * keep
    first = lax.broadcasted_iota(jnp.int32, (tm, 1), 0) == 0
    prev = jnp.where(first, prev_row, pltpu.roll(h, 1, axis=0))
    hx = h + (prev - h) * mu_ref[...]
    r = hx[:, 0:RWKV_DIM]
    k = hx[:, RWKV_DIM:2 * RWKV_DIM]
    v = hx[:, 2 * RWKV_DIM:3 * RWKV_DIM]
    xwa = hx[:, 3 * RWKV_DIM:3 * RWKV_DIM + LANES]
    xg = hx[:, 3 * RWKV_DIM + LANES:]
    w_logit = w0_ref[...] + _dot(jnp.tanh(xwa).astype(BF16), w2_ref[...])
    decay = jnp.exp(-jnp.exp(_log_sigmoid(w_logit) - 0.5))
    a = jax.nn.sigmoid(a0_ref[...] + _dot(xwa.astype(BF16), a2_ref[...]))
    g = _dot(jax.nn.sigmoid(xg).astype(BF16), g2_ref[...])
    ones = _head_ones()
    kk = k * kk_ref[...]
    kk = kk / jnp.maximum(jnp.sqrt(_head_sum(kk * kk, ones)), 1e-12)
    k2 = k * (1.0 + (a - 1.0) * ka_ref[...])
    r_out[...] = r
    w_out[...] = decay
    k_out[...] = k2
    v_out[...] = v
    kap_out[...] = kk
    bet_out[...] = kk * a
    g_out[...] = g
    bon_out[...] = _head_sum(r * k2 * rk_ref[...], ones) * v


def _rwkv_prep(hr, prev8, prev_map, tm, tiles_per_seq, params):
    rows = hr.shape[0]
    out = jax.ShapeDtypeStruct((rows, RWKV_DIM), F32)
    return pl.pallas_call(
        functools.partial(_rwkv_prep_body, tiles_per_seq=tiles_per_seq),
        grid=(rows // tm,),
        in_specs=[_row_spec(tm, RWKV_COLS), pl.BlockSpec((1, 8, RWKV_COLS), prev_map)]
        + [_full_spec(p.shape) for p in params],
        out_specs=[_row_spec(tm, RWKV_DIM)] * 8,
        out_shape=[out] * 8,
        compiler_params=_cparams("parallel"),
        name="rwkv_prep",
    )(hr, prev8, *params)


def _to_chains_body(x_ref, o_ref, st_ref, *, nb, n_out, offsets):
    for b in range(nb):
        st_ref[b] = x_ref[b].T
    for j in range(n_out):
        groups = [st_ref[b, off * RWKV_HEADS:(off + 1) * RWKV_HEADS, :] for off in offsets(j) for b in range(nb)]
        o_ref[:, j * LANES:(j + 1) * LANES] = jnp.concatenate(groups, axis=0).T


def _to_chains_pallas(x, n_out, offsets):
    nb, t, _ = x.shape
    tt = LANES
    return pl.pallas_call(
        functools.partial(_to_chains_body, nb=nb, n_out=n_out, offsets=offsets),
        grid=(t // tt,),
        in_specs=[pl.BlockSpec((nb, tt, RWKV_DIM), lambda i: (0, i, 0))],
        out_specs=pl.BlockSpec((tt, n_out * LANES), lambda i: (i, 0)),
        out_shape=jax.ShapeDtypeStruct((t, n_out * LANES), F32),
        scratch_shapes=[pltpu.VMEM((nb, RWKV_DIM, tt), F32)],
        compiler_params=_cparams("parallel"),
        name="to_chains",
    )(x).reshape(t, n_out, LANES)


def _from_chains_body(o_ref, x_ref, st_ref, *, nb, nv, dup):
    nh = RWKV_HEADS
    for vp in range(nv):
        tile = o_ref[:, vp, :].T
        for vh in range(dup):
            ch = vh * nv + vp
            for b in range(nb):
                r0 = (vh * nb + b) * nh
                st_ref[b, ch * nh:(ch + 1) * nh, :] = tile[r0:r0 + nh, :]
    for b in range(nb):
        x_ref[b] = st_ref[b].T


def _from_chains_pallas(o, nb):
    t, nv, _ = o.shape
    dup = HEAD_DIM // nv
    tt = LANES
    return pl.pallas_call(
        functools.partial(_from_chains_body, nb=nb, nv=nv, dup=dup),
        grid=(t // tt,),
        in_specs=[pl.BlockSpec((tt, nv, LANES), lambda i: (i, 0, 0))],
        out_specs=pl.BlockSpec((nb, tt, RWKV_DIM), lambda i: (0, i, 0)),
        out_shape=jax.ShapeDtypeStruct((nb, t, RWKV_DIM), F32),
        scratch_shapes=[pltpu.VMEM((nb, RWKV_DIM, tt), F32)],
        compiler_params=_cparams("parallel"),
        name="from_chains",
    )(o)


def _rwkv_scan_body(w_ref, kap_ref, bet_ref, k2_ref, r_ref, v_ref, s0_ref, o_ref, s_ref, *, tb, nv):
    @pl.when(pl.program_id(0) == 0)
    def _():
        s_ref[...] = s0_ref[...]

    packed = w_ref.shape[1] < HEAD_DIM
    low = lax.broadcasted_iota(jnp.int32, (w_ref.shape[1], LANES), 1) < LANES // 2

    def operand(ref, t):
        x = ref[t]
        if not packed:
            return x
        swapped = pltpu.roll(x, LANES // 2, axis=1)
        return jnp.concatenate([jnp.where(low, x, swapped), jnp.where(low, swapped, x)], axis=0)

    def operands(t):
        return tuple(operand(ref, t) for ref in (w_ref, kap_ref, bet_ref, k2_ref, r_ref))

    def token(t, current):
        following = operands(jnp.minimum(t + 1, tb - 1))
        w, kap, bet, k2, r = current
        for vp in range(nv):
            s = s_ref[vp]
            rho = jnp.sum(s * kap, axis=0, keepdims=True)
            sn = s * w - bet * rho + k2 * v_ref[t, vp:vp + 1, :]
            s_ref[vp] = sn
            o_ref[t, vp:vp + 1, :] = jnp.sum(sn * r, axis=0, keepdims=True)
        return following

    lax.fori_loop(0, tb, token, operands(0))


def _rwkv_scan(w, kap, bet, k2, r, v, s0):
    t = w.shape[0]
    nv = v.shape[1]
    tb = min(SCAN_TOKENS, t)
    op_spec = pl.BlockSpec((tb, w.shape[1], LANES), lambda i: (i, 0, 0))
    v_spec = pl.BlockSpec((tb, nv, LANES), lambda i: (i, 0, 0))
    s_spec = pl.BlockSpec((nv, HEAD_DIM, LANES), lambda i: (0, 0, 0))
    return pl.pallas_call(
        functools.partial(_rwkv_scan_body, tb=tb, nv=nv),
        grid=(t // tb,),
        in_specs=[op_spec] * 5 + [v_spec, s_spec],
        out_specs=[v_spec, s_spec],
        out_shape=[jax.ShapeDtypeStruct((t, nv, LANES), F32), jax.ShapeDtypeStruct((nv, HEAD_DIM, LANES), F32)],
        compiler_params=_cparams("arbitrary"),
        name="rwkv_scan",
    )(w, kap, bet, k2, r, v, s0)


def _outproj_even_core(x, a, o, bon, gate, lng, lnb, w_ref):
    ones = _head_ones()
    d = o - _head_sum(o, ones) * (1.0 / HEAD_DIM)
    var = _head_sum(d * d, ones) * (1.0 / HEAD_DIM)
    y = d * lax.rsqrt(var + RWKV_GN_EPS) * lng + lnb
    b = (y + bon) * gate
    return x + _dot(a.astype(BF16), w_ref[0:FOX_DIM, :]) + _dot(b.astype(BF16), w_ref[FOX_DIM:, :])


def _outproj_even_body(x_ref, a_ref, o_ref, bon_ref, g_ref, lng_ref, lnb_ref, w_ref, out_ref):
    out_ref[...] = _outproj_even_core(x_ref[...], a_ref[...], o_ref[...], bon_ref[...], g_ref[...],
                                      lng_ref[...], lnb_ref[...], w_ref)


def _outproj_even(x, a, o, bon, g, lng, lnb, w):
    rows = x.shape[0]
    tm = min(ROW_TILE, rows)
    return pl.pallas_call(
        _outproj_even_body,
        grid=(rows // tm,),
        in_specs=[_row_spec(tm, D_MODEL)] + [_row_spec(tm, FOX_DIM)] * 4
        + [_full_spec(lng.shape), _full_spec(lnb.shape), _full_spec(w.shape)],
        out_specs=_row_spec(tm, D_MODEL),
        out_shape=jax.ShapeDtypeStruct(x.shape, F32),
        compiler_params=_cparams("parallel"),
        name="outproj_even",
    )(x, a, o, bon, g, lng, lnb, w)


def _rope(x, cos, sin_up, sin_dn):
    return x * cos + pltpu.roll(x, ROPE_DIM // 2, axis=1) * sin_up + pltpu.roll(x, LANES - ROPE_DIM // 2, axis=1) * sin_dn


def _inproj_odd_body(x_ref, g_ref, w_ref, qn_ref, kn_ref, vn_ref, cos_ref, sup_ref, sdn_ref,
                     q_ref, k_ref, v_ref, u_ref, gv_ref):
    n = _rms(x_ref[...], g_ref[...]).astype(BF16)
    cos, sup, sdn = cos_ref[...], sup_ref[...], sdn_ref[...]
    ones64 = _group_ones(HEAD_DIM)
    qw = SWA_Q
    for j in range(SWA_Q // LANES):
        cols = slice(j * LANES, (j + 1) * LANES)
        hq = _dot(n, w_ref[:, cols])
        qh = hq * lax.rsqrt(_group_sum(hq * hq, ones64) * (1.0 / HEAD_DIM) + NORM_EPS) * qn_ref[:, cols]
        q_ref[:, cols] = (_rope(qh, cos, sup, sdn) * ATT_SCALE).astype(BF16)
    hk = _dot(n, w_ref[:, qw:qw + SWA_KV])
    kn = hk * lax.rsqrt(_group_sum(hk * hk, ones64) * (1.0 / HEAD_DIM) + NORM_EPS) * kn_ref[...]
    k_ref[...] = _rope(kn, cos, sup, sdn)
    v_ref[...] = _dot(n, w_ref[:, qw + SWA_KV:qw + 2 * SWA_KV])
    s0 = qw + 2 * SWA_KV
    u_ref[...] = _gelu_tanh(_dot(n, w_ref[:, s0:s0 + SGU_DIM]))
    gv = _gelu_tanh(_dot(n, w_ref[:, s0 + SGU_DIM:]))
    gv_ref[...] = _rms(gv, vn_ref[...])


def _inproj_odd(x, g, w, qn, kn, vn, cos, sup, sdn, table_blocks):
    rows = x.shape[0]
    tm = min(ROW_TILE, rows)
    tab = pl.BlockSpec((tm, LANES), lambda i: (i % table_blocks, 0))
    outs = [jax.ShapeDtypeStruct((rows, SWA_Q), BF16), jax.ShapeDtypeStruct((rows, SWA_KV), F32),
            jax.ShapeDtypeStruct((rows, SWA_KV), F32), jax.ShapeDtypeStruct((rows, SGU_DIM), F32),
            jax.ShapeDtypeStruct((rows, SGU_DIM), F32)]
    return pl.pallas_call(
        _inproj_odd_body,
        grid=(rows // tm,),
        in_specs=[_row_spec(tm, D_MODEL), _full_spec(g.shape), _full_spec(w.shape), _full_spec(qn.shape),
                  _full_spec(kn.shape), _full_spec(vn.shape), tab, tab, tab],
        out_specs=[_row_spec(tm, SWA_Q), _row_spec(tm, SWA_KV), _row_spec(tm, SWA_KV),
                   _row_spec(tm, SGU_DIM), _row_spec(tm, SGU_DIM)],
        out_shape=outs,
        compiler_params=_cparams("parallel"),
        name="inproj_odd",
    )(x, g, w, qn, kn, vn, cos, sup, sdn)


def _sink_attend(q, kb, vb, sink, visible):
    s = jnp.where(visible, _dot_nt(q, kb), NEG_BIG)
    m = jnp.maximum(jnp.max(s, axis=1, keepdims=True), sink)
    p = jnp.exp(s - m)
    l = jnp.sum(p, axis=1, keepdims=True) + jnp.exp(sink - m)
    return _dot(p.astype(BF16), vb) / l


def _swa_body(sink_ref, q_ref, kp_ref, kc_ref, vp_ref, vc_ref, mask_ref, o_ref):
    kw = jnp.concatenate([kp_ref[0], kc_ref[0]], axis=0)
    vw = jnp.concatenate([vp_ref[0], vc_ref[0]], axis=0)
    ks = (kw.astype(BF16), pltpu.roll(kw, HEAD_DIM, axis=1).astype(BF16))
    vs = (vw.astype(BF16), pltpu.roll(vw, HEAD_DIM, axis=1).astype(BF16))
    visible = mask_ref[0] > 0.5
    half = _lane_half((q_ref.shape[1], LANES))
    for j in range(SWA_Q // LANES):
        cols = slice(j * LANES, (j + 1) * LANES)
        qb = q_ref[0, :, cols]
        outs = []
        for e in range(2):
            head = 2 * j + e
            swapped = int(head // SWA_GROUP != e)
            qh = jnp.where(half if e == 0 else jnp.logical_not(half), qb, jnp.zeros_like(qb))
            outs.append(_sink_attend(qh, ks[swapped], vs[swapped], sink_ref[head], visible))
        o_ref[0, :, cols] = jnp.where(half, outs[0], outs[1]).astype(BF16)


def _swa(sinks, q, k_prev, k_cur, v_prev, v_cur, mask, tq, prev_rows, prev_map, mask_map):
    b, t, qw = q.shape
    nk = prev_rows + tq
    cur = lambda bi, i: (bi, i, 0)
    return pl.pallas_call(
        _swa_body,
        grid=(b, t // tq),
        in_specs=[pl.BlockSpec(memory_space=pltpu.SMEM),
                  pl.BlockSpec((1, tq, qw), cur),
                  pl.BlockSpec((1, prev_rows, SWA_KV), prev_map), pl.BlockSpec((1, tq, SWA_KV), cur),
                  pl.BlockSpec((1, prev_rows, SWA_KV), prev_map), pl.BlockSpec((1, tq, SWA_KV), cur),
                  pl.BlockSpec((1, tq, nk), mask_map)],
        out_specs=pl.BlockSpec((1, tq, qw), cur),
        out_shape=jax.ShapeDtypeStruct((b, t, qw), BF16),
        compiler_params=_cparams("parallel", "parallel"),
        name="swa",
    )(sinks, q, k_prev, k_cur, v_prev, v_cur, mask)


def _sgu_body(u_ref, v_ref, w_ref, b_ref, o_ref, *, length, n_chunks):
    tril = (lax.broadcasted_iota(jnp.int32, (length, length), 1)
            <= lax.broadcasted_iota(jnp.int32, (length, length), 0))
    ws = [jnp.where(tril, w_ref[g], 0.0).astype(BF16) for g in range(SGU_GROUPS)]
    half = _lane_half((length, LANES))
    for c in range(n_chunks):
        rows = slice(c * length, (c + 1) * length)
        for j in range(SGU_GROUPS // 2):
            cols = slice(j * LANES, (j + 1) * LANES)
            vb = v_ref[rows, cols].astype(BF16)
            mixed = jnp.where(half, _dot(ws[2 * j], vb), _dot(ws[2 * j + 1], vb)) + b_ref[:, cols]
            o_ref[rows, cols] = u_ref[rows, cols] * mixed


def _sgu(u, v, w, bias, length, n_chunks):
    rows = u.shape[0]
    tm = length * n_chunks
    return pl.pallas_call(
        functools.partial(_sgu_body, length=length, n_chunks=n_chunks),
        grid=(rows // tm,),
        in_specs=[_row_spec(tm, SGU_DIM), _row_spec(tm, SGU_DIM), _full_spec(w.shape), _full_spec(bias.shape)],
        out_specs=_row_spec(tm, SGU_DIM),
        out_shape=jax.ShapeDtypeStruct(u.shape, F32),
        compiler_params=_cparams("parallel"),
        name="sgu",
    )(u, v, w, bias)


def _outproj_odd_core(x, c, d, w_ref):
    return x + _dot(c, w_ref[0:SWA_Q, :]) + _dot(d.astype(BF16), w_ref[SWA_Q:, :])


def _outproj_odd_body(x_ref, c_ref, d_ref, w_ref, out_ref):
    out_ref[...] = _outproj_odd_core(x_ref[...], c_ref[...], d_ref[...], w_ref)


def _outproj_odd(x, c, d, w):
    rows = x.shape[0]
    tm = min(ROW_TILE, rows)
    return pl.pallas_call(
        _outproj_odd_body,
        grid=(rows // tm,),
        in_specs=[_row_spec(tm, D_MODEL), _row_spec(tm, SWA_Q), _row_spec(tm, SGU_DIM), _full_spec(w.shape)],
        out_specs=_row_spec(tm, D_MODEL),
        out_shape=jax.ShapeDtypeStruct(x.shape, F32),
        compiler_params=_cparams("parallel"),
        name="outproj_odd",
    )(x, c, d, w)


def _memkv_body(m_ref, g_ref, w_ref, kn_ref, k_ref, v_ref):
    n = _rms(m_ref[0], g_ref[...]).astype(BF16)
    ones = _group_ones(HEAD_DIM)
    hk = _dot(n, w_ref[:, 0:MEM_DIM])
    k = hk * lax.rsqrt(_group_sum(hk * hk, ones) * (1.0 / HEAD_DIM) + NORM_EPS) * kn_ref[...]
    k_ref[0] = k.T
    v_ref[0] = _dot(n, w_ref[:, MEM_DIM:]).T


def _memkv(mem, g, w, kn):
    b, m, _ = mem.shape
    out = jax.ShapeDtypeStruct((b, MEM_DIM, m), F32)
    spec = pl.BlockSpec((1, MEM_DIM, m), lambda i: (i, 0, 0))
    return pl.pallas_call(
        _memkv_body,
        grid=(b,),
        in_specs=[pl.BlockSpec((1, m, D_MODEL), lambda i: (i, 0, 0)), _full_spec(g.shape), _full_spec(w.shape),
                  _full_spec(kn.shape)],
        out_specs=[spec, spec],
        out_shape=[out, out],
        compiler_params=_cparams("parallel"),
        name="memkv",
    )(mem, g, w, kn)


def _xattn_core(x, g, wq_ref, qn, mk_ref, mv_ref, wo_ref):
    tq = x.shape[0]
    n = _rms(x, g).astype(BF16)
    hq = _dot(n, wq_ref[...])
    ones = _group_ones(HEAD_DIM)
    q = (hq * lax.rsqrt(_group_sum(hq * hq, ones) * (1.0 / HEAD_DIM) + NORM_EPS) * qn * ATT_SCALE).astype(BF16)
    half = _lane_half((tq, LANES))
    blocks = []
    for j in range(MEM_HEADS // 2):
        cols = slice(j * LANES, (j + 1) * LANES)
        kb = mk_ref[0, cols, :].astype(BF16)
        vb = mv_ref[0, cols, :].astype(BF16)
        qb = q[:, cols]
        outs = []
        for h in range(2):
            sel = half if h == 0 else jnp.logical_not(half)
            s = _dot(jnp.where(sel, qb, jnp.zeros_like(qb)), kb)
            p = jnp.exp(s - jnp.max(s, axis=1, keepdims=True))
            outs.append(_dot_nt(p.astype(BF16), vb) / jnp.sum(p, axis=1, keepdims=True))
        blocks.append(jnp.where(half, outs[0], outs[1]))
    o = jnp.concatenate(blocks, axis=1).astype(BF16)
    return x + _dot(o, wo_ref[...])


def _xattn_body(x_ref, g_ref, wq_ref, qn_ref, mk_ref, mv_ref, wo_ref, out_ref):
    out_ref[0] = _xattn_core(x_ref[0], g_ref[...], wq_ref, qn_ref[...], mk_ref, mv_ref, wo_ref)


def _tail_body(*refs, even):
    n_mix = 8 if even else 4
    mix, rest = refs[:n_mix], refs[n_mix:]
    gx_ref, wq_ref, qn_ref, mk_ref, mv_ref, wo_ref, gf_ref, wg_ref, wu_ref, wd_ref, out_ref, acc_ref = rest
    if even:
        x_ref, a_ref, o_ref, bon_ref, gate_ref, lng_ref, lnb_ref, w_ref = mix
        x = _outproj_even_core(x_ref[0], a_ref[0], o_ref[0], bon_ref[0], gate_ref[0], lng_ref[...], lnb_ref[...], w_ref)
    else:
        x_ref, c_ref, d_ref, w_ref = mix
        x = _outproj_odd_core(x_ref[0], c_ref[0], d_ref[0], w_ref)
    x = _xattn_core(x, gx_ref[...], wq_ref, qn_ref[...], mk_ref, mv_ref, wo_ref)
    out_ref[0] = _ffn_core(x, gf_ref[...], wg_ref, wu_ref, wd_ref, acc_ref)


def _tail(even, layer, x, mix_rows, mix_params, xattn_params, mk, mv, wo, ffn_params, tm):
    b, t, _ = x.shape
    m = mk.shape[2]
    rows3 = lambda c: pl.BlockSpec((1, tm, c), lambda bi, i: (bi, i, 0))
    mem = pl.BlockSpec((1, MEM_DIM, m), lambda bi, i: (bi, 0, 0))
    args = [x, *mix_rows, *mix_params, *xattn_params, mk, mv, wo, *ffn_params]
    in_specs = ([rows3(a.shape[2]) for a in (x, *mix_rows)] + [_full_spec(p.shape) for p in mix_params]
                + [_full_spec(p.shape) for p in xattn_params] + [mem, mem, _full_spec(wo.shape)]
                + [_full_spec(ffn_params[0].shape)] + [_layer_spec(w, layer) for w in ffn_params[1:]])
    return pl.pallas_call(
        functools.partial(_tail_body, even=even),
        grid=(b, t // tm),
        in_specs=in_specs,
        out_specs=rows3(D_MODEL),
        out_shape=jax.ShapeDtypeStruct(x.shape, F32),
        scratch_shapes=[pltpu.VMEM((tm, D_MODEL), F32)],
        compiler_params=_cparams("parallel", "parallel"),
        name="tail_even" if even else "tail_odd",
    )(*args)


def _xattn(x, g, wq, qn, mk, mv, wo, tq):
    b, t, _ = x.shape
    m = mk.shape[2]
    return pl.pallas_call(
        _xattn_body,
        grid=(b, t // tq),
        in_specs=[pl.BlockSpec((1, tq, D_MODEL), lambda bi, i: (bi, i, 0)), _full_spec(g.shape),
                  _full_spec(wq.shape), _full_spec(qn.shape),
                  pl.BlockSpec((1, MEM_DIM, m), lambda bi, i: (bi, 0, 0)),
                  pl.BlockSpec((1, MEM_DIM, m), lambda bi, i: (bi, 0, 0)), _full_spec(wo.shape)],
        out_specs=pl.BlockSpec((1, tq, D_MODEL), lambda bi, i: (bi, i, 0)),
        out_shape=jax.ShapeDtypeStruct(x.shape, F32),
        compiler_params=_cparams("parallel", "parallel"),
        name="xattn",
    )(x, g, wq, qn, mk, mv, wo)


def _head_minor(x, axis=-1):
    axis %= x.ndim
    shape = x.shape
    y = x.reshape(shape[:axis] + (RWKV_HEADS, HEAD_DIM) + shape[axis + 1:])
    return jnp.swapaxes(y, axis, axis + 1).reshape(shape)


def _head_major(x, axis=-1):
    axis %= x.ndim
    shape = x.shape
    y = x.reshape(shape[:axis] + (HEAD_DIM, RWKV_HEADS) + shape[axis + 1:])
    return jnp.swapaxes(y, axis, axis + 1).reshape(shape)


def _cols_head_minor(x, inverse=False):
    n = 3 * RWKV_DIM
    blocks = x[..., :n].reshape(x.shape[:-1] + (3, RWKV_DIM))
    blocks = (_head_major if inverse else _head_minor)(blocks)
    return jnp.concatenate([blocks.reshape(x.shape[:-1] + (n,)), x[..., n:]], axis=-1)


def _to_chains(x, b, t, dup):
    y = x.reshape(b, t, HEAD_DIM, RWKV_HEADS).transpose(1, 2, 0, 3).reshape(t, HEAD_DIM, b * RWKV_HEADS)
    return jnp.concatenate([y] * dup, axis=-1) if dup > 1 else y


def _rope_tables(pos):
    half = ROPE_DIM // 2
    inv_freq = jnp.power(ROPE_THETA, -jnp.arange(half, dtype=F32) / half)
    ang = pos.astype(F32)[:, None] * inv_freq[None, :]
    cos, sin = jnp.cos(ang), jnp.sin(ang)
    n = pos.shape[0]
    pad = jnp.zeros((n, HEAD_DIM - ROPE_DIM), F32)
    zero = jnp.zeros((n, half), F32)
    cos_t = jnp.concatenate([cos, cos, pad + 1.0], axis=1)
    up_t = jnp.concatenate([zero, sin, pad], axis=1)
    dn_t = jnp.concatenate([-sin, zero, pad], axis=1)
    two = lambda a: jnp.concatenate([a, a], axis=1)
    return two(cos_t), two(up_t), two(dn_t)


def _swa_prompt_mask(tq):
    span = WINDOW + tq
    qc = np.arange(tq)[:, None] // CHUNK
    kc = np.arange(span)[None, :] // CHUNK - WINDOW_CHUNKS
    band = (kc <= qc) & (kc >= qc - WINDOW_CHUNKS)
    first = band & (kc >= 0)
    return jnp.asarray(np.stack([first, band]).astype(np.float32))


def _swa_sample_mask(past, rows, t):
    kc = (past - rows + np.arange(rows + t)) // CHUNK
    qc = (past + np.arange(t)) // CHUNK
    m = (kc[None, :] <= qc[:, None]) & (kc[None, :] >= qc[:, None] - WINDOW_CHUNKS)
    return jnp.asarray(m[None].astype(np.float32))


def kernel(x_prompt, x_sample, cache_fox_k, cache_fox_v, cache_fox_logf, state_rwkv, state_rwkv_shift, cache_swa_k, cache_swa_v, cache_mem_k, cache_mem_v, mem_prompt, ffn1_norm, ffn1_w_gate, ffn1_w_up, ffn1_w_down, mix_norm, ev_w_in, fox_b_f, fox_q_norm, fox_k_norm, rwkv_mu, rwkv_w0, rwkv_w2, rwkv_a0, rwkv_a2, rwkv_g2, rwkv_k_k, rwkv_k_a, rwkv_r_k, rwkv_ln_g, rwkv_ln_b, ev_w_out, od_w_in, swa_q_norm, swa_k_norm, swa_sinks, sgu_v_norm, sgu_w_s, sgu_b, od_w_out, xattn_norm, mem_norm, xattn_wq, xattn_wkv, xattn_q_norm, xattn_k_norm, xattn_wo, ffn2_norm, ffn2_w_gate, ffn2_w_up, ffn2_w_down):
    bp, tp, _ = x_prompt.shape
    bs, ts, _ = x_sample.shape
    depth = ffn1_norm.shape[0]
    past = cache_fox_k.shape[2]
    mem_tokens = mem_prompt.shape[1]
    xp = x_prompt.reshape(bp * tp, D_MODEL)
    xs = x_sample.reshape(bs * ts, D_MODEL)
    row = lambda a: a.reshape(1, -1)
    tile_heads = lambda a, n: jnp.tile(a, n).reshape(1, -1)
    ffn1_w = (ffn1_w_gate.astype(BF16), ffn1_w_up.astype(BF16), ffn1_w_down.astype(BF16))
    ffn2_w = (ffn2_w_gate.astype(BF16), ffn2_w_up.astype(BF16), ffn2_w_down.astype(BF16))

    out = {k: [] for k in ("p_fox_k", "p_fox_v", "p_fox_logf", "p_rwkv_state", "p_rwkv_shift", "p_swa_k", "p_swa_v",
                           "p_mem_k", "p_mem_v", "s_fox_k", "s_fox_v", "s_fox_logf", "s_rwkv_state", "s_rwkv_shift",
                           "s_swa_k", "s_swa_v", "s_sgu_v")}

    for l in range(depth):
        f1 = (row(ffn1_norm[l]),) + ffn1_w
        xp = _ffn(xp, l, *f1)
        xs = _ffn(xs, l, *f1)
        g_mix = row(mix_norm[l])
        if l % 2 == 0:
            e = l // 2
            w_in = ev_w_in[e]
            f0 = 3 * FOX_DIM
            w_cat = jnp.concatenate([w_in[:, :f0], jnp.pad(w_in[:, f0:f0 + FOX_HEADS], ((0, 0), (0, LANES - FOX_HEADS))),
                                     _cols_head_minor(w_in[:, f0 + FOX_HEADS:])], axis=1).astype(BF16)
            bf = jnp.pad(fox_b_f[e], (0, LANES - FOX_HEADS)).reshape(1, LANES)
            qn = tile_heads(fox_q_norm[e], FOX_HEADS)
            kn = tile_heads(fox_k_norm[e], FOX_HEADS)
            qp, ktp, vtp, lfp, hrp = _inproj_even(xp, g_mix, w_cat, bf, qn, kn, seq_len=tp)
            qs, ks, vs, lfs, hrs = _inproj_even(xs, g_mix, w_cat, bf, qn, kn)

            cum_p, cumt_p = _cumsum(lfp.reshape(bp, tp, LANES), CUM_TILE)
            a_p = _fox_prompt(qp.reshape(bp, tp, FOX_DIM), ktp, vtp, cum_p, cumt_p, ATT_TILE)
            lfs8 = lfs.reshape(bs, ts, LANES)[:, :, :FOX_HEADS]
            tot = past + ts
            padded = -(-tot // CUM_TILE) * CUM_TILE
            lf_all = jnp.concatenate([cache_fox_logf[e].astype(F32), lfs8], axis=1)
            lf_all = jnp.pad(lf_all, ((0, 0), (0, padded - tot), (0, LANES - FOX_HEADS)))
            cum_s, cumt_s = _cumsum(lf_all, CUM_TILE)
            token_minor = lambda c: jnp.transpose(c, (0, 2, 3, 1)).reshape(bs, FOX_DIM, past)
            a_s = _fox_sample(qs.reshape(bs, ts, FOX_DIM), ks.reshape(bs, ts, FOX_DIM), vs.reshape(bs, ts, FOX_DIM),
                              token_minor(cache_fox_k[e]), token_minor(cache_fox_v[e]), cum_s, cumt_s)

            hm = _head_minor
            w2p = jnp.pad(hm(rwkv_w2[e]), ((0, LANES - DECAY_LORA), (0, 0))).astype(BF16)
            a2p = jnp.pad(hm(rwkv_a2[e]), ((DECAY_LORA, 0), (0, 0))).astype(BF16)
            params = (row(_cols_head_minor(rwkv_mu[e])), row(hm(rwkv_w0[e])), w2p, row(hm(rwkv_a0[e])), a2p,
                      hm(rwkv_g2[e]).astype(BF16), row(hm(rwkv_k_k[e])), row(hm(rwkv_k_a[e])),
                      row(hm(rwkv_r_k[e].reshape(-1))))
            tiles_per_seq = tp // ROW_TILE
            prev_p = hrp.reshape(bp * tp // 8, 8, RWKV_COLS)
            prep_p = _rwkv_prep(hrp, prev_p, lambda i: (jnp.maximum(i * (ROW_TILE // 8) - 1, 0), 0, 0),
                                ROW_TILE, tiles_per_seq, params)
            prev_s = jnp.pad(_cols_head_minor(state_rwkv_shift[e].astype(F32)), ((0, 0), (7, 0), (0, 0)))
            prep_s = _rwkv_prep(hrs, prev_s, lambda i: (i, 0, 0), ts, 0, params)

            def scan(prep, b, t, state0):
                r, w, k2, v, kap, bet, g, bon = prep
                dup = LANES // (b * RWKV_HEADS)
                nv = HEAD_DIM // dup
                in_kernel = t % LANES == 0
                if in_kernel:
                    halves = lambda j: tuple(half * nv + j for half in range(dup))
                    ops = [_to_chains_pallas(a.reshape(b, t, RWKV_DIM), nv, halves) for a in (w, kap, bet, k2, r)]
                    vt = _to_chains_pallas(v.reshape(b, t, RWKV_DIM), nv, halves)
                else:
                    ops = [_to_chains(a, b, t, dup) for a in (w, kap, bet, k2, r)]
                    vt = v.reshape(b, t, dup, nv, RWKV_HEADS).transpose(1, 3, 2, 0, 4).reshape(t, nv, LANES)
                s0 = state0.reshape(b, RWKV_HEADS, dup, nv, HEAD_DIM).transpose(3, 4, 2, 0, 1).reshape(nv, HEAD_DIM, LANES)
                o, sT = _rwkv_scan(*ops, vt, s0)
                if in_kernel:
                    o = _from_chains_pallas(o, b).reshape(b * t, RWKV_DIM)
                else:
                    o = o.reshape(t, nv, dup, b, RWKV_HEADS).transpose(3, 0, 2, 1, 4).reshape(b * t, RWKV_DIM)
                sT = sT.reshape(nv, HEAD_DIM, dup, b, RWKV_HEADS).transpose(3, 4, 2, 0, 1).reshape(b, RWKV_HEADS, HEAD_DIM, HEAD_DIM)
                return o, sT, g, bon

            o_p, st_p, g_p, bon_p = scan(prep_p, bp, tp, jnp.zeros((bp, RWKV_HEADS, HEAD_DIM, HEAD_DIM), F32))
            o_s, st_s, g_s, bon_s = scan(prep_s, bs, ts, state_rwkv[e].astype(F32))

            w_out = jnp.concatenate([ev_w_out[e][:FOX_DIM], hm(ev_w_out[e][FOX_DIM:], axis=0)], axis=0).astype(BF16)
            lng, lnb = row(hm(rwkv_ln_g[e])), row(hm(rwkv_ln_b[e]))
            seq = lambda a: a.reshape(bp, tp, -1)
            mix_p = (True, (a_p, seq(o_p), seq(bon_p), seq(g_p)), (lng, lnb, w_out))
            xs = _outproj_even(xs, a_s.reshape(bs * ts, FOX_DIM), o_s, bon_s, g_s, lng, lnb, w_out)

            rows_of = lambda a: jnp.transpose(a.reshape(bp, FOX_HEADS, HEAD_DIM, tp), (0, 3, 1, 2))
            out["p_fox_k"].append(rows_of(ktp))
            out["p_fox_v"].append(rows_of(vtp))
            out["p_fox_logf"].append(lfp.reshape(bp, tp, LANES)[:, :, :FOX_HEADS])
            out["p_rwkv_state"].append(st_p)
            out["p_rwkv_shift"].append(_cols_head_minor(hrp.reshape(bp, tp, RWKV_COLS)[:, -1:], inverse=True))
            out["s_fox_k"].append(ks.reshape(bs, ts, FOX_HEADS, HEAD_DIM))
            out["s_fox_v"].append(vs.reshape(bs, ts, FOX_HEADS, HEAD_DIM))
            out["s_fox_logf"].append(lfs8)
            out["s_rwkv_state"].append(st_s)
            out["s_rwkv_shift"].append(_cols_head_minor(hrs.reshape(bs, ts, RWKV_COLS)[:, -1:], inverse=True))
        else:
            j = l // 2
            w_cat = od_w_in[j].astype(BF16)
            qn = tile_heads(swa_q_norm[j], SWA_HEADS)
            kn = tile_heads(swa_k_norm[j], SWA_KV_HEADS)
            vn = row(sgu_v_norm[j])
            tabs_p = _rope_tables(jnp.arange(tp))
            tabs_s = _rope_tables(past + jnp.arange(bs * ts) % ts)
            qp, kp, vp, up, gp = _inproj_odd(xp, g_mix, w_cat, qn, kn, vn, *tabs_p, tp // ROW_TILE)
            qs, ks, vs, us, gs = _inproj_odd(xs, g_mix, w_cat, qn, kn, vn, *tabs_s, 1)

            qw = SWA_Q
            kp3, vp3 = kp.reshape(bp, tp, SWA_KV), vp.reshape(bp, tp, SWA_KV)
            ratio = SWA_TILE // WINDOW
            c_p = _swa(swa_sinks[j], qp.reshape(bp, tp, qw), kp3, kp3, vp3, vp3, _swa_prompt_mask(SWA_TILE),
                       SWA_TILE, WINDOW, lambda bi, i: (bi, jnp.maximum(i * ratio - 1, 0), 0),
                       lambda bi, i: (jnp.minimum(i, 1), 0, 0))
            rows_c = cache_swa_k.shape[2]
            ck3 = cache_swa_k[j].reshape(bs, rows_c, SWA_KV)
            cv3 = cache_swa_v[j].reshape(bs, rows_c, SWA_KV)
            ks3, vs3 = ks.reshape(bs, ts, SWA_KV), vs.reshape(bs, ts, SWA_KV)
            c_s = _swa(swa_sinks[j], qs.reshape(bs, ts, qw), ck3, ks3, cv3, vs3, _swa_sample_mask(past, rows_c, ts),
                       ts, rows_c, lambda bi, i: (bi, 0, 0), lambda bi, i: (0, 0, 0))

            bias = jnp.repeat(jnp.transpose(sgu_b[j]), HEAD_DIM, axis=1)
            d_p = _sgu(up, gp, sgu_w_s[j], bias, SGU_CHUNK, ROW_TILE // SGU_CHUNK)
            d_s = _sgu(us, gs, sgu_w_s[j][:, :ts, :ts], bias[:ts], ts, 1)

            w_out = od_w_out[j].astype(BF16)
            mix_p = (False, (c_p, d_p.reshape(bp, tp, SGU_DIM)), (w_out,))
            xs = _outproj_odd(xs, c_s.reshape(bs * ts, qw), d_s, w_out)

            out["p_swa_k"].append(kp3[:, -WINDOW:].reshape(bp, WINDOW, SWA_KV_HEADS, HEAD_DIM))
            out["p_swa_v"].append(vp3[:, -WINDOW:].reshape(bp, WINDOW, SWA_KV_HEADS, HEAD_DIM))
            out["s_swa_k"].append(jnp.concatenate([ck3, ks3], axis=1)[:, -rows_c:].reshape(bs, rows_c, SWA_KV_HEADS, HEAD_DIM))
            out["s_swa_v"].append(jnp.concatenate([cv3, vs3], axis=1)[:, -rows_c:].reshape(bs, rows_c, SWA_KV_HEADS, HEAD_DIM))
            out["s_sgu_v"].append(gs.reshape(bs, ts, SGU_DIM))

        mk, mv = _memkv(mem_prompt, row(mem_norm[l]), xattn_wkv[l].astype(BF16), tile_heads(xattn_k_norm[l], MEM_HEADS))
        xa = (row(xattn_norm[l]), xattn_wq[l].astype(BF16), tile_heads(xattn_q_norm[l], MEM_HEADS))
        wo = xattn_wo[l].astype(BF16)
        f2 = (row(ffn2_norm[l]),) + ffn2_w
        xp = _tail(mix_p[0], l, xp.reshape(bp, tp, D_MODEL), mix_p[1], mix_p[2], xa, mk, mv, wo, f2,
                   ROW_TILE).reshape(bp * tp, D_MODEL)
        mem_minor = lambda c: jnp.transpose(c, (0, 2, 3, 1)).reshape(bs, MEM_DIM, mem_tokens)
        xs = _xattn(xs.reshape(bs, ts, D_MODEL), *xa, mem_minor(cache_mem_k[l]), mem_minor(cache_mem_v[l]),
                    wo, ts).reshape(bs * ts, D_MODEL)
        mem_rows_of = lambda a: jnp.transpose(a.reshape(bp, MEM_HEADS, HEAD_DIM, mem_tokens), (0, 3, 1, 2))
        out["p_mem_k"].append(mem_rows_of(mk))
        out["p_mem_v"].append(mem_rows_of(mv))

        xs = _ffn(xs, l, *f2)

    order = ("p_fox_k", "p_fox_v", "p_fox_logf", "p_rwkv_state", "p_rwkv_shift", "p_swa_k", "p_swa_v", "p_mem_k",
             "p_mem_v", "s_fox_k", "s_fox_v", "s_fox_logf", "s_rwkv_state", "s_rwkv_shift", "s_swa_k", "s_swa_v",
             "s_sgu_v")
    return (xp.reshape(bp, tp, D_MODEL), xs.reshape(bs, ts, D_MODEL)) + tuple(jnp.stack(out[k]) for k in order)
```

```python
import functools

import numpy as np
import jax
import jax.numpy as jnp
from jax import lax
from jax.experimental import pallas as pl
from jax.experimental.pallas import tpu as pltpu

F32 = jnp.float32
BF16 = jnp.bfloat16

D_MODEL = 1024
HEAD_DIM = 64
NORM_EPS = 1e-6
ROPE_THETA = 500000.0
ROPE_DIM = HEAD_DIM // 4
CHUNK = 64
FOX_HEADS = 8
FOX_DIM = FOX_HEADS * HEAD_DIM
RWKV_HEADS = 8
RWKV_DIM = RWKV_HEADS * HEAD_DIM
DECAY_LORA = 64
ICLR_LORA = 64
GATE_LORA = 128
RWKV_COLS = 3 * RWKV_DIM + DECAY_LORA + ICLR_LORA + GATE_LORA
RWKV_GN_EPS = 64e-5
SWA_HEADS = 8
SWA_KV_HEADS = 2
SWA_GROUP = SWA_HEADS // SWA_KV_HEADS
SWA_Q = SWA_HEADS * HEAD_DIM
SWA_KV = SWA_KV_HEADS * HEAD_DIM
WINDOW = 128
WINDOW_CHUNKS = WINDOW // CHUNK
SGU_GROUPS = 8
SGU_DIM = SGU_GROUPS * HEAD_DIM
SGU_CHUNK = 128
MEM_HEADS = 4
MEM_DIM = MEM_HEADS * HEAD_DIM
D_FF = 2816

LANES = 128
ROW_TILE = 512
FF_TILE = 256
ATT_TILE = 512
CUM_TILE = 256
SWA_TILE = 256
SCAN_TOKENS = 64
VMEM_LIMIT = 56 * 1024 * 1024
ATT_SCALE = HEAD_DIM ** -0.5
LOG2E = 1.4426950408889634
NEG_BIG = -1e30


def _cparams(*sem):
    return pltpu.CompilerParams(dimension_semantics=sem, vmem_limit_bytes=VMEM_LIMIT)


def _dot(a, b):
    return jnp.dot(a, b, preferred_element_type=F32)


def _dot_nt(a, b):
    return lax.dot_general(a, b, (((1,), (1,)), ((), ())), preferred_element_type=F32)


def _rms(x, g):
    ms = jnp.mean(x * x, axis=-1, keepdims=True)
    return (x * lax.rsqrt(ms + NORM_EPS)) * g


def _group_ones(group):
    shift = int(np.log2(group))
    r = lax.broadcasted_iota(jnp.int32, (LANES, LANES), 0) >> shift
    c = lax.broadcasted_iota(jnp.int32, (LANES, LANES), 1) >> shift
    return jnp.where(r == c, 1.0, 0.0).astype(BF16)


def _group_sum(x, ones):
    parts = []
    for j in range(x.shape[1] // LANES):
        blk = x[:, j * LANES:(j + 1) * LANES]
        hi = blk.astype(BF16)
        lo = (blk - hi.astype(F32)).astype(BF16)
        parts.append(_dot(hi, ones) + _dot(lo, ones))
    return parts[0] if len(parts) == 1 else jnp.concatenate(parts, axis=1)


def _head_ones():
    r = lax.broadcasted_iota(jnp.int32, (LANES, LANES), 0) & (RWKV_HEADS - 1)
    c = lax.broadcasted_iota(jnp.int32, (LANES, LANES), 1) & (RWKV_HEADS - 1)
    return jnp.where(r == c, 1.0, 0.0).astype(BF16)


def _head_sum(x, ones):
    part = x[:, 0:LANES]
    for j in range(1, x.shape[1] // LANES):
        part = part + x[:, j * LANES:(j + 1) * LANES]
    hi = part.astype(BF16)
    lo = (part - hi.astype(F32)).astype(BF16)
    tot = _dot(hi, ones) + _dot(lo, ones)
    return jnp.concatenate([tot] * (x.shape[1] // LANES), axis=1)


def _log_sigmoid(z):
    return jnp.minimum(z, 0.0) - jnp.log(1.0 + jnp.exp(-jnp.abs(z)))


def _gelu_tanh(x):
    return 0.5 * x * (1.0 + jnp.tanh(0.7978845608028654 * (x + 0.044715 * (x * x * x))))


def _lane_half(shape):
    return lax.broadcasted_iota(jnp.int32, shape, 1) < HEAD_DIM


def _row_spec(tm, cols):
    return pl.BlockSpec((tm, cols), lambda i: (i, 0))


def _full_spec(shape):
    nd = len(shape)
    return pl.BlockSpec(shape, lambda *_: (0,) * nd, pipeline_mode=pl.Buffered(1))


def _ffn_core(x, g, wg_ref, wu_ref, wd_ref, acc_ref):
    n = _rms(x, g).astype(BF16)
    for c in range(wg_ref.shape[2] // FF_TILE):
        cols = slice(c * FF_TILE, (c + 1) * FF_TILE)
        gate = _dot(n, wg_ref[0, :, cols])
        up = _dot(n, wu_ref[0, :, cols])
        act = (gate * jax.nn.sigmoid(gate) * up).astype(BF16)
        part = _dot(act, wd_ref[0, cols, :])
        if c == 0:
            acc_ref[...] = part
        else:
            acc_ref[...] += part
    return x + 0.5 * acc_ref[...]


def _ffn_body(x_ref, g_ref, wg_ref, wu_ref, wd_ref, o_ref, acc_ref):
    o_ref[...] = _ffn_core(x_ref[...], g_ref[...], wg_ref, wu_ref, wd_ref, acc_ref)


def _layer_spec(w, layer):
    nd = w.ndim
    return pl.BlockSpec((1,) + w.shape[1:], lambda *_: (layer,) + (0,) * (nd - 1), pipeline_mode=pl.Buffered(1))


def _ffn(x, layer, g, wg, wu, wd):
    rows = x.shape[0]
    tm = min(ROW_TILE, rows)
    return pl.pallas_call(
        _ffn_body,
        grid=(rows // tm,),
        in_specs=[_row_spec(tm, D_MODEL), _full_spec(g.shape), _layer_spec(wg, layer),
                  _layer_spec(wu, layer), _layer_spec(wd, layer)],
        out_specs=_row_spec(tm, D_MODEL),
        out_shape=jax.ShapeDtypeStruct(x.shape, F32),
        scratch_shapes=[pltpu.VMEM((tm, D_MODEL), F32)],
        compiler_params=_cparams("parallel"),
        name="ffn",
    )(x, g, wg, wu, wd)


def _inproj_even_body(x_ref, g_ref, w_ref, bf_ref, qn_ref, kn_ref,
                      q_ref, k_ref, v_ref, lf_ref, hr_ref, *, token_minor):
    n = _rms(x_ref[...], g_ref[...]).astype(BF16)
    ones = _group_ones(HEAD_DIM)
    hq = _dot(n, w_ref[:, 0:FOX_DIM])
    q = hq * lax.rsqrt(_group_sum(hq * hq, ones) * (1.0 / HEAD_DIM) + NORM_EPS) * qn_ref[...]
    q_ref[...] = (q * (ATT_SCALE * LOG2E)).astype(BF16)
    hk = _dot(n, w_ref[:, FOX_DIM:2 * FOX_DIM])
    k = hk * lax.rsqrt(_group_sum(hk * hk, ones) * (1.0 / HEAD_DIM) + NORM_EPS) * kn_ref[...]
    v = _dot(n, w_ref[:, 2 * FOX_DIM:3 * FOX_DIM])
    if token_minor:
        k_ref[0] = k.T
        v_ref[0] = v.T
    else:
        k_ref[...] = k
        v_ref[...] = v
    f0 = 3 * FOX_DIM
    lf_ref[...] = _log_sigmoid(_dot(n, w_ref[:, f0:f0 + LANES]) + bf_ref[...])
    hr_ref[...] = _dot(n, w_ref[:, f0 + LANES:])


def _inproj_even(x, g, w, bf, qn, kn, seq_len=None):
    rows = x.shape[0]
    tm = min(ROW_TILE, rows)
    if seq_len is None:
        kv_shape = jax.ShapeDtypeStruct((rows, FOX_DIM), F32)
        kv_spec = _row_spec(tm, FOX_DIM)
    else:
        per_seq = seq_len // tm
        kv_shape = jax.ShapeDtypeStruct((rows // seq_len, FOX_DIM, seq_len), F32)
        kv_spec = pl.BlockSpec((1, FOX_DIM, tm), lambda i: (i // per_seq, 0, i % per_seq))
    outs = [jax.ShapeDtypeStruct((rows, FOX_DIM), BF16), kv_shape, kv_shape,
            jax.ShapeDtypeStruct((rows, LANES), F32), jax.ShapeDtypeStruct((rows, RWKV_COLS), F32)]
    return pl.pallas_call(
        functools.partial(_inproj_even_body, token_minor=seq_len is not None),
        grid=(rows // tm,),
        in_specs=[_row_spec(tm, D_MODEL), _full_spec(g.shape), _full_spec(w.shape), _full_spec(bf.shape),
                  _full_spec(qn.shape), _full_spec(kn.shape)],
        out_specs=[_row_spec(tm, FOX_DIM), kv_spec, kv_spec, _row_spec(tm, LANES), _row_spec(tm, RWKV_COLS)],
        out_shape=outs,
        compiler_params=_cparams("parallel"),
        name="inproj_even",
    )(x, g, w, bf, qn, kn)


def _split3(x):
    hi = x.astype(BF16)
    r1 = x - hi.astype(F32)
    mid = r1.astype(BF16)
    lo = (r1 - mid.astype(F32)).astype(BF16)
    return hi, mid, lo


def _cumsum_body(lf_ref, col_ref, row_ref, *, n_chunks, tk):
    r = lax.broadcasted_iota(jnp.int32, (tk, tk), 0)
    c = lax.broadcasted_iota(jnp.int32, (tk, tk), 1)
    tri = jnp.where(c <= r, 1.0, 0.0).astype(BF16)
    carry = jnp.zeros((1, LANES), F32)
    for i in range(n_chunks):
        hi, mid, lo = _split3(lf_ref[0, i * tk:(i + 1) * tk, :])
        cs = _dot(tri, hi) + _dot(tri, mid) + _dot(tri, lo) + carry
        col_ref[0, i * tk:(i + 1) * tk, :] = cs
        row_ref[0, :, i * tk:(i + 1) * tk] = cs.T[0:FOX_HEADS, :]
        carry = cs[tk - 1:tk, :]


def _cumsum(lf, tk):
    b, length, _ = lf.shape
    n_chunks = length // tk
    return pl.pallas_call(
        functools.partial(_cumsum_body, n_chunks=n_chunks, tk=tk),
        grid=(b,),
        in_specs=[pl.BlockSpec((1, length, LANES), lambda i: (i, 0, 0))],
        out_specs=[pl.BlockSpec((1, length, LANES), lambda i: (i, 0, 0)),
                   pl.BlockSpec((1, FOX_HEADS, length), lambda i: (i, 0, 0))],
        out_shape=[jax.ShapeDtypeStruct((b, length, LANES), F32),
                   jax.ShapeDtypeStruct((b, FOX_HEADS, length), F32)],
        compiler_params=_cparams("parallel"),
        name="cumsum",
    )(lf)


def _place3(terms, src_lane, dst_lane):
    r = lax.broadcasted_iota(jnp.int32, (LANES, LANES), 0)
    c = lax.broadcasted_iota(jnp.int32, (LANES, LANES), 1)
    out = None
    for i, t in enumerate(terms):
        sel = jnp.where((r == src_lane) & (c == dst_lane + i), 1.0, 0.0).astype(BF16)
        out = _dot(t, sel) if out is None else out + _dot(t, sel)
    return out


def _fox_q(q_pair, cum_col, head, h):
    n = q_pair.shape[0]
    qf = q_pair.astype(F32)
    if h == 1:
        qf = pltpu.roll(qf, HEAD_DIM, axis=1)
    lane = lax.broadcasted_iota(jnp.int32, (n, LANES), 1)
    ones = jnp.where((lane >= HEAD_DIM + 3) & (lane < HEAD_DIM + 6), 1.0, 0.0)
    aug = _place3(_split3(cum_col * LOG2E), head, HEAD_DIM) + ones
    return jnp.where(lane < HEAD_DIM, qf, aug).astype(BF16)


def _fox_kt(kt, cum_row):
    n = kt.shape[1]
    hi, mid, lo = (x.astype(F32) for x in _split3(cum_row * LOG2E))
    row = lax.broadcasted_iota(jnp.int32, (8, n), 0)
    aug = jnp.where(row < 3, 1.0, jnp.where(row == 3, -hi, jnp.where(row == 4, -mid, jnp.where(row == 5, -lo, 0.0))))
    return jnp.concatenate([kt, aug, jnp.zeros((HEAD_DIM - 8, n), F32)], axis=0).astype(BF16)


def _fox_vt(vt):
    n = vt.shape[1]
    row = lax.broadcasted_iota(jnp.int32, (HEAD_DIM, n), 0)
    return jnp.concatenate([vt, jnp.where(row == 0, 1.0, 0.0)], axis=0).astype(BF16)


def _fox_finish(acc):
    return acc / acc[:, HEAD_DIM:HEAD_DIM + 1]


def _fox_prompt_body(q_ref, cum_ref, cumt_ref, kt_ref, vt_ref, o_ref, ka_sc, va_sc, m_sc, acc_sc, *, tq, t):
    hp = pl.program_id(1)
    i = pl.program_id(2)

    @pl.when(i == 0)
    def _():
        for h in range(2):
            rows = slice(h * HEAD_DIM, (h + 1) * HEAD_DIM)
            cum_row = cumt_ref[0, pl.ds(2 * hp + h, 1), :]
            for c in range(t // tq):
                cols = slice(c * tq, (c + 1) * tq)
                ka_sc[h, c] = _fox_kt(kt_ref[0, rows, cols], cum_row[:, cols])
                va_sc[h, c] = _fox_vt(vt_ref[0, rows, cols])

    off_q = pl.multiple_of(i * tq, tq)
    cq = cum_ref[0, pl.ds(off_q, tq), :]
    qs = [_fox_q(q_ref[0], cq, 2 * hp + h, h) for h in range(2)]
    m_sc[...] = jnp.full(m_sc.shape, NEG_BIG, F32)
    acc_sc[...] = jnp.zeros(acc_sc.shape, F32)
    causal = (lax.broadcasted_iota(jnp.int32, (tq, tq), 1) <= lax.broadcasted_iota(jnp.int32, (tq, tq), 0))

    def step(j, masked):
        scores = [_dot(qs[h], ka_sc[h, j]) for h in range(2)]
        for h in range(2):
            s = scores[h]
            if masked:
                s = jnp.where(causal, s, NEG_BIG)
            m_old = m_sc[h]
            m_new = jnp.maximum(m_old, jnp.max(s, axis=1, keepdims=True))
            p = jnp.concatenate([jnp.exp2(s[:, c * LANES:(c + 1) * LANES] - m_new) for c in range(tq // LANES)],
                                axis=1).astype(BF16)
            acc_sc[h] = jnp.exp2(m_old - m_new) * acc_sc[h] + _dot_nt(p, va_sc[h, j])
            m_sc[h] = m_new

    def past(j, carry):
        step(j, False)
        return carry

    lax.fori_loop(0, i, past, 0)
    step(i, True)
    second = pltpu.roll(_fox_finish(acc_sc[1]), HEAD_DIM, axis=1)
    o_ref[0] = jnp.where(_lane_half((tq, LANES)), _fox_finish(acc_sc[0]), second)


def _fox_prompt(q, kt, vt, cum, cumt, tq):
    b, t, _ = q.shape
    pairs = FOX_HEADS // 2
    nblk = t // tq
    return pl.pallas_call(
        functools.partial(_fox_prompt_body, tq=tq, t=t),
        grid=(b, pairs, nblk),
        in_specs=[pl.BlockSpec((1, tq, LANES), lambda bi, hp, i: (bi, i, hp)),
                  pl.BlockSpec((1, t, LANES), lambda bi, hp, i: (bi, 0, 0)),
                  pl.BlockSpec((1, FOX_HEADS, t), lambda bi, hp, i: (bi, 0, 0)),
                  pl.BlockSpec((1, LANES, t), lambda bi, hp, i: (bi, hp, 0)),
                  pl.BlockSpec((1, LANES, t), lambda bi, hp, i: (bi, hp, 0))],
        out_specs=pl.BlockSpec((1, tq, LANES), lambda bi, hp, i: (bi, i, hp)),
        out_shape=jax.ShapeDtypeStruct((b, t, FOX_DIM), F32),
        scratch_shapes=[pltpu.VMEM((2, nblk, LANES, tq), BF16), pltpu.VMEM((2, nblk, LANES, tq), BF16),
                        pltpu.VMEM((2, tq, LANES), F32), pltpu.VMEM((2, tq, LANES), F32)],
        compiler_params=_cparams("parallel", "parallel", "arbitrary"),
        name="fox_prompt",
    )(q, cum, cumt, kt, vt)


def _fox_sample_body(q_ref, cum_ref, cumt_ref, kt_ref, vt_ref, kn_ref, vn_ref, o_ref, *, ts, past):
    cum_n = cum_ref[0, 0:ts, :] * LOG2E
    causal = (lax.broadcasted_iota(jnp.int32, (ts, ts), 1) <= lax.broadcasted_iota(jnp.int32, (ts, ts), 0))
    lane = lax.broadcasted_iota(jnp.int32, (ts, LANES), 1)
    q = q_ref[0].astype(F32)
    kn = kn_ref[0]
    vn = vn_ref[0]
    outs = []
    for h in range(FOX_HEADS):
        cols = slice(h * HEAD_DIM, (h + 1) * HEAD_DIM)
        cq = jnp.sum(jnp.where(lane == h, cum_n, 0.0), axis=1, keepdims=True)
        ck_p = cumt_ref[0, h:h + 1, 0:past] * LOG2E
        ck_n = cumt_ref[0, h:h + 1, past:past + ts] * LOG2E
        qh = q[:, cols].astype(BF16)
        s_p = _dot(qh, kt_ref[0, cols, :].astype(BF16)) + (cq - ck_p)
        s_n = jnp.where(causal, _dot_nt(qh, kn[:, cols].astype(BF16)) + (cq - ck_n), NEG_BIG)
        m = jnp.maximum(jnp.max(s_p, axis=1, keepdims=True), jnp.max(s_n, axis=1, keepdims=True))
        p_p = jnp.exp2(s_p - m)
        p_n = jnp.exp2(s_n - m)
        l = jnp.sum(p_p, axis=1, keepdims=True) + jnp.sum(p_n, axis=1, keepdims=True)
        acc = (_dot_nt(p_p.astype(BF16), vt_ref[0, cols, :].astype(BF16))
               + _dot(p_n.astype(BF16), vn[:, cols].astype(BF16)))
        outs.append(acc / l)
    o_ref[0] = jnp.concatenate(outs, axis=1)


def _fox_sample(q, k_new, v_new, kt_past, vt_past, cum, cumt):
    b, ts, _ = q.shape
    p = kt_past.shape[2]
    length = cum.shape[1]
    assert p % CUM_TILE == 0 and ts <= CUM_TILE
    new = pl.BlockSpec((1, ts, FOX_DIM), lambda bi: (bi, 0, 0))
    old = pl.BlockSpec((1, FOX_DIM, p), lambda bi: (bi, 0, 0))
    return pl.pallas_call(
        functools.partial(_fox_sample_body, ts=ts, past=p),
        grid=(b,),
        in_specs=[new, pl.BlockSpec((1, CUM_TILE, LANES), lambda bi: (bi, p // CUM_TILE, 0)),
                  pl.BlockSpec((1, FOX_HEADS, length), lambda bi: (bi, 0, 0)), old, old, new, new],
        out_specs=new,
        out_shape=jax.ShapeDtypeStruct((b, ts, FOX_DIM), F32),
        compiler_params=_cparams("parallel"),
        name="fox_sample",
    )(q, cum, cumt, kt_past, vt_past, k_new, v_new)


def _rwkv_prep_body(h_ref, prev_ref, mu_ref, w0_ref, w2_ref, a0_ref, a2_ref, g2_ref, kk_ref, ka_ref, rk_ref,
                    r_out, w_out, k_out, v_out, kap_out, bet_out, g_out, bon_out, *, tiles_per_seq):
    h = h_ref[...]
    tm = h.shape[0]
    prev_row = prev_ref[0, 7:8, :]
    if tiles_per_seq:
        keep = jnp.where(pl.program_id(0) % tiles_per_seq != 0, 1.0, 0.0)
        prev_row = prev_row * keep
    first = lax.broadcasted_iota(jnp.int32, (tm, 1), 0) == 0
    prev = jnp.where(first, prev_row, pltpu.roll(h, 1, axis=0))
    hx = h + (prev - h) * mu_ref[...]
    r = hx[:, 0:RWKV_DIM]
    k = hx[:, RWKV_DIM:2 * RWKV_DIM]
    v = hx[:, 2 * RWKV_DIM:3 * RWKV_DIM]
    xwa = hx[:, 3 * RWKV_DIM:3 * RWKV_DIM + LANES]
    xg = hx[:, 3 * RWKV_DIM + LANES:]
    w_logit = w0_ref[...] + _dot(jnp.tanh(xwa).astype(BF16), w2_ref[...])
    decay = jnp.exp(-jnp.exp(_log_sigmoid(w_logit) - 0.5))
    a = jax.nn.sigmoid(a0_ref[...] + _dot(xwa.astype(BF16), a2_ref[...]))
    g = _dot(jax.nn.sigmoid(xg).astype(BF16), g2_ref[...])
    ones = _head_ones()
    kk = k * kk_ref[...]
    kk = kk / jnp.maximum(jnp.sqrt(_head_sum(kk * kk, ones)), 1e-12)
    k2 = k * (1.0 + (a - 1.0) * ka_ref[...])
    r_out[...] = r
    w_out[...] = decay
    k_out[...] = k2
    v_out[...] = v
    kap_out[...] = kk
    bet_out[...] = kk * a
    g_out[...] = g
    bon_out[...] = _head_sum(r * k2 * rk_ref[...], ones) * v


def _rwkv_prep(hr, prev8, prev_map, tm, tiles_per_seq, params):
    rows = hr.shape[0]
    out = jax.ShapeDtypeStruct((rows, RWKV_DIM), F32)
    return pl.pallas_call(
        functools.partial(_rwkv_prep_body, tiles_per_seq=tiles_per_seq),
        grid=(rows // tm,),
        in_specs=[_row_spec(tm, RWKV_COLS), pl.BlockSpec((1, 8, RWKV_COLS), prev_map)]
        + [_full_spec(p.shape) for p in params],
        out_specs=[_row_spec(tm, RWKV_DIM)] * 8,
        out_shape=[out] * 8,
        compiler_params=_cparams("parallel"),
        name="rwkv_prep",
    )(hr, prev8, *params)


def _to_chains_body(x_ref, o_ref, st_ref, *, nb, n_out, offsets):
    for b in range(nb):
        st_ref[b] = x_ref[b].T
    for j in range(n_out):
        groups = [st_ref[b, off * RWKV_HEADS:(off + 1) * RWKV_HEADS, :] for off in offsets(j) for b in range(nb)]
        o_ref[:, j * LANES:(j + 1) * LANES] = jnp.concatenate(groups, axis=0).T


def _to_chains_pallas(x, n_out, offsets):
    nb, t, _ = x.shape
    tt = LANES
    return pl.pallas_call(
        functools.partial(_to_chains_body, nb=nb, n_out=n_out, offsets=offsets),
        grid=(t // tt,),
        in_specs=[pl.BlockSpec((nb, tt, RWKV_DIM), lambda i: (0, i, 0))],
        out_specs=pl.BlockSpec((tt, n_out * LANES), lambda i: (i, 0)),
        out_shape=jax.ShapeDtypeStruct((t, n_out * LANES), F32),
        scratch_shapes=[pltpu.VMEM((nb, RWKV_DIM, tt), F32)],
        compiler_params=_cparams("parallel"),
        name="to_chains",
    )(x).reshape(t, n_out, LANES)


def _from_chains_body(o_ref, x_ref, st_ref, *, nb, nv, dup):
    nh = RWKV_HEADS
    for vp in range(nv):
        tile = o_ref[:, vp, :].T
        for vh in range(dup):
            ch = vh * nv + vp
            for b in range(nb):
                r0 = (vh * nb + b) * nh
                st_ref[b, ch * nh:(ch + 1) * nh, :] = tile[r0:r0 + nh, :]
    for b in range(nb):
        x_ref[b] = st_ref[b].T


def _from_chains_pallas(o, nb):
    t, nv, _ = o.shape
    dup = HEAD_DIM // nv
    tt = LANES
    return pl.pallas_call(
        functools.partial(_from_chains_body, nb=nb, nv=nv, dup=dup),
        grid=(t // tt,),
        in_specs=[pl.BlockSpec((tt, nv, LANES), lambda i: (i, 0, 0))],
        out_specs=pl.BlockSpec((nb, tt, RWKV_DIM), lambda i: (0, i, 0)),
        out_shape=jax.ShapeDtypeStruct((nb, t, RWKV_DIM), F32),
        scratch_shapes=[pltpu.VMEM((nb, RWKV_DIM, tt), F32)],
        compiler_params=_cparams("parallel"),
        name="from_chains",
    )(o)


def _rwkv_scan_body(w_ref, kap_ref, bet_ref, k2_ref, r_ref, v_ref, s0_ref, o_ref, s_ref, *, tb, nv):
    @pl.when(pl.program_id(0) == 0)
    def _():
        s_ref[...] = s0_ref[...]

    packed = w_ref.shape[1] < HEAD_DIM
    low = lax.broadcasted_iota(jnp.int32, (w_ref.shape[1], LANES), 1) < LANES // 2

    def operand(ref, t):
        x = ref[t]
        if not packed:
            return x
        swapped = pltpu.roll(x, LANES // 2, axis=1)
        return jnp.concatenate([jnp.where(low, x, swapped), jnp.where(low, swapped, x)], axis=0)

    def operands(t):
        return tuple(operand(ref, t) for ref in (w_ref, kap_ref, bet_ref, k2_ref, r_ref))

    def token(t, current):
        following = operands(jnp.minimum(t + 1, tb - 1))
        w, kap, bet, k2, r = current
        for vp in range(nv):
            s = s_ref[vp]
            rho = jnp.sum(s * kap, axis=0, keepdims=True)
            sn = s * w - bet * rho + k2 * v_ref[t, vp:vp + 1, :]
            s_ref[vp] = sn
            o_ref[t, vp:vp + 1, :] = jnp.sum(sn * r, axis=0, keepdims=True)
        return following

    lax.fori_loop(0, tb, token, operands(0))


def _rwkv_scan(w, kap, bet, k2, r, v, s0):
    t = w.shape[0]
    nv = v.shape[1]
    tb = min(SCAN_TOKENS, t)
    op_spec = pl.BlockSpec((tb, w.shape[1], LANES), lambda i: (i, 0, 0))
    v_spec = pl.BlockSpec((tb, nv, LANES), lambda i: (i, 0, 0))
    s_spec = pl.BlockSpec((nv, HEAD_DIM, LANES), lambda i: (0, 0, 0))
    return pl.pallas_call(
        functools.partial(_rwkv_scan_body, tb=tb, nv=nv),
        grid=(t // tb,),
        in_specs=[op_spec] * 5 + [v_spec, s_spec],
        out_specs=[v_spec, s_spec],
        out_shape=[jax.ShapeDtypeStruct((t, nv, LANES), F32), jax.ShapeDtypeStruct((nv, HEAD_DIM, LANES), F32)],
        compiler_params=_cparams("arbitrary"),
        name="rwkv_scan",
    )(w, kap, bet, k2, r, v, s0)


def _outproj_even_core(x, a, o, bon, gate, lng, lnb, w_ref):
    ones = _head_ones()
    d = o - _head_sum(o, ones) * (1.0 / HEAD_DIM)
    var = _head_sum(d * d, ones) * (1.0 / HEAD_DIM)
    y = d * lax.rsqrt(var + RWKV_GN_EPS) * lng + lnb
    b = (y + bon) * gate
    return x + _dot(a.astype(BF16), w_ref[0:FOX_DIM, :]) + _dot(b.astype(BF16), w_ref[FOX_DIM:, :])


def _outproj_even_body(x_ref, a_ref, o_ref, bon_ref, g_ref, lng_ref, lnb_ref, w_ref, out_ref):
    out_ref[...] = _outproj_even_core(x_ref[...], a_ref[...], o_ref[...], bon_ref[...], g_ref[...],
                                      lng_ref[...], lnb_ref[...], w_ref)


def _outproj_even(x, a, o, bon, g, lng, lnb, w):
    rows = x.shape[0]
    tm = min(ROW_TILE, rows)
    return pl.pallas_call(
        _outproj_even_body,
        grid=(rows // tm,),
        in_specs=[_row_spec(tm, D_MODEL)] + [_row_spec(tm, FOX_DIM)] * 4
        + [_full_spec(lng.shape), _full_spec(lnb.shape), _full_spec(w.shape)],
        out_specs=_row_spec(tm, D_MODEL),
        out_shape=jax.ShapeDtypeStruct(x.shape, F32),
        compiler_params=_cparams("parallel"),
        name="outproj_even",
    )(x, a, o, bon, g, lng, lnb, w)


def _rope(x, cos, sin_up, sin_dn):
    return x * cos + pltpu.roll(x, ROPE_DIM // 2, axis=1) * sin_up + pltpu.roll(x, LANES - ROPE_DIM // 2, axis=1) * sin_dn


def _inproj_odd_body(x_ref, g_ref, w_ref, qn_ref, kn_ref, vn_ref, cos_ref, sup_ref, sdn_ref,
                     q_ref, k_ref, v_ref, u_ref, gv_ref):
    n = _rms(x_ref[...], g_ref[...]).astype(BF16)
    cos, sup, sdn = cos_ref[...], sup_ref[...], sdn_ref[...]
    ones64 = _group_ones(HEAD_DIM)
    qw = SWA_Q

    def head_norm(h, gain):
        return h * lax.rsqrt(_group_sum(h * h, ones64) * (1.0 / HEAD_DIM) + NORM_EPS) * gain

    def q_block(j):
        cols = slice(j * LANES, (j + 1) * LANES)

        def finish(h):
            q_ref[:, cols] = (_rope(head_norm(h, qn_ref[:, cols]), cos, sup, sdn) * ATT_SCALE).astype(BF16)
        return cols, finish

    def k_finish(h):
        k_ref[...] = _rope(head_norm(h, kn_ref[...]), cos, sup, sdn)

    def v_finish(h):
        v_ref[...] = h

    def u_finish(h):
        u_ref[...] = _gelu_tanh(h)

    def gv_finish(h):
        gv_ref[...] = _rms(_gelu_tanh(h), vn_ref[...])

    s0 = qw + 2 * SWA_KV
    stages = [q_block(j) for j in range(SWA_Q // LANES)] + [
        (slice(qw, qw + SWA_KV), k_finish), (slice(qw + SWA_KV, s0), v_finish),
        (slice(s0, s0 + SGU_DIM), u_finish), (slice(s0 + SGU_DIM, s0 + 2 * SGU_DIM), gv_finish)]
    ahead = _dot(n, w_ref[:, stages[0][0]])
    for idx, (_, finish) in enumerate(stages):
        h = ahead
        if idx + 1 < len(stages):
            ahead = _dot(n, w_ref[:, stages[idx + 1][0]])
        finish(h)


def _inproj_odd(x, g, w, qn, kn, vn, cos, sup, sdn, table_blocks):
    rows = x.shape[0]
    tm = min(ROW_TILE, rows)
    tab = pl.BlockSpec((tm, LANES), lambda i: (i % table_blocks, 0))
    outs = [jax.ShapeDtypeStruct((rows, SWA_Q), BF16), jax.ShapeDtypeStruct((rows, SWA_KV), F32),
            jax.ShapeDtypeStruct((rows, SWA_KV), F32), jax.ShapeDtypeStruct((rows, SGU_DIM), F32),
            jax.ShapeDtypeStruct((rows, SGU_DIM), F32)]
    return pl.pallas_call(
        _inproj_odd_body,
        grid=(rows // tm,),
        in_specs=[_row_spec(tm, D_MODEL), _full_spec(g.shape), _full_spec(w.shape), _full_spec(qn.shape),
                  _full_spec(kn.shape), _full_spec(vn.shape), tab, tab, tab],
        out_specs=[_row_spec(tm, SWA_Q), _row_spec(tm, SWA_KV), _row_spec(tm, SWA_KV),
                   _row_spec(tm, SGU_DIM), _row_spec(tm, SGU_DIM)],
        out_shape=outs,
        compiler_params=_cparams("parallel"),
        name="inproj_odd",
    )(x, g, w, qn, kn, vn, cos, sup, sdn)


def _sink_attend(scores, vb, sink, visible):
    s = jnp.where(visible, scores, NEG_BIG)
    m = jnp.maximum(jnp.max(s, axis=1, keepdims=True), sink)
    p = jnp.exp(s - m)
    l = jnp.sum(p, axis=1, keepdims=True) + jnp.exp(sink - m)
    return _dot(p.astype(BF16), vb) / l


def _swa_body(sink_ref, q_ref, kp_ref, kc_ref, vp_ref, vc_ref, mask_ref, o_ref):
    tq = q_ref.shape[1]
    kw = jnp.concatenate([kp_ref[0], kc_ref[0]], axis=0)
    vw = jnp.concatenate([vp_ref[0], vc_ref[0]], axis=0)
    nk = kw.shape[0]
    kw_sw = pltpu.roll(kw, HEAD_DIM, axis=1)
    vw_sw = pltpu.roll(vw, HEAD_DIM, axis=1)
    key_half = _lane_half((nk, LANES))
    ks = [jnp.where(key_half, a, b).astype(BF16) for a, b in ((kw, kw_sw), (kw_sw, kw))]
    vs = [jnp.where(key_half, a, b).astype(BF16) for a, b in ((vw, vw_sw), (vw_sw, vw))]
    visible = jnp.concatenate([mask_ref[0] > 0.5] * 2, axis=0)
    half = _lane_half((tq, LANES))
    first = lax.broadcasted_iota(jnp.int32, (2 * tq, 1), 0) < tq
    n_blocks = SWA_Q // LANES
    scores = []
    for j in range(n_blocks):
        qb = q_ref[0, :, j * LANES:(j + 1) * LANES]
        zero = jnp.zeros_like(qb)
        stacked = jnp.concatenate([jnp.where(half, qb, zero), jnp.where(half, zero, qb)], axis=0)
        scores.append(_dot_nt(stacked, ks[(2 * j) // SWA_GROUP]))
    for j in range(n_blocks):
        sink = jnp.where(first, sink_ref[2 * j], sink_ref[2 * j + 1])
        o = _sink_attend(scores[j], vs[(2 * j) // SWA_GROUP], sink, visible)
        o_ref[0, :, j * LANES:(j + 1) * LANES] = jnp.where(half, o[0:tq], o[tq:]).astype(BF16)


def _swa(sinks, q, k_prev, k_cur, v_prev, v_cur, mask, tq, prev_rows, prev_map, mask_map):
    b, t, qw = q.shape
    nk = prev_rows + tq
    cur = lambda bi, i: (bi, i, 0)
    return pl.pallas_call(
        _swa_body,
        grid=(b, t // tq),
        in_specs=[pl.BlockSpec(memory_space=pltpu.SMEM),
                  pl.BlockSpec((1, tq, qw), cur),
                  pl.BlockSpec((1, prev_rows, SWA_KV), prev_map), pl.BlockSpec((1, tq, SWA_KV), cur),
                  pl.BlockSpec((1, prev_rows, SWA_KV), prev_map), pl.BlockSpec((1, tq, SWA_KV), cur),
                  pl.BlockSpec((1, tq, nk), mask_map)],
        out_specs=pl.BlockSpec((1, tq, qw), cur),
        out_shape=jax.ShapeDtypeStruct((b, t, qw), BF16),
        compiler_params=_cparams("parallel", "parallel"),
        name="swa",
    )(sinks, q, k_prev, k_cur, v_prev, v_cur, mask)


def _sgu_body(u_ref, v_ref, w_ref, b_ref, o_ref, *, length, n_chunks):
    tril = (lax.broadcasted_iota(jnp.int32, (length, length), 1)
            <= lax.broadcasted_iota(jnp.int32, (length, length), 0))
    ws = [jnp.where(tril, w_ref[g], 0.0).astype(BF16) for g in range(SGU_GROUPS)]
    half = _lane_half((length, LANES))
    for c in range(n_chunks):
        rows = slice(c * length, (c + 1) * length)
        for j in range(SGU_GROUPS // 2):
            cols = slice(j * LANES, (j + 1) * LANES)
            vb = v_ref[rows, cols].astype(BF16)
            mixed = jnp.where(half, _dot(ws[2 * j], vb), _dot(ws[2 * j + 1], vb)) + b_ref[:, cols]
            o_ref[rows, cols] = u_ref[rows, cols] * mixed


def _sgu(u, v, w, bias, length, n_chunks):
    rows = u.shape[0]
    tm = length * n_chunks
    return pl.pallas_call(
        functools.partial(_sgu_body, length=length, n_chunks=n_chunks),
        grid=(rows // tm,),
        in_specs=[_row_spec(tm, SGU_DIM), _row_spec(tm, SGU_DIM), _full_spec(w.shape), _full_spec(bias.shape)],
        out_specs=_row_spec(tm, SGU_DIM),
        out_shape=jax.ShapeDtypeStruct(u.shape, F32),
        compiler_params=_cparams("parallel"),
        name="sgu",
    )(u, v, w, bias)


def _outproj_odd_core(x, c, d, w_ref):
    return x + _dot(c, w_ref[0:SWA_Q, :]) + _dot(d.astype(BF16), w_ref[SWA_Q:, :])


def _outproj_odd_body(x_ref, c_ref, d_ref, w_ref, out_ref):
    out_ref[...] = _outproj_odd_core(x_ref[...], c_ref[...], d_ref[...], w_ref)


def _outproj_odd(x, c, d, w):
    rows = x.shape[0]
    tm = min(ROW_TILE, rows)
    return pl.pallas_call(
        _outproj_odd_body,
        grid=(rows // tm,),
        in_specs=[_row_spec(tm, D_MODEL), _row_spec(tm, SWA_Q), _row_spec(tm, SGU_DIM), _full_spec(w.shape)],
        out_specs=_row_spec(tm, D_MODEL),
        out_shape=jax.ShapeDtypeStruct(x.shape, F32),
        compiler_params=_cparams("parallel"),
        name="outproj_odd",
    )(x, c, d, w)


def _memkv_body(m_ref, g_ref, w_ref, kn_ref, k_ref, v_ref):
    n = _rms(m_ref[0], g_ref[...]).astype(BF16)
    ones = _group_ones(HEAD_DIM)
    hk = _dot(n, w_ref[:, 0:MEM_DIM])
    k = hk * lax.rsqrt(_group_sum(hk * hk, ones) * (1.0 / HEAD_DIM) + NORM_EPS) * kn_ref[...]
    k_ref[0] = k.T
    v_ref[0] = _dot(n, w_ref[:, MEM_DIM:]).T


def _memkv(mem, g, w, kn):
    b, m, _ = mem.shape
    out = jax.ShapeDtypeStruct((b, MEM_DIM, m), F32)
    spec = pl.BlockSpec((1, MEM_DIM, m), lambda i: (i, 0, 0))
    return pl.pallas_call(
        _memkv_body,
        grid=(b,),
        in_specs=[pl.BlockSpec((1, m, D_MODEL), lambda i: (i, 0, 0)), _full_spec(g.shape), _full_spec(w.shape),
                  _full_spec(kn.shape)],
        out_specs=[spec, spec],
        out_shape=[out, out],
        compiler_params=_cparams("parallel"),
        name="memkv",
    )(mem, g, w, kn)


def _xattn_core(x, g, wq_ref, qn, mk_ref, mv_ref, wo_ref):
    tq = x.shape[0]
    n = _rms(x, g).astype(BF16)
    hq = _dot(n, wq_ref[...])
    ones = _group_ones(HEAD_DIM)
    q = (hq * lax.rsqrt(_group_sum(hq * hq, ones) * (1.0 / HEAD_DIM) + NORM_EPS) * qn * ATT_SCALE).astype(BF16)
    half = _lane_half((tq, LANES))
    scores, values = [], []
    for j in range(MEM_HEADS // 2):
        cols = slice(j * LANES, (j + 1) * LANES)
        kb = mk_ref[0, cols, :].astype(BF16)
        values.append(mv_ref[0, cols, :].astype(BF16))
        qb = q[:, cols]
        zero = jnp.zeros_like(qb)
        heads = [jnp.where(half, qb, zero), jnp.where(half, zero, qb)]
        scores.append([_dot(qh, kb) for qh in ([jnp.concatenate(heads, axis=0)] if tq <= LANES else heads)])
    blocks = []
    for per_block, vb in zip(scores, values):
        outs = []
        for s in per_block:
            p = jnp.exp(s - jnp.max(s, axis=1, keepdims=True))
            outs.append(_dot_nt(p.astype(BF16), vb) / jnp.sum(p, axis=1, keepdims=True))
        if len(outs) == 1:
            outs = [outs[0][0:tq], outs[0][tq:]]
        blocks.append(jnp.where(half, outs[0], outs[1]))
    o = jnp.concatenate(blocks, axis=1).astype(BF16)
    return x + _dot(o, wo_ref[...])


def _xattn_body(x_ref, g_ref, wq_ref, qn_ref, mk_ref, mv_ref, wo_ref, out_ref):
    out_ref[0] = _xattn_core(x_ref[0], g_ref[...], wq_ref, qn_ref[...], mk_ref, mv_ref, wo_ref)


def _tail_body(*refs, even):
    n_mix = 8 if even else 4
    mix, rest = refs[:n_mix], refs[n_mix:]
    gx_ref, wq_ref, qn_ref, mk_ref, mv_ref, wo_ref, gf_ref, wg_ref, wu_ref, wd_ref, out_ref, acc_ref = rest
    if even:
        x_ref, a_ref, o_ref, bon_ref, gate_ref, lng_ref, lnb_ref, w_ref = mix
        x = _outproj_even_core(x_ref[0], a_ref[0], o_ref[0], bon_ref[0], gate_ref[0], lng_ref[...], lnb_ref[...], w_ref)
    else:
        x_ref, c_ref, d_ref, w_ref = mix
        x = _outproj_odd_core(x_ref[0], c_ref[0], d_ref[0], w_ref)
    x = _xattn_core(x, gx_ref[...], wq_ref, qn_ref[...], mk_ref, mv_ref, wo_ref)
    out_ref[0] = _ffn_core(x, gf_ref[...], wg_ref, wu_ref, wd_ref, acc_ref)


def _tail(even, layer, x, mix_rows, mix_params, xattn_params, mk, mv, wo, ffn_params, tm):
    b, t, _ = x.shape
    m = mk.shape[2]
    rows3 = lambda c: pl.BlockSpec((1, tm, c), lambda bi, i: (bi, i, 0))
    mem = pl.BlockSpec((1, MEM_DIM, m), lambda bi, i: (bi, 0, 0))
    args = [x, *mix_rows, *mix_params, *xattn_params, mk, mv, wo, *ffn_params]
    in_specs = ([rows3(a.shape[2]) for a in (x, *mix_rows)] + [_full_spec(p.shape) for p in mix_params]
                + [_full_spec(p.shape) for p in xattn_params] + [mem, mem, _full_spec(wo.shape)]
                + [_full_spec(ffn_params[0].shape)] + [_layer_spec(w, layer) for w in ffn_params[1:]])
    return pl.pallas_call(
        functools.partial(_tail_body, even=even),
        grid=(b, t // tm),
        in_specs=in_specs,
        out_specs=rows3(D_MODEL),
        out_shape=jax.ShapeDtypeStruct(x.shape, F32),
        scratch_shapes=[pltpu.VMEM((tm, D_MODEL), F32)],
        compiler_params=_cparams("parallel", "parallel"),
        name="tail_even" if even else "tail_odd",
    )(*args)


def _xattn(x, g, wq, qn, mk, mv, wo, tq):
    b, t, _ = x.shape
    m = mk.shape[2]
    return pl.pallas_call(
        _xattn_body,
        grid=(b, t // tq),
        in_specs=[pl.BlockSpec((1, tq, D_MODEL), lambda bi, i: (bi, i, 0)), _full_spec(g.shape),
                  _full_spec(wq.shape), _full_spec(qn.shape),
                  pl.BlockSpec((1, MEM_DIM, m), lambda bi, i: (bi, 0, 0)),
                  pl.BlockSpec((1, MEM_DIM, m), lambda bi, i: (bi, 0, 0)), _full_spec(wo.shape)],
        out_specs=pl.BlockSpec((1, tq, D_MODEL), lambda bi, i: (bi, i, 0)),
        out_shape=jax.ShapeDtypeStruct(x.shape, F32),
        compiler_params=_cparams("parallel", "parallel"),
        name="xattn",
    )(x, g, wq, qn, mk, mv, wo)


def _head_minor(x, axis=-1):
    axis %= x.ndim
    shape = x.shape
    y = x.reshape(shape[:axis] + (RWKV_HEADS, HEAD_DIM) + shape[axis + 1:])
    return jnp.swapaxes(y, axis, axis + 1).reshape(shape)


def _head_major(x, axis=-1):
    axis %= x.ndim
    shape = x.shape
    y = x.reshape(shape[:axis] + (HEAD_DIM, RWKV_HEADS) + shape[axis + 1:])
    return jnp.swapaxes(y, axis, axis + 1).reshape(shape)


def _cols_head_minor(x, inverse=False):
    n = 3 * RWKV_DIM
    blocks = x[..., :n].reshape(x.shape[:-1] + (3, RWKV_DIM))
    blocks = (_head_major if inverse else _head_minor)(blocks)
    return jnp.concatenate([blocks.reshape(x.shape[:-1] + (n,)), x[..., n:]], axis=-1)


def _to_chains(x, b, t, dup):
    y = x.reshape(b, t, HEAD_DIM, RWKV_HEADS).transpose(1, 2, 0, 3).reshape(t, HEAD_DIM, b * RWKV_HEADS)
    return jnp.concatenate([y] * dup, axis=-1) if dup > 1 else y


def _rope_tables(pos):
    half = ROPE_DIM // 2
    inv_freq = jnp.power(ROPE_THETA, -jnp.arange(half, dtype=F32) / half)
    ang = pos.astype(F32)[:, None] * inv_freq[None, :]
    cos, sin = jnp.cos(ang), jnp.sin(ang)
    n = pos.shape[0]
    pad = jnp.zeros((n, HEAD_DIM - ROPE_DIM), F32)
    zero = jnp.zeros((n, half), F32)
    cos_t = jnp.concatenate([cos, cos, pad + 1.0], axis=1)
    up_t = jnp.concatenate([zero, sin, pad], axis=1)
    dn_t = jnp.concatenate([-sin, zero, pad], axis=1)
    two = lambda a: jnp.concatenate([a, a], axis=1)
    return two(cos_t), two(up_t), two(dn_t)


def _swa_prompt_mask(tq):
    span = WINDOW + tq
    qc = np.arange(tq)[:, None] // CHUNK
    kc = np.arange(span)[None, :] // CHUNK - WINDOW_CHUNKS
    band = (kc <= qc) & (kc >= qc - WINDOW_CHUNKS)
    first = band & (kc >= 0)
    return jnp.asarray(np.stack([first, band]).astype(np.float32))


def _swa_sample_mask(past, rows, t):
    kc = (past - rows + np.arange(rows + t)) // CHUNK
    qc = (past + np.arange(t)) // CHUNK
    m = (kc[None, :] <= qc[:, None]) & (kc[None, :] >= qc[:, None] - WINDOW_CHUNKS)
    return jnp.asarray(m[None].astype(np.float32))


def kernel(x_prompt, x_sample, cache_fox_k, cache_fox_v, cache_fox_logf, state_rwkv, state_rwkv_shift, cache_swa_k, cache_swa_v, cache_mem_k, cache_mem_v, mem_prompt, ffn1_norm, ffn1_w_gate, ffn1_w_up, ffn1_w_down, mix_norm, ev_w_in, fox_b_f, fox_q_norm, fox_k_norm, rwkv_mu, rwkv_w0, rwkv_w2, rwkv_a0, rwkv_a2, rwkv_g2, rwkv_k_k, rwkv_k_a, rwkv_r_k, rwkv_ln_g, rwkv_ln_b, ev_w_out, od_w_in, swa_q_norm, swa_k_norm, swa_sinks, sgu_v_norm, sgu_w_s, sgu_b, od_w_out, xattn_norm, mem_norm, xattn_wq, xattn_wkv, xattn_q_norm, xattn_k_norm, xattn_wo, ffn2_norm, ffn2_w_gate, ffn2_w_up, ffn2_w_down):
    bp, tp, _ = x_prompt.shape
    bs, ts, _ = x_sample.shape
    depth = ffn1_norm.shape[0]
    past = cache_fox_k.shape[2]
    mem_tokens = mem_prompt.shape[1]
    xp = x_prompt.reshape(bp * tp, D_MODEL)
    xs = x_sample.reshape(bs * ts, D_MODEL)
    row = lambda a: a.reshape(1, -1)
    tile_heads = lambda a, n: jnp.tile(a, n).reshape(1, -1)
    ffn1_w = (ffn1_w_gate.astype(BF16), ffn1_w_up.astype(BF16), ffn1_w_down.astype(BF16))
    ffn2_w = (ffn2_w_gate.astype(BF16), ffn2_w_up.astype(BF16), ffn2_w_down.astype(BF16))

    out = {k: [] for k in ("p_fox_k", "p_fox_v", "p_fox_logf", "p_rwkv_state", "p_rwkv_shift", "p_swa_k", "p_swa_v",
                           "p_mem_k", "p_mem_v", "s_fox_k", "s_fox_v", "s_fox_logf", "s_rwkv_state", "s_rwkv_shift",
                           "s_swa_k", "s_swa_v", "s_sgu_v")}

    for l in range(depth):
        f1 = (row(ffn1_norm[l]),) + ffn1_w
        xp = _ffn(xp, l, *f1)
        xs = _ffn(xs, l, *f1)
        g_mix = row(mix_norm[l])
        if l % 2 == 0:
            e = l // 2
            w_in = ev_w_in[e]
            f0 = 3 * FOX_DIM
            w_cat = jnp.concatenate([w_in[:, :f0], jnp.pad(w_in[:, f0:f0 + FOX_HEADS], ((0, 0), (0, LANES - FOX_HEADS))),
                                     _cols_head_minor(w_in[:, f0 + FOX_HEADS:])], axis=1).astype(BF16)
            bf = jnp.pad(fox_b_f[e], (0, LANES - FOX_HEADS)).reshape(1, LANES)
            qn = tile_heads(fox_q_norm[e], FOX_HEADS)
            kn = tile_heads(fox_k_norm[e], FOX_HEADS)
            qp, ktp, vtp, lfp, hrp = _inproj_even(xp, g_mix, w_cat, bf, qn, kn, seq_len=tp)
            qs, ks, vs, lfs, hrs = _inproj_even(xs, g_mix, w_cat, bf, qn, kn)

            cum_p, cumt_p = _cumsum(lfp.reshape(bp, tp, LANES), CUM_TILE)
            a_p = _fox_prompt(qp.reshape(bp, tp, FOX_DIM), ktp, vtp, cum_p, cumt_p, ATT_TILE)
            lfs8 = lfs.reshape(bs, ts, LANES)[:, :, :FOX_HEADS]
            tot = past + ts
            padded = -(-tot // CUM_TILE) * CUM_TILE
            lf_all = jnp.concatenate([cache_fox_logf[e].astype(F32), lfs8], axis=1)
            lf_all = jnp.pad(lf_all, ((0, 0), (0, padded - tot), (0, LANES - FOX_HEADS)))
            cum_s, cumt_s = _cumsum(lf_all, CUM_TILE)
            token_minor = lambda c: jnp.transpose(c, (0, 2, 3, 1)).reshape(bs, FOX_DIM, past)
            a_s = _fox_sample(qs.reshape(bs, ts, FOX_DIM), ks.reshape(bs, ts, FOX_DIM), vs.reshape(bs, ts, FOX_DIM),
                              token_minor(cache_fox_k[e]), token_minor(cache_fox_v[e]), cum_s, cumt_s)

            hm = _head_minor
            w2p = jnp.pad(hm(rwkv_w2[e]), ((0, LANES - DECAY_LORA), (0, 0))).astype(BF16)
            a2p = jnp.pad(hm(rwkv_a2[e]), ((DECAY_LORA, 0), (0, 0))).astype(BF16)
            params = (row(_cols_head_minor(rwkv_mu[e])), row(hm(rwkv_w0[e])), w2p, row(hm(rwkv_a0[e])), a2p,
                      hm(rwkv_g2[e]).astype(BF16), row(hm(rwkv_k_k[e])), row(hm(rwkv_k_a[e])),
                      row(hm(rwkv_r_k[e].reshape(-1))))
            tiles_per_seq = tp // ROW_TILE
            prev_p = hrp.reshape(bp * tp // 8, 8, RWKV_COLS)
            prep_p = _rwkv_prep(hrp, prev_p, lambda i: (jnp.maximum(i * (ROW_TILE // 8) - 1, 0), 0, 0),
                                ROW_TILE, tiles_per_seq, params)
            prev_s = jnp.pad(_cols_head_minor(state_rwkv_shift[e].astype(F32)), ((0, 0), (7, 0), (0, 0)))
            prep_s = _rwkv_prep(hrs, prev_s, lambda i: (i, 0, 0), ts, 0, params)

            def scan(prep, b, t, state0):
                r, w, k2, v, kap, bet, g, bon = prep
                dup = LANES // (b * RWKV_HEADS)
                nv = HEAD_DIM // dup
                in_kernel = t % LANES == 0
                if in_kernel:
                    halves = lambda j: tuple(half * nv + j for half in range(dup))
                    ops = [_to_chains_pallas(a.reshape(b, t, RWKV_DIM), nv, halves) for a in (w, kap, bet, k2, r)]
                    vt = _to_chains_pallas(v.reshape(b, t, RWKV_DIM), nv, halves)
                else:
                    ops = [_to_chains(a, b, t, dup) for a in (w, kap, bet, k2, r)]
                    vt = v.reshape(b, t, dup, nv, RWKV_HEADS).transpose(1, 3, 2, 0, 4).reshape(t, nv, LANES)
                s0 = state0.reshape(b, RWKV_HEADS, dup, nv, HEAD_DIM).transpose(3, 4, 2, 0, 1).reshape(nv, HEAD_DIM, LANES)
                o, sT = _rwkv_scan(*ops, vt, s0)
                if in_kernel:
                    o = _from_chains_pallas(o, b).reshape(b * t, RWKV_DIM)
                else:
                    o = o.reshape(t, nv, dup, b, RWKV_HEADS).transpose(3, 0, 2, 1, 4).reshape(b * t, RWKV_DIM)
                sT = sT.reshape(nv, HEAD_DIM, dup, b, RWKV_HEADS).transpose(3, 4, 2, 0, 1).reshape(b, RWKV_HEADS, HEAD_DIM, HEAD_DIM)
                return o, sT, g, bon

            o_p, st_p, g_p, bon_p = scan(prep_p, bp, tp, jnp.zeros((bp, RWKV_HEADS, HEAD_DIM, HEAD_DIM), F32))
            o_s, st_s, g_s, bon_s = scan(prep_s, bs, ts, state_rwkv[e].astype(F32))

            w_out = jnp.concatenate([ev_w_out[e][:FOX_DIM], hm(ev_w_out[e][FOX_DIM:], axis=0)], axis=0).astype(BF16)
            lng, lnb = row(hm(rwkv_ln_g[e])), row(hm(rwkv_ln_b[e]))
            seq = lambda a: a.reshape(bp, tp, -1)
            mix_p = (True, (a_p, seq(o_p), seq(bon_p), seq(g_p)), (lng, lnb, w_out))
            xs = _outproj_even(xs, a_s.reshape(bs * ts, FOX_DIM), o_s, bon_s, g_s, lng, lnb, w_out)

            rows_of = lambda a: jnp.transpose(a.reshape(bp, FOX_HEADS, HEAD_DIM, tp), (0, 3, 1, 2))
            out["p_fox_k"].append(rows_of(ktp))
            out["p_fox_v"].append(rows_of(vtp))
            out["p_fox_logf"].append(lfp.reshape(bp, tp, LANES)[:, :, :FOX_HEADS])
            out["p_rwkv_state"].append(st_p)
            out["p_rwkv_shift"].append(_cols_head_minor(hrp.reshape(bp, tp, RWKV_COLS)[:, -1:], inverse=True))
            out["s_fox_k"].append(ks.reshape(bs, ts, FOX_HEADS, HEAD_DIM))
            out["s_fox_v"].append(vs.reshape(bs, ts, FOX_HEADS, HEAD_DIM))
            out["s_fox_logf"].append(lfs8)
            out["s_rwkv_state"].append(st_s)
            out["s_rwkv_shift"].append(_cols_head_minor(hrs.reshape(bs, ts, RWKV_COLS)[:, -1:], inverse=True))
        else:
            j = l // 2
            w_cat = od_w_in[j].astype(BF16)
            qn = tile_heads(swa_q_norm[j], SWA_HEADS)
            kn = tile_heads(swa_k_norm[j], SWA_KV_HEADS)
            vn = row(sgu_v_norm[j])
            tabs_p = _rope_tables(jnp.arange(tp))
            tabs_s = _rope_tables(past + jnp.arange(bs * ts) % ts)
            qp, kp, vp, up, gp = _inproj_odd(xp, g_mix, w_cat, qn, kn, vn, *tabs_p, tp // ROW_TILE)
            qs, ks, vs, us, gs = _inproj_odd(xs, g_mix, w_cat, qn, kn, vn, *tabs_s, 1)

            qw = SWA_Q
            kp3, vp3 = kp.reshape(bp, tp, SWA_KV), vp.reshape(bp, tp, SWA_KV)
            ratio = SWA_TILE // WINDOW
            c_p = _swa(swa_sinks[j], qp.reshape(bp, tp, qw), kp3, kp3, vp3, vp3, _swa_prompt_mask(SWA_TILE),
                       SWA_TILE, WINDOW, lambda bi, i: (bi, jnp.maximum(i * ratio - 1, 0), 0),
                       lambda bi, i: (jnp.minimum(i, 1), 0, 0))
            rows_c = cache_swa_k.shape[2]
            ck3 = cache_swa_k[j].reshape(bs, rows_c, SWA_KV)
            cv3 = cache_swa_v[j].reshape(bs, rows_c, SWA_KV)
            ks3, vs3 = ks.reshape(bs, ts, SWA_KV), vs.reshape(bs, ts, SWA_KV)
            c_s = _swa(swa_sinks[j], qs.reshape(bs, ts, qw), ck3, ks3, cv3, vs3, _swa_sample_mask(past, rows_c, ts),
                       ts, rows_c, lambda bi, i: (bi, 0, 0), lambda bi, i: (0, 0, 0))

            bias = jnp.repeat(jnp.transpose(sgu_b[j]), HEAD_DIM, axis=1)
            d_p = _sgu(up, gp, sgu_w_s[j], bias, SGU_CHUNK, ROW_TILE // SGU_CHUNK)
            d_s = _sgu(us, gs, sgu_w_s[j][:, :ts, :ts], bias[:ts], ts, 1)

            w_out = od_w_out[j].astype(BF16)
            mix_p = (False, (c_p, d_p.reshape(bp, tp, SGU_DIM)), (w_out,))
            xs = _outproj_odd(xs, c_s.reshape(bs * ts, qw), d_s, w_out)

            out["p_swa_k"].append(kp3[:, -WINDOW:].reshape(bp, WINDOW, SWA_KV_HEADS, HEAD_DIM))
            out["p_swa_v"].append(vp3[:, -WINDOW:].reshape(bp, WINDOW, SWA_KV_HEADS, HEAD_DIM))
            out["s_swa_k"].append(jnp.concatenate([ck3, ks3], axis=1)[:, -rows_c:].reshape(bs, rows_c, SWA_KV_HEADS, HEAD_DIM))
            out["s_swa_v"].append(jnp.concatenate([cv3, vs3], axis=1)[:, -rows_c:].reshape(bs, rows_c, SWA_KV_HEADS, HEAD_DIM))
            out["s_sgu_v"].append(gs.reshape(bs, ts, SGU_DIM))

        mk, mv = _memkv(mem_prompt, row(mem_norm[l]), xattn_wkv[l].astype(BF16), tile_heads(xattn_k_norm[l], MEM_HEADS))
        xa = (row(xattn_norm[l]), xattn_wq[l].astype(BF16), tile_heads(xattn_q_norm[l], MEM_HEADS))
        wo = xattn_wo[l].astype(BF16)
        f2 = (row(ffn2_norm[l]),) + ffn2_w
        xp = _tail(mix_p[0], l, xp.reshape(bp, tp, D_MODEL), mix_p[1], mix_p[2], xa, mk, mv, wo, f2,
                   ROW_TILE).reshape(bp * tp, D_MODEL)
        mem_minor = lambda c: jnp.transpose(c, (0, 2, 3, 1)).reshape(bs, MEM_DIM, mem_tokens)
        xs = _xattn(xs.reshape(bs, ts, D_MODEL), *xa, mem_minor(cache_mem_k[l]), mem_minor(cache_mem_v[l]),
                    wo, ts).reshape(bs * ts, D_MODEL)
        mem_rows_of = lambda a: jnp.transpose(a.reshape(bp, MEM_HEADS, HEAD_DIM, mem_tokens), (0, 3, 1, 2))
        out["p_mem_k"].append(mem_rows_of(mk))
        out["p_mem_v"].append(mem_rows_of(mv))

        xs = _ffn(xs, l, *f2)

    order = ("p_fox_k", "p_fox_v", "p_fox_logf", "p_rwkv_state", "p_rwkv_shift", "p_swa_k", "p_swa_v", "p_mem_k",
             "p_mem_v", "s_fox_k", "s_fox_v", "s_fox_logf", "s_rwkv_state", "s_rwkv_shift", "s_swa_k", "s_swa_v",
             "s_sgu_v")
    return (xp.reshape(bp, tp, D_MODEL), xs.reshape(bs, ts, D_MODEL)) + tuple(jnp.stack(out[k]) for k in order)
```

```python
import functools

import numpy as np
import jax
import jax.numpy as jnp
from jax import lax
from jax.experimental import pallas as pl
from jax.experimental.pallas import tpu as pltpu

F32 = jnp.float32
BF16 = jnp.bfloat16

D_MODEL = 1024
HEAD_DIM = 64
NORM_EPS = 1e-6
ROPE_THETA = 500000.0
ROPE_DIM = HEAD_DIM // 4
CHUNK = 64
FOX_HEADS = 8
FOX_DIM = FOX_HEADS * HEAD_DIM
RWKV_HEADS = 8
RWKV_DIM = RWKV_HEADS * HEAD_DIM
DECAY_LORA = 64
ICLR_LORA = 64
GATE_LORA = 128
RWKV_COLS = 3 * RWKV_DIM + DECAY_LORA + ICLR_LORA + GATE_LORA
RWKV_GN_EPS = 64e-5
SWA_HEADS = 8
SWA_KV_HEADS = 2
SWA_GROUP = SWA_HEADS // SWA_KV_HEADS
SWA_Q = SWA_HEADS * HEAD_DIM
SWA_KV = SWA_KV_HEADS * HEAD_DIM
WINDOW = 128
WINDOW_CHUNKS = WINDOW // CHUNK
SGU_GROUPS = 8
SGU_DIM = SGU_GROUPS * HEAD_DIM
SGU_CHUNK = 128
MEM_HEADS = 4
MEM_DIM = MEM_HEADS * HEAD_DIM
D_FF = 2816

LANES = 128
ROW_TILE = 512
FF_TILE = 256
ATT_TILE = 512
CUM_TILE = 256
SWA_TILE = 256
SCAN_TOKENS = 64
VMEM_LIMIT = 56 * 1024 * 1024
ATT_SCALE = HEAD_DIM ** -0.5
LOG2E = 1.4426950408889634
NEG_BIG = -1e30


def _cparams(*sem):
    return pltpu.CompilerParams(dimension_semantics=sem, vmem_limit_bytes=VMEM_LIMIT)


def _dot(a, b):
    return jnp.dot(a, b, preferred_element_type=F32)


def _dot_nt(a, b):
    return lax.dot_general(a, b, (((1,), (1,)), ((), ())), preferred_element_type=F32)


def _rms(x, g):
    ms = jnp.mean(x * x, axis=-1, keepdims=True)
    return (x * lax.rsqrt(ms + NORM_EPS)) * g


def _group_ones(group):
    shift = int(np.log2(group))
    r = lax.broadcasted_iota(jnp.int32, (LANES, LANES), 0) >> shift
    c = lax.broadcasted_iota(jnp.int32, (LANES, LANES), 1) >> shift
    return jnp.where(r == c, 1.0, 0.0).astype(BF16)


def _group_sum(x, ones):
    parts = []
    for j in range(x.shape[1] // LANES):
        blk = x[:, j * LANES:(j + 1) * LANES]
        hi = blk.astype(BF16)
        lo = (blk - hi.astype(F32)).astype(BF16)
        parts.append(_dot(hi, ones) + _dot(lo, ones))
    return parts[0] if len(parts) == 1 else jnp.concatenate(parts, axis=1)


def _head_ones():
    r = lax.broadcasted_iota(jnp.int32, (LANES, LANES), 0) & (RWKV_HEADS - 1)
    c = lax.broadcasted_iota(jnp.int32, (LANES, LANES), 1) & (RWKV_HEADS - 1)
    return jnp.where(r == c, 1.0, 0.0).astype(BF16)


def _head_sum(x, ones):
    part = x[:, 0:LANES]
    for j in range(1, x.shape[1] // LANES):
        part = part + x[:, j * LANES:(j + 1) * LANES]
    hi = part.astype(BF16)
    lo = (part - hi.astype(F32)).astype(BF16)
    tot = _dot(hi, ones) + _dot(lo, ones)
    return jnp.concatenate([tot] * (x.shape[1] // LANES), axis=1)


def _log_sigmoid(z):
    return jnp.minimum(z, 0.0) - jnp.log(1.0 + jnp.exp(-jnp.abs(z)))


def _gelu_tanh(x):
    return 0.5 * x * (1.0 + jnp.tanh(0.7978845608028654 * (x + 0.044715 * (x * x * x))))


def _lane_half(shape):
    return lax.broadcasted_iota(jnp.int32, shape, 1) < HEAD_DIM


def _row_spec(tm, cols):
    return pl.BlockSpec((tm, cols), lambda i: (i, 0))


def _full_spec(shape):
    nd = len(shape)
    return pl.BlockSpec(shape, lambda *_: (0,) * nd, pipeline_mode=pl.Buffered(1))


def _ffn_core(x, g, wg_ref, wu_ref, wd_ref, acc_ref):
    n = _rms(x, g).astype(BF16)
    for c in range(wg_ref.shape[2] // FF_TILE):
        cols = slice(c * FF_TILE, (c + 1) * FF_TILE)
        gate = _dot(n, wg_ref[0, :, cols])
        up = _dot(n, wu_ref[0, :, cols])
        act = (gate * jax.nn.sigmoid(gate) * up).astype(BF16)
        part = _dot(act, wd_ref[0, cols, :])
        if c == 0:
            acc_ref[...] = part
        else:
            acc_ref[...] += part
    return x + 0.5 * acc_ref[...]


def _ffn_body(x_ref, g_ref, wg_ref, wu_ref, wd_ref, o_ref, acc_ref):
    o_ref[...] = _ffn_core(x_ref[...], g_ref[...], wg_ref, wu_ref, wd_ref, acc_ref)


def _layer_spec(w, layer):
    nd = w.ndim
    return pl.BlockSpec((1,) + w.shape[1:], lambda *_: (layer,) + (0,) * (nd - 1), pipeline_mode=pl.Buffered(1))


def _ffn(x, layer, g, wg, wu, wd):
    rows = x.shape[0]
    tm = min(ROW_TILE, rows)
    return pl.pallas_call(
        _ffn_body,
        grid=(rows // tm,),
        in_specs=[_row_spec(tm, D_MODEL), _full_spec(g.shape), _layer_spec(wg, layer),
                  _layer_spec(wu, layer), _layer_spec(wd, layer)],
        out_specs=_row_spec(tm, D_MODEL),
        out_shape=jax.ShapeDtypeStruct(x.shape, F32),
        scratch_shapes=[pltpu.VMEM((tm, D_MODEL), F32)],
        compiler_params=_cparams("parallel"),
        name="ffn",
    )(x, g, wg, wu, wd)


def _inproj_even_body(x_ref, g_ref, w_ref, bf_ref, qn_ref, kn_ref,
                      q_ref, k_ref, v_ref, lf_ref, hr_ref, *, token_minor):
    n = _rms(x_ref[...], g_ref[...]).astype(BF16)
    ones = _group_ones(HEAD_DIM)
    hq = _dot(n, w_ref[:, 0:FOX_DIM])
    q = hq * lax.rsqrt(_group_sum(hq * hq, ones) * (1.0 / HEAD_DIM) + NORM_EPS) * qn_ref[...]
    q_ref[...] = (q * (ATT_SCALE * LOG2E)).astype(BF16)
    hk = _dot(n, w_ref[:, FOX_DIM:2 * FOX_DIM])
    k = hk * lax.rsqrt(_group_sum(hk * hk, ones) * (1.0 / HEAD_DIM) + NORM_EPS) * kn_ref[...]
    v = _dot(n, w_ref[:, 2 * FOX_DIM:3 * FOX_DIM])
    if token_minor:
        k_ref[0] = k.T
        v_ref[0] = v.T
    else:
        k_ref[...] = k
        v_ref[...] = v
    f0 = 3 * FOX_DIM
    lf_ref[...] = _log_sigmoid(_dot(n, w_ref[:, f0:f0 + LANES]) + bf_ref[...])
    hr_ref[...] = _dot(n, w_ref[:, f0 + LANES:])


def _inproj_even(x, g, w, bf, qn, kn, seq_len=None):
    rows = x.shape[0]
    tm = min(ROW_TILE, rows)
    if seq_len is None:
        kv_shape = jax.ShapeDtypeStruct((rows, FOX_DIM), F32)
        kv_spec = _row_spec(tm, FOX_DIM)
    else:
        per_seq = seq_len // tm
        kv_shape = jax.ShapeDtypeStruct((rows // seq_len, FOX_DIM, seq_len), F32)
        kv_spec = pl.BlockSpec((1, FOX_DIM, tm), lambda i: (i // per_seq, 0, i % per_seq))
    outs = [jax.ShapeDtypeStruct((rows, FOX_DIM), BF16), kv_shape, kv_shape,
            jax.ShapeDtypeStruct((rows, LANES), F32), jax.ShapeDtypeStruct((rows, RWKV_COLS), F32)]
    return pl.pallas_call(
        functools.partial(_inproj_even_body, token_minor=seq_len is not None),
        grid=(rows // tm,),
        in_specs=[_row_spec(tm, D_MODEL), _full_spec(g.shape), _full_spec(w.shape), _full_spec(bf.shape),
                  _full_spec(qn.shape), _full_spec(kn.shape)],
        out_specs=[_row_spec(tm, FOX_DIM), kv_spec, kv_spec, _row_spec(tm, LANES), _row_spec(tm, RWKV_COLS)],
        out_shape=outs,
        compiler_params=_cparams("parallel"),
        name="inproj_even",
    )(x, g, w, bf, qn, kn)


def _split3(x):
    hi = x.astype(BF16)
    r1 = x - hi.astype(F32)
    mid = r1.astype(BF16)
    lo = (r1 - mid.astype(F32)).astype(BF16)
    return hi, mid, lo


def _cumsum_body(lf_ref, col_ref, row_ref, *, n_chunks, tk):
    r = lax.broadcasted_iota(jnp.int32, (tk, tk), 0)
    c = lax.broadcasted_iota(jnp.int32, (tk, tk), 1)
    tri = jnp.where(c <= r, 1.0, 0.0).astype(BF16)
    carry = jnp.zeros((1, LANES), F32)
    for i in range(n_chunks):
        hi, mid, lo = _split3(lf_ref[0, i * tk:(i + 1) * tk, :])
        cs = _dot(tri, hi) + _dot(tri, mid) + _dot(tri, lo) + carry
        col_ref[0, i * tk:(i + 1) * tk, :] = cs
        row_ref[0, :, i * tk:(i + 1) * tk] = cs.T[0:FOX_HEADS, :]
        carry = cs[tk - 1:tk, :]


def _cumsum(lf, tk):
    b, length, _ = lf.shape
    n_chunks = length // tk
    return pl.pallas_call(
        functools.partial(_cumsum_body, n_chunks=n_chunks, tk=tk),
        grid=(b,),
        in_specs=[pl.BlockSpec((1, length, LANES), lambda i: (i, 0, 0))],
        out_specs=[pl.BlockSpec((1, length, LANES), lambda i: (i, 0, 0)),
                   pl.BlockSpec((1, FOX_HEADS, length), lambda i: (i, 0, 0))],
        out_shape=[jax.ShapeDtypeStruct((b, length, LANES), F32),
                   jax.ShapeDtypeStruct((b, FOX_HEADS, length), F32)],
        compiler_params=_cparams("parallel"),
        name="cumsum",
    )(lf)


def _place3(terms, src_lane, dst_lane):
    r = lax.broadcasted_iota(jnp.int32, (LANES, LANES), 0)
    c = lax.broadcasted_iota(jnp.int32, (LANES, LANES), 1)
    out = None
    for i, t in enumerate(terms):
        sel = jnp.where((r == src_lane) & (c == dst_lane + i), 1.0, 0.0).astype(BF16)
        out = _dot(t, sel) if out is None else out + _dot(t, sel)
    return out


def _fox_q(q_pair, cum_col, head, h):
    n = q_pair.shape[0]
    qf = q_pair.astype(F32)
    if h == 1:
        qf = pltpu.roll(qf, HEAD_DIM, axis=1)
    lane = lax.broadcasted_iota(jnp.int32, (n, LANES), 1)
    ones = jnp.where((lane >= HEAD_DIM + 3) & (lane < HEAD_DIM + 6), 1.0, 0.0)
    aug = _place3(_split3(cum_col * LOG2E), head, HEAD_DIM) + ones
    return jnp.where(lane < HEAD_DIM, qf, aug).astype(BF16)


def _fox_kt(kt, cum_row):
    n = kt.shape[1]
    hi, mid, lo = (x.astype(F32) for x in _split3(cum_row * LOG2E))
    row = lax.broadcasted_iota(jnp.int32, (8, n), 0)
    aug = jnp.where(row < 3, 1.0, jnp.where(row == 3, -hi, jnp.where(row == 4, -mid, jnp.where(row == 5, -lo, 0.0))))
    return jnp.concatenate([kt, aug, jnp.zeros((HEAD_DIM - 8, n), F32)], axis=0).astype(BF16)


def _fox_vt(vt):
    n = vt.shape[1]
    row = lax.broadcasted_iota(jnp.int32, (HEAD_DIM, n), 0)
    return jnp.concatenate([vt, jnp.where(row == 0, 1.0, 0.0)], axis=0).astype(BF16)


def _fox_finish(acc):
    return acc / acc[:, HEAD_DIM:HEAD_DIM + 1]


def _fox_prompt_body(q_ref, cum_ref, cumt_ref, kt_ref, vt_ref, o_ref, ka_sc, va_sc, m_sc, acc_sc, *, tq, t):
    hp = pl.program_id(1)
    i = pl.program_id(2)

    @pl.when(i == 0)
    def _():
        for h in range(2):
            rows = slice(h * HEAD_DIM, (h + 1) * HEAD_DIM)
            cum_row = cumt_ref[0, pl.ds(2 * hp + h, 1), :]
            for c in range(t // tq):
                cols = slice(c * tq, (c + 1) * tq)
                ka_sc[h, c] = _fox_kt(kt_ref[0, rows, cols], cum_row[:, cols])
                va_sc[h, c] = _fox_vt(vt_ref[0, rows, cols])

    off_q = pl.multiple_of(i * tq, tq)
    cq = cum_ref[0, pl.ds(off_q, tq), :]
    qs = [_fox_q(q_ref[0], cq, 2 * hp + h, h) for h in range(2)]
    m_sc[...] = jnp.full(m_sc.shape, NEG_BIG, F32)
    acc_sc[...] = jnp.zeros(acc_sc.shape, F32)
    causal = (lax.broadcasted_iota(jnp.int32, (tq, tq), 1) <= lax.broadcasted_iota(jnp.int32, (tq, tq), 0))

    def step(j, masked):
        scores = [_dot(qs[h], ka_sc[h, j]) for h in range(2)]
        for h in range(2):
            s = scores[h]
            if masked:
                s = jnp.where(causal, s, NEG_BIG)
            m_old = m_sc[h]
            m_new = jnp.maximum(m_old, jnp.max(s, axis=1, keepdims=True))
            p = jnp.concatenate([jnp.exp2(s[:, c * LANES:(c + 1) * LANES] - m_new) for c in range(tq // LANES)],
                                axis=1).astype(BF16)
            acc_sc[h] = jnp.exp2(m_old - m_new) * acc_sc[h] + _dot_nt(p, va_sc[h, j])
            m_sc[h] = m_new

    def past(j, carry):
        step(j, False)
        return carry

    lax.fori_loop(0, i, past, 0)
    step(i, True)
    second = pltpu.roll(_fox_finish(acc_sc[1]), HEAD_DIM, axis=1)
    o_ref[0] = jnp.where(_lane_half((tq, LANES)), _fox_finish(acc_sc[0]), second)


def _fox_prompt(q, kt, vt, cum, cumt, tq):
    b, t, _ = q.shape
    pairs = FOX_HEADS // 2
    nblk = t // tq
    return pl.pallas_call(
        functools.partial(_fox_prompt_body, tq=tq, t=t),
        grid=(b, pairs, nblk),
        in_specs=[pl.BlockSpec((1, tq, LANES), lambda bi, hp, i: (bi, i, hp)),
                  pl.BlockSpec((1, t, LANES), lambda bi, hp, i: (bi, 0, 0)),
                  pl.BlockSpec((1, FOX_HEADS, t), lambda bi, hp, i: (bi, 0, 0)),
                  pl.BlockSpec((1, LANES, t), lambda bi, hp, i: (bi, hp, 0)),
                  pl.BlockSpec((1, LANES, t), lambda bi, hp, i: (bi, hp, 0))],
        out_specs=pl.BlockSpec((1, tq, LANES), lambda bi, hp, i: (bi, i, hp)),
        out_shape=jax.ShapeDtypeStruct((b, t, FOX_DIM), F32),
        scratch_shapes=[pltpu.VMEM((2, nblk, LANES, tq), BF16), pltpu.VMEM((2, nblk, LANES, tq), BF16),
                        pltpu.VMEM((2, tq, LANES), F32), pltpu.VMEM((2, tq, LANES), F32)],
        compiler_params=_cparams("parallel", "parallel", "arbitrary"),
        name="fox_prompt",
    )(q, cum, cumt, kt, vt)


def _fox_sample_body(q_ref, cum_ref, cumt_ref, kt_ref, vt_ref, kn_ref, vn_ref, o_ref, *, ts, past):
    cum_n = cum_ref[0, 0:ts, :] * LOG2E
    causal = (lax.broadcasted_iota(jnp.int32, (ts, ts), 1) <= lax.broadcasted_iota(jnp.int32, (ts, ts), 0))
    lane = lax.broadcasted_iota(jnp.int32, (ts, LANES), 1)
    q = q_ref[0].astype(F32)
    kn = kn_ref[0]
    vn = vn_ref[0]
    outs = []
    for h in range(FOX_HEADS):
        cols = slice(h * HEAD_DIM, (h + 1) * HEAD_DIM)
        cq = jnp.sum(jnp.where(lane == h, cum_n, 0.0), axis=1, keepdims=True)
        ck_p = cumt_ref[0, h:h + 1, 0:past] * LOG2E
        ck_n = cumt_ref[0, h:h + 1, past:past + ts] * LOG2E
        qh = q[:, cols].astype(BF16)
        s_p = _dot(qh, kt_ref[0, cols, :].astype(BF16)) + (cq - ck_p)
        s_n = jnp.where(causal, _dot_nt(qh, kn[:, cols].astype(BF16)) + (cq - ck_n), NEG_BIG)
        m = jnp.maximum(jnp.max(s_p, axis=1, keepdims=True), jnp.max(s_n, axis=1, keepdims=True))
        p_p = jnp.exp2(s_p - m)
        p_n = jnp.exp2(s_n - m)
        l = jnp.sum(p_p, axis=1, keepdims=True) + jnp.sum(p_n, axis=1, keepdims=True)
        acc = (_dot_nt(p_p.astype(BF16), vt_ref[0, cols, :].astype(BF16))
               + _dot(p_n.astype(BF16), vn[:, cols].astype(BF16)))
        outs.append(acc / l)
    o_ref[0] = jnp.concatenate(outs, axis=1)


def _fox_sample(q, k_new, v_new, kt_past, vt_past, cum, cumt):
    b, ts, _ = q.shape
    p = kt_past.shape[2]
    length = cum.shape[1]
    assert p % CUM_TILE == 0 and ts <= CUM_TILE
    new = pl.BlockSpec((1, ts, FOX_DIM), lambda bi: (bi, 0, 0))
    old = pl.BlockSpec((1, FOX_DIM, p), lambda bi: (bi, 0, 0))
    return pl.pallas_call(
        functools.partial(_fox_sample_body, ts=ts, past=p),
        grid=(b,),
        in_specs=[new, pl.BlockSpec((1, CUM_TILE, LANES), lambda bi: (bi, p // CUM_TILE, 0)),
                  pl.BlockSpec((1, FOX_HEADS, length), lambda bi: (bi, 0, 0)), old, old, new, new],
        out_specs=new,
        out_shape=jax.ShapeDtypeStruct((b, ts, FOX_DIM), F32),
        compiler_params=_cparams("parallel"),
        name="fox_sample",
    )(q, cum, cumt, kt_past, vt_past, k_new, v_new)


def _rwkv_prep_body(h_ref, prev_ref, mu_ref, w0_ref, w2_ref, a0_ref, a2_ref, g2_ref, kk_ref, ka_ref, rk_ref,
                    r_out, w_out, k_out, v_out, kap_out, bet_out, g_out, bon_out, *, tiles_per_seq):
    h = h_ref[...]
    tm = h.shape[0]
    prev_row = prev_ref[0, 7:8, :]
    if tiles_per_seq:
        keep = jnp.where(pl.program_id(0) % tiles_per_seq != 0, 1.0, 0.0)
        prev_row = prev_row * keep
    first = lax.broadcasted_iota(jnp.int32, (tm, 1), 0) == 0
    prev = jnp.where(first, prev_row, pltpu.roll(h, 1, axis=0))
    hx = h + (prev - h) * mu_ref[...]
    r = hx[:, 0:RWKV_DIM]
    k = hx[:, RWKV_DIM:2 * RWKV_DIM]
    v = hx[:, 2 * RWKV_DIM:3 * RWKV_DIM]
    xwa = hx[:, 3 * RWKV_DIM:3 * RWKV_DIM + LANES]
    xg = hx[:, 3 * RWKV_DIM + LANES:]
    w_logit = w0_ref[...] + _dot(jnp.tanh(xwa).astype(BF16), w2_ref[...])
    decay = jnp.exp(-jnp.exp(_log_sigmoid(w_logit) - 0.5))
    a = jax.nn.sigmoid(a0_ref[...] + _dot(xwa.astype(BF16), a2_ref[...]))
    g = _dot(jax.nn.sigmoid(xg).astype(BF16), g2_ref[...])
    ones = _head_ones()
    kk = k * kk_ref[...]
    kk = kk / jnp.maximum(jnp.sqrt(_head_sum(kk * kk, ones)), 1e-12)
    k2 = k * (1.0 + (a - 1.0) * ka_ref[...])
    r_out[...] = r
    w_out[...] = decay
    k_out[...] = k2
    v_out[...] = v
    kap_out[...] = kk
    bet_out[...] = kk * a
    g_out[...] = g
    bon_out[...] = _head_sum(r * k2 * rk_ref[...], ones) * v


def _rwkv_prep(hr, prev8, prev_map, tm, tiles_per_seq, params):
    rows = hr.shape[0]
    out = jax.ShapeDtypeStruct((rows, RWKV_DIM), F32)
    return pl.pallas_call(
        functools.partial(_rwkv_prep_body, tiles_per_seq=tiles_per_seq),
        grid=(rows // tm,),
        in_specs=[_row_spec(tm, RWKV_COLS), pl.BlockSpec((1, 8, RWKV_COLS), prev_map)]
        + [_full_spec(p.shape) for p in params],
        out_specs=[_row_spec(tm, RWKV_DIM)] * 8,
        out_shape=[out] * 8,
        compiler_params=_cparams("parallel"),
        name="rwkv_prep",
    )(hr, prev8, *params)


def _to_chains_body(x_ref, o_ref, st_ref, *, nb, n_out, offsets):
    for b in range(nb):
        st_ref[b] = x_ref[b].T
    for j in range(n_out):
        groups = [st_ref[b, off * RWKV_HEADS:(off + 1) * RWKV_HEADS, :] for off in offsets(j) for b in range(nb)]
        o_ref[:, j * LANES:(j + 1) * LANES] = jnp.concatenate(groups, axis=0).T


def _to_chains_pallas(x, n_out, offsets):
    nb, t, _ = x.shape
    tt = LANES
    return pl.pallas_call(
        functools.partial(_to_chains_body, nb=nb, n_out=n_out, offsets=offsets),
        grid=(t // tt,),
        in_specs=[pl.BlockSpec((nb, tt, RWKV_DIM), lambda i: (0, i, 0))],
        out_specs=pl.BlockSpec((tt, n_out * LANES), lambda i: (i, 0)),
        out_shape=jax.ShapeDtypeStruct((t, n_out * LANES), F32),
        scratch_shapes=[pltpu.VMEM((nb, RWKV_DIM, tt), F32)],
        compiler_params=_cparams("parallel"),
        name="to_chains",
    )(x).reshape(t, n_out, LANES)


def _from_chains_body(o_ref, x_ref, st_ref, *, nb, nv, dup):
    nh = RWKV_HEADS
    for vp in range(nv):
        tile = o_ref[:, vp, :].T
        for vh in range(dup):
            ch = vh * nv + vp
            for b in range(nb):
                r0 = (vh * nb + b) * nh
                st_ref[b, ch * nh:(ch + 1) * nh, :] = tile[r0:r0 + nh, :]
    for b in range(nb):
        x_ref[b] = st_ref[b].T


def _from_chains_pallas(o, nb):
    t, nv, _ = o.shape
    dup = HEAD_DIM // nv
    tt = LANES
    return pl.pallas_call(
        functools.partial(_from_chains_body, nb=nb, nv=nv, dup=dup),
        grid=(t // tt,),
        in_specs=[pl.BlockSpec((tt, nv, LANES), lambda i: (i, 0, 0))],
        out_specs=pl.BlockSpec((nb, tt, RWKV_DIM), lambda i: (0, i, 0)),
        out_shape=jax.ShapeDtypeStruct((nb, t, RWKV_DIM), F32),
        scratch_shapes=[pltpu.VMEM((nb, RWKV_DIM, tt), F32)],
        compiler_params=_cparams("parallel"),
        name="from_chains",
    )(o)


SUBLANES = 8


def _rwkv_scan_body(w_ref, kap_ref, bet_ref, k2_ref, r_ref, v_ref, s0_ref, o_ref, s_ref, ops_ref, *, tb, nv):
    @pl.when(pl.program_id(0) == 0)
    def _():
        s_ref[...] = s0_ref[...]

    packed = w_ref.shape[1] < HEAD_DIM
    low = lax.broadcasted_iota(jnp.int32, (w_ref.shape[1], LANES), 1) < LANES // 2

    def operand(ref, t):
        x = ref[t]
        if not packed:
            return x
        swapped = pltpu.roll(x, LANES // 2, axis=1)
        return jnp.concatenate([jnp.where(low, x, swapped), jnp.where(low, swapped, x)], axis=0)

    def operands(t):
        return tuple(operand(ref, t) for ref in (w_ref, kap_ref, bet_ref, k2_ref, r_ref))

    groups = nv // SUBLANES
    lanes_of_sum = 2

    def accumulate(partials, g, k, term):
        slot = k % lanes_of_sum
        partials[g][slot] = term if partials[g][slot] is None else partials[g][slot] + term

    def total(partials, g):
        return functools.reduce(lambda a, b: a + b, partials[g])

    def token(t, current):
        following = operands(jnp.minimum(t + 1, tb - 1))
        for i, x in enumerate(current):
            ops_ref[i] = x

        def row(i, k):
            return ops_ref[i, pl.ds(k, SUBLANES, stride=0), :]

        v8 = [v_ref[t, g * SUBLANES:(g + 1) * SUBLANES, :] for g in range(groups)]
        removal = [[None] * lanes_of_sum for _ in range(groups)]
        for k in range(HEAD_DIM):
            kap_k = row(1, k)
            for g in range(groups):
                accumulate(removal, g, k, s_ref[g, k] * kap_k)
        rho = [total(removal, g) for g in range(groups)]
        out = [[None] * lanes_of_sum for _ in range(groups)]
        for k in range(HEAD_DIM):
            w_k, bet_k, k2_k, r_k = row(0, k), row(2, k), row(3, k), row(4, k)
            for g in range(groups):
                sn = s_ref[g, k] * w_k - bet_k * rho[g] + k2_k * v8[g]
                s_ref[g, k] = sn
                accumulate(out, g, k, sn * r_k)
        for g in range(groups):
            o_ref[t, g * SUBLANES:(g + 1) * SUBLANES, :] = total(out, g)
        return following

    lax.fori_loop(0, tb, token, operands(0))


def _rwkv_scan(w, kap, bet, k2, r, v, s0):
    t = w.shape[0]
    nv = v.shape[1]
    tb = min(SCAN_TOKENS, t)
    op_spec = pl.BlockSpec((tb, w.shape[1], LANES), lambda i: (i, 0, 0))
    v_spec = pl.BlockSpec((tb, nv, LANES), lambda i: (i, 0, 0))
    s_shape = (nv // SUBLANES, HEAD_DIM, SUBLANES, LANES)
    s_spec = pl.BlockSpec(s_shape, lambda i: (0, 0, 0, 0))
    return pl.pallas_call(
        functools.partial(_rwkv_scan_body, tb=tb, nv=nv),
        grid=(t // tb,),
        in_specs=[op_spec] * 5 + [v_spec, s_spec],
        out_specs=[v_spec, s_spec],
        out_shape=[jax.ShapeDtypeStruct((t, nv, LANES), F32), jax.ShapeDtypeStruct(s_shape, F32)],
        scratch_shapes=[pltpu.VMEM((5, HEAD_DIM, LANES), F32)],
        compiler_params=_cparams("arbitrary"),
        name="rwkv_scan",
    )(w, kap, bet, k2, r, v, s0)


def _outproj_even_core(x, a, o, bon, gate, lng, lnb, w_ref):
    ones = _head_ones()
    d = o - _head_sum(o, ones) * (1.0 / HEAD_DIM)
    var = _head_sum(d * d, ones) * (1.0 / HEAD_DIM)
    y = d * lax.rsqrt(var + RWKV_GN_EPS) * lng + lnb
    b = (y + bon) * gate
    return x + _dot(a.astype(BF16), w_ref[0:FOX_DIM, :]) + _dot(b.astype(BF16), w_ref[FOX_DIM:, :])


def _outproj_even_body(x_ref, a_ref, o_ref, bon_ref, g_ref, lng_ref, lnb_ref, w_ref, out_ref):
    out_ref[...] = _outproj_even_core(x_ref[...], a_ref[...], o_ref[...], bon_ref[...], g_ref[...],
                                      lng_ref[...], lnb_ref[...], w_ref)


def _outproj_even(x, a, o, bon, g, lng, lnb, w):
    rows = x.shape[0]
    tm = min(ROW_TILE, rows)
    return pl.pallas_call(
        _outproj_even_body,
        grid=(rows // tm,),
        in_specs=[_row_spec(tm, D_MODEL)] + [_row_spec(tm, FOX_DIM)] * 4
        + [_full_spec(lng.shape), _full_spec(lnb.shape), _full_spec(w.shape)],
        out_specs=_row_spec(tm, D_MODEL),
        out_shape=jax.ShapeDtypeStruct(x.shape, F32),
        compiler_params=_cparams("parallel"),
        name="outproj_even",
    )(x, a, o, bon, g, lng, lnb, w)


def _rope(x, cos, sin_up, sin_dn):
    return x * cos + pltpu.roll(x, ROPE_DIM // 2, axis=1) * sin_up + pltpu.roll(x, LANES - ROPE_DIM // 2, axis=1) * sin_dn


def _inproj_odd_body(x_ref, g_ref, w_ref, qn_ref, kn_ref, vn_ref, cos_ref, sup_ref, sdn_ref,
                     q_ref, k_ref, v_ref, u_ref, gv_ref):
    n = _rms(x_ref[...], g_ref[...]).astype(BF16)
    cos, sup, sdn = cos_ref[...], sup_ref[...], sdn_ref[...]
    ones64 = _group_ones(HEAD_DIM)
    qw = SWA_Q

    def head_norm(h, gain):
        return h * lax.rsqrt(_group_sum(h * h, ones64) * (1.0 / HEAD_DIM) + NORM_EPS) * gain

    def q_block(j):
        cols = slice(j * LANES, (j + 1) * LANES)

        def finish(h):
            q_ref[:, cols] = (_rope(head_norm(h, qn_ref[:, cols]), cos, sup, sdn) * ATT_SCALE).astype(BF16)
        return cols, finish

    def k_finish(h):
        k_ref[...] = _rope(head_norm(h, kn_ref[...]), cos, sup, sdn)

    def v_finish(h):
        v_ref[...] = h

    def u_finish(h):
        u_ref[...] = _gelu_tanh(h)

    def gv_finish(h):
        gv_ref[...] = _rms(_gelu_tanh(h), vn_ref[...])

    s0 = qw + 2 * SWA_KV
    stages = [q_block(j) for j in range(SWA_Q // LANES)] + [
        (slice(qw, qw + SWA_KV), k_finish), (slice(qw + SWA_KV, s0), v_finish),
        (slice(s0, s0 + SGU_DIM), u_finish), (slice(s0 + SGU_DIM, s0 + 2 * SGU_DIM), gv_finish)]
    ahead = _dot(n, w_ref[:, stages[0][0]])
    for idx, (_, finish) in enumerate(stages):
        h = ahead
        if idx + 1 < len(stages):
            ahead = _dot(n, w_ref[:, stages[idx + 1][0]])
        finish(h)


def _inproj_odd(x, g, w, qn, kn, vn, cos, sup, sdn, table_blocks):
    rows = x.shape[0]
    tm = min(ROW_TILE, rows)
    tab = pl.BlockSpec((tm, LANES), lambda i: (i % table_blocks, 0))
    outs = [jax.ShapeDtypeStruct((rows, SWA_Q), BF16), jax.ShapeDtypeStruct((rows, SWA_KV), F32),
            jax.ShapeDtypeStruct((rows, SWA_KV), F32), jax.ShapeDtypeStruct((rows, SGU_DIM), F32),
            jax.ShapeDtypeStruct((rows, SGU_DIM), F32)]
    return pl.pallas_call(
        _inproj_odd_body,
        grid=(rows // tm,),
        in_specs=[_row_spec(tm, D_MODEL), _full_spec(g.shape), _full_spec(w.shape), _full_spec(qn.shape),
                  _full_spec(kn.shape), _full_spec(vn.shape), tab, tab, tab],
        out_specs=[_row_spec(tm, SWA_Q), _row_spec(tm, SWA_KV), _row_spec(tm, SWA_KV),
                   _row_spec(tm, SGU_DIM), _row_spec(tm, SGU_DIM)],
        out_shape=outs,
        compiler_params=_cparams("parallel"),
        name="inproj_odd",
    )(x, g, w, qn, kn, vn, cos, sup, sdn)


def _sink_attend(scores, vb, sink, visible):
    s = jnp.where(visible, scores, NEG_BIG)
    m = jnp.maximum(jnp.max(s, axis=1, keepdims=True), sink)
    p = jnp.exp(s - m)
    l = jnp.sum(p, axis=1, keepdims=True) + jnp.exp(sink - m)
    return _dot(p.astype(BF16), vb) / l


def _swa_body(sink_ref, q_ref, kp_ref, kc_ref, vp_ref, vc_ref, mask_ref, o_ref):
    tq = q_ref.shape[1]
    kw = jnp.concatenate([kp_ref[0], kc_ref[0]], axis=0)
    vw = jnp.concatenate([vp_ref[0], vc_ref[0]], axis=0)
    nk = kw.shape[0]
    kw_sw = pltpu.roll(kw, HEAD_DIM, axis=1)
    vw_sw = pltpu.roll(vw, HEAD_DIM, axis=1)
    key_half = _lane_half((nk, LANES))
    ks = [jnp.where(key_half, a, b).astype(BF16) for a, b in ((kw, kw_sw), (kw_sw, kw))]
    vs = [jnp.where(key_half, a, b).astype(BF16) for a, b in ((vw, vw_sw), (vw_sw, vw))]
    visible = jnp.concatenate([mask_ref[0] > 0.5] * 2, axis=0)
    half = _lane_half((tq, LANES))
    first = lax.broadcasted_iota(jnp.int32, (2 * tq, 1), 0) < tq
    n_blocks = SWA_Q // LANES
    scores = []
    for j in range(n_blocks):
        qb = q_ref[0, :, j * LANES:(j + 1) * LANES]
        zero = jnp.zeros_like(qb)
        stacked = jnp.concatenate([jnp.where(half, qb, zero), jnp.where(half, zero, qb)], axis=0)
        scores.append(_dot_nt(stacked, ks[(2 * j) // SWA_GROUP]))
    for j in range(n_blocks):
        sink = jnp.where(first, sink_ref[2 * j], sink_ref[2 * j + 1])
        o = _sink_attend(scores[j], vs[(2 * j) // SWA_GROUP], sink, visible)
        o_ref[0, :, j * LANES:(j + 1) * LANES] = jnp.where(half, o[0:tq], o[tq:]).astype(BF16)


def _swa(sinks, q, k_prev, k_cur, v_prev, v_cur, mask, tq, prev_rows, prev_map, mask_map):
    b, t, qw = q.shape
    nk = prev_rows + tq
    cur = lambda bi, i: (bi, i, 0)
    return pl.pallas_call(
        _swa_body,
        grid=(b, t // tq),
        in_specs=[pl.BlockSpec(memory_space=pltpu.SMEM),
                  pl.BlockSpec((1, tq, qw), cur),
                  pl.BlockSpec((1, prev_rows, SWA_KV), prev_map), pl.BlockSpec((1, tq, SWA_KV), cur),
                  pl.BlockSpec((1, prev_rows, SWA_KV), prev_map), pl.BlockSpec((1, tq, SWA_KV), cur),
                  pl.BlockSpec((1, tq, nk), mask_map)],
        out_specs=pl.BlockSpec((1, tq, qw), cur),
        out_shape=jax.ShapeDtypeStruct((b, t, qw), BF16),
        compiler_params=_cparams("parallel", "parallel"),
        name="swa",
    )(sinks, q, k_prev, k_cur, v_prev, v_cur, mask)


def _sgu_body(u_ref, v_ref, w_ref, b_ref, o_ref, *, length, n_chunks):
    tril = (lax.broadcasted_iota(jnp.int32, (length, length), 1)
            <= lax.broadcasted_iota(jnp.int32, (length, length), 0))
    ws = [jnp.where(tril, w_ref[g], 0.0).astype(BF16) for g in range(SGU_GROUPS)]
    half = _lane_half((length, LANES))
    for c in range(n_chunks):
        rows = slice(c * length, (c + 1) * length)
        for j in range(SGU_GROUPS // 2):
            cols = slice(j * LANES, (j + 1) * LANES)
            vb = v_ref[rows, cols].astype(BF16)
            mixed = jnp.where(half, _dot(ws[2 * j], vb), _dot(ws[2 * j + 1], vb)) + b_ref[:, cols]
            o_ref[rows, cols] = u_ref[rows, cols] * mixed


def _sgu(u, v, w, bias, length, n_chunks):
    rows = u.shape[0]
    tm = length * n_chunks
    return pl.pallas_call(
        functools.partial(_sgu_body, length=length, n_chunks=n_chunks),
        grid=(rows // tm,),
        in_specs=[_row_spec(tm, SGU_DIM), _row_spec(tm, SGU_DIM), _full_spec(w.shape), _full_spec(bias.shape)],
        out_specs=_row_spec(tm, SGU_DIM),
        out_shape=jax.ShapeDtypeStruct(u.shape, F32),
        compiler_params=_cparams("parallel"),
        name="sgu",
    )(u, v, w, bias)


def _outproj_odd_core(x, c, d, w_ref):
    return x + _dot(c, w_ref[0:SWA_Q, :]) + _dot(d.astype(BF16), w_ref[SWA_Q:, :])


def _outproj_odd_body(x_ref, c_ref, d_ref, w_ref, out_ref):
    out_ref[...] = _outproj_odd_core(x_ref[...], c_ref[...], d_ref[...], w_ref)


def _outproj_odd(x, c, d, w):
    rows = x.shape[0]
    tm = min(ROW_TILE, rows)
    return pl.pallas_call(
        _outproj_odd_body,
        grid=(rows // tm,),
        in_specs=[_row_spec(tm, D_MODEL), _row_spec(tm, SWA_Q), _row_spec(tm, SGU_DIM), _full_spec(w.shape)],
        out_specs=_row_spec(tm, D_MODEL),
        out_shape=jax.ShapeDtypeStruct(x.shape, F32),
        compiler_params=_cparams("parallel"),
        name="outproj_odd",
    )(x, c, d, w)


def _memkv_body(m_ref, g_ref, w_ref, kn_ref, k_ref, v_ref):
    n = _rms(m_ref[0], g_ref[...]).astype(BF16)
    ones = _group_ones(HEAD_DIM)
    hk = _dot(n, w_ref[:, 0:MEM_DIM])
    k = hk * lax.rsqrt(_group_sum(hk * hk, ones) * (1.0 / HEAD_DIM) + NORM_EPS) * kn_ref[...]
    k_ref[0] = k.T
    v_ref[0] = _dot(n, w_ref[:, MEM_DIM:]).T


def _memkv(mem, g, w, kn):
    b, m, _ = mem.shape
    out = jax.ShapeDtypeStruct((b, MEM_DIM, m), F32)
    spec = pl.BlockSpec((1, MEM_DIM, m), lambda i: (i, 0, 0))
    return pl.pallas_call(
        _memkv_body,
        grid=(b,),
        in_specs=[pl.BlockSpec((1, m, D_MODEL), lambda i: (i, 0, 0)), _full_spec(g.shape), _full_spec(w.shape),
                  _full_spec(kn.shape)],
        out_specs=[spec, spec],
        out_shape=[out, out],
        compiler_params=_cparams("parallel"),
        name="memkv",
    )(mem, g, w, kn)


def _xattn_core(x, g, wq_ref, qn, mk_ref, mv_ref, wo_ref):
    tq = x.shape[0]
    n = _rms(x, g).astype(BF16)
    hq = _dot(n, wq_ref[...])
    ones = _group_ones(HEAD_DIM)
    q = (hq * lax.rsqrt(_group_sum(hq * hq, ones) * (1.0 / HEAD_DIM) + NORM_EPS) * qn * ATT_SCALE).astype(BF16)
    half = _lane_half((tq, LANES))
    scores, values = [], []
    for j in range(MEM_HEADS // 2):
        cols = slice(j * LANES, (j + 1) * LANES)
        kb = mk_ref[0, cols, :].astype(BF16)
        values.append(mv_ref[0, cols, :].astype(BF16))
        qb = q[:, cols]
        zero = jnp.zeros_like(qb)
        heads = [jnp.where(half, qb, zero), jnp.where(half, zero, qb)]
        scores.append([_dot(qh, kb) for qh in ([jnp.concatenate(heads, axis=0)] if tq <= LANES else heads)])
    blocks = []
    for per_block, vb in zip(scores, values):
        outs = []
        for s in per_block:
            p = jnp.exp(s - jnp.max(s, axis=1, keepdims=True))
            outs.append(_dot_nt(p.astype(BF16), vb) / jnp.sum(p, axis=1, keepdims=True))
        if len(outs) == 1:
            outs = [outs[0][0:tq], outs[0][tq:]]
        blocks.append(jnp.where(half, outs[0], outs[1]))
    o = jnp.concatenate(blocks, axis=1).astype(BF16)
    return x + _dot(o, wo_ref[...])


def _xattn_body(x_ref, g_ref, wq_ref, qn_ref, mk_ref, mv_ref, wo_ref, out_ref):
    out_ref[0] = _xattn_core(x_ref[0], g_ref[...], wq_ref, qn_ref[...], mk_ref, mv_ref, wo_ref)


def _tail_body(*refs, even):
    n_mix = 8 if even else 4
    mix, rest = refs[:n_mix], refs[n_mix:]
    gx_ref, wq_ref, qn_ref, mk_ref, mv_ref, wo_ref, gf_ref, wg_ref, wu_ref, wd_ref, out_ref, acc_ref = rest
    if even:
        x_ref, a_ref, o_ref, bon_ref, gate_ref, lng_ref, lnb_ref, w_ref = mix
        x = _outproj_even_core(x_ref[0], a_ref[0], o_ref[0], bon_ref[0], gate_ref[0], lng_ref[...], lnb_ref[...], w_ref)
    else:
        x_ref, c_ref, d_ref, w_ref = mix
        x = _outproj_odd_core(x_ref[0], c_ref[0], d_ref[0], w_ref)
    x = _xattn_core(x, gx_ref[...], wq_ref, qn_ref[...], mk_ref, mv_ref, wo_ref)
    out_ref[0] = _ffn_core(x, gf_ref[...], wg_ref, wu_ref, wd_ref, acc_ref)


def _tail(even, layer, x, mix_rows, mix_params, xattn_params, mk, mv, wo, ffn_params, tm):
    b, t, _ = x.shape
    m = mk.shape[2]
    rows3 = lambda c: pl.BlockSpec((1, tm, c), lambda bi, i: (bi, i, 0))
    mem = pl.BlockSpec((1, MEM_DIM, m), lambda bi, i: (bi, 0, 0))
    args = [x, *mix_rows, *mix_params, *xattn_params, mk, mv, wo, *ffn_params]
    in_specs = ([rows3(a.shape[2]) for a in (x, *mix_rows)] + [_full_spec(p.shape) for p in mix_params]
                + [_full_spec(p.shape) for p in xattn_params] + [mem, mem, _full_spec(wo.shape)]
                + [_full_spec(ffn_params[0].shape)] + [_layer_spec(w, layer) for w in ffn_params[1:]])
    return pl.pallas_call(
        functools.partial(_tail_body, even=even),
        grid=(b, t // tm),
        in_specs=in_specs,
        out_specs=rows3(D_MODEL),
        out_shape=jax.ShapeDtypeStruct(x.shape, F32),
        scratch_shapes=[pltpu.VMEM((tm, D_MODEL), F32)],
        compiler_params=_cparams("parallel", "parallel"),
        name="tail_even" if even else "tail_odd",
    )(*args)


def _xattn(x, g, wq, qn, mk, mv, wo, tq):
    b, t, _ = x.shape
    m = mk.shape[2]
    return pl.pallas_call(
        _xattn_body,
        grid=(b, t // tq),
        in_specs=[pl.BlockSpec((1, tq, D_MODEL), lambda bi, i: (bi, i, 0)), _full_spec(g.shape),
                  _full_spec(wq.shape), _full_spec(qn.shape),
                  pl.BlockSpec((1, MEM_DIM, m), lambda bi, i: (bi, 0, 0)),
                  pl.BlockSpec((1, MEM_DIM, m), lambda bi, i: (bi, 0, 0)), _full_spec(wo.shape)],
        out_specs=pl.BlockSpec((1, tq, D_MODEL), lambda bi, i: (bi, i, 0)),
        out_shape=jax.ShapeDtypeStruct(x.shape, F32),
        compiler_params=_cparams("parallel", "parallel"),
        name="xattn",
    )(x, g, wq, qn, mk, mv, wo)


def _head_minor(x, axis=-1):
    axis %= x.ndim
    shape = x.shape
    y = x.reshape(shape[:axis] + (RWKV_HEADS, HEAD_DIM) + shape[axis + 1:])
    return jnp.swapaxes(y, axis, axis + 1).reshape(shape)


def _head_major(x, axis=-1):
    axis %= x.ndim
    shape = x.shape
    y = x.reshape(shape[:axis] + (HEAD_DIM, RWKV_HEADS) + shape[axis + 1:])
    return jnp.swapaxes(y, axis, axis + 1).reshape(shape)


def _cols_head_minor(x, inverse=False):
    n = 3 * RWKV_DIM
    blocks = x[..., :n].reshape(x.shape[:-1] + (3, RWKV_DIM))
    blocks = (_head_major if inverse else _head_minor)(blocks)
    return jnp.concatenate([blocks.reshape(x.shape[:-1] + (n,)), x[..., n:]], axis=-1)


def _to_chains(x, b, t, dup):
    y = x.reshape(b, t, HEAD_DIM, RWKV_HEADS).transpose(1, 2, 0, 3).reshape(t, HEAD_DIM, b * RWKV_HEADS)
    return jnp.concatenate([y] * dup, axis=-1) if dup > 1 else y


def _rope_tables(pos):
    half = ROPE_DIM // 2
    inv_freq = jnp.power(ROPE_THETA, -jnp.arange(half, dtype=F32) / half)
    ang = pos.astype(F32)[:, None] * inv_freq[None, :]
    cos, sin = jnp.cos(ang), jnp.sin(ang)
    n = pos.shape[0]
    pad = jnp.zeros((n, HEAD_DIM - ROPE_DIM), F32)
    zero = jnp.zeros((n, half), F32)
    cos_t = jnp.concatenate([cos, cos, pad + 1.0], axis=1)
    up_t = jnp.concatenate([zero, sin, pad], axis=1)
    dn_t = jnp.concatenate([-sin, zero, pad], axis=1)
    two = lambda a: jnp.concatenate([a, a], axis=1)
    return two(cos_t), two(up_t), two(dn_t)


def _swa_prompt_mask(tq):
    span = WINDOW + tq
    qc = np.arange(tq)[:, None] // CHUNK
    kc = np.arange(span)[None, :] // CHUNK - WINDOW_CHUNKS
    band = (kc <= qc) & (kc >= qc - WINDOW_CHUNKS)
    first = band & (kc >= 0)
    return jnp.asarray(np.stack([first, band]).astype(np.float32))


def _swa_sample_mask(past, rows, t):
    kc = (past - rows + np.arange(rows + t)) // CHUNK
    qc = (past + np.arange(t)) // CHUNK
    m = (kc[None, :] <= qc[:, None]) & (kc[None, :] >= qc[:, None] - WINDOW_CHUNKS)
    return jnp.asarray(m[None].astype(np.float32))


def kernel(x_prompt, x_sample, cache_fox_k, cache_fox_v, cache_fox_logf, state_rwkv, state_rwkv_shift, cache_swa_k, cache_swa_v, cache_mem_k, cache_mem_v, mem_prompt, ffn1_norm, ffn1_w_gate, ffn1_w_up, ffn1_w_down, mix_norm, ev_w_in, fox_b_f, fox_q_norm, fox_k_norm, rwkv_mu, rwkv_w0, rwkv_w2, rwkv_a0, rwkv_a2, rwkv_g2, rwkv_k_k, rwkv_k_a, rwkv_r_k, rwkv_ln_g, rwkv_ln_b, ev_w_out, od_w_in, swa_q_norm, swa_k_norm, swa_sinks, sgu_v_norm, sgu_w_s, sgu_b, od_w_out, xattn_norm, mem_norm, xattn_wq, xattn_wkv, xattn_q_norm, xattn_k_norm, xattn_wo, ffn2_norm, ffn2_w_gate, ffn2_w_up, ffn2_w_down):
    bp, tp, _ = x_prompt.shape
    bs, ts, _ = x_sample.shape
    depth = ffn1_norm.shape[0]
    past = cache_fox_k.shape[2]
    mem_tokens = mem_prompt.shape[1]
    xp = x_prompt.reshape(bp * tp, D_MODEL)
    xs = x_sample.reshape(bs * ts, D_MODEL)
    row = lambda a: a.reshape(1, -1)
    tile_heads = lambda a, n: jnp.tile(a, n).reshape(1, -1)
    ffn1_w = (ffn1_w_gate.astype(BF16), ffn1_w_up.astype(BF16), ffn1_w_down.astype(BF16))
    ffn2_w = (ffn2_w_gate.astype(BF16), ffn2_w_up.astype(BF16), ffn2_w_down.astype(BF16))

    out = {k: [] for k in ("p_fox_k", "p_fox_v", "p_fox_logf", "p_rwkv_state", "p_rwkv_shift", "p_swa_k", "p_swa_v",
                           "p_mem_k", "p_mem_v", "s_fox_k", "s_fox_v", "s_fox_logf", "s_rwkv_state", "s_rwkv_shift",
                           "s_swa_k", "s_swa_v", "s_sgu_v")}

    for l in range(depth):
        f1 = (row(ffn1_norm[l]),) + ffn1_w
        xp = _ffn(xp, l, *f1)
        xs = _ffn(xs, l, *f1)
        g_mix = row(mix_norm[l])
        if l % 2 == 0:
            e = l // 2
            w_in = ev_w_in[e]
            f0 = 3 * FOX_DIM
            w_cat = jnp.concatenate([w_in[:, :f0], jnp.pad(w_in[:, f0:f0 + FOX_HEADS], ((0, 0), (0, LANES - FOX_HEADS))),
                                     _cols_head_minor(w_in[:, f0 + FOX_HEADS:])], axis=1).astype(BF16)
            bf = jnp.pad(fox_b_f[e], (0, LANES - FOX_HEADS)).reshape(1, LANES)
            qn = tile_heads(fox_q_norm[e], FOX_HEADS)
            kn = tile_heads(fox_k_norm[e], FOX_HEADS)
            qp, ktp, vtp, lfp, hrp = _inproj_even(xp, g_mix, w_cat, bf, qn, kn, seq_len=tp)
            qs, ks, vs, lfs, hrs = _inproj_even(xs, g_mix, w_cat, bf, qn, kn)

            cum_p, cumt_p = _cumsum(lfp.reshape(bp, tp, LANES), CUM_TILE)
            a_p = _fox_prompt(qp.reshape(bp, tp, FOX_DIM), ktp, vtp, cum_p, cumt_p, ATT_TILE)
            lfs8 = lfs.reshape(bs, ts, LANES)[:, :, :FOX_HEADS]
            tot = past + ts
            padded = -(-tot // CUM_TILE) * CUM_TILE
            lf_all = jnp.concatenate([cache_fox_logf[e].astype(F32), lfs8], axis=1)
            lf_all = jnp.pad(lf_all, ((0, 0), (0, padded - tot), (0, LANES - FOX_HEADS)))
            cum_s, cumt_s = _cumsum(lf_all, CUM_TILE)
            token_minor = lambda c: jnp.transpose(c, (0, 2, 3, 1)).reshape(bs, FOX_DIM, past)
            a_s = _fox_sample(qs.reshape(bs, ts, FOX_DIM), ks.reshape(bs, ts, FOX_DIM), vs.reshape(bs, ts, FOX_DIM),
                              token_minor(cache_fox_k[e]), token_minor(cache_fox_v[e]), cum_s, cumt_s)

            hm = _head_minor
            w2p = jnp.pad(hm(rwkv_w2[e]), ((0, LANES - DECAY_LORA), (0, 0))).astype(BF16)
            a2p = jnp.pad(hm(rwkv_a2[e]), ((DECAY_LORA, 0), (0, 0))).astype(BF16)
            params = (row(_cols_head_minor(rwkv_mu[e])), row(hm(rwkv_w0[e])), w2p, row(hm(rwkv_a0[e])), a2p,
                      hm(rwkv_g2[e]).astype(BF16), row(hm(rwkv_k_k[e])), row(hm(rwkv_k_a[e])),
                      row(hm(rwkv_r_k[e].reshape(-1))))
            tiles_per_seq = tp // ROW_TILE
            prev_p = hrp.reshape(bp * tp // 8, 8, RWKV_COLS)
            prep_p = _rwkv_prep(hrp, prev_p, lambda i: (jnp.maximum(i * (ROW_TILE // 8) - 1, 0), 0, 0),
                                ROW_TILE, tiles_per_seq, params)
            prev_s = jnp.pad(_cols_head_minor(state_rwkv_shift[e].astype(F32)), ((0, 0), (7, 0), (0, 0)))
            prep_s = _rwkv_prep(hrs, prev_s, lambda i: (i, 0, 0), ts, 0, params)

            def scan(prep, b, t, state0):
                r, w, k2, v, kap, bet, g, bon = prep
                dup = LANES // (b * RWKV_HEADS)
                nv = HEAD_DIM // dup
                in_kernel = t % LANES == 0
                if in_kernel:
                    halves = lambda j: tuple(half * nv + j for half in range(dup))
                    ops = [_to_chains_pallas(a.reshape(b, t, RWKV_DIM), nv, halves) for a in (w, kap, bet, k2, r)]
                    vt = _to_chains_pallas(v.reshape(b, t, RWKV_DIM), nv, halves)
                else:
                    ops = [_to_chains(a, b, t, dup) for a in (w, kap, bet, k2, r)]
                    vt = v.reshape(b, t, dup, nv, RWKV_HEADS).transpose(1, 3, 2, 0, 4).reshape(t, nv, LANES)
                s0 = state0.reshape(b, RWKV_HEADS, dup, nv // SUBLANES, SUBLANES, HEAD_DIM)
                s0 = s0.transpose(3, 5, 4, 2, 0, 1).reshape(nv // SUBLANES, HEAD_DIM, SUBLANES, LANES)
                o, sT = _rwkv_scan(*ops, vt, s0)
                if in_kernel:
                    o = _from_chains_pallas(o, b).reshape(b * t, RWKV_DIM)
                else:
                    o = o.reshape(t, nv, dup, b, RWKV_HEADS).transpose(3, 0, 2, 1, 4).reshape(b * t, RWKV_DIM)
                sT = sT.reshape(nv // SUBLANES, HEAD_DIM, SUBLANES, dup, b, RWKV_HEADS)
                sT = sT.transpose(4, 5, 3, 0, 2, 1).reshape(b, RWKV_HEADS, HEAD_DIM, HEAD_DIM)
                return o, sT, g, bon

            o_p, st_p, g_p, bon_p = scan(prep_p, bp, tp, jnp.zeros((bp, RWKV_HEADS, HEAD_DIM, HEAD_DIM), F32))
            o_s, st_s, g_s, bon_s = scan(prep_s, bs, ts, state_rwkv[e].astype(F32))

            w_out = jnp.concatenate([ev_w_out[e][:FOX_DIM], hm(ev_w_out[e][FOX_DIM:], axis=0)], axis=0).astype(BF16)
            lng, lnb = row(hm(rwkv_ln_g[e])), row(hm(rwkv_ln_b[e]))
            seq = lambda a: a.reshape(bp, tp, -1)
            mix_p = (True, (a_p, seq(o_p), seq(bon_p), seq(g_p)), (lng, lnb, w_out))
            xs = _outproj_even(xs, a_s.reshape(bs * ts, FOX_DIM), o_s, bon_s, g_s, lng, lnb, w_out)

            rows_of = lambda a: jnp.transpose(a.reshape(bp, FOX_HEADS, HEAD_DIM, tp), (0, 3, 1, 2))
            out["p_fox_k"].append(rows_of(ktp))
            out["p_fox_v"].append(rows_of(vtp))
            out["p_fox_logf"].append(lfp.reshape(bp, tp, LANES)[:, :, :FOX_HEADS])
            out["p_rwkv_state"].append(st_p)
            out["p_rwkv_shift"].append(_cols_head_minor(hrp.reshape(bp, tp, RWKV_COLS)[:, -1:], inverse=True))
            out["s_fox_k"].append(ks.reshape(bs, ts, FOX_HEADS, HEAD_DIM))
            out["s_fox_v"].append(vs.reshape(bs, ts, FOX_HEADS, HEAD_DIM))
            out["s_fox_logf"].append(lfs8)
            out["s_rwkv_state"].append(st_s)
            out["s_rwkv_shift"].append(_cols_head_minor(hrs.reshape(bs, ts, RWKV_COLS)[:, -1:], inverse=True))
        else:
            j = l // 2
            w_cat = od_w_in[j].astype(BF16)
            qn = tile_heads(swa_q_norm[j], SWA_HEADS)
            kn = tile_heads(swa_k_norm[j], SWA_KV_HEADS)
            vn = row(sgu_v_norm[j])
            tabs_p = _rope_tables(jnp.arange(tp))
            tabs_s = _rope_tables(past + jnp.arange(bs * ts) % ts)
            qp, kp, vp, up, gp = _inproj_odd(xp, g_mix, w_cat, qn, kn, vn, *tabs_p, tp // ROW_TILE)
            qs, ks, vs, us, gs = _inproj_odd(xs, g_mix, w_cat, qn, kn, vn, *tabs_s, 1)

            qw = SWA_Q
            kp3, vp3 = kp.reshape(bp, tp, SWA_KV), vp.reshape(bp, tp, SWA_KV)
            ratio = SWA_TILE // WINDOW
            c_p = _swa(swa_sinks[j], qp.reshape(bp, tp, qw), kp3, kp3, vp3, vp3, _swa_prompt_mask(SWA_TILE),
                       SWA_TILE, WINDOW, lambda bi, i: (bi, jnp.maximum(i * ratio - 1, 0), 0),
                       lambda bi, i: (jnp.minimum(i, 1), 0, 0))
            rows_c = cache_swa_k.shape[2]
            ck3 = cache_swa_k[j].reshape(bs, rows_c, SWA_KV)
            cv3 = cache_swa_v[j].reshape(bs, rows_c, SWA_KV)
            ks3, vs3 = ks.reshape(bs, ts, SWA_KV), vs.reshape(bs, ts, SWA_KV)
            c_s = _swa(swa_sinks[j], qs.reshape(bs, ts, qw), ck3, ks3, cv3, vs3, _swa_sample_mask(past, rows_c, ts),
                       ts, rows_c, lambda bi, i: (bi, 0, 0), lambda bi, i: (0, 0, 0))

            bias = jnp.repeat(jnp.transpose(sgu_b[j]), HEAD_DIM, axis=1)
            d_p = _sgu(up, gp, sgu_w_s[j], bias, SGU_CHUNK, ROW_TILE // SGU_CHUNK)
            d_s = _sgu(us, gs, sgu_w_s[j][:, :ts, :ts], bias[:ts], ts, 1)

            w_out = od_w_out[j].astype(BF16)
            mix_p = (False, (c_p, d_p.reshape(bp, tp, SGU_DIM)), (w_out,))
            xs = _outproj_odd(xs, c_s.reshape(bs * ts, qw), d_s, w_out)

            out["p_swa_k"].append(kp3[:, -WINDOW:].reshape(bp, WINDOW, SWA_KV_HEADS, HEAD_DIM))
            out["p_swa_v"].append(vp3[:, -WINDOW:].reshape(bp, WINDOW, SWA_KV_HEADS, HEAD_DIM))
            out["s_swa_k"].append(jnp.concatenate([ck3, ks3], axis=1)[:, -rows_c:].reshape(bs, rows_c, SWA_KV_HEADS, HEAD_DIM))
            out["s_swa_v"].append(jnp.concatenate([cv3, vs3], axis=1)[:, -rows_c:].reshape(bs, rows_c, SWA_KV_HEADS, HEAD_DIM))
            out["s_sgu_v"].append(gs.reshape(bs, ts, SGU_DIM))

        mk, mv = _memkv(mem_prompt, row(mem_norm[l]), xattn_wkv[l].astype(BF16), tile_heads(xattn_k_norm[l], MEM_HEADS))
        xa = (row(xattn_norm[l]), xattn_wq[l].astype(BF16), tile_heads(xattn_q_norm[l], MEM_HEADS))
        wo = xattn_wo[l].astype(BF16)
        f2 = (row(ffn2_norm[l]),) + ffn2_w
        xp = _tail(mix_p[0], l, xp.reshape(bp, tp, D_MODEL), mix_p[1], mix_p[2], xa, mk, mv, wo, f2,
                   ROW_TILE).reshape(bp * tp, D_MODEL)
        mem_minor = lambda c: jnp.transpose(c, (0, 2, 3, 1)).reshape(bs, MEM_DIM, mem_tokens)
        xs = _xattn(xs.reshape(bs, ts, D_MODEL), *xa, mem_minor(cache_mem_k[l]), mem_minor(cache_mem_v[l]),
                    wo, ts).reshape(bs * ts, D_MODEL)
        mem_rows_of = lambda a: jnp.transpose(a.reshape(bp, MEM_HEADS, HEAD_DIM, mem_tokens), (0, 3, 1, 2))
        out["p_mem_k"].append(mem_rows_of(mk))
        out["p_mem_v"].append(mem_rows_of(mv))

        xs = _ffn(xs, l, *f2)

    order = ("p_fox_k", "p_fox_v", "p_fox_logf", "p_rwkv_state", "p_rwkv_shift", "p_swa_k", "p_swa_v", "p_mem_k",
             "p_mem_v", "s_fox_k", "s_fox_v", "s_fox_logf", "s_rwkv_state", "s_rwkv_shift", "s_swa_k", "s_swa_v",
             "s_sgu_v")
    return (xp.reshape(bp, tp, D_MODEL), xs.reshape(bs, ts, D_MODEL)) + tuple(jnp.stack(out[k]) for k in order)
```

```python
import functools

import numpy as np
import jax
import jax.numpy as jnp
from jax import lax
from jax.experimental import pallas as pl
from jax.experimental.pallas import tpu as pltpu

F32 = jnp.float32
BF16 = jnp.bfloat16

D_MODEL = 1024
HEAD_DIM = 64
NORM_EPS = 1e-6
ROPE_THETA = 500000.0
ROPE_DIM = HEAD_DIM // 4
CHUNK = 64
FOX_HEADS = 8
FOX_DIM = FOX_HEADS * HEAD_DIM
RWKV_HEADS = 8
RWKV_DIM = RWKV_HEADS * HEAD_DIM
DECAY_LORA = 64
ICLR_LORA = 64
GATE_LORA = 128
RWKV_COLS = 3 * RWKV_DIM + DECAY_LORA + ICLR_LORA + GATE_LORA
RWKV_GN_EPS = 64e-5
SWA_HEADS = 8
SWA_KV_HEADS = 2
SWA_GROUP = SWA_HEADS // SWA_KV_HEADS
SWA_Q = SWA_HEADS * HEAD_DIM
SWA_KV = SWA_KV_HEADS * HEAD_DIM
WINDOW = 128
WINDOW_CHUNKS = WINDOW // CHUNK
SGU_GROUPS = 8
SGU_DIM = SGU_GROUPS * HEAD_DIM
SGU_CHUNK = 128
MEM_HEADS = 4
MEM_DIM = MEM_HEADS * HEAD_DIM
D_FF = 2816

LANES = 128
ROW_TILE = 512
FF_TILE = 256
ATT_TILE = 512
CUM_TILE = 256
SWA_TILE = 256
SCAN_TOKENS = 64
VMEM_LIMIT = 56 * 1024 * 1024
ATT_SCALE = HEAD_DIM ** -0.5
LOG2E = 1.4426950408889634
NEG_BIG = -1e30


def _cparams(*sem):
    return pltpu.CompilerParams(dimension_semantics=sem, vmem_limit_bytes=VMEM_LIMIT)


def _dot(a, b):
    return jnp.dot(a, b, preferred_element_type=F32)


def _dot_nt(a, b):
    return lax.dot_general(a, b, (((1,), (1,)), ((), ())), preferred_element_type=F32)


def _rms(x, g):
    ms = jnp.mean(x * x, axis=-1, keepdims=True)
    return (x * lax.rsqrt(ms + NORM_EPS)) * g


def _group_ones(group):
    shift = int(np.log2(group))
    r = lax.broadcasted_iota(jnp.int32, (LANES, LANES), 0) >> shift
    c = lax.broadcasted_iota(jnp.int32, (LANES, LANES), 1) >> shift
    return jnp.where(r == c, 1.0, 0.0).astype(BF16)


def _group_sum(x, ones):
    parts = []
    for j in range(x.shape[1] // LANES):
        blk = x[:, j * LANES:(j + 1) * LANES]
        hi = blk.astype(BF16)
        lo = (blk - hi.astype(F32)).astype(BF16)
        parts.append(_dot(hi, ones) + _dot(lo, ones))
    return parts[0] if len(parts) == 1 else jnp.concatenate(parts, axis=1)


def _head_ones():
    r = lax.broadcasted_iota(jnp.int32, (LANES, LANES), 0) & (RWKV_HEADS - 1)
    c = lax.broadcasted_iota(jnp.int32, (LANES, LANES), 1) & (RWKV_HEADS - 1)
    return jnp.where(r == c, 1.0, 0.0).astype(BF16)


def _head_sum(x, ones):
    part = x[:, 0:LANES]
    for j in range(1, x.shape[1] // LANES):
        part = part + x[:, j * LANES:(j + 1) * LANES]
    hi = part.astype(BF16)
    lo = (part - hi.astype(F32)).astype(BF16)
    tot = _dot(hi, ones) + _dot(lo, ones)
    return jnp.concatenate([tot] * (x.shape[1] // LANES), axis=1)


def _log_sigmoid(z):
    return jnp.minimum(z, 0.0) - jnp.log(1.0 + jnp.exp(-jnp.abs(z)))


def _gelu_tanh(x):
    return 0.5 * x * (1.0 + jnp.tanh(0.7978845608028654 * (x + 0.044715 * (x * x * x))))


def _lane_half(shape):
    return lax.broadcasted_iota(jnp.int32, shape, 1) < HEAD_DIM


def _row_spec(tm, cols):
    return pl.BlockSpec((tm, cols), lambda i: (i, 0))


def _full_spec(shape):
    nd = len(shape)
    return pl.BlockSpec(shape, lambda *_: (0,) * nd, pipeline_mode=pl.Buffered(1))


def _ffn_core(x, g, wg_ref, wu_ref, wd_ref, acc_ref):
    n = _rms(x, g).astype(BF16)
    for c in range(wg_ref.shape[2] // FF_TILE):
        cols = slice(c * FF_TILE, (c + 1) * FF_TILE)
        gate = _dot(n, wg_ref[0, :, cols])
        up = _dot(n, wu_ref[0, :, cols])
        act = (gate * jax.nn.sigmoid(gate) * up).astype(BF16)
        part = _dot(act, wd_ref[0, cols, :])
        if c == 0:
            acc_ref[...] = part
        else:
            acc_ref[...] += part
    return x + 0.5 * acc_ref[...]


def _ffn_body(x_ref, g_ref, wg_ref, wu_ref, wd_ref, o_ref, acc_ref):
    o_ref[...] = _ffn_core(x_ref[...], g_ref[...], wg_ref, wu_ref, wd_ref, acc_ref)


def _layer_spec(w, layer):
    nd = w.ndim
    return pl.BlockSpec((1,) + w.shape[1:], lambda *_: (layer,) + (0,) * (nd - 1), pipeline_mode=pl.Buffered(1))


def _ffn(x, layer, g, wg, wu, wd):
    rows = x.shape[0]
    tm = min(ROW_TILE, rows)
    return pl.pallas_call(
        _ffn_body,
        grid=(rows // tm,),
        in_specs=[_row_spec(tm, D_MODEL), _full_spec(g.shape), _layer_spec(wg, layer),
                  _layer_spec(wu, layer), _layer_spec(wd, layer)],
        out_specs=_row_spec(tm, D_MODEL),
        out_shape=jax.ShapeDtypeStruct(x.shape, F32),
        scratch_shapes=[pltpu.VMEM((tm, D_MODEL), F32)],
        compiler_params=_cparams("parallel"),
        name="ffn",
    )(x, g, wg, wu, wd)


def _inproj_even_body(x_ref, g_ref, w_ref, bf_ref, qn_ref, kn_ref, *refs, per_seq):
    if per_seq is None:
        q_ref, k_ref, v_ref, lf_ref, hr_ref = refs
    else:
        prep_params, (q_ref, k_ref, v_ref, lf_ref, last_ref), prep_outs, prev_ref = (
            refs[:9], refs[9:14], refs[14:22], refs[22])
    token_minor = per_seq is not None
    n = _rms(x_ref[...], g_ref[...]).astype(BF16)
    ones = _group_ones(HEAD_DIM)
    f0 = 3 * FOX_DIM

    def head_norm(h, gain):
        return h * lax.rsqrt(_group_sum(h * h, ones) * (1.0 / HEAD_DIM) + NORM_EPS) * gain

    def q_finish(h):
        q_ref[...] = (head_norm(h, qn_ref[...]) * (ATT_SCALE * LOG2E)).astype(BF16)

    def k_finish(h):
        k = head_norm(h, kn_ref[...])
        if token_minor:
            k_ref[0] = k.T
        else:
            k_ref[...] = k

    def v_finish(h):
        if token_minor:
            v_ref[0] = h.T
        else:
            v_ref[...] = h

    def lf_finish(h):
        lf_ref[...] = _log_sigmoid(h + bf_ref[...])

    def hr_finish(hr):
        if per_seq is None:
            hr_ref[...] = hr
            return
        _rwkv_prep_core(hr, prev_ref[SUBLANES - 1:SUBLANES, :], prep_params, prep_outs)
        tail = hr[hr.shape[0] - SUBLANES:, :]
        prev_ref[...] = tail
        last_ref[0] = tail

    if per_seq is not None:
        @pl.when(pl.program_id(0) % per_seq == 0)
        def _():
            prev_ref[...] = jnp.zeros(prev_ref.shape, F32)

    stages = [(slice(f0 + LANES, w_ref.shape[1]), hr_finish), (slice(0, FOX_DIM), q_finish),
              (slice(FOX_DIM, 2 * FOX_DIM), k_finish), (slice(2 * FOX_DIM, f0), v_finish),
              (slice(f0, f0 + LANES), lf_finish)]
    ahead = _dot(n, w_ref[:, stages[0][0]])
    for idx, (_, finish) in enumerate(stages):
        h = ahead
        if idx + 1 < len(stages):
            ahead = _dot(n, w_ref[:, stages[idx + 1][0]])
        finish(h)


def _inproj_even(x, g, w, bf, qn, kn, seq_len=None, prep_params=()):
    rows = x.shape[0]
    tm = min(ROW_TILE, rows)
    in_specs = [_row_spec(tm, D_MODEL), _full_spec(g.shape), _full_spec(w.shape), _full_spec(bf.shape),
                _full_spec(qn.shape), _full_spec(kn.shape)]
    q_out = (jax.ShapeDtypeStruct((rows, FOX_DIM), BF16), _row_spec(tm, FOX_DIM))
    lf_out = (jax.ShapeDtypeStruct((rows, LANES), F32), _row_spec(tm, LANES))
    if seq_len is None:
        kv = (jax.ShapeDtypeStruct((rows, FOX_DIM), F32), _row_spec(tm, FOX_DIM))
        outs = [q_out, kv, kv, lf_out, (jax.ShapeDtypeStruct((rows, RWKV_COLS), F32), _row_spec(tm, RWKV_COLS))]
        per_seq, scratch, sem = None, [], "parallel"
    else:
        per_seq = seq_len // tm
        kv = (jax.ShapeDtypeStruct((rows // seq_len, FOX_DIM, seq_len), F32),
              pl.BlockSpec((1, FOX_DIM, tm), lambda i: (i // per_seq, 0, i % per_seq)))
        last = (jax.ShapeDtypeStruct((rows // tm, SUBLANES, RWKV_COLS), F32),
                pl.BlockSpec((1, SUBLANES, RWKV_COLS), lambda i: (i, 0, 0)))
        operand = (jax.ShapeDtypeStruct((rows, RWKV_DIM), F32), _row_spec(tm, RWKV_DIM))
        outs = [q_out, kv, kv, lf_out, last] + [operand] * 8
        in_specs += [_full_spec(p.shape) for p in prep_params]
        scratch, sem = [pltpu.VMEM((SUBLANES, RWKV_COLS), F32)], "arbitrary"
    return pl.pallas_call(
        functools.partial(_inproj_even_body, per_seq=per_seq),
        grid=(rows // tm,),
        in_specs=in_specs,
        out_specs=[spec for _, spec in outs],
        out_shape=[shape for shape, _ in outs],
        scratch_shapes=scratch,
        compiler_params=_cparams(sem),
        name="inproj_even",
    )(x, g, w, bf, qn, kn, *prep_params)


def _split3(x):
    hi = x.astype(BF16)
    r1 = x - hi.astype(F32)
    mid = r1.astype(BF16)
    lo = (r1 - mid.astype(F32)).astype(BF16)
    return hi, mid, lo


def _cumsum_body(lf_ref, col_ref, row_ref, *, n_chunks, tk):
    r = lax.broadcasted_iota(jnp.int32, (tk, tk), 0)
    c = lax.broadcasted_iota(jnp.int32, (tk, tk), 1)
    tri = jnp.where(c <= r, 1.0, 0.0).astype(BF16)
    carry = jnp.zeros((1, LANES), F32)
    for i in range(n_chunks):
        hi, mid, lo = _split3(lf_ref[0, i * tk:(i + 1) * tk, :])
        cs = _dot(tri, hi) + _dot(tri, mid) + _dot(tri, lo) + carry
        col_ref[0, i * tk:(i + 1) * tk, :] = cs
        row_ref[0, :, i * tk:(i + 1) * tk] = cs.T[0:FOX_HEADS, :]
        carry = cs[tk - 1:tk, :]


def _cumsum(lf, tk):
    b, length, _ = lf.shape
    n_chunks = length // tk
    return pl.pallas_call(
        functools.partial(_cumsum_body, n_chunks=n_chunks, tk=tk),
        grid=(b,),
        in_specs=[pl.BlockSpec((1, length, LANES), lambda i: (i, 0, 0))],
        out_specs=[pl.BlockSpec((1, length, LANES), lambda i: (i, 0, 0)),
                   pl.BlockSpec((1, FOX_HEADS, length), lambda i: (i, 0, 0))],
        out_shape=[jax.ShapeDtypeStruct((b, length, LANES), F32),
                   jax.ShapeDtypeStruct((b, FOX_HEADS, length), F32)],
        compiler_params=_cparams("parallel"),
        name="cumsum",
    )(lf)


def _place3(terms, src_lane, dst_lane):
    r = lax.broadcasted_iota(jnp.int32, (LANES, LANES), 0)
    c = lax.broadcasted_iota(jnp.int32, (LANES, LANES), 1)
    out = None
    for i, t in enumerate(terms):
        sel = jnp.where((r == src_lane) & (c == dst_lane + i), 1.0, 0.0).astype(BF16)
        out = _dot(t, sel) if out is None else out + _dot(t, sel)
    return out


def _fox_q(q_pair, cum_col, head, h):
    n = q_pair.shape[0]
    qf = q_pair.astype(F32)
    if h == 1:
        qf = pltpu.roll(qf, HEAD_DIM, axis=1)
    lane = lax.broadcasted_iota(jnp.int32, (n, LANES), 1)
    ones = jnp.where((lane >= HEAD_DIM + 3) & (lane < HEAD_DIM + 6), 1.0, 0.0)
    aug = _place3(_split3(cum_col * LOG2E), head, HEAD_DIM) + ones
    return jnp.where(lane < HEAD_DIM, qf, aug).astype(BF16)


def _fox_kt(kt, cum_row):
    n = kt.shape[1]
    hi, mid, lo = (x.astype(F32) for x in _split3(cum_row * LOG2E))
    row = lax.broadcasted_iota(jnp.int32, (8, n), 0)
    aug = jnp.where(row < 3, 1.0, jnp.where(row == 3, -hi, jnp.where(row == 4, -mid, jnp.where(row == 5, -lo, 0.0))))
    return jnp.concatenate([kt, aug, jnp.zeros((HEAD_DIM - 8, n), F32)], axis=0).astype(BF16)


def _fox_vt(vt):
    n = vt.shape[1]
    row = lax.broadcasted_iota(jnp.int32, (HEAD_DIM, n), 0)
    return jnp.concatenate([vt, jnp.where(row == 0, 1.0, 0.0)], axis=0).astype(BF16)


def _fox_finish(acc):
    return acc / acc[:, HEAD_DIM:HEAD_DIM + 1]


def _fox_prompt_body(q_ref, cum_ref, cumt_ref, kt_ref, vt_ref, o_ref, ka_sc, va_sc, m_sc, acc_sc, *, tq, t):
    hp = pl.program_id(1)
    i = pl.program_id(2)

    @pl.when(i == 0)
    def _():
        for h in range(2):
            rows = slice(h * HEAD_DIM, (h + 1) * HEAD_DIM)
            cum_row = cumt_ref[0, pl.ds(2 * hp + h, 1), :]
            for c in range(t // tq):
                cols = slice(c * tq, (c + 1) * tq)
                ka_sc[h, c] = _fox_kt(kt_ref[0, rows, cols], cum_row[:, cols])
                va_sc[h, c] = _fox_vt(vt_ref[0, rows, cols])

    off_q = pl.multiple_of(i * tq, tq)
    cq = cum_ref[0, pl.ds(off_q, tq), :]
    qs = [_fox_q(q_ref[0], cq, 2 * hp + h, h) for h in range(2)]
    m_sc[...] = jnp.full(m_sc.shape, NEG_BIG, F32)
    acc_sc[...] = jnp.zeros(acc_sc.shape, F32)
    causal = (lax.broadcasted_iota(jnp.int32, (tq, tq), 1) <= lax.broadcasted_iota(jnp.int32, (tq, tq), 0))

    def step(j, masked):
        scores = [_dot(qs[h], ka_sc[h, j]) for h in range(2)]
        for h in range(2):
            s = scores[h]
            if masked:
                s = jnp.where(causal, s, NEG_BIG)
            m_old = m_sc[h]
            m_new = jnp.maximum(m_old, jnp.max(s, axis=1, keepdims=True))
            p = jnp.concatenate([jnp.exp2(s[:, c * LANES:(c + 1) * LANES] - m_new) for c in range(tq // LANES)],
                                axis=1).astype(BF16)
            acc_sc[h] = jnp.exp2(m_old - m_new) * acc_sc[h] + _dot_nt(p, va_sc[h, j])
            m_sc[h] = m_new

    def past(j, carry):
        step(j, False)
        return carry

    lax.fori_loop(0, i, past, 0)
    step(i, True)
    second = pltpu.roll(_fox_finish(acc_sc[1]), HEAD_DIM, axis=1)
    o_ref[0] = jnp.where(_lane_half((tq, LANES)), _fox_finish(acc_sc[0]), second)


def _fox_prompt(q, kt, vt, cum, cumt, tq):
    b, t, _ = q.shape
    pairs = FOX_HEADS // 2
    nblk = t // tq
    return pl.pallas_call(
        functools.partial(_fox_prompt_body, tq=tq, t=t),
        grid=(b, pairs, nblk),
        in_specs=[pl.BlockSpec((1, tq, LANES), lambda bi, hp, i: (bi, i, hp)),
                  pl.BlockSpec((1, t, LANES), lambda bi, hp, i: (bi, 0, 0)),
                  pl.BlockSpec((1, FOX_HEADS, t), lambda bi, hp, i: (bi, 0, 0)),
                  pl.BlockSpec((1, LANES, t), lambda bi, hp, i: (bi, hp, 0)),
                  pl.BlockSpec((1, LANES, t), lambda bi, hp, i: (bi, hp, 0))],
        out_specs=pl.BlockSpec((1, tq, LANES), lambda bi, hp, i: (bi, i, hp)),
        out_shape=jax.ShapeDtypeStruct((b, t, FOX_DIM), F32),
        scratch_shapes=[pltpu.VMEM((2, nblk, LANES, tq), BF16), pltpu.VMEM((2, nblk, LANES, tq), BF16),
                        pltpu.VMEM((2, tq, LANES), F32), pltpu.VMEM((2, tq, LANES), F32)],
        compiler_params=_cparams("parallel", "parallel", "arbitrary"),
        name="fox_prompt",
    )(q, cum, cumt, kt, vt)


def _fox_sample_body(q_ref, cum_ref, cumt_ref, kt_ref, vt_ref, kn_ref, vn_ref, o_ref, *, ts, past):
    cum_n = cum_ref[0, 0:ts, :] * LOG2E
    causal = (lax.broadcasted_iota(jnp.int32, (ts, ts), 1) <= lax.broadcasted_iota(jnp.int32, (ts, ts), 0))
    lane = lax.broadcasted_iota(jnp.int32, (ts, LANES), 1)
    q = q_ref[0].astype(F32)
    kn = kn_ref[0]
    vn = vn_ref[0]
    outs = []
    for h in range(FOX_HEADS):
        cols = slice(h * HEAD_DIM, (h + 1) * HEAD_DIM)
        cq = jnp.sum(jnp.where(lane == h, cum_n, 0.0), axis=1, keepdims=True)
        ck_p = cumt_ref[0, h:h + 1, 0:past] * LOG2E
        ck_n = cumt_ref[0, h:h + 1, past:past + ts] * LOG2E
        qh = q[:, cols].astype(BF16)
        s_p = _dot(qh, kt_ref[0, cols, :].astype(BF16)) + (cq - ck_p)
        s_n = jnp.where(causal, _dot_nt(qh, kn[:, cols].astype(BF16)) + (cq - ck_n), NEG_BIG)
        m = jnp.maximum(jnp.max(s_p, axis=1, keepdims=True), jnp.max(s_n, axis=1, keepdims=True))
        p_p = jnp.exp2(s_p - m)
        p_n = jnp.exp2(s_n - m)
        l = jnp.sum(p_p, axis=1, keepdims=True) + jnp.sum(p_n, axis=1, keepdims=True)
        acc = (_dot_nt(p_p.astype(BF16), vt_ref[0, cols, :].astype(BF16))
               + _dot(p_n.astype(BF16), vn[:, cols].astype(BF16)))
        outs.append(acc / l)
    o_ref[0] = jnp.concatenate(outs, axis=1)


def _fox_sample(q, k_new, v_new, kt_past, vt_past, cum, cumt):
    b, ts, _ = q.shape
    p = kt_past.shape[2]
    length = cum.shape[1]
    assert p % CUM_TILE == 0 and ts <= CUM_TILE
    new = pl.BlockSpec((1, ts, FOX_DIM), lambda bi: (bi, 0, 0))
    old = pl.BlockSpec((1, FOX_DIM, p), lambda bi: (bi, 0, 0))
    return pl.pallas_call(
        functools.partial(_fox_sample_body, ts=ts, past=p),
        grid=(b,),
        in_specs=[new, pl.BlockSpec((1, CUM_TILE, LANES), lambda bi: (bi, p // CUM_TILE, 0)),
                  pl.BlockSpec((1, FOX_HEADS, length), lambda bi: (bi, 0, 0)), old, old, new, new],
        out_specs=new,
        out_shape=jax.ShapeDtypeStruct((b, ts, FOX_DIM), F32),
        compiler_params=_cparams("parallel"),
        name="fox_sample",
    )(q, cum, cumt, kt_past, vt_past, k_new, v_new)


def _rwkv_prep_core(h, prev_row, param_refs, out_refs):
    mu_ref, w0_ref, w2_ref, a0_ref, a2_ref, g2_ref, kk_ref, ka_ref, rk_ref = param_refs
    r_out, w_out, k_out, v_out, kap_out, bet_out, g_out, bon_out = out_refs
    tm = h.shape[0]
    first = lax.broadcasted_iota(jnp.int32, (tm, 1), 0) == 0
    prev = jnp.where(first, prev_row, pltpu.roll(h, 1, axis=0))
    hx = h + (prev - h) * mu_ref[...]
    r = hx[:, 0:RWKV_DIM]
    k = hx[:, RWKV_DIM:2 * RWKV_DIM]
    v = hx[:, 2 * RWKV_DIM:3 * RWKV_DIM]
    xwa = hx[:, 3 * RWKV_DIM:3 * RWKV_DIM + LANES]
    xg = hx[:, 3 * RWKV_DIM + LANES:]
    w_logit = w0_ref[...] + _dot(jnp.tanh(xwa).astype(BF16), w2_ref[...])
    decay = jnp.exp(-jnp.exp(_log_sigmoid(w_logit) - 0.5))
    a = jax.nn.sigmoid(a0_ref[...] + _dot(xwa.astype(BF16), a2_ref[...]))
    g = _dot(jax.nn.sigmoid(xg).astype(BF16), g2_ref[...])
    ones = _head_ones()
    kk = k * kk_ref[...]
    kk = kk / jnp.maximum(jnp.sqrt(_head_sum(kk * kk, ones)), 1e-12)
    k2 = k * (1.0 + (a - 1.0) * ka_ref[...])
    r_out[...] = r
    w_out[...] = decay
    k_out[...] = k2
    v_out[...] = v
    kap_out[...] = kk
    bet_out[...] = kk * a
    g_out[...] = g
    bon_out[...] = _head_sum(r * k2 * rk_ref[...], ones) * v


def _rwkv_prep_body(h_ref, prev_ref, *refs):
    _rwkv_prep_core(h_ref[...], prev_ref[0, 7:8, :], refs[:9], refs[9:])


def _rwkv_prep(hr, prev8, tm, params):
    rows = hr.shape[0]
    out = jax.ShapeDtypeStruct((rows, RWKV_DIM), F32)
    return pl.pallas_call(
        _rwkv_prep_body,
        grid=(rows // tm,),
        in_specs=[_row_spec(tm, RWKV_COLS), pl.BlockSpec((1, 8, RWKV_COLS), lambda i: (i, 0, 0))]
        + [_full_spec(p.shape) for p in params],
        out_specs=[_row_spec(tm, RWKV_DIM)] * 8,
        out_shape=[out] * 8,
        compiler_params=_cparams("parallel"),
        name="rwkv_prep",
    )(hr, prev8, *params)


def _to_chains_body(x_ref, o_ref, st_ref, *, nb, n_out, offsets):
    for b in range(nb):
        st_ref[b] = x_ref[b].T
    for j in range(n_out):
        groups = [st_ref[b, off * RWKV_HEADS:(off + 1) * RWKV_HEADS, :] for off in offsets(j) for b in range(nb)]
        o_ref[:, j * LANES:(j + 1) * LANES] = jnp.concatenate(groups, axis=0).T


def _to_chains_pallas(x, n_out, offsets):
    nb, t, _ = x.shape
    tt = LANES
    return pl.pallas_call(
        functools.partial(_to_chains_body, nb=nb, n_out=n_out, offsets=offsets),
        grid=(t // tt,),
        in_specs=[pl.BlockSpec((nb, tt, RWKV_DIM), lambda i: (0, i, 0))],
        out_specs=pl.BlockSpec((tt, n_out * LANES), lambda i: (i, 0)),
        out_shape=jax.ShapeDtypeStruct((t, n_out * LANES), F32),
        scratch_shapes=[pltpu.VMEM((nb, RWKV_DIM, tt), F32)],
        compiler_params=_cparams("parallel"),
        name="to_chains",
    )(x).reshape(t, n_out, LANES)


def _from_chains_body(o_ref, x_ref, st_ref, *, nb, nv, dup):
    nh = RWKV_HEADS
    for vp in range(nv):
        tile = o_ref[:, vp, :].T
        for vh in range(dup):
            ch = vh * nv + vp
            for b in range(nb):
                r0 = (vh * nb + b) * nh
                st_ref[b, ch * nh:(ch + 1) * nh, :] = tile[r0:r0 + nh, :]
    for b in range(nb):
        x_ref[b] = st_ref[b].T


def _from_chains_pallas(o, nb):
    t, nv, _ = o.shape
    dup = HEAD_DIM // nv
    tt = LANES
    return pl.pallas_call(
        functools.partial(_from_chains_body, nb=nb, nv=nv, dup=dup),
        grid=(t // tt,),
        in_specs=[pl.BlockSpec((tt, nv, LANES), lambda i: (i, 0, 0))],
        out_specs=pl.BlockSpec((nb, tt, RWKV_DIM), lambda i: (0, i, 0)),
        out_shape=jax.ShapeDtypeStruct((nb, t, RWKV_DIM), F32),
        scratch_shapes=[pltpu.VMEM((nb, RWKV_DIM, tt), F32)],
        compiler_params=_cparams("parallel"),
        name="from_chains",
    )(o)


SUBLANES = 8


def _rwkv_scan_body(w_ref, kap_ref, bet_ref, k2_ref, r_ref, v_ref, s0_ref, o_ref, s_ref, ops_ref, *, tb, nv):
    @pl.when(pl.program_id(0) == 0)
    def _():
        s_ref[...] = s0_ref[...]

    packed = w_ref.shape[1] < HEAD_DIM
    low = lax.broadcasted_iota(jnp.int32, (w_ref.shape[1], LANES), 1) < LANES // 2

    def operand(ref, t):
        x = ref[t]
        if not packed:
            return x
        swapped = pltpu.roll(x, LANES // 2, axis=1)
        return jnp.concatenate([jnp.where(low, x, swapped), jnp.where(low, swapped, x)], axis=0)

    def operands(t):
        return tuple(operand(ref, t) for ref in (w_ref, kap_ref, bet_ref, k2_ref, r_ref))

    groups = nv // SUBLANES
    lanes_of_sum = 2

    def accumulate(partials, g, k, term):
        slot = k % lanes_of_sum
        partials[g][slot] = term if partials[g][slot] is None else partials[g][slot] + term

    def total(partials, g):
        return functools.reduce(lambda a, b: a + b, partials[g])

    def token(t, current):
        following = operands(jnp.minimum(t + 1, tb - 1))
        for i, x in enumerate(current):
            ops_ref[i] = x

        def row(i, k):
            return ops_ref[i, pl.ds(k, SUBLANES, stride=0), :]

        v8 = [v_ref[t, g * SUBLANES:(g + 1) * SUBLANES, :] for g in range(groups)]
        removal = [[None] * lanes_of_sum for _ in range(groups)]
        for k in range(HEAD_DIM):
            kap_k = row(1, k)
            for g in range(groups):
                accumulate(removal, g, k, s_ref[g, k] * kap_k)
        rho = [total(removal, g) for g in range(groups)]
        out = [[None] * lanes_of_sum for _ in range(groups)]
        for k in range(HEAD_DIM):
            w_k, bet_k, k2_k, r_k = row(0, k), row(2, k), row(3, k), row(4, k)
            for g in range(groups):
                sn = s_ref[g, k] * w_k - bet_k * rho[g] + k2_k * v8[g]
                s_ref[g, k] = sn
                accumulate(out, g, k, sn * r_k)
        for g in range(groups):
            o_ref[t, g * SUBLANES:(g + 1) * SUBLANES, :] = total(out, g)
        return following

    lax.fori_loop(0, tb, token, operands(0))


def _rwkv_scan(w, kap, bet, k2, r, v, s0):
    t = w.shape[0]
    nv = v.shape[1]
    tb = min(SCAN_TOKENS, t)
    op_spec = pl.BlockSpec((tb, w.shape[1], LANES), lambda i: (i, 0, 0))
    v_spec = pl.BlockSpec((tb, nv, LANES), lambda i: (i, 0, 0))
    s_shape = (nv // SUBLANES, HEAD_DIM, SUBLANES, LANES)
    s_spec = pl.BlockSpec(s_shape, lambda i: (0, 0, 0, 0))
    return pl.pallas_call(
        functools.partial(_rwkv_scan_body, tb=tb, nv=nv),
        grid=(t // tb,),
        in_specs=[op_spec] * 5 + [v_spec, s_spec],
        out_specs=[v_spec, s_spec],
        out_shape=[jax.ShapeDtypeStruct((t, nv, LANES), F32), jax.ShapeDtypeStruct(s_shape, F32)],
        scratch_shapes=[pltpu.VMEM((5, HEAD_DIM, LANES), F32)],
        compiler_params=_cparams("arbitrary"),
        name="rwkv_scan",
    )(w, kap, bet, k2, r, v, s0)


def _outproj_even_core(x, a, o, bon, gate, lng, lnb, w_ref):
    ones = _head_ones()
    d = o - _head_sum(o, ones) * (1.0 / HEAD_DIM)
    var = _head_sum(d * d, ones) * (1.0 / HEAD_DIM)
    y = d * lax.rsqrt(var + RWKV_GN_EPS) * lng + lnb
    b = (y + bon) * gate
    return x + _dot(a.astype(BF16), w_ref[0:FOX_DIM, :]) + _dot(b.astype(BF16), w_ref[FOX_DIM:, :])


def _outproj_even_body(x_ref, a_ref, o_ref, bon_ref, g_ref, lng_ref, lnb_ref, w_ref, out_ref):
    out_ref[...] = _outproj_even_core(x_ref[...], a_ref[...], o_ref[...], bon_ref[...], g_ref[...],
                                      lng_ref[...], lnb_ref[...], w_ref)


def _outproj_even(x, a, o, bon, g, lng, lnb, w):
    rows = x.shape[0]
    tm = min(ROW_TILE, rows)
    return pl.pallas_call(
        _outproj_even_body,
        grid=(rows // tm,),
        in_specs=[_row_spec(tm, D_MODEL)] + [_row_spec(tm, FOX_DIM)] * 4
        + [_full_spec(lng.shape), _full_spec(lnb.shape), _full_spec(w.shape)],
        out_specs=_row_spec(tm, D_MODEL),
        out_shape=jax.ShapeDtypeStruct(x.shape, F32),
        compiler_params=_cparams("parallel"),
        name="outproj_even",
    )(x, a, o, bon, g, lng, lnb, w)


def _rope(x, cos, sin_up, sin_dn):
    return x * cos + pltpu.roll(x, ROPE_DIM // 2, axis=1) * sin_up + pltpu.roll(x, LANES - ROPE_DIM // 2, axis=1) * sin_dn


def _inproj_odd_body(x_ref, g_ref, w_ref, qn_ref, kn_ref, vn_ref, cos_ref, sup_ref, sdn_ref, *refs, fuse_sgu):
    if fuse_sgu:
        ws_ref, sb_ref, q_ref, k_ref, v_ref, d_ref = refs
    else:
        q_ref, k_ref, v_ref, u_ref, gv_ref = refs
    n = _rms(x_ref[...], g_ref[...]).astype(BF16)
    cos, sup, sdn = cos_ref[...], sup_ref[...], sdn_ref[...]
    ones64 = _group_ones(HEAD_DIM)
    qw = SWA_Q

    def head_norm(h, gain):
        return h * lax.rsqrt(_group_sum(h * h, ones64) * (1.0 / HEAD_DIM) + NORM_EPS) * gain

    def q_block(j):
        cols = slice(j * LANES, (j + 1) * LANES)

        def finish(h):
            q_ref[:, cols] = (_rope(head_norm(h, qn_ref[:, cols]), cos, sup, sdn) * ATT_SCALE).astype(BF16)
        return cols, finish

    def k_finish(h):
        k_ref[...] = _rope(head_norm(h, kn_ref[...]), cos, sup, sdn)

    def v_finish(h):
        v_ref[...] = h

    held = {}

    def u_finish(h):
        if fuse_sgu:
            held["u"] = _gelu_tanh(h)
        else:
            u_ref[...] = _gelu_tanh(h)

    def gv_finish(h):
        gv = _rms(_gelu_tanh(h), vn_ref[...])
        if fuse_sgu:
            _sgu_core(held["u"], gv, ws_ref, sb_ref, d_ref, sb_ref.shape[0])
        else:
            gv_ref[...] = gv

    s0 = qw + 2 * SWA_KV
    stages = [q_block(j) for j in range(SWA_Q // LANES)] + [
        (slice(qw, qw + SWA_KV), k_finish), (slice(qw + SWA_KV, s0), v_finish),
        (slice(s0, s0 + SGU_DIM), u_finish), (slice(s0 + SGU_DIM, s0 + 2 * SGU_DIM), gv_finish)]
    ahead = _dot(n, w_ref[:, stages[0][0]])
    for idx, (_, finish) in enumerate(stages):
        h = ahead
        if idx + 1 < len(stages):
            ahead = _dot(n, w_ref[:, stages[idx + 1][0]])
        finish(h)


def _inproj_odd(x, g, w, qn, kn, vn, cos, sup, sdn, table_blocks, sgu=None):
    rows = x.shape[0]
    tm = min(ROW_TILE, rows)
    tab = pl.BlockSpec((tm, LANES), lambda i: (i % table_blocks, 0))
    wide = (jax.ShapeDtypeStruct((rows, SGU_DIM), F32), _row_spec(tm, SGU_DIM))
    narrow = (jax.ShapeDtypeStruct((rows, SWA_KV), F32), _row_spec(tm, SWA_KV))
    outs = [(jax.ShapeDtypeStruct((rows, SWA_Q), BF16), _row_spec(tm, SWA_Q)), narrow, narrow, wide]
    outs += [] if sgu else [wide]
    extra = list(sgu) if sgu else []
    return pl.pallas_call(
        functools.partial(_inproj_odd_body, fuse_sgu=sgu is not None),
        grid=(rows // tm,),
        in_specs=[_row_spec(tm, D_MODEL), _full_spec(g.shape), _full_spec(w.shape), _full_spec(qn.shape),
                  _full_spec(kn.shape), _full_spec(vn.shape), tab, tab, tab] + [_full_spec(a.shape) for a in extra],
        out_specs=[spec for _, spec in outs],
        out_shape=[shape for shape, _ in outs],
        compiler_params=_cparams("parallel"),
        name="inproj_odd",
    )(x, g, w, qn, kn, vn, cos, sup, sdn, *extra)


def _sink_attend(scores, vb, sink, visible):
    s = jnp.where(visible, scores, NEG_BIG)
    m = jnp.maximum(jnp.max(s, axis=1, keepdims=True), sink)
    p = jnp.exp(s - m)
    l = jnp.sum(p, axis=1, keepdims=True) + jnp.exp(sink - m)
    return _dot(p.astype(BF16), vb) / l


def _swa_body(sink_ref, q_ref, kp_ref, kc_ref, vp_ref, vc_ref, mask_ref, o_ref):
    tq = q_ref.shape[1]
    kw = jnp.concatenate([kp_ref[0], kc_ref[0]], axis=0)
    vw = jnp.concatenate([vp_ref[0], vc_ref[0]], axis=0)
    nk = kw.shape[0]
    kw_sw = pltpu.roll(kw, HEAD_DIM, axis=1)
    vw_sw = pltpu.roll(vw, HEAD_DIM, axis=1)
    key_half = _lane_half((nk, LANES))
    ks = [jnp.where(key_half, a, b).astype(BF16) for a, b in ((kw, kw_sw), (kw_sw, kw))]
    vs = [jnp.where(key_half, a, b).astype(BF16) for a, b in ((vw, vw_sw), (vw_sw, vw))]
    visible = jnp.concatenate([mask_ref[0] > 0.5] * 2, axis=0)
    half = _lane_half((tq, LANES))
    first = lax.broadcasted_iota(jnp.int32, (2 * tq, 1), 0) < tq
    n_blocks = SWA_Q // LANES
    scores = []
    for j in range(n_blocks):
        qb = q_ref[0, :, j * LANES:(j + 1) * LANES]
        zero = jnp.zeros_like(qb)
        stacked = jnp.concatenate([jnp.where(half, qb, zero), jnp.where(half, zero, qb)], axis=0)
        scores.append(_dot_nt(stacked, ks[(2 * j) // SWA_GROUP]))
    for j in range(n_blocks):
        sink = jnp.where(first, sink_ref[2 * j], sink_ref[2 * j + 1])
        o = _sink_attend(scores[j], vs[(2 * j) // SWA_GROUP], sink, visible)
        o_ref[0, :, j * LANES:(j + 1) * LANES] = jnp.where(half, o[0:tq], o[tq:]).astype(BF16)


def _swa(sinks, q, k_prev, k_cur, v_prev, v_cur, mask, tq, prev_rows, prev_map, mask_map):
    b, t, qw = q.shape
    nk = prev_rows + tq
    cur = lambda bi, i: (bi, i, 0)
    return pl.pallas_call(
        _swa_body,
        grid=(b, t // tq),
        in_specs=[pl.BlockSpec(memory_space=pltpu.SMEM),
                  pl.BlockSpec((1, tq, qw), cur),
                  pl.BlockSpec((1, prev_rows, SWA_KV), prev_map), pl.BlockSpec((1, tq, SWA_KV), cur),
                  pl.BlockSpec((1, prev_rows, SWA_KV), prev_map), pl.BlockSpec((1, tq, SWA_KV), cur),
                  pl.BlockSpec((1, tq, nk), mask_map)],
        out_specs=pl.BlockSpec((1, tq, qw), cur),
        out_shape=jax.ShapeDtypeStruct((b, t, qw), BF16),
        compiler_params=_cparams("parallel", "parallel"),
        name="swa",
    )(sinks, q, k_prev, k_cur, v_prev, v_cur, mask)


def _sgu_core(u, v, w_ref, b_ref, o_ref, length):
    tril = (lax.broadcasted_iota(jnp.int32, (length, length), 1)
            <= lax.broadcasted_iota(jnp.int32, (length, length), 0))
    ws = [jnp.where(tril, w_ref[g], 0.0).astype(BF16) for g in range(SGU_GROUPS)]
    half = _lane_half((length, LANES))
    for c in range(u.shape[0] // length):
        rows = slice(c * length, (c + 1) * length)
        for j in range(SGU_GROUPS // 2):
            cols = slice(j * LANES, (j + 1) * LANES)
            vb = v[rows, cols].astype(BF16)
            mixed = jnp.where(half, _dot(ws[2 * j], vb), _dot(ws[2 * j + 1], vb)) + b_ref[:, cols]
            o_ref[rows, cols] = u[rows, cols] * mixed


def _sgu_body(u_ref, v_ref, w_ref, b_ref, o_ref, *, length):
    _sgu_core(u_ref[...], v_ref[...], w_ref, b_ref, o_ref, length)


def _sgu(u, v, w, bias, length, n_chunks):
    rows = u.shape[0]
    tm = length * n_chunks
    return pl.pallas_call(
        functools.partial(_sgu_body, length=length),
        grid=(rows // tm,),
        in_specs=[_row_spec(tm, SGU_DIM), _row_spec(tm, SGU_DIM), _full_spec(w.shape), _full_spec(bias.shape)],
        out_specs=_row_spec(tm, SGU_DIM),
        out_shape=jax.ShapeDtypeStruct(u.shape, F32),
        compiler_params=_cparams("parallel"),
        name="sgu",
    )(u, v, w, bias)


def _outproj_odd_core(x, c, d, w_ref):
    return x + _dot(c, w_ref[0:SWA_Q, :]) + _dot(d.astype(BF16), w_ref[SWA_Q:, :])


def _outproj_odd_body(x_ref, c_ref, d_ref, w_ref, out_ref):
    out_ref[...] = _outproj_odd_core(x_ref[...], c_ref[...], d_ref[...], w_ref)


def _outproj_odd(x, c, d, w):
    rows = x.shape[0]
    tm = min(ROW_TILE, rows)
    return pl.pallas_call(
        _outproj_odd_body,
        grid=(rows // tm,),
        in_specs=[_row_spec(tm, D_MODEL), _row_spec(tm, SWA_Q), _row_spec(tm, SGU_DIM), _full_spec(w.shape)],
        out_specs=_row_spec(tm, D_MODEL),
        out_shape=jax.ShapeDtypeStruct(x.shape, F32),
        compiler_params=_cparams("parallel"),
        name="outproj_odd",
    )(x, c, d, w)


def _memkv_body(m_ref, g_ref, w_ref, kn_ref, k_ref, v_ref):
    n = _rms(m_ref[0], g_ref[...]).astype(BF16)
    ones = _group_ones(HEAD_DIM)
    hk = _dot(n, w_ref[:, 0:MEM_DIM])
    k = hk * lax.rsqrt(_group_sum(hk * hk, ones) * (1.0 / HEAD_DIM) + NORM_EPS) * kn_ref[...]
    k_ref[0] = k.T
    v_ref[0] = _dot(n, w_ref[:, MEM_DIM:]).T


def _memkv(mem, g, w, kn):
    b, m, _ = mem.shape
    out = jax.ShapeDtypeStruct((b, MEM_DIM, m), F32)
    spec = pl.BlockSpec((1, MEM_DIM, m), lambda i: (i, 0, 0))
    return pl.pallas_call(
        _memkv_body,
        grid=(b,),
        in_specs=[pl.BlockSpec((1, m, D_MODEL), lambda i: (i, 0, 0)), _full_spec(g.shape), _full_spec(w.shape),
                  _full_spec(kn.shape)],
        out_specs=[spec, spec],
        out_shape=[out, out],
        compiler_params=_cparams("parallel"),
        name="memkv",
    )(mem, g, w, kn)


def _xattn_core(x, g, wq_ref, qn, mk_ref, mv_ref, wo_ref):
    tq = x.shape[0]
    n = _rms(x, g).astype(BF16)
    hq = _dot(n, wq_ref[...])
    ones = _group_ones(HEAD_DIM)
    q = (hq * lax.rsqrt(_group_sum(hq * hq, ones) * (1.0 / HEAD_DIM) + NORM_EPS) * qn * ATT_SCALE).astype(BF16)
    half = _lane_half((tq, LANES))
    scores, values = [], []
    for j in range(MEM_HEADS // 2):
        cols = slice(j * LANES, (j + 1) * LANES)
        kb = mk_ref[0, cols, :].astype(BF16)
        values.append(mv_ref[0, cols, :].astype(BF16))
        qb = q[:, cols]
        zero = jnp.zeros_like(qb)
        heads = [jnp.where(half, qb, zero), jnp.where(half, zero, qb)]
        scores.append([_dot(qh, kb) for qh in ([jnp.concatenate(heads, axis=0)] if tq <= LANES else heads)])
    blocks = []
    for per_block, vb in zip(scores, values):
        outs = []
        for s in per_block:
            p = jnp.exp(s - jnp.max(s, axis=1, keepdims=True))
            outs.append(_dot_nt(p.astype(BF16), vb) / jnp.sum(p, axis=1, keepdims=True))
        if len(outs) == 1:
            outs = [outs[0][0:tq], outs[0][tq:]]
        blocks.append(jnp.where(half, outs[0], outs[1]))
    o = jnp.concatenate(blocks, axis=1).astype(BF16)
    return x + _dot(o, wo_ref[...])


def _xattn_body(x_ref, g_ref, wq_ref, qn_ref, mk_ref, mv_ref, wo_ref, out_ref):
    out_ref[0] = _xattn_core(x_ref[0], g_ref[...], wq_ref, qn_ref[...], mk_ref, mv_ref, wo_ref)


def _tail_body(*refs, even):
    n_mix = 8 if even else 4
    mix, rest = refs[:n_mix], refs[n_mix:]
    gx_ref, wq_ref, qn_ref, mk_ref, mv_ref, wo_ref, gf_ref, wg_ref, wu_ref, wd_ref, out_ref, acc_ref = rest
    if even:
        x_ref, a_ref, o_ref, bon_ref, gate_ref, lng_ref, lnb_ref, w_ref = mix
        x = _outproj_even_core(x_ref[0], a_ref[0], o_ref[0], bon_ref[0], gate_ref[0], lng_ref[...], lnb_ref[...], w_ref)
    else:
        x_ref, c_ref, d_ref, w_ref = mix
        x = _outproj_odd_core(x_ref[0], c_ref[0], d_ref[0], w_ref)
    x = _xattn_core(x, gx_ref[...], wq_ref, qn_ref[...], mk_ref, mv_ref, wo_ref)
    out_ref[0] = _ffn_core(x, gf_ref[...], wg_ref, wu_ref, wd_ref, acc_ref)


def _tail(even, layer, x, mix_rows, mix_params, xattn_params, mk, mv, wo, ffn_params, tm):
    b, t, _ = x.shape
    m = mk.shape[2]
    rows3 = lambda c: pl.BlockSpec((1, tm, c), lambda bi, i: (bi, i, 0))
    mem = pl.BlockSpec((1, MEM_DIM, m), lambda bi, i: (bi, 0, 0))
    args = [x, *mix_rows, *mix_params, *xattn_params, mk, mv, wo, *ffn_params]
    in_specs = ([rows3(a.shape[2]) for a in (x, *mix_rows)] + [_full_spec(p.shape) for p in mix_params]
                + [_full_spec(p.shape) for p in xattn_params] + [mem, mem, _full_spec(wo.shape)]
                + [_full_spec(ffn_params[0].shape)] + [_layer_spec(w, layer) for w in ffn_params[1:]])
    return pl.pallas_call(
        functools.partial(_tail_body, even=even),
        grid=(b, t // tm),
        in_specs=in_specs,
        out_specs=rows3(D_MODEL),
        out_shape=jax.ShapeDtypeStruct(x.shape, F32),
        scratch_shapes=[pltpu.VMEM((tm, D_MODEL), F32)],
        compiler_params=_cparams("parallel", "parallel"),
        name="tail_even" if even else "tail_odd",
    )(*args)


def _xattn(x, g, wq, qn, mk, mv, wo, tq):
    b, t, _ = x.shape
    m = mk.shape[2]
    return pl.pallas_call(
        _xattn_body,
        grid=(b, t // tq),
        in_specs=[pl.BlockSpec((1, tq, D_MODEL), lambda bi, i: (bi, i, 0)), _full_spec(g.shape),
                  _full_spec(wq.shape), _full_spec(qn.shape),
                  pl.BlockSpec((1, MEM_DIM, m), lambda bi, i: (bi, 0, 0)),
                  pl.BlockSpec((1, MEM_DIM, m), lambda bi, i: (bi, 0, 0)), _full_spec(wo.shape)],
        out_specs=pl.BlockSpec((1, tq, D_MODEL), lambda bi, i: (bi, i, 0)),
        out_shape=jax.ShapeDtypeStruct(x.shape, F32),
        compiler_params=_cparams("parallel", "parallel"),
        name="xattn",
    )(x, g, wq, qn, mk, mv, wo)


def _head_minor(x, axis=-1):
    axis %= x.ndim
    shape = x.shape
    y = x.reshape(shape[:axis] + (RWKV_HEADS, HEAD_DIM) + shape[axis + 1:])
    return jnp.swapaxes(y, axis, axis + 1).reshape(shape)


def _head_major(x, axis=-1):
    axis %= x.ndim
    shape = x.shape
    y = x.reshape(shape[:axis] + (HEAD_DIM, RWKV_HEADS) + shape[axis + 1:])
    return jnp.swapaxes(y, axis, axis + 1).reshape(shape)


def _cols_head_minor(x, inverse=False):
    n = 3 * RWKV_DIM
    blocks = x[..., :n].reshape(x.shape[:-1] + (3, RWKV_DIM))
    blocks = (_head_major if inverse else _head_minor)(blocks)
    return jnp.concatenate([blocks.reshape(x.shape[:-1] + (n,)), x[..., n:]], axis=-1)


def _to_chains(x, b, t, dup):
    y = x.reshape(b, t, HEAD_DIM, RWKV_HEADS).transpose(1, 2, 0, 3).reshape(t, HEAD_DIM, b * RWKV_HEADS)
    return jnp.concatenate([y] * dup, axis=-1) if dup > 1 else y


def _rope_tables(pos):
    half = ROPE_DIM // 2
    inv_freq = jnp.power(ROPE_THETA, -jnp.arange(half, dtype=F32) / half)
    ang = pos.astype(F32)[:, None] * inv_freq[None, :]
    cos, sin = jnp.cos(ang), jnp.sin(ang)
    n = pos.shape[0]
    pad = jnp.zeros((n, HEAD_DIM - ROPE_DIM), F32)
    zero = jnp.zeros((n, half), F32)
    cos_t = jnp.concatenate([cos, cos, pad + 1.0], axis=1)
    up_t = jnp.concatenate([zero, sin, pad], axis=1)
    dn_t = jnp.concatenate([-sin, zero, pad], axis=1)
    two = lambda a: jnp.concatenate([a, a], axis=1)
    return two(cos_t), two(up_t), two(dn_t)


def _swa_prompt_mask(tq):
    span = WINDOW + tq
    qc = np.arange(tq)[:, None] // CHUNK
    kc = np.arange(span)[None, :] // CHUNK - WINDOW_CHUNKS
    band = (kc <= qc) & (kc >= qc - WINDOW_CHUNKS)
    first = band & (kc >= 0)
    return jnp.asarray(np.stack([first, band]).astype(np.float32))


def _swa_sample_mask(past, rows, t):
    kc = (past - rows + np.arange(rows + t)) // CHUNK
    qc = (past + np.arange(t)) // CHUNK
    m = (kc[None, :] <= qc[:, None]) & (kc[None, :] >= qc[:, None] - WINDOW_CHUNKS)
    return jnp.asarray(m[None].astype(np.float32))


def kernel(x_prompt, x_sample, cache_fox_k, cache_fox_v, cache_fox_logf, state_rwkv, state_rwkv_shift, cache_swa_k, cache_swa_v, cache_mem_k, cache_mem_v, mem_prompt, ffn1_norm, ffn1_w_gate, ffn1_w_up, ffn1_w_down, mix_norm, ev_w_in, fox_b_f, fox_q_norm, fox_k_norm, rwkv_mu, rwkv_w0, rwkv_w2, rwkv_a0, rwkv_a2, rwkv_g2, rwkv_k_k, rwkv_k_a, rwkv_r_k, rwkv_ln_g, rwkv_ln_b, ev_w_out, od_w_in, swa_q_norm, swa_k_norm, swa_sinks, sgu_v_norm, sgu_w_s, sgu_b, od_w_out, xattn_norm, mem_norm, xattn_wq, xattn_wkv, xattn_q_norm, xattn_k_norm, xattn_wo, ffn2_norm, ffn2_w_gate, ffn2_w_up, ffn2_w_down):
    bp, tp, _ = x_prompt.shape
    bs, ts, _ = x_sample.shape
    depth = ffn1_norm.shape[0]
    past = cache_fox_k.shape[2]
    mem_tokens = mem_prompt.shape[1]
    xp = x_prompt.reshape(bp * tp, D_MODEL)
    xs = x_sample.reshape(bs * ts, D_MODEL)
    row = lambda a: a.reshape(1, -1)
    tile_heads = lambda a, n: jnp.tile(a, n).reshape(1, -1)
    ffn1_w = (ffn1_w_gate.astype(BF16), ffn1_w_up.astype(BF16), ffn1_w_down.astype(BF16))
    ffn2_w = (ffn2_w_gate.astype(BF16), ffn2_w_up.astype(BF16), ffn2_w_down.astype(BF16))

    out = {k: [] for k in ("p_fox_k", "p_fox_v", "p_fox_logf", "p_rwkv_state", "p_rwkv_shift", "p_swa_k", "p_swa_v",
                           "p_mem_k", "p_mem_v", "s_fox_k", "s_fox_v", "s_fox_logf", "s_rwkv_state", "s_rwkv_shift",
                           "s_swa_k", "s_swa_v", "s_sgu_v")}

    for l in range(depth):
        f1 = (row(ffn1_norm[l]),) + ffn1_w
        xp = _ffn(xp, l, *f1)
        xs = _ffn(xs, l, *f1)
        g_mix = row(mix_norm[l])
        if l % 2 == 0:
            e = l // 2
            w_in = ev_w_in[e]
            f0 = 3 * FOX_DIM
            w_cat = jnp.concatenate([w_in[:, :f0], jnp.pad(w_in[:, f0:f0 + FOX_HEADS], ((0, 0), (0, LANES - FOX_HEADS))),
                                     _cols_head_minor(w_in[:, f0 + FOX_HEADS:])], axis=1).astype(BF16)
            bf = jnp.pad(fox_b_f[e], (0, LANES - FOX_HEADS)).reshape(1, LANES)
            qn = tile_heads(fox_q_norm[e], FOX_HEADS)
            kn = tile_heads(fox_k_norm[e], FOX_HEADS)
            hm = _head_minor
            w2p = jnp.pad(hm(rwkv_w2[e]), ((0, LANES - DECAY_LORA), (0, 0))).astype(BF16)
            a2p = jnp.pad(hm(rwkv_a2[e]), ((DECAY_LORA, 0), (0, 0))).astype(BF16)
            params = (row(_cols_head_minor(rwkv_mu[e])), row(hm(rwkv_w0[e])), w2p, row(hm(rwkv_a0[e])), a2p,
                      hm(rwkv_g2[e]).astype(BF16), row(hm(rwkv_k_k[e])), row(hm(rwkv_k_a[e])),
                      row(hm(rwkv_r_k[e].reshape(-1))))
            qp, ktp, vtp, lfp, last_p, *prep_p = _inproj_even(xp, g_mix, w_cat, bf, qn, kn, seq_len=tp,
                                                              prep_params=params)
            qs, ks, vs, lfs, hrs = _inproj_even(xs, g_mix, w_cat, bf, qn, kn)

            cum_p, cumt_p = _cumsum(lfp.reshape(bp, tp, LANES), CUM_TILE)
            a_p = _fox_prompt(qp.reshape(bp, tp, FOX_DIM), ktp, vtp, cum_p, cumt_p, ATT_TILE)
            lfs8 = lfs.reshape(bs, ts, LANES)[:, :, :FOX_HEADS]
            tot = past + ts
            padded = -(-tot // CUM_TILE) * CUM_TILE
            lf_all = jnp.concatenate([cache_fox_logf[e].astype(F32), lfs8], axis=1)
            lf_all = jnp.pad(lf_all, ((0, 0), (0, padded - tot), (0, LANES - FOX_HEADS)))
            cum_s, cumt_s = _cumsum(lf_all, CUM_TILE)
            token_minor = lambda c: jnp.transpose(c, (0, 2, 3, 1)).reshape(bs, FOX_DIM, past)
            a_s = _fox_sample(qs.reshape(bs, ts, FOX_DIM), ks.reshape(bs, ts, FOX_DIM), vs.reshape(bs, ts, FOX_DIM),
                              token_minor(cache_fox_k[e]), token_minor(cache_fox_v[e]), cum_s, cumt_s)

            prev_s = jnp.pad(_cols_head_minor(state_rwkv_shift[e].astype(F32)), ((0, 0), (7, 0), (0, 0)))
            prep_s = _rwkv_prep(hrs, prev_s, ts, params)

            def scan(prep, b, t, state0):
                r, w, k2, v, kap, bet, g, bon = prep
                dup = LANES // (b * RWKV_HEADS)
                nv = HEAD_DIM // dup
                in_kernel = t % LANES == 0
                if in_kernel:
                    halves = lambda j: tuple(half * nv + j for half in range(dup))
                    ops = [_to_chains_pallas(a.reshape(b, t, RWKV_DIM), nv, halves) for a in (w, kap, bet, k2, r)]
                    vt = _to_chains_pallas(v.reshape(b, t, RWKV_DIM), nv, halves)
                else:
                    ops = [_to_chains(a, b, t, dup) for a in (w, kap, bet, k2, r)]
                    vt = v.reshape(b, t, dup, nv, RWKV_HEADS).transpose(1, 3, 2, 0, 4).reshape(t, nv, LANES)
                s0 = state0.reshape(b, RWKV_HEADS, dup, nv // SUBLANES, SUBLANES, HEAD_DIM)
                s0 = s0.transpose(3, 5, 4, 2, 0, 1).reshape(nv // SUBLANES, HEAD_DIM, SUBLANES, LANES)
                o, sT = _rwkv_scan(*ops, vt, s0)
                if in_kernel:
                    o = _from_chains_pallas(o, b).reshape(b * t, RWKV_DIM)
                else:
                    o = o.reshape(t, nv, dup, b, RWKV_HEADS).transpose(3, 0, 2, 1, 4).reshape(b * t, RWKV_DIM)
                sT = sT.reshape(nv // SUBLANES, HEAD_DIM, SUBLANES, dup, b, RWKV_HEADS)
                sT = sT.transpose(4, 5, 3, 0, 2, 1).reshape(b, RWKV_HEADS, HEAD_DIM, HEAD_DIM)
                return o, sT, g, bon

            o_p, st_p, g_p, bon_p = scan(prep_p, bp, tp, jnp.zeros((bp, RWKV_HEADS, HEAD_DIM, HEAD_DIM), F32))
            o_s, st_s, g_s, bon_s = scan(prep_s, bs, ts, state_rwkv[e].astype(F32))

            w_out = jnp.concatenate([ev_w_out[e][:FOX_DIM], hm(ev_w_out[e][FOX_DIM:], axis=0)], axis=0).astype(BF16)
            lng, lnb = row(hm(rwkv_ln_g[e])), row(hm(rwkv_ln_b[e]))
            seq = lambda a: a.reshape(bp, tp, -1)
            mix_p = (True, (a_p, seq(o_p), seq(bon_p), seq(g_p)), (lng, lnb, w_out))
            xs = _outproj_even(xs, a_s.reshape(bs * ts, FOX_DIM), o_s, bon_s, g_s, lng, lnb, w_out)

            rows_of = lambda a: jnp.transpose(a.reshape(bp, FOX_HEADS, HEAD_DIM, tp), (0, 3, 1, 2))
            out["p_fox_k"].append(rows_of(ktp))
            out["p_fox_v"].append(rows_of(vtp))
            out["p_fox_logf"].append(lfp.reshape(bp, tp, LANES)[:, :, :FOX_HEADS])
            out["p_rwkv_state"].append(st_p)
            final_rows = last_p.reshape(bp, tp // ROW_TILE, SUBLANES, RWKV_COLS)[:, -1, SUBLANES - 1:, :]
            out["p_rwkv_shift"].append(_cols_head_minor(final_rows, inverse=True))
            out["s_fox_k"].append(ks.reshape(bs, ts, FOX_HEADS, HEAD_DIM))
            out["s_fox_v"].append(vs.reshape(bs, ts, FOX_HEADS, HEAD_DIM))
            out["s_fox_logf"].append(lfs8)
            out["s_rwkv_state"].append(st_s)
            out["s_rwkv_shift"].append(_cols_head_minor(hrs.reshape(bs, ts, RWKV_COLS)[:, -1:], inverse=True))
        else:
            j = l // 2
            w_cat = od_w_in[j].astype(BF16)
            qn = tile_heads(swa_q_norm[j], SWA_HEADS)
            kn = tile_heads(swa_k_norm[j], SWA_KV_HEADS)
            vn = row(sgu_v_norm[j])
            tabs_p = _rope_tables(jnp.arange(tp))
            tabs_s = _rope_tables(past + jnp.arange(bs * ts) % ts)
            bias = jnp.repeat(jnp.transpose(sgu_b[j]), HEAD_DIM, axis=1)
            qp, kp, vp, d_p = _inproj_odd(xp, g_mix, w_cat, qn, kn, vn, *tabs_p, tp // ROW_TILE,
                                          sgu=(sgu_w_s[j], bias))
            qs, ks, vs, us, gs = _inproj_odd(xs, g_mix, w_cat, qn, kn, vn, *tabs_s, 1)

            qw = SWA_Q
            kp3, vp3 = kp.reshape(bp, tp, SWA_KV), vp.reshape(bp, tp, SWA_KV)
            ratio = SWA_TILE // WINDOW
            c_p = _swa(swa_sinks[j], qp.reshape(bp, tp, qw), kp3, kp3, vp3, vp3, _swa_prompt_mask(SWA_TILE),
                       SWA_TILE, WINDOW, lambda bi, i: (bi, jnp.maximum(i * ratio - 1, 0), 0),
                       lambda bi, i: (jnp.minimum(i, 1), 0, 0))
            rows_c = cache_swa_k.shape[2]
            ck3 = cache_swa_k[j].reshape(bs, rows_c, SWA_KV)
            cv3 = cache_swa_v[j].reshape(bs, rows_c, SWA_KV)
            ks3, vs3 = ks.reshape(bs, ts, SWA_KV), vs.reshape(bs, ts, SWA_KV)
            c_s = _swa(swa_sinks[j], qs.reshape(bs, ts, qw), ck3, ks3, cv3, vs3, _swa_sample_mask(past, rows_c, ts),
                       ts, rows_c, lambda bi, i: (bi, 0, 0), lambda bi, i: (0, 0, 0))

            d_s = _sgu(us, gs, sgu_w_s[j][:, :ts, :ts], bias[:ts], ts, 1)

            w_out = od_w_out[j].astype(BF16)
            mix_p = (False, (c_p, d_p.reshape(bp, tp, SGU_DIM)), (w_out,))
            xs = _outproj_odd(xs, c_s.reshape(bs * ts, qw), d_s, w_out)

            out["p_swa_k"].append(kp3[:, -WINDOW:].reshape(bp, WINDOW, SWA_KV_HEADS, HEAD_DIM))
            out["p_swa_v"].append(vp3[:, -WINDOW:].reshape(bp, WINDOW, SWA_KV_HEADS, HEAD_DIM))
            out["s_swa_k"].append(jnp.concatenate([ck3, ks3], axis=1)[:, -rows_c:].reshape(bs, rows_c, SWA_KV_HEADS, HEAD_DIM))
            out["s_swa_v"].append(jnp.concatenate([cv3, vs3], axis=1)[:, -rows_c:].reshape(bs, rows_c, SWA_KV_HEADS, HEAD_DIM))
            out["s_sgu_v"].append(gs.reshape(bs, ts, SGU_DIM))

        mk, mv = _memkv(mem_prompt, row(mem_norm[l]), xattn_wkv[l].astype(BF16), tile_heads(xattn_k_norm[l], MEM_HEADS))
        xa = (row(xattn_norm[l]), xattn_wq[l].astype(BF16), tile_heads(xattn_q_norm[l], MEM_HEADS))
        wo = xattn_wo[l].astype(BF16)
        f2 = (row(ffn2_norm[l]),) + ffn2_w
        xp = _tail(mix_p[0], l, xp.reshape(bp, tp, D_MODEL), mix_p[1], mix_p[2], xa, mk, mv, wo, f2,
                   ROW_TILE).reshape(bp * tp, D_MODEL)
        mem_minor = lambda c: jnp.transpose(c, (0, 2, 3, 1)).reshape(bs, MEM_DIM, mem_tokens)
        xs = _xattn(xs.reshape(bs, ts, D_MODEL), *xa, mem_minor(cache_mem_k[l]), mem_minor(cache_mem_v[l]),
                    wo, ts).reshape(bs * ts, D_MODEL)
        mem_rows_of = lambda a: jnp.transpose(a.reshape(bp, MEM_HEADS, HEAD_DIM, mem_tokens), (0, 3, 1, 2))
        out["p_mem_k"].append(mem_rows_of(mk))
        out["p_mem_v"].append(mem_rows_of(mv))

        xs = _ffn(xs, l, *f2)

    order = ("p_fox_k", "p_fox_v", "p_fox_logf", "p_rwkv_state", "p_rwkv_shift", "p_swa_k", "p_swa_v", "p_mem_k",
             "p_mem_v", "s_fox_k", "s_fox_v", "s_fox_logf", "s_rwkv_state", "s_rwkv_shift", "s_swa_k", "s_swa_v",
             "s_sgu_v")
    return (xp.reshape(bp, tp, D_MODEL), xs.reshape(bs, ts, D_MODEL)) + tuple(jnp.stack(out[k]) for k in order)
```

```python
import functools

import numpy as np
import jax
import jax.numpy as jnp
from jax import lax
from jax.experimental import pallas as pl
from jax.experimental.pallas import tpu as pltpu

F32 = jnp.float32
BF16 = jnp.bfloat16

D_MODEL = 1024
HEAD_DIM = 64
NORM_EPS = 1e-6
ROPE_THETA = 500000.0
ROPE_DIM = HEAD_DIM // 4
CHUNK = 64
FOX_HEADS = 8
FOX_DIM = FOX_HEADS * HEAD_DIM
RWKV_HEADS = 8
RWKV_DIM = RWKV_HEADS * HEAD_DIM
DECAY_LORA = 64
ICLR_LORA = 64
GATE_LORA = 128
RWKV_COLS = 3 * RWKV_DIM + DECAY_LORA + ICLR_LORA + GATE_LORA
RWKV_GN_EPS = 64e-5
SWA_HEADS = 8
SWA_KV_HEADS = 2
SWA_GROUP = SWA_HEADS // SWA_KV_HEADS
SWA_Q = SWA_HEADS * HEAD_DIM
SWA_KV = SWA_KV_HEADS * HEAD_DIM
WINDOW = 128
WINDOW_CHUNKS = WINDOW // CHUNK
SGU_GROUPS = 8
SGU_DIM = SGU_GROUPS * HEAD_DIM
SGU_CHUNK = 128
MEM_HEADS = 4
MEM_DIM = MEM_HEADS * HEAD_DIM
D_FF = 2816

LANES = 128
ROW_TILE = 512
FF_TILE = 256
ATT_TILE = 512
CUM_TILE = 256
SWA_TILE = 256
SCAN_TOKENS = 64
VMEM_LIMIT = 56 * 1024 * 1024
ATT_SCALE = HEAD_DIM ** -0.5
LOG2E = 1.4426950408889634
NEG_BIG = -1e30


def _cparams(*sem):
    return pltpu.CompilerParams(dimension_semantics=sem, vmem_limit_bytes=VMEM_LIMIT)


def _dot(a, b):
    return jnp.dot(a, b, preferred_element_type=F32)


def _dot_nt(a, b):
    return lax.dot_general(a, b, (((1,), (1,)), ((), ())), preferred_element_type=F32)


def _rms(x, g):
    ms = jnp.mean(x * x, axis=-1, keepdims=True)
    return (x * lax.rsqrt(ms + NORM_EPS)) * g


def _group_ones(group):
    shift = int(np.log2(group))
    r = lax.broadcasted_iota(jnp.int32, (LANES, LANES), 0) >> shift
    c = lax.broadcasted_iota(jnp.int32, (LANES, LANES), 1) >> shift
    return jnp.where(r == c, 1.0, 0.0).astype(BF16)


def _group_sum(x, ones):
    parts = []
    for j in range(x.shape[1] // LANES):
        blk = x[:, j * LANES:(j + 1) * LANES]
        hi = blk.astype(BF16)
        lo = (blk - hi.astype(F32)).astype(BF16)
        parts.append(_dot(hi, ones) + _dot(lo, ones))
    return parts[0] if len(parts) == 1 else jnp.concatenate(parts, axis=1)


def _head_ones():
    r = lax.broadcasted_iota(jnp.int32, (LANES, LANES), 0) & (RWKV_HEADS - 1)
    c = lax.broadcasted_iota(jnp.int32, (LANES, LANES), 1) & (RWKV_HEADS - 1)
    return jnp.where(r == c, 1.0, 0.0).astype(BF16)


def _head_sum(x, ones):
    part = x[:, 0:LANES]
    for j in range(1, x.shape[1] // LANES):
        part = part + x[:, j * LANES:(j + 1) * LANES]
    hi = part.astype(BF16)
    lo = (part - hi.astype(F32)).astype(BF16)
    tot = _dot(hi, ones) + _dot(lo, ones)
    return jnp.concatenate([tot] * (x.shape[1] // LANES), axis=1)


def _log_sigmoid(z):
    return jnp.minimum(z, 0.0) - jnp.log(1.0 + jnp.exp(-jnp.abs(z)))


def _gelu_tanh(x):
    return 0.5 * x * (1.0 + jnp.tanh(0.7978845608028654 * (x + 0.044715 * (x * x * x))))


def _lane_half(shape):
    return lax.broadcasted_iota(jnp.int32, shape, 1) < HEAD_DIM


def _row_spec(tm, cols):
    return pl.BlockSpec((tm, cols), lambda i: (i, 0))


def _full_spec(shape):
    nd = len(shape)
    return pl.BlockSpec(shape, lambda *_: (0,) * nd, pipeline_mode=pl.Buffered(1))


def _ffn_core(x, g, wg_ref, wu_ref, wd_ref, acc_ref):
    n = _rms(x, g).astype(BF16)
    for c in range(wg_ref.shape[2] // FF_TILE):
        cols = slice(c * FF_TILE, (c + 1) * FF_TILE)
        gate = _dot(n, wg_ref[0, :, cols])
        up = _dot(n, wu_ref[0, :, cols])
        act = (gate * jax.nn.sigmoid(gate) * up).astype(BF16)
        part = _dot(act, wd_ref[0, cols, :])
        if c == 0:
            acc_ref[...] = part
        else:
            acc_ref[...] += part
    return x + 0.5 * acc_ref[...]


def _ffn_body(x_ref, g_ref, wg_ref, wu_ref, wd_ref, o_ref, acc_ref):
    o_ref[...] = _ffn_core(x_ref[...], g_ref[...], wg_ref, wu_ref, wd_ref, acc_ref)


def _layer_spec(w, layer):
    nd = w.ndim
    return pl.BlockSpec((1,) + w.shape[1:], lambda *_: (layer,) + (0,) * (nd - 1), pipeline_mode=pl.Buffered(1))


def _ffn(x, layer, g, wg, wu, wd):
    rows = x.shape[0]
    tm = min(ROW_TILE, rows)
    return pl.pallas_call(
        _ffn_body,
        grid=(rows // tm,),
        in_specs=[_row_spec(tm, D_MODEL), _full_spec(g.shape), _layer_spec(wg, layer),
                  _layer_spec(wu, layer), _layer_spec(wd, layer)],
        out_specs=_row_spec(tm, D_MODEL),
        out_shape=jax.ShapeDtypeStruct(x.shape, F32),
        scratch_shapes=[pltpu.VMEM((tm, D_MODEL), F32)],
        compiler_params=_cparams("parallel"),
        name="ffn",
    )(x, g, wg, wu, wd)


def _inproj_even_body(x_ref, g_ref, w_ref, bf_ref, qn_ref, kn_ref, *refs, per_seq):
    if per_seq is None:
        q_ref, k_ref, v_ref, lf_ref, hr_ref = refs
    else:
        prep_params, (q_ref, k_ref, v_ref, lf_ref, last_ref), prep_outs, prev_ref = (
            refs[:9], refs[9:14], refs[14:22], refs[22])
    token_minor = per_seq is not None
    n = _rms(x_ref[...], g_ref[...]).astype(BF16)
    ones = _group_ones(HEAD_DIM)
    f0 = 3 * FOX_DIM

    def head_norm(h, gain):
        return h * lax.rsqrt(_group_sum(h * h, ones) * (1.0 / HEAD_DIM) + NORM_EPS) * gain

    def q_finish(h):
        q_ref[...] = (head_norm(h, qn_ref[...]) * (ATT_SCALE * LOG2E)).astype(BF16)

    def k_finish(h):
        k = head_norm(h, kn_ref[...])
        if token_minor:
            k_ref[0] = k.T
        else:
            k_ref[...] = k

    def v_finish(h):
        if token_minor:
            v_ref[0] = h.T
        else:
            v_ref[...] = h

    def lf_finish(h):
        lf_ref[...] = _log_sigmoid(h + bf_ref[...])

    def hr_finish(hr):
        if per_seq is None:
            hr_ref[...] = hr
            return
        _rwkv_prep_core(hr, prev_ref[SUBLANES - 1:SUBLANES, :], prep_params, prep_outs)
        tail = hr[hr.shape[0] - SUBLANES:, :]
        prev_ref[...] = tail
        last_ref[0] = tail

    if per_seq is not None:
        @pl.when(pl.program_id(0) % per_seq == 0)
        def _():
            prev_ref[...] = jnp.zeros(prev_ref.shape, F32)

    stages = [(slice(f0 + LANES, w_ref.shape[1]), hr_finish), (slice(0, FOX_DIM), q_finish),
              (slice(FOX_DIM, 2 * FOX_DIM), k_finish), (slice(2 * FOX_DIM, f0), v_finish),
              (slice(f0, f0 + LANES), lf_finish)]
    ahead = _dot(n, w_ref[:, stages[0][0]])
    for idx, (_, finish) in enumerate(stages):
        h = ahead
        if idx + 1 < len(stages):
            ahead = _dot(n, w_ref[:, stages[idx + 1][0]])
        finish(h)


def _inproj_even(x, g, w, bf, qn, kn, seq_len=None, prep_params=()):
    rows = x.shape[0]
    tm = min(ROW_TILE, rows)
    in_specs = [_row_spec(tm, D_MODEL), _full_spec(g.shape), _full_spec(w.shape), _full_spec(bf.shape),
                _full_spec(qn.shape), _full_spec(kn.shape)]
    q_out = (jax.ShapeDtypeStruct((rows, FOX_DIM), BF16), _row_spec(tm, FOX_DIM))
    lf_out = (jax.ShapeDtypeStruct((rows, LANES), F32), _row_spec(tm, LANES))
    if seq_len is None:
        kv = (jax.ShapeDtypeStruct((rows, FOX_DIM), F32), _row_spec(tm, FOX_DIM))
        outs = [q_out, kv, kv, lf_out, (jax.ShapeDtypeStruct((rows, RWKV_COLS), F32), _row_spec(tm, RWKV_COLS))]
        per_seq, scratch, sem = None, [], "parallel"
    else:
        per_seq = seq_len // tm
        kv = (jax.ShapeDtypeStruct((rows // seq_len, FOX_DIM, seq_len), F32),
              pl.BlockSpec((1, FOX_DIM, tm), lambda i: (i // per_seq, 0, i % per_seq)))
        last = (jax.ShapeDtypeStruct((rows // tm, SUBLANES, RWKV_COLS), F32),
                pl.BlockSpec((1, SUBLANES, RWKV_COLS), lambda i: (i, 0, 0)))
        operand = (jax.ShapeDtypeStruct((rows, RWKV_DIM), F32), _row_spec(tm, RWKV_DIM))
        outs = [q_out, kv, kv, lf_out, last] + [operand] * 8
        in_specs += [_full_spec(p.shape) for p in prep_params]
        scratch, sem = [pltpu.VMEM((SUBLANES, RWKV_COLS), F32)], "arbitrary"
    return pl.pallas_call(
        functools.partial(_inproj_even_body, per_seq=per_seq),
        grid=(rows // tm,),
        in_specs=in_specs,
        out_specs=[spec for _, spec in outs],
        out_shape=[shape for shape, _ in outs],
        scratch_shapes=scratch,
        compiler_params=_cparams(sem),
        name="inproj_even",
    )(x, g, w, bf, qn, kn, *prep_params)


def _split3(x):
    hi = x.astype(BF16)
    r1 = x - hi.astype(F32)
    mid = r1.astype(BF16)
    lo = (r1 - mid.astype(F32)).astype(BF16)
    return hi, mid, lo


def _cumsum_body(lf_ref, col_ref, row_ref, *, n_chunks, tk):
    r = lax.broadcasted_iota(jnp.int32, (tk, tk), 0)
    c = lax.broadcasted_iota(jnp.int32, (tk, tk), 1)
    tri = jnp.where(c <= r, 1.0, 0.0).astype(BF16)
    carry = jnp.zeros((1, LANES), F32)
    for i in range(n_chunks):
        hi, mid, lo = _split3(lf_ref[0, i * tk:(i + 1) * tk, :])
        cs = _dot(tri, hi) + _dot(tri, mid) + _dot(tri, lo) + carry
        col_ref[0, i * tk:(i + 1) * tk, :] = cs
        row_ref[0, :, i * tk:(i + 1) * tk] = cs.T[0:FOX_HEADS, :]
        carry = cs[tk - 1:tk, :]


def _cumsum(lf, tk):
    b, length, _ = lf.shape
    n_chunks = length // tk
    return pl.pallas_call(
        functools.partial(_cumsum_body, n_chunks=n_chunks, tk=tk),
        grid=(b,),
        in_specs=[pl.BlockSpec((1, length, LANES), lambda i: (i, 0, 0))],
        out_specs=[pl.BlockSpec((1, length, LANES), lambda i: (i, 0, 0)),
                   pl.BlockSpec((1, FOX_HEADS, length), lambda i: (i, 0, 0))],
        out_shape=[jax.ShapeDtypeStruct((b, length, LANES), F32),
                   jax.ShapeDtypeStruct((b, FOX_HEADS, length), F32)],
        compiler_params=_cparams("parallel"),
        name="cumsum",
    )(lf)


def _place3(terms, src_lane, dst_lane):
    r = lax.broadcasted_iota(jnp.int32, (LANES, LANES), 0)
    c = lax.broadcasted_iota(jnp.int32, (LANES, LANES), 1)
    out = None
    for i, t in enumerate(terms):
        sel = jnp.where((r == src_lane) & (c == dst_lane + i), 1.0, 0.0).astype(BF16)
        out = _dot(t, sel) if out is None else out + _dot(t, sel)
    return out


def _fox_q(q_pair, cum_col, head, h):
    n = q_pair.shape[0]
    qf = q_pair.astype(F32)
    if h == 1:
        qf = pltpu.roll(qf, HEAD_DIM, axis=1)
    lane = lax.broadcasted_iota(jnp.int32, (n, LANES), 1)
    ones = jnp.where((lane >= HEAD_DIM + 3) & (lane < HEAD_DIM + 6), 1.0, 0.0)
    aug = _place3(_split3(cum_col * LOG2E), head, HEAD_DIM) + ones
    return jnp.where(lane < HEAD_DIM, qf, aug).astype(BF16)


def _fox_kt(kt, cum_row):
    n = kt.shape[1]
    hi, mid, lo = (x.astype(F32) for x in _split3(cum_row * LOG2E))
    row = lax.broadcasted_iota(jnp.int32, (8, n), 0)
    aug = jnp.where(row < 3, 1.0, jnp.where(row == 3, -hi, jnp.where(row == 4, -mid, jnp.where(row == 5, -lo, 0.0))))
    return jnp.concatenate([kt, aug, jnp.zeros((HEAD_DIM - 8, n), F32)], axis=0).astype(BF16)


def _fox_vt(vt):
    n = vt.shape[1]
    row = lax.broadcasted_iota(jnp.int32, (HEAD_DIM, n), 0)
    return jnp.concatenate([vt, jnp.where(row == 0, 1.0, 0.0)], axis=0).astype(BF16)


def _fox_finish(acc):
    return acc / acc[:, HEAD_DIM:HEAD_DIM + 1]


def _fox_prompt_body(q_ref, cum_ref, cumt_ref, kt_ref, vt_ref, o_ref, ka_sc, va_sc, m_sc, acc_sc, *, tq, t):
    hp = pl.program_id(1)
    i = pl.program_id(2)

    @pl.when(i == 0)
    def _():
        for h in range(2):
            rows = slice(h * HEAD_DIM, (h + 1) * HEAD_DIM)
            cum_row = cumt_ref[0, pl.ds(2 * hp + h, 1), :]
            for c in range(t // tq):
                cols = slice(c * tq, (c + 1) * tq)
                ka_sc[h, c] = _fox_kt(kt_ref[0, rows, cols], cum_row[:, cols])
                va_sc[h, c] = _fox_vt(vt_ref[0, rows, cols])

    off_q = pl.multiple_of(i * tq, tq)
    cq = cum_ref[0, pl.ds(off_q, tq), :]
    qs = [_fox_q(q_ref[0], cq, 2 * hp + h, h) for h in range(2)]
    m_sc[...] = jnp.full(m_sc.shape, NEG_BIG, F32)
    acc_sc[...] = jnp.zeros(acc_sc.shape, F32)
    causal = (lax.broadcasted_iota(jnp.int32, (tq, tq), 1) <= lax.broadcasted_iota(jnp.int32, (tq, tq), 0))

    def step(j, masked):
        scores = [_dot(qs[h], ka_sc[h, j]) for h in range(2)]
        for h in range(2):
            s = scores[h]
            if masked:
                s = jnp.where(causal, s, NEG_BIG)
            m_old = m_sc[h]
            m_new = jnp.maximum(m_old, jnp.max(s, axis=1, keepdims=True))
            p = jnp.concatenate([jnp.exp2(s[:, c * LANES:(c + 1) * LANES] - m_new) for c in range(tq // LANES)],
                                axis=1).astype(BF16)
            acc_sc[h] = jnp.exp2(m_old - m_new) * acc_sc[h] + _dot_nt(p, va_sc[h, j])
            m_sc[h] = m_new

    def past(j, carry):
        step(j, False)
        return carry

    lax.fori_loop(0, i, past, 0)
    step(i, True)
    second = pltpu.roll(_fox_finish(acc_sc[1]), HEAD_DIM, axis=1)
    o_ref[0] = jnp.where(_lane_half((tq, LANES)), _fox_finish(acc_sc[0]), second)


def _fox_prompt(q, kt, vt, cum, cumt, tq):
    b, t, _ = q.shape
    pairs = FOX_HEADS // 2
    nblk = t // tq
    return pl.pallas_call(
        functools.partial(_fox_prompt_body, tq=tq, t=t),
        grid=(b, pairs, nblk),
        in_specs=[pl.BlockSpec((1, tq, LANES), lambda bi, hp, i: (bi, i, hp)),
                  pl.BlockSpec((1, t, LANES), lambda bi, hp, i: (bi, 0, 0)),
                  pl.BlockSpec((1, FOX_HEADS, t), lambda bi, hp, i: (bi, 0, 0)),
                  pl.BlockSpec((1, LANES, t), lambda bi, hp, i: (bi, hp, 0)),
                  pl.BlockSpec((1, LANES, t), lambda bi, hp, i: (bi, hp, 0))],
        out_specs=pl.BlockSpec((1, tq, LANES), lambda bi, hp, i: (bi, i, hp)),
        out_shape=jax.ShapeDtypeStruct((b, t, FOX_DIM), F32),
        scratch_shapes=[pltpu.VMEM((2, nblk, LANES, tq), BF16), pltpu.VMEM((2, nblk, LANES, tq), BF16),
                        pltpu.VMEM((2, tq, LANES), F32), pltpu.VMEM((2, tq, LANES), F32)],
        compiler_params=_cparams("parallel", "parallel", "arbitrary"),
        name="fox_prompt",
    )(q, cum, cumt, kt, vt)


def _fox_sample_body(q_ref, cum_ref, cumt_ref, kt_ref, vt_ref, kn_ref, vn_ref, o_ref, *, ts, past):
    cum_n = cum_ref[0, 0:ts, :] * LOG2E
    causal = (lax.broadcasted_iota(jnp.int32, (ts, ts), 1) <= lax.broadcasted_iota(jnp.int32, (ts, ts), 0))
    lane = lax.broadcasted_iota(jnp.int32, (ts, LANES), 1)
    q = q_ref[0].astype(F32)
    kn = kn_ref[0]
    vn = vn_ref[0]
    outs = []
    for h in range(FOX_HEADS):
        cols = slice(h * HEAD_DIM, (h + 1) * HEAD_DIM)
        cq = jnp.sum(jnp.where(lane == h, cum_n, 0.0), axis=1, keepdims=True)
        ck_p = cumt_ref[0, h:h + 1, 0:past] * LOG2E
        ck_n = cumt_ref[0, h:h + 1, past:past + ts] * LOG2E
        qh = q[:, cols].astype(BF16)
        s_p = _dot(qh, kt_ref[0, cols, :].astype(BF16)) + (cq - ck_p)
        s_n = jnp.where(causal, _dot_nt(qh, kn[:, cols].astype(BF16)) + (cq - ck_n), NEG_BIG)
        m = jnp.maximum(jnp.max(s_p, axis=1, keepdims=True), jnp.max(s_n, axis=1, keepdims=True))
        p_p = jnp.exp2(s_p - m)
        p_n = jnp.exp2(s_n - m)
        l = jnp.sum(p_p, axis=1, keepdims=True) + jnp.sum(p_n, axis=1, keepdims=True)
        acc = (_dot_nt(p_p.astype(BF16), vt_ref[0, cols, :].astype(BF16))
               + _dot(p_n.astype(BF16), vn[:, cols].astype(BF16)))
        outs.append(acc / l)
    o_ref[0] = jnp.concatenate(outs, axis=1)


def _fox_sample(q, k_new, v_new, kt_past, vt_past, cum, cumt):
    b, ts, _ = q.shape
    p = kt_past.shape[2]
    length = cum.shape[1]
    assert p % CUM_TILE == 0 and ts <= CUM_TILE
    new = pl.BlockSpec((1, ts, FOX_DIM), lambda bi: (bi, 0, 0))
    old = pl.BlockSpec((1, FOX_DIM, p), lambda bi: (bi, 0, 0))
    return pl.pallas_call(
        functools.partial(_fox_sample_body, ts=ts, past=p),
        grid=(b,),
        in_specs=[new, pl.BlockSpec((1, CUM_TILE, LANES), lambda bi: (bi, p // CUM_TILE, 0)),
                  pl.BlockSpec((1, FOX_HEADS, length), lambda bi: (bi, 0, 0)), old, old, new, new],
        out_specs=new,
        out_shape=jax.ShapeDtypeStruct((b, ts, FOX_DIM), F32),
        compiler_params=_cparams("parallel"),
        name="fox_sample",
    )(q, cum, cumt, kt_past, vt_past, k_new, v_new)


def _rwkv_prep_core(h, prev_row, param_refs, out_refs):
    mu_ref, w0_ref, w2_ref, a0_ref, a2_ref, g2_ref, kk_ref, ka_ref, rk_ref = param_refs
    r_out, w_out, k_out, v_out, kap_out, bet_out, g_out, bon_out = out_refs
    tm = h.shape[0]
    first = lax.broadcasted_iota(jnp.int32, (tm, 1), 0) == 0
    prev = jnp.where(first, prev_row, pltpu.roll(h, 1, axis=0))
    hx = h + (prev - h) * mu_ref[...]
    r = hx[:, 0:RWKV_DIM]
    k = hx[:, RWKV_DIM:2 * RWKV_DIM]
    v = hx[:, 2 * RWKV_DIM:3 * RWKV_DIM]
    xwa = hx[:, 3 * RWKV_DIM:3 * RWKV_DIM + LANES]
    xg = hx[:, 3 * RWKV_DIM + LANES:]
    w_logit = w0_ref[...] + _dot(jnp.tanh(xwa).astype(BF16), w2_ref[...])
    decay = jnp.exp(-jnp.exp(_log_sigmoid(w_logit) - 0.5))
    a = jax.nn.sigmoid(a0_ref[...] + _dot(xwa.astype(BF16), a2_ref[...]))
    g = _dot(jax.nn.sigmoid(xg).astype(BF16), g2_ref[...])
    ones = _head_ones()
    kk = k * kk_ref[...]
    kk = kk / jnp.maximum(jnp.sqrt(_head_sum(kk * kk, ones)), 1e-12)
    k2 = k * (1.0 + (a - 1.0) * ka_ref[...])
    r_out[...] = r
    w_out[...] = decay
    k_out[...] = k2
    v_out[...] = v
    kap_out[...] = kk
    bet_out[...] = kk * a
    g_out[...] = g
    bon_out[...] = _head_sum(r * k2 * rk_ref[...], ones) * v


def _rwkv_prep_body(h_ref, prev_ref, *refs):
    _rwkv_prep_core(h_ref[...], prev_ref[0, 7:8, :], refs[:9], refs[9:])


def _rwkv_prep(hr, prev8, tm, params):
    rows = hr.shape[0]
    out = jax.ShapeDtypeStruct((rows, RWKV_DIM), F32)
    return pl.pallas_call(
        _rwkv_prep_body,
        grid=(rows // tm,),
        in_specs=[_row_spec(tm, RWKV_COLS), pl.BlockSpec((1, 8, RWKV_COLS), lambda i: (i, 0, 0))]
        + [_full_spec(p.shape) for p in params],
        out_specs=[_row_spec(tm, RWKV_DIM)] * 8,
        out_shape=[out] * 8,
        compiler_params=_cparams("parallel"),
        name="rwkv_prep",
    )(hr, prev8, *params)


def _to_chains_body(x_ref, o_ref, st_ref, *, nb, n_out, offsets):
    for b in range(nb):
        st_ref[b] = x_ref[b].T
    for j in range(n_out):
        groups = [st_ref[b, off * RWKV_HEADS:(off + 1) * RWKV_HEADS, :] for off in offsets(j) for b in range(nb)]
        o_ref[:, j * LANES:(j + 1) * LANES] = jnp.concatenate(groups, axis=0).T


def _to_chains_pallas(x, n_out, offsets):
    nb, t, _ = x.shape
    tt = LANES
    return pl.pallas_call(
        functools.partial(_to_chains_body, nb=nb, n_out=n_out, offsets=offsets),
        grid=(t // tt,),
        in_specs=[pl.BlockSpec((nb, tt, RWKV_DIM), lambda i: (0, i, 0))],
        out_specs=pl.BlockSpec((tt, n_out * LANES), lambda i: (i, 0)),
        out_shape=jax.ShapeDtypeStruct((t, n_out * LANES), F32),
        scratch_shapes=[pltpu.VMEM((nb, RWKV_DIM, tt), F32)],
        compiler_params=_cparams("parallel"),
        name="to_chains",
    )(x).reshape(t, n_out, LANES)


def _from_chains_body(o_ref, x_ref, st_ref, *, nb, nv, dup):
    nh = RWKV_HEADS
    for vp in range(nv):
        tile = o_ref[:, vp, :].T
        for vh in range(dup):
            ch = vh * nv + vp
            for b in range(nb):
                r0 = (vh * nb + b) * nh
                st_ref[b, ch * nh:(ch + 1) * nh, :] = tile[r0:r0 + nh, :]
    for b in range(nb):
        x_ref[b] = st_ref[b].T


def _from_chains_pallas(o, nb):
    t, nv, _ = o.shape
    dup = HEAD_DIM // nv
    tt = LANES
    return pl.pallas_call(
        functools.partial(_from_chains_body, nb=nb, nv=nv, dup=dup),
        grid=(t // tt,),
        in_specs=[pl.BlockSpec((tt, nv, LANES), lambda i: (i, 0, 0))],
        out_specs=pl.BlockSpec((nb, tt, RWKV_DIM), lambda i: (0, i, 0)),
        out_shape=jax.ShapeDtypeStruct((nb, t, RWKV_DIM), F32),
        scratch_shapes=[pltpu.VMEM((nb, RWKV_DIM, tt), F32)],
        compiler_params=_cparams("parallel"),
        name="from_chains",
    )(o)


SUBLANES = 8


def _rwkv_scan_body(w_ref, kap_ref, bet_ref, k2_ref, r_ref, v_ref, s0_ref, o_ref, s_ref, ops_ref, *, tb, nv):
    @pl.when(pl.program_id(0) == 0)
    def _():
        s_ref[...] = s0_ref[...]

    packed = w_ref.shape[1] < HEAD_DIM
    low = lax.broadcasted_iota(jnp.int32, (w_ref.shape[1], LANES), 1) < LANES // 2

    def operand(ref, t):
        x = ref[t]
        if not packed:
            return x
        swapped = pltpu.roll(x, LANES // 2, axis=1)
        return jnp.concatenate([jnp.where(low, x, swapped), jnp.where(low, swapped, x)], axis=0)

    def operands(t):
        return tuple(operand(ref, t) for ref in (w_ref, kap_ref, bet_ref, k2_ref, r_ref))

    groups = nv // SUBLANES
    lanes_of_sum = 2

    def accumulate(partials, g, k, term):
        slot = k % lanes_of_sum
        partials[g][slot] = term if partials[g][slot] is None else partials[g][slot] + term

    def total(partials, g):
        return functools.reduce(lambda a, b: a + b, partials[g])

    def token(t, current):
        following = operands(jnp.minimum(t + 1, tb - 1))
        for i, x in enumerate(current):
            ops_ref[i] = x

        def row(i, k):
            return ops_ref[i, pl.ds(k, SUBLANES, stride=0), :]

        v8 = [v_ref[t, g * SUBLANES:(g + 1) * SUBLANES, :] for g in range(groups)]
        removal = [[None] * lanes_of_sum for _ in range(groups)]
        for k in range(HEAD_DIM):
            kap_k = row(1, k)
            for g in range(groups):
                accumulate(removal, g, k, s_ref[g, k] * kap_k)
        rho = [total(removal, g) for g in range(groups)]
        out = [[None] * lanes_of_sum for _ in range(groups)]
        for k in range(HEAD_DIM):
            w_k, bet_k, k2_k, r_k = row(0, k), row(2, k), row(3, k), row(4, k)
            for g in range(groups):
                sn = s_ref[g, k] * w_k - bet_k * rho[g] + k2_k * v8[g]
                s_ref[g, k] = sn
                accumulate(out, g, k, sn * r_k)
        for g in range(groups):
            o_ref[t, g * SUBLANES:(g + 1) * SUBLANES, :] = total(out, g)
        return following

    lax.fori_loop(0, tb, token, operands(0))


def _rwkv_scan(w, kap, bet, k2, r, v, s0):
    t = w.shape[0]
    nv = v.shape[1]
    tb = min(SCAN_TOKENS, t)
    op_spec = pl.BlockSpec((tb, w.shape[1], LANES), lambda i: (i, 0, 0))
    v_spec = pl.BlockSpec((tb, nv, LANES), lambda i: (i, 0, 0))
    s_shape = (nv // SUBLANES, HEAD_DIM, SUBLANES, LANES)
    s_spec = pl.BlockSpec(s_shape, lambda i: (0, 0, 0, 0))
    return pl.pallas_call(
        functools.partial(_rwkv_scan_body, tb=tb, nv=nv),
        grid=(t // tb,),
        in_specs=[op_spec] * 5 + [v_spec, s_spec],
        out_specs=[v_spec, s_spec],
        out_shape=[jax.ShapeDtypeStruct((t, nv, LANES), F32), jax.ShapeDtypeStruct(s_shape, F32)],
        scratch_shapes=[pltpu.VMEM((5, HEAD_DIM, LANES), F32)],
        compiler_params=_cparams("arbitrary"),
        name="rwkv_scan",
    )(w, kap, bet, k2, r, v, s0)


def _outproj_even_core(x, a, o, bon, gate, lng, lnb, w_ref):
    ones = _head_ones()
    d = o - _head_sum(o, ones) * (1.0 / HEAD_DIM)
    var = _head_sum(d * d, ones) * (1.0 / HEAD_DIM)
    y = d * lax.rsqrt(var + RWKV_GN_EPS) * lng + lnb
    b = (y + bon) * gate
    return x + _dot(a.astype(BF16), w_ref[0:FOX_DIM, :]) + _dot(b.astype(BF16), w_ref[FOX_DIM:, :])


def _rope(x, cos, sin_up, sin_dn):
    return x * cos + pltpu.roll(x, ROPE_DIM // 2, axis=1) * sin_up + pltpu.roll(x, LANES - ROPE_DIM // 2, axis=1) * sin_dn


def _inproj_odd_body(x_ref, g_ref, w_ref, qn_ref, kn_ref, vn_ref, cos_ref, sup_ref, sdn_ref, *refs, fuse_sgu):
    if fuse_sgu:
        ws_ref, sb_ref, q_ref, k_ref, v_ref, d_ref = refs
    else:
        q_ref, k_ref, v_ref, u_ref, gv_ref = refs
    n = _rms(x_ref[...], g_ref[...]).astype(BF16)
    cos, sup, sdn = cos_ref[...], sup_ref[...], sdn_ref[...]
    ones64 = _group_ones(HEAD_DIM)
    qw = SWA_Q

    def head_norm(h, gain):
        return h * lax.rsqrt(_group_sum(h * h, ones64) * (1.0 / HEAD_DIM) + NORM_EPS) * gain

    def q_block(j):
        cols = slice(j * LANES, (j + 1) * LANES)

        def finish(h):
            q_ref[:, cols] = (_rope(head_norm(h, qn_ref[:, cols]), cos, sup, sdn) * ATT_SCALE).astype(BF16)
        return cols, finish

    def k_finish(h):
        k_ref[...] = _rope(head_norm(h, kn_ref[...]), cos, sup, sdn)

    def v_finish(h):
        v_ref[...] = h

    held = {}

    def u_finish(h):
        if fuse_sgu:
            held["u"] = _gelu_tanh(h)
        else:
            u_ref[...] = _gelu_tanh(h)

    def gv_finish(h):
        gv = _rms(_gelu_tanh(h), vn_ref[...])
        if fuse_sgu:
            _sgu_core(held["u"], gv, ws_ref, sb_ref, d_ref, sb_ref.shape[0])
        else:
            gv_ref[...] = gv

    s0 = qw + 2 * SWA_KV
    stages = [q_block(j) for j in range(SWA_Q // LANES)] + [
        (slice(qw, qw + SWA_KV), k_finish), (slice(qw + SWA_KV, s0), v_finish),
        (slice(s0, s0 + SGU_DIM), u_finish), (slice(s0 + SGU_DIM, s0 + 2 * SGU_DIM), gv_finish)]
    ahead = _dot(n, w_ref[:, stages[0][0]])
    for idx, (_, finish) in enumerate(stages):
        h = ahead
        if idx + 1 < len(stages):
            ahead = _dot(n, w_ref[:, stages[idx + 1][0]])
        finish(h)


def _inproj_odd(x, g, w, qn, kn, vn, cos, sup, sdn, table_blocks, sgu=None):
    rows = x.shape[0]
    tm = min(ROW_TILE, rows)
    tab = pl.BlockSpec((tm, LANES), lambda i: (i % table_blocks, 0))
    wide = (jax.ShapeDtypeStruct((rows, SGU_DIM), F32), _row_spec(tm, SGU_DIM))
    narrow = (jax.ShapeDtypeStruct((rows, SWA_KV), F32), _row_spec(tm, SWA_KV))
    outs = [(jax.ShapeDtypeStruct((rows, SWA_Q), BF16), _row_spec(tm, SWA_Q)), narrow, narrow, wide]
    outs += [] if sgu else [wide]
    extra = list(sgu) if sgu else []
    return pl.pallas_call(
        functools.partial(_inproj_odd_body, fuse_sgu=sgu is not None),
        grid=(rows // tm,),
        in_specs=[_row_spec(tm, D_MODEL), _full_spec(g.shape), _full_spec(w.shape), _full_spec(qn.shape),
                  _full_spec(kn.shape), _full_spec(vn.shape), tab, tab, tab] + [_full_spec(a.shape) for a in extra],
        out_specs=[spec for _, spec in outs],
        out_shape=[shape for shape, _ in outs],
        compiler_params=_cparams("parallel"),
        name="inproj_odd",
    )(x, g, w, qn, kn, vn, cos, sup, sdn, *extra)


def _sink_attend(scores, vb, sink, visible):
    s = jnp.where(visible, scores, NEG_BIG)
    m = jnp.maximum(jnp.max(s, axis=1, keepdims=True), sink)
    p = jnp.exp(s - m)
    l = jnp.sum(p, axis=1, keepdims=True) + jnp.exp(sink - m)
    return _dot(p.astype(BF16), vb) / l


def _swa_body(sink_ref, q_ref, kp_ref, kc_ref, vp_ref, vc_ref, mask_ref, o_ref):
    tq = q_ref.shape[1]
    kw = jnp.concatenate([kp_ref[0], kc_ref[0]], axis=0)
    vw = jnp.concatenate([vp_ref[0], vc_ref[0]], axis=0)
    nk = kw.shape[0]
    kw_sw = pltpu.roll(kw, HEAD_DIM, axis=1)
    vw_sw = pltpu.roll(vw, HEAD_DIM, axis=1)
    key_half = _lane_half((nk, LANES))
    ks = [jnp.where(key_half, a, b).astype(BF16) for a, b in ((kw, kw_sw), (kw_sw, kw))]
    vs = [jnp.where(key_half, a, b).astype(BF16) for a, b in ((vw, vw_sw), (vw_sw, vw))]
    visible = jnp.concatenate([mask_ref[0] > 0.5] * 2, axis=0)
    half = _lane_half((tq, LANES))
    first = lax.broadcasted_iota(jnp.int32, (2 * tq, 1), 0) < tq
    n_blocks = SWA_Q // LANES
    scores = []
    for j in range(n_blocks):
        qb = q_ref[0, :, j * LANES:(j + 1) * LANES]
        zero = jnp.zeros_like(qb)
        stacked = jnp.concatenate([jnp.where(half, qb, zero), jnp.where(half, zero, qb)], axis=0)
        scores.append(_dot_nt(stacked, ks[(2 * j) // SWA_GROUP]))
    for j in range(n_blocks):
        sink = jnp.where(first, sink_ref[2 * j], sink_ref[2 * j + 1])
        o = _sink_attend(scores[j], vs[(2 * j) // SWA_GROUP], sink, visible)
        o_ref[0, :, j * LANES:(j + 1) * LANES] = jnp.where(half, o[0:tq], o[tq:]).astype(BF16)


def _swa(sinks, q, k_prev, k_cur, v_prev, v_cur, mask, tq, prev_rows, prev_map, mask_map):
    b, t, qw = q.shape
    nk = prev_rows + tq
    cur = lambda bi, i: (bi, i, 0)
    return pl.pallas_call(
        _swa_body,
        grid=(b, t // tq),
        in_specs=[pl.BlockSpec(memory_space=pltpu.SMEM),
                  pl.BlockSpec((1, tq, qw), cur),
                  pl.BlockSpec((1, prev_rows, SWA_KV), prev_map), pl.BlockSpec((1, tq, SWA_KV), cur),
                  pl.BlockSpec((1, prev_rows, SWA_KV), prev_map), pl.BlockSpec((1, tq, SWA_KV), cur),
                  pl.BlockSpec((1, tq, nk), mask_map)],
        out_specs=pl.BlockSpec((1, tq, qw), cur),
        out_shape=jax.ShapeDtypeStruct((b, t, qw), BF16),
        compiler_params=_cparams("parallel", "parallel"),
        name="swa",
    )(sinks, q, k_prev, k_cur, v_prev, v_cur, mask)


def _sgu_core(u, v, w_ref, b_ref, o_ref, length):
    tril = (lax.broadcasted_iota(jnp.int32, (length, length), 1)
            <= lax.broadcasted_iota(jnp.int32, (length, length), 0))
    ws = [jnp.where(tril, w_ref[g], 0.0).astype(BF16) for g in range(SGU_GROUPS)]
    half = _lane_half((length, LANES))
    for c in range(u.shape[0] // length):
        rows = slice(c * length, (c + 1) * length)
        for j in range(SGU_GROUPS // 2):
            cols = slice(j * LANES, (j + 1) * LANES)
            vb = v[rows, cols].astype(BF16)
            mixed = jnp.where(half, _dot(ws[2 * j], vb), _dot(ws[2 * j + 1], vb)) + b_ref[:, cols]
            o_ref[rows, cols] = u[rows, cols] * mixed


def _sgu_body(u_ref, v_ref, w_ref, b_ref, o_ref, *, length):
    _sgu_core(u_ref[...], v_ref[...], w_ref, b_ref, o_ref, length)


def _sgu(u, v, w, bias, length, n_chunks):
    rows = u.shape[0]
    tm = length * n_chunks
    return pl.pallas_call(
        functools.partial(_sgu_body, length=length),
        grid=(rows // tm,),
        in_specs=[_row_spec(tm, SGU_DIM), _row_spec(tm, SGU_DIM), _full_spec(w.shape), _full_spec(bias.shape)],
        out_specs=_row_spec(tm, SGU_DIM),
        out_shape=jax.ShapeDtypeStruct(u.shape, F32),
        compiler_params=_cparams("parallel"),
        name="sgu",
    )(u, v, w, bias)


def _outproj_odd_core(x, c, d, w_ref):
    return x + _dot(c, w_ref[0:SWA_Q, :]) + _dot(d.astype(BF16), w_ref[SWA_Q:, :])


def _memkv_body(m_ref, g_ref, w_ref, kn_ref, k_ref, v_ref):
    n = _rms(m_ref[0], g_ref[...]).astype(BF16)
    ones = _group_ones(HEAD_DIM)
    hk = _dot(n, w_ref[:, 0:MEM_DIM])
    k = hk * lax.rsqrt(_group_sum(hk * hk, ones) * (1.0 / HEAD_DIM) + NORM_EPS) * kn_ref[...]
    k_ref[0] = k.T
    v_ref[0] = _dot(n, w_ref[:, MEM_DIM:]).T


def _memkv(mem, g, w, kn):
    b, m, _ = mem.shape
    out = jax.ShapeDtypeStruct((b, MEM_DIM, m), F32)
    spec = pl.BlockSpec((1, MEM_DIM, m), lambda i: (i, 0, 0))
    return pl.pallas_call(
        _memkv_body,
        grid=(b,),
        in_specs=[pl.BlockSpec((1, m, D_MODEL), lambda i: (i, 0, 0)), _full_spec(g.shape), _full_spec(w.shape),
                  _full_spec(kn.shape)],
        out_specs=[spec, spec],
        out_shape=[out, out],
        compiler_params=_cparams("parallel"),
        name="memkv",
    )(mem, g, w, kn)


def _xattn_core(x, g, wq_ref, qn, mk_ref, mv_ref, wo_ref):
    n_seq = mk_ref.shape[0]
    tq = x.shape[0] // n_seq
    n = _rms(x, g).astype(BF16)
    hq = _dot(n, wq_ref[...])
    ones = _group_ones(HEAD_DIM)
    q = (hq * lax.rsqrt(_group_sum(hq * hq, ones) * (1.0 / HEAD_DIM) + NORM_EPS) * qn * ATT_SCALE).astype(BF16)
    half = _lane_half((tq, LANES))
    scores, values = [], []
    for sq in range(n_seq):
        for j in range(MEM_HEADS // 2):
            cols = slice(j * LANES, (j + 1) * LANES)
            kb = mk_ref[sq, cols, :].astype(BF16)
            values.append(mv_ref[sq, cols, :].astype(BF16))
            qb = q[sq * tq:(sq + 1) * tq, cols]
            zero = jnp.zeros_like(qb)
            heads = [jnp.where(half, qb, zero), jnp.where(half, zero, qb)]
            scores.append([_dot(qh, kb) for qh in ([jnp.concatenate(heads, axis=0)] if tq <= LANES else heads)])
    blocks = []
    for per_block, vb in zip(scores, values):
        outs = []
        for s in per_block:
            p = jnp.exp(s - jnp.max(s, axis=1, keepdims=True))
            outs.append(_dot_nt(p.astype(BF16), vb) / jnp.sum(p, axis=1, keepdims=True))
        if len(outs) == 1:
            outs = [outs[0][0:tq], outs[0][tq:]]
        blocks.append(jnp.where(half, outs[0], outs[1]))
    per_seq = [jnp.concatenate(blocks[i:i + MEM_HEADS // 2], axis=1) for i in range(0, len(blocks), MEM_HEADS // 2)]
    o = (per_seq[0] if n_seq == 1 else jnp.concatenate(per_seq, axis=0)).astype(BF16)
    return x + _dot(o, wo_ref[...])


def _tail_body(*refs, even):
    n_mix = 8 if even else 4
    mix, rest = refs[:n_mix], refs[n_mix:]
    gx_ref, wq_ref, qn_ref, mk_ref, mv_ref, wo_ref, gf_ref, wg_ref, wu_ref, wd_ref, out_ref, acc_ref = rest
    if even:
        x_ref, a_ref, o_ref, bon_ref, gate_ref, lng_ref, lnb_ref, w_ref = mix
        x = _outproj_even_core(x_ref[0], a_ref[0], o_ref[0], bon_ref[0], gate_ref[0], lng_ref[...], lnb_ref[...], w_ref)
    else:
        x_ref, c_ref, d_ref, w_ref = mix
        x = _outproj_odd_core(x_ref[0], c_ref[0], d_ref[0], w_ref)
    x = _xattn_core(x, gx_ref[...], wq_ref, qn_ref[...], mk_ref, mv_ref, wo_ref)
    out_ref[0] = _ffn_core(x, gf_ref[...], wg_ref, wu_ref, wd_ref, acc_ref)


def _tail(even, layer, x, mix_rows, mix_params, xattn_params, mk, mv, wo, ffn_params, tm, seqs_per_tile=1):
    b, t, _ = x.shape
    m = mk.shape[2]
    assert seqs_per_tile == 1 or b == 1
    rows3 = lambda c: pl.BlockSpec((1, tm, c), lambda bi, i: (bi, i, 0))
    mem = pl.BlockSpec((seqs_per_tile, MEM_DIM, m),
                       (lambda bi, i: (bi, 0, 0)) if seqs_per_tile == 1 else (lambda bi, i: (i, 0, 0)))
    args = [x, *mix_rows, *mix_params, *xattn_params, mk, mv, wo, *ffn_params]
    in_specs = ([rows3(a.shape[2]) for a in (x, *mix_rows)] + [_full_spec(p.shape) for p in mix_params]
                + [_full_spec(p.shape) for p in xattn_params] + [mem, mem, _full_spec(wo.shape)]
                + [_full_spec(ffn_params[0].shape)] + [_layer_spec(w, layer) for w in ffn_params[1:]])
    return pl.pallas_call(
        functools.partial(_tail_body, even=even),
        grid=(b, t // tm),
        in_specs=in_specs,
        out_specs=rows3(D_MODEL),
        out_shape=jax.ShapeDtypeStruct(x.shape, F32),
        scratch_shapes=[pltpu.VMEM((tm, D_MODEL), F32)],
        compiler_params=_cparams("parallel", "parallel"),
        name="tail_even" if even else "tail_odd",
    )(*args)


def _head_minor(x, axis=-1):
    axis %= x.ndim
    shape = x.shape
    y = x.reshape(shape[:axis] + (RWKV_HEADS, HEAD_DIM) + shape[axis + 1:])
    return jnp.swapaxes(y, axis, axis + 1).reshape(shape)


def _head_major(x, axis=-1):
    axis %= x.ndim
    shape = x.shape
    y = x.reshape(shape[:axis] + (HEAD_DIM, RWKV_HEADS) + shape[axis + 1:])
    return jnp.swapaxes(y, axis, axis + 1).reshape(shape)


def _cols_head_minor(x, inverse=False):
    n = 3 * RWKV_DIM
    blocks = x[..., :n].reshape(x.shape[:-1] + (3, RWKV_DIM))
    blocks = (_head_major if inverse else _head_minor)(blocks)
    return jnp.concatenate([blocks.reshape(x.shape[:-1] + (n,)), x[..., n:]], axis=-1)


def _to_chains(x, b, t, dup):
    y = x.reshape(b, t, HEAD_DIM, RWKV_HEADS).transpose(1, 2, 0, 3).reshape(t, HEAD_DIM, b * RWKV_HEADS)
    return jnp.concatenate([y] * dup, axis=-1) if dup > 1 else y


def _rope_tables(pos):
    half = ROPE_DIM // 2
    inv_freq = jnp.power(ROPE_THETA, -jnp.arange(half, dtype=F32) / half)
    ang = pos.astype(F32)[:, None] * inv_freq[None, :]
    cos, sin = jnp.cos(ang), jnp.sin(ang)
    n = pos.shape[0]
    pad = jnp.zeros((n, HEAD_DIM - ROPE_DIM), F32)
    zero = jnp.zeros((n, half), F32)
    cos_t = jnp.concatenate([cos, cos, pad + 1.0], axis=1)
    up_t = jnp.concatenate([zero, sin, pad], axis=1)
    dn_t = jnp.concatenate([-sin, zero, pad], axis=1)
    two = lambda a: jnp.concatenate([a, a], axis=1)
    return two(cos_t), two(up_t), two(dn_t)


def _swa_prompt_mask(tq):
    span = WINDOW + tq
    qc = np.arange(tq)[:, None] // CHUNK
    kc = np.arange(span)[None, :] // CHUNK - WINDOW_CHUNKS
    band = (kc <= qc) & (kc >= qc - WINDOW_CHUNKS)
    first = band & (kc >= 0)
    return jnp.asarray(np.stack([first, band]).astype(np.float32))


def _swa_sample_mask(past, rows, t):
    kc = (past - rows + np.arange(rows + t)) // CHUNK
    qc = (past + np.arange(t)) // CHUNK
    m = (kc[None, :] <= qc[:, None]) & (kc[None, :] >= qc[:, None] - WINDOW_CHUNKS)
    return jnp.asarray(m[None].astype(np.float32))


def kernel(x_prompt, x_sample, cache_fox_k, cache_fox_v, cache_fox_logf, state_rwkv, state_rwkv_shift, cache_swa_k, cache_swa_v, cache_mem_k, cache_mem_v, mem_prompt, ffn1_norm, ffn1_w_gate, ffn1_w_up, ffn1_w_down, mix_norm, ev_w_in, fox_b_f, fox_q_norm, fox_k_norm, rwkv_mu, rwkv_w0, rwkv_w2, rwkv_a0, rwkv_a2, rwkv_g2, rwkv_k_k, rwkv_k_a, rwkv_r_k, rwkv_ln_g, rwkv_ln_b, ev_w_out, od_w_in, swa_q_norm, swa_k_norm, swa_sinks, sgu_v_norm, sgu_w_s, sgu_b, od_w_out, xattn_norm, mem_norm, xattn_wq, xattn_wkv, xattn_q_norm, xattn_k_norm, xattn_wo, ffn2_norm, ffn2_w_gate, ffn2_w_up, ffn2_w_down):
    bp, tp, _ = x_prompt.shape
    bs, ts, _ = x_sample.shape
    depth = ffn1_norm.shape[0]
    past = cache_fox_k.shape[2]
    mem_tokens = mem_prompt.shape[1]
    xp = x_prompt.reshape(bp * tp, D_MODEL)
    xs = x_sample.reshape(bs * ts, D_MODEL)
    row = lambda a: a.reshape(1, -1)
    tile_heads = lambda a, n: jnp.tile(a, n).reshape(1, -1)
    ffn1_w = (ffn1_w_gate.astype(BF16), ffn1_w_up.astype(BF16), ffn1_w_down.astype(BF16))
    ffn2_w = (ffn2_w_gate.astype(BF16), ffn2_w_up.astype(BF16), ffn2_w_down.astype(BF16))

    out = {k: [] for k in ("p_fox_k", "p_fox_v", "p_fox_logf", "p_rwkv_state", "p_rwkv_shift", "p_swa_k", "p_swa_v",
                           "p_mem_k", "p_mem_v", "s_fox_k", "s_fox_v", "s_fox_logf", "s_rwkv_state", "s_rwkv_shift",
                           "s_swa_k", "s_swa_v", "s_sgu_v")}

    for l in range(depth):
        f1 = (row(ffn1_norm[l]),) + ffn1_w
        xp = _ffn(xp, l, *f1)
        xs = _ffn(xs, l, *f1)
        g_mix = row(mix_norm[l])
        if l % 2 == 0:
            e = l // 2
            w_in = ev_w_in[e]
            f0 = 3 * FOX_DIM
            w_cat = jnp.concatenate([w_in[:, :f0], jnp.pad(w_in[:, f0:f0 + FOX_HEADS], ((0, 0), (0, LANES - FOX_HEADS))),
                                     _cols_head_minor(w_in[:, f0 + FOX_HEADS:])], axis=1).astype(BF16)
            bf = jnp.pad(fox_b_f[e], (0, LANES - FOX_HEADS)).reshape(1, LANES)
            qn = tile_heads(fox_q_norm[e], FOX_HEADS)
            kn = tile_heads(fox_k_norm[e], FOX_HEADS)
            hm = _head_minor
            w2p = jnp.pad(hm(rwkv_w2[e]), ((0, LANES - DECAY_LORA), (0, 0))).astype(BF16)
            a2p = jnp.pad(hm(rwkv_a2[e]), ((DECAY_LORA, 0), (0, 0))).astype(BF16)
            params = (row(_cols_head_minor(rwkv_mu[e])), row(hm(rwkv_w0[e])), w2p, row(hm(rwkv_a0[e])), a2p,
                      hm(rwkv_g2[e]).astype(BF16), row(hm(rwkv_k_k[e])), row(hm(rwkv_k_a[e])),
                      row(hm(rwkv_r_k[e].reshape(-1))))
            qp, ktp, vtp, lfp, last_p, *prep_p = _inproj_even(xp, g_mix, w_cat, bf, qn, kn, seq_len=tp,
                                                              prep_params=params)
            qs, ks, vs, lfs, hrs = _inproj_even(xs, g_mix, w_cat, bf, qn, kn)

            cum_p, cumt_p = _cumsum(lfp.reshape(bp, tp, LANES), CUM_TILE)
            a_p = _fox_prompt(qp.reshape(bp, tp, FOX_DIM), ktp, vtp, cum_p, cumt_p, ATT_TILE)
            lfs8 = lfs.reshape(bs, ts, LANES)[:, :, :FOX_HEADS]
            tot = past + ts
            padded = -(-tot // CUM_TILE) * CUM_TILE
            lf_all = jnp.concatenate([cache_fox_logf[e].astype(F32), lfs8], axis=1)
            lf_all = jnp.pad(lf_all, ((0, 0), (0, padded - tot), (0, LANES - FOX_HEADS)))
            cum_s, cumt_s = _cumsum(lf_all, CUM_TILE)
            token_minor = lambda c: jnp.transpose(c, (0, 2, 3, 1)).reshape(bs, FOX_DIM, past)
            a_s = _fox_sample(qs.reshape(bs, ts, FOX_DIM), ks.reshape(bs, ts, FOX_DIM), vs.reshape(bs, ts, FOX_DIM),
                              token_minor(cache_fox_k[e]), token_minor(cache_fox_v[e]), cum_s, cumt_s)

            prev_s = jnp.pad(_cols_head_minor(state_rwkv_shift[e].astype(F32)), ((0, 0), (7, 0), (0, 0)))
            prep_s = _rwkv_prep(hrs, prev_s, ts, params)

            def scan(prep, b, t, state0):
                r, w, k2, v, kap, bet, g, bon = prep
                dup = LANES // (b * RWKV_HEADS)
                nv = HEAD_DIM // dup
                in_kernel = t % LANES == 0
                if in_kernel:
                    halves = lambda j: tuple(half * nv + j for half in range(dup))
                    ops = [_to_chains_pallas(a.reshape(b, t, RWKV_DIM), nv, halves) for a in (w, kap, bet, k2, r)]
                    vt = _to_chains_pallas(v.reshape(b, t, RWKV_DIM), nv, halves)
                else:
                    ops = [_to_chains(a, b, t, dup) for a in (w, kap, bet, k2, r)]
                    vt = v.reshape(b, t, dup, nv, RWKV_HEADS).transpose(1, 3, 2, 0, 4).reshape(t, nv, LANES)
                s0 = state0.reshape(b, RWKV_HEADS, dup, nv // SUBLANES, SUBLANES, HEAD_DIM)
                s0 = s0.transpose(3, 5, 4, 2, 0, 1).reshape(nv // SUBLANES, HEAD_DIM, SUBLANES, LANES)
                o, sT = _rwkv_scan(*ops, vt, s0)
                if in_kernel:
                    o = _from_chains_pallas(o, b).reshape(b * t, RWKV_DIM)
                else:
                    o = o.reshape(t, nv, dup, b, RWKV_HEADS).transpose(3, 0, 2, 1, 4).reshape(b * t, RWKV_DIM)
                sT = sT.reshape(nv // SUBLANES, HEAD_DIM, SUBLANES, dup, b, RWKV_HEADS)
                sT = sT.transpose(4, 5, 3, 0, 2, 1).reshape(b, RWKV_HEADS, HEAD_DIM, HEAD_DIM)
                return o, sT, g, bon

            o_p, st_p, g_p, bon_p = scan(prep_p, bp, tp, jnp.zeros((bp, RWKV_HEADS, HEAD_DIM, HEAD_DIM), F32))
            o_s, st_s, g_s, bon_s = scan(prep_s, bs, ts, state_rwkv[e].astype(F32))

            w_out = jnp.concatenate([ev_w_out[e][:FOX_DIM], hm(ev_w_out[e][FOX_DIM:], axis=0)], axis=0).astype(BF16)
            lng, lnb = row(hm(rwkv_ln_g[e])), row(hm(rwkv_ln_b[e]))
            seq = lambda a: a.reshape(bp, tp, -1)
            mix_p = (True, (a_p, seq(o_p), seq(bon_p), seq(g_p)), (lng, lnb, w_out))
            flat = lambda a: a.reshape(1, bs * ts, -1)
            mix_s = (True, (flat(a_s), flat(o_s), flat(bon_s), flat(g_s)), (lng, lnb, w_out))

            rows_of = lambda a: jnp.transpose(a.reshape(bp, FOX_HEADS, HEAD_DIM, tp), (0, 3, 1, 2))
            out["p_fox_k"].append(rows_of(ktp))
            out["p_fox_v"].append(rows_of(vtp))
            out["p_fox_logf"].append(lfp.reshape(bp, tp, LANES)[:, :, :FOX_HEADS])
            out["p_rwkv_state"].append(st_p)
            final_rows = last_p.reshape(bp, tp // ROW_TILE, SUBLANES, RWKV_COLS)[:, -1, SUBLANES - 1:, :]
            out["p_rwkv_shift"].append(_cols_head_minor(final_rows, inverse=True))
            out["s_fox_k"].append(ks.reshape(bs, ts, FOX_HEADS, HEAD_DIM))
            out["s_fox_v"].append(vs.reshape(bs, ts, FOX_HEADS, HEAD_DIM))
            out["s_fox_logf"].append(lfs8)
            out["s_rwkv_state"].append(st_s)
            out["s_rwkv_shift"].append(_cols_head_minor(hrs.reshape(bs, ts, RWKV_COLS)[:, -1:], inverse=True))
        else:
            j = l // 2
            w_cat = od_w_in[j].astype(BF16)
            qn = tile_heads(swa_q_norm[j], SWA_HEADS)
            kn = tile_heads(swa_k_norm[j], SWA_KV_HEADS)
            vn = row(sgu_v_norm[j])
            tabs_p = _rope_tables(jnp.arange(tp))
            tabs_s = _rope_tables(past + jnp.arange(bs * ts) % ts)
            bias = jnp.repeat(jnp.transpose(sgu_b[j]), HEAD_DIM, axis=1)
            qp, kp, vp, d_p = _inproj_odd(xp, g_mix, w_cat, qn, kn, vn, *tabs_p, tp // ROW_TILE,
                                          sgu=(sgu_w_s[j], bias))
            qs, ks, vs, us, gs = _inproj_odd(xs, g_mix, w_cat, qn, kn, vn, *tabs_s, 1)

            qw = SWA_Q
            kp3, vp3 = kp.reshape(bp, tp, SWA_KV), vp.reshape(bp, tp, SWA_KV)
            ratio = SWA_TILE // WINDOW
            c_p = _swa(swa_sinks[j], qp.reshape(bp, tp, qw), kp3, kp3, vp3, vp3, _swa_prompt_mask(SWA_TILE),
                       SWA_TILE, WINDOW, lambda bi, i: (bi, jnp.maximum(i * ratio - 1, 0), 0),
                       lambda bi, i: (jnp.minimum(i, 1), 0, 0))
            rows_c = cache_swa_k.shape[2]
            ck3 = cache_swa_k[j].reshape(bs, rows_c, SWA_KV)
            cv3 = cache_swa_v[j].reshape(bs, rows_c, SWA_KV)
            ks3, vs3 = ks.reshape(bs, ts, SWA_KV), vs.reshape(bs, ts, SWA_KV)
            c_s = _swa(swa_sinks[j], qs.reshape(bs, ts, qw), ck3, ks3, cv3, vs3, _swa_sample_mask(past, rows_c, ts),
                       ts, rows_c, lambda bi, i: (bi, 0, 0), lambda bi, i: (0, 0, 0))

            d_s = _sgu(us, gs, sgu_w_s[j][:, :ts, :ts], bias[:ts], ts, 1)

            w_out = od_w_out[j].astype(BF16)
            mix_p = (False, (c_p, d_p.reshape(bp, tp, SGU_DIM)), (w_out,))
            mix_s = (False, (c_s.reshape(1, bs * ts, qw), d_s.reshape(1, bs * ts, SGU_DIM)), (w_out,))

            out["p_swa_k"].append(kp3[:, -WINDOW:].reshape(bp, WINDOW, SWA_KV_HEADS, HEAD_DIM))
            out["p_swa_v"].append(vp3[:, -WINDOW:].reshape(bp, WINDOW, SWA_KV_HEADS, HEAD_DIM))
            out["s_swa_k"].append(jnp.concatenate([ck3, ks3], axis=1)[:, -rows_c:].reshape(bs, rows_c, SWA_KV_HEADS, HEAD_DIM))
            out["s_swa_v"].append(jnp.concatenate([cv3, vs3], axis=1)[:, -rows_c:].reshape(bs, rows_c, SWA_KV_HEADS, HEAD_DIM))
            out["s_sgu_v"].append(gs.reshape(bs, ts, SGU_DIM))

        mk, mv = _memkv(mem_prompt, row(mem_norm[l]), xattn_wkv[l].astype(BF16), tile_heads(xattn_k_norm[l], MEM_HEADS))
        xa = (row(xattn_norm[l]), xattn_wq[l].astype(BF16), tile_heads(xattn_q_norm[l], MEM_HEADS))
        wo = xattn_wo[l].astype(BF16)
        f2 = (row(ffn2_norm[l]),) + ffn2_w
        xp = _tail(mix_p[0], l, xp.reshape(bp, tp, D_MODEL), mix_p[1], mix_p[2], xa, mk, mv, wo, f2,
                   ROW_TILE).reshape(bp * tp, D_MODEL)
        mem_minor = lambda c: jnp.transpose(c, (0, 2, 3, 1)).reshape(bs, MEM_DIM, mem_tokens)
        tile_s = min(ROW_TILE, bs * ts)
        xs = _tail(mix_s[0], l, xs.reshape(1, bs * ts, D_MODEL), mix_s[1], mix_s[2], xa, mem_minor(cache_mem_k[l]),
                   mem_minor(cache_mem_v[l]), wo, f2, tile_s, seqs_per_tile=tile_s // ts).reshape(bs * ts, D_MODEL)
        mem_rows_of = lambda a: jnp.transpose(a.reshape(bp, MEM_HEADS, HEAD_DIM, mem_tokens), (0, 3, 1, 2))
        out["p_mem_k"].append(mem_rows_of(mk))
        out["p_mem_v"].append(mem_rows_of(mv))

    order = ("p_fox_k", "p_fox_v", "p_fox_logf", "p_rwkv_state", "p_rwkv_shift", "p_swa_k", "p_swa_v", "p_mem_k",
             "p_mem_v", "s_fox_k", "s_fox_v", "s_fox_logf", "s_rwkv_state", "s_rwkv_shift", "s_swa_k", "s_swa_v",
             "s_sgu_v")
    return (xp.reshape(bp, tp, D_MODEL), xs.reshape(bs, ts, D_MODEL)) + tuple(jnp.stack(out[k]) for k in order)
```

```python
import functools

import numpy as np
import jax
import jax.numpy as jnp
from jax import lax
from jax.experimental import pallas as pl
from jax.experimental.pallas import tpu as pltpu

F32 = jnp.float32
BF16 = jnp.bfloat16

D_MODEL = 1024
HEAD_DIM = 64
NORM_EPS = 1e-6
ROPE_THETA = 500000.0
ROPE_DIM = HEAD_DIM // 4
CHUNK = 64
FOX_HEADS = 8
FOX_DIM = FOX_HEADS * HEAD_DIM
RWKV_HEADS = 8
RWKV_DIM = RWKV_HEADS * HEAD_DIM
DECAY_LORA = 64
ICLR_LORA = 64
GATE_LORA = 128
RWKV_COLS = 3 * RWKV_DIM + DECAY_LORA + ICLR_LORA + GATE_LORA
RWKV_GN_EPS = 64e-5
SWA_HEADS = 8
SWA_KV_HEADS = 2
SWA_GROUP = SWA_HEADS // SWA_KV_HEADS
SWA_Q = SWA_HEADS * HEAD_DIM
SWA_KV = SWA_KV_HEADS * HEAD_DIM
WINDOW = 128
WINDOW_CHUNKS = WINDOW // CHUNK
SGU_GROUPS = 8
SGU_DIM = SGU_GROUPS * HEAD_DIM
SGU_CHUNK = 128
MEM_HEADS = 4
MEM_DIM = MEM_HEADS * HEAD_DIM
D_FF = 2816

LANES = 128
ROW_TILE = 512
FF_TILE = 256
ATT_TILE = 512
CUM_TILE = 256
SWA_TILE = 256
SCAN_TOKENS = 64
VMEM_LIMIT = 56 * 1024 * 1024
ATT_SCALE = HEAD_DIM ** -0.5
LOG2E = 1.4426950408889634
NEG_BIG = -1e30


def _cparams(*sem):
    return pltpu.CompilerParams(dimension_semantics=sem, vmem_limit_bytes=VMEM_LIMIT)


def _dot(a, b):
    return jnp.dot(a, b, preferred_element_type=F32)


def _dot_nt(a, b):
    return lax.dot_general(a, b, (((1,), (1,)), ((), ())), preferred_element_type=F32)


def _rms(x, g):
    ms = jnp.mean(x * x, axis=-1, keepdims=True)
    return (x * lax.rsqrt(ms + NORM_EPS)) * g


def _group_ones(group):
    shift = int(np.log2(group))
    r = lax.broadcasted_iota(jnp.int32, (LANES, LANES), 0) >> shift
    c = lax.broadcasted_iota(jnp.int32, (LANES, LANES), 1) >> shift
    return jnp.where(r == c, 1.0, 0.0).astype(BF16)


def _group_sum(x, ones):
    parts = []
    for j in range(x.shape[1] // LANES):
        blk = x[:, j * LANES:(j + 1) * LANES]
        hi = blk.astype(BF16)
        lo = (blk - hi.astype(F32)).astype(BF16)
        parts.append(_dot(hi, ones) + _dot(lo, ones))
    return parts[0] if len(parts) == 1 else jnp.concatenate(parts, axis=1)


def _head_ones():
    r = lax.broadcasted_iota(jnp.int32, (LANES, LANES), 0) & (RWKV_HEADS - 1)
    c = lax.broadcasted_iota(jnp.int32, (LANES, LANES), 1) & (RWKV_HEADS - 1)
    return jnp.where(r == c, 1.0, 0.0).astype(BF16)


def _head_sum(x, ones):
    part = x[:, 0:LANES]
    for j in range(1, x.shape[1] // LANES):
        part = part + x[:, j * LANES:(j + 1) * LANES]
    hi = part.astype(BF16)
    lo = (part - hi.astype(F32)).astype(BF16)
    tot = _dot(hi, ones) + _dot(lo, ones)
    return jnp.concatenate([tot] * (x.shape[1] // LANES), axis=1)


def _log_sigmoid(z):
    return jnp.minimum(z, 0.0) - jnp.log(1.0 + jnp.exp(-jnp.abs(z)))


def _gelu_tanh(x):
    return 0.5 * x * (1.0 + jnp.tanh(0.7978845608028654 * (x + 0.044715 * (x * x * x))))


def _lane_half(shape):
    return lax.broadcasted_iota(jnp.int32, shape, 1) < HEAD_DIM


def _row_spec(tm, cols):
    return pl.BlockSpec((tm, cols), lambda i: (i, 0))


def _full_spec(shape):
    nd = len(shape)
    return pl.BlockSpec(shape, lambda *_: (0,) * nd, pipeline_mode=pl.Buffered(1))


def _ffn_core(x, g, wg_ref, wu_ref, wd_ref, acc_ref):
    n = _rms(x, g).astype(BF16)
    for c in range(wg_ref.shape[2] // FF_TILE):
        cols = slice(c * FF_TILE, (c + 1) * FF_TILE)
        gate = _dot(n, wg_ref[0, :, cols])
        up = _dot(n, wu_ref[0, :, cols])
        act = (gate * jax.nn.sigmoid(gate) * up).astype(BF16)
        part = _dot(act, wd_ref[0, cols, :])
        if c == 0:
            acc_ref[...] = part
        else:
            acc_ref[...] += part
    return x + 0.5 * acc_ref[...]


def _ffn_body(x_ref, g_ref, wg_ref, wu_ref, wd_ref, o_ref, acc_ref):
    o_ref[...] = _ffn_core(x_ref[...], g_ref[...], wg_ref, wu_ref, wd_ref, acc_ref)


def _layer_spec(w, layer):
    nd = w.ndim
    return pl.BlockSpec((1,) + w.shape[1:], lambda *_: (layer,) + (0,) * (nd - 1), pipeline_mode=pl.Buffered(1))


def _ffn(x, layer, g, wg, wu, wd):
    rows = x.shape[0]
    tm = min(ROW_TILE, rows)
    return pl.pallas_call(
        _ffn_body,
        grid=(rows // tm,),
        in_specs=[_row_spec(tm, D_MODEL), _full_spec(g.shape), _layer_spec(wg, layer),
                  _layer_spec(wu, layer), _layer_spec(wd, layer)],
        out_specs=_row_spec(tm, D_MODEL),
        out_shape=jax.ShapeDtypeStruct(x.shape, F32),
        scratch_shapes=[pltpu.VMEM((tm, D_MODEL), F32)],
        compiler_params=_cparams("parallel"),
        name="ffn",
    )(x, g, wg, wu, wd)


def _inproj_even_body(x_ref, g_ref, w_ref, bf_ref, qn_ref, kn_ref, *refs, per_seq):
    if per_seq is None:
        q_ref, k_ref, v_ref, lf_ref, hr_ref = refs
    else:
        prep_params, (q_ref, k_ref, v_ref, lf_ref, last_ref), prep_outs, prev_ref = (
            refs[:9], refs[9:14], refs[14:22], refs[22])
    token_minor = per_seq is not None
    n = _rms(x_ref[...], g_ref[...]).astype(BF16)
    ones = _group_ones(HEAD_DIM)
    f0 = 3 * FOX_DIM

    def head_norm(h, gain):
        return h * lax.rsqrt(_group_sum(h * h, ones) * (1.0 / HEAD_DIM) + NORM_EPS) * gain

    def q_finish(h):
        q_ref[...] = (head_norm(h, qn_ref[...]) * (ATT_SCALE * LOG2E)).astype(BF16)

    def k_finish(h):
        k = head_norm(h, kn_ref[...])
        if token_minor:
            k_ref[0] = k.T
        else:
            k_ref[...] = k

    def v_finish(h):
        if token_minor:
            v_ref[0] = h.T
        else:
            v_ref[...] = h

    def lf_finish(h):
        lf_ref[...] = _log_sigmoid(h + bf_ref[...])

    def hr_finish(hr):
        if per_seq is None:
            hr_ref[...] = hr
            return
        _rwkv_prep_core(hr, prev_ref[SUBLANES - 1:SUBLANES, :], prep_params, prep_outs)
        tail = hr[hr.shape[0] - SUBLANES:, :]
        prev_ref[...] = tail
        last_ref[0] = tail

    if per_seq is not None:
        @pl.when(pl.program_id(0) % per_seq == 0)
        def _():
            prev_ref[...] = jnp.zeros(prev_ref.shape, F32)

    stages = [(slice(f0 + LANES, w_ref.shape[1]), hr_finish), (slice(0, FOX_DIM), q_finish),
              (slice(FOX_DIM, 2 * FOX_DIM), k_finish), (slice(2 * FOX_DIM, f0), v_finish),
              (slice(f0, f0 + LANES), lf_finish)]
    ahead = _dot(n, w_ref[:, stages[0][0]])
    for idx, (_, finish) in enumerate(stages):
        h = ahead
        if idx + 1 < len(stages):
            ahead = _dot(n, w_ref[:, stages[idx + 1][0]])
        finish(h)


def _inproj_even(x, g, w, bf, qn, kn, seq_len=None, prep_params=()):
    rows = x.shape[0]
    tm = min(ROW_TILE, rows)
    in_specs = [_row_spec(tm, D_MODEL), _full_spec(g.shape), _full_spec(w.shape), _full_spec(bf.shape),
                _full_spec(qn.shape), _full_spec(kn.shape)]
    q_out = (jax.ShapeDtypeStruct((rows, FOX_DIM), BF16), _row_spec(tm, FOX_DIM))
    lf_out = (jax.ShapeDtypeStruct((rows, LANES), F32), _row_spec(tm, LANES))
    if seq_len is None:
        kv = (jax.ShapeDtypeStruct((rows, FOX_DIM), F32), _row_spec(tm, FOX_DIM))
        outs = [q_out, kv, kv, lf_out, (jax.ShapeDtypeStruct((rows, RWKV_COLS), F32), _row_spec(tm, RWKV_COLS))]
        per_seq, scratch, sem = None, [], "parallel"
    else:
        per_seq = seq_len // tm
        kv = (jax.ShapeDtypeStruct((rows // seq_len, FOX_DIM, seq_len), F32),
              pl.BlockSpec((1, FOX_DIM, tm), lambda i: (i // per_seq, 0, i % per_seq)))
        last = (jax.ShapeDtypeStruct((rows // tm, SUBLANES, RWKV_COLS), F32),
                pl.BlockSpec((1, SUBLANES, RWKV_COLS), lambda i: (i, 0, 0)))
        operand = (jax.ShapeDtypeStruct((rows, RWKV_DIM), F32), _row_spec(tm, RWKV_DIM))
        outs = [q_out, kv, kv, lf_out, last] + [operand] * 8
        in_specs += [_full_spec(p.shape) for p in prep_params]
        scratch, sem = [pltpu.VMEM((SUBLANES, RWKV_COLS), F32)], "arbitrary"
    return pl.pallas_call(
        functools.partial(_inproj_even_body, per_seq=per_seq),
        grid=(rows // tm,),
        in_specs=in_specs,
        out_specs=[spec for _, spec in outs],
        out_shape=[shape for shape, _ in outs],
        scratch_shapes=scratch,
        compiler_params=_cparams(sem),
        name="inproj_even",
    )(x, g, w, bf, qn, kn, *prep_params)


def _split3(x):
    hi = x.astype(BF16)
    r1 = x - hi.astype(F32)
    mid = r1.astype(BF16)
    lo = (r1 - mid.astype(F32)).astype(BF16)
    return hi, mid, lo


def _cumsum_body(lf_ref, col_ref, row_ref, *, n_chunks, tk):
    r = lax.broadcasted_iota(jnp.int32, (tk, tk), 0)
    c = lax.broadcasted_iota(jnp.int32, (tk, tk), 1)
    tri = jnp.where(c <= r, 1.0, 0.0).astype(BF16)
    carry = jnp.zeros((1, LANES), F32)
    for i in range(n_chunks):
        hi, mid, lo = _split3(lf_ref[0, i * tk:(i + 1) * tk, :])
        cs = _dot(tri, hi) + _dot(tri, mid) + _dot(tri, lo) + carry
        col_ref[0, i * tk:(i + 1) * tk, :] = cs
        row_ref[0, :, i * tk:(i + 1) * tk] = cs.T[0:FOX_HEADS, :]
        carry = cs[tk - 1:tk, :]


def _cumsum(lf, tk):
    b, length, _ = lf.shape
    n_chunks = length // tk
    return pl.pallas_call(
        functools.partial(_cumsum_body, n_chunks=n_chunks, tk=tk),
        grid=(b,),
        in_specs=[pl.BlockSpec((1, length, LANES), lambda i: (i, 0, 0))],
        out_specs=[pl.BlockSpec((1, length, LANES), lambda i: (i, 0, 0)),
                   pl.BlockSpec((1, FOX_HEADS, length), lambda i: (i, 0, 0))],
        out_shape=[jax.ShapeDtypeStruct((b, length, LANES), F32),
                   jax.ShapeDtypeStruct((b, FOX_HEADS, length), F32)],
        compiler_params=_cparams("parallel"),
        name="cumsum",
    )(lf)


def _place3(terms, src_lane, dst_lane):
    r = lax.broadcasted_iota(jnp.int32, (LANES, LANES), 0)
    c = lax.broadcasted_iota(jnp.int32, (LANES, LANES), 1)
    out = None
    for i, t in enumerate(terms):
        sel = jnp.where((r == src_lane) & (c == dst_lane + i), 1.0, 0.0).astype(BF16)
        out = _dot(t, sel) if out is None else out + _dot(t, sel)
    return out


def _fox_q(q_pair, cum_col, head, h):
    n = q_pair.shape[0]
    qf = q_pair.astype(F32)
    if h == 1:
        qf = pltpu.roll(qf, HEAD_DIM, axis=1)
    lane = lax.broadcasted_iota(jnp.int32, (n, LANES), 1)
    ones = jnp.where((lane >= HEAD_DIM + 3) & (lane < HEAD_DIM + 6), 1.0, 0.0)
    aug = _place3(_split3(cum_col * LOG2E), head, HEAD_DIM) + ones
    return jnp.where(lane < HEAD_DIM, qf, aug).astype(BF16)


def _fox_kt(kt, cum_row):
    n = kt.shape[1]
    hi, mid, lo = (x.astype(F32) for x in _split3(cum_row * LOG2E))
    row = lax.broadcasted_iota(jnp.int32, (8, n), 0)
    aug = jnp.where(row < 3, 1.0, jnp.where(row == 3, -hi, jnp.where(row == 4, -mid, jnp.where(row == 5, -lo, 0.0))))
    return jnp.concatenate([kt, aug, jnp.zeros((HEAD_DIM - 8, n), F32)], axis=0).astype(BF16)


def _fox_vt(vt):
    n = vt.shape[1]
    row = lax.broadcasted_iota(jnp.int32, (HEAD_DIM, n), 0)
    return jnp.concatenate([vt, jnp.where(row == 0, 1.0, 0.0)], axis=0).astype(BF16)


def _fox_finish(acc):
    return acc / acc[:, HEAD_DIM:HEAD_DIM + 1]


def _fox_prompt_body(q_ref, cum_ref, cumt_ref, kt_ref, vt_ref, o_ref, ka_sc, va_sc, m_sc, acc_sc, *, tq, t):
    hp = pl.program_id(1)
    i = pl.program_id(2)

    @pl.when(i == 0)
    def _():
        for h in range(2):
            rows = slice(h * HEAD_DIM, (h + 1) * HEAD_DIM)
            cum_row = cumt_ref[0, pl.ds(2 * hp + h, 1), :]
            for c in range(t // tq):
                cols = slice(c * tq, (c + 1) * tq)
                ka_sc[h, c] = _fox_kt(kt_ref[0, rows, cols], cum_row[:, cols])
                va_sc[h, c] = _fox_vt(vt_ref[0, rows, cols])

    off_q = pl.multiple_of(i * tq, tq)
    cq = cum_ref[0, pl.ds(off_q, tq), :]
    qs = [_fox_q(q_ref[0], cq, 2 * hp + h, h) for h in range(2)]
    m_sc[...] = jnp.full(m_sc.shape, NEG_BIG, F32)
    acc_sc[...] = jnp.zeros(acc_sc.shape, F32)
    causal = (lax.broadcasted_iota(jnp.int32, (tq, tq), 1) <= lax.broadcasted_iota(jnp.int32, (tq, tq), 0))

    def step(j, masked):
        scores = [_dot(qs[h], ka_sc[h, j]) for h in range(2)]
        for h in range(2):
            s = scores[h]
            if masked:
                s = jnp.where(causal, s, NEG_BIG)
            m_old = m_sc[h]
            m_new = jnp.maximum(m_old, jnp.max(s, axis=1, keepdims=True))
            p = jnp.concatenate([jnp.exp2(s[:, c * LANES:(c + 1) * LANES] - m_new) for c in range(tq // LANES)],
                                axis=1).astype(BF16)
            acc_sc[h] = jnp.exp2(m_old - m_new) * acc_sc[h] + _dot_nt(p, va_sc[h, j])
            m_sc[h] = m_new

    def past(j, carry):
        step(j, False)
        return carry

    lax.fori_loop(0, i, past, 0)
    step(i, True)
    second = pltpu.roll(_fox_finish(acc_sc[1]), HEAD_DIM, axis=1)
    o_ref[0] = jnp.where(_lane_half((tq, LANES)), _fox_finish(acc_sc[0]), second)


def _fox_prompt(q, kt, vt, cum, cumt, tq):
    b, t, _ = q.shape
    pairs = FOX_HEADS // 2
    nblk = t // tq
    return pl.pallas_call(
        functools.partial(_fox_prompt_body, tq=tq, t=t),
        grid=(b, pairs, nblk),
        in_specs=[pl.BlockSpec((1, tq, LANES), lambda bi, hp, i: (bi, i, hp)),
                  pl.BlockSpec((1, t, LANES), lambda bi, hp, i: (bi, 0, 0)),
                  pl.BlockSpec((1, FOX_HEADS, t), lambda bi, hp, i: (bi, 0, 0)),
                  pl.BlockSpec((1, LANES, t), lambda bi, hp, i: (bi, hp, 0)),
                  pl.BlockSpec((1, LANES, t), lambda bi, hp, i: (bi, hp, 0))],
        out_specs=pl.BlockSpec((1, tq, LANES), lambda bi, hp, i: (bi, i, hp)),
        out_shape=jax.ShapeDtypeStruct((b, t, FOX_DIM), F32),
        scratch_shapes=[pltpu.VMEM((2, nblk, LANES, tq), BF16), pltpu.VMEM((2, nblk, LANES, tq), BF16),
                        pltpu.VMEM((2, tq, LANES), F32), pltpu.VMEM((2, tq, LANES), F32)],
        compiler_params=_cparams("parallel", "parallel", "arbitrary"),
        name="fox_prompt",
    )(q, cum, cumt, kt, vt)


def _fox_sample_body(q_ref, cum_ref, cumt_ref, kt_ref, vt_ref, kn_ref, vn_ref, o_ref, *, ts, past):
    cum_n = cum_ref[0, 0:ts, :] * LOG2E
    causal = (lax.broadcasted_iota(jnp.int32, (ts, ts), 1) <= lax.broadcasted_iota(jnp.int32, (ts, ts), 0))
    lane = lax.broadcasted_iota(jnp.int32, (ts, LANES), 1)
    q = q_ref[0].astype(F32)
    kn = kn_ref[0]
    vn = vn_ref[0]
    outs = []
    for h in range(FOX_HEADS):
        cols = slice(h * HEAD_DIM, (h + 1) * HEAD_DIM)
        cq = jnp.sum(jnp.where(lane == h, cum_n, 0.0), axis=1, keepdims=True)
        ck_p = cumt_ref[0, h:h + 1, 0:past] * LOG2E
        ck_n = cumt_ref[0, h:h + 1, past:past + ts] * LOG2E
        qh = q[:, cols].astype(BF16)
        s_p = _dot(qh, kt_ref[0, cols, :].astype(BF16)) + (cq - ck_p)
        s_n = jnp.where(causal, _dot_nt(qh, kn[:, cols].astype(BF16)) + (cq - ck_n), NEG_BIG)
        m = jnp.maximum(jnp.max(s_p, axis=1, keepdims=True), jnp.max(s_n, axis=1, keepdims=True))
        p_p = jnp.exp2(s_p - m)
        p_n = jnp.exp2(s_n - m)
        l = jnp.sum(p_p, axis=1, keepdims=True) + jnp.sum(p_n, axis=1, keepdims=True)
        acc = (_dot_nt(p_p.astype(BF16), vt_ref[0, cols, :].astype(BF16))
               + _dot(p_n.astype(BF16), vn[:, cols].astype(BF16)))
        outs.append(acc / l)
    o_ref[0] = jnp.concatenate(outs, axis=1)


def _fox_sample(q, k_new, v_new, kt_past, vt_past, cum, cumt):
    b, ts, _ = q.shape
    p = kt_past.shape[2]
    length = cum.shape[1]
    assert p % CUM_TILE == 0 and ts <= CUM_TILE
    new = pl.BlockSpec((1, ts, FOX_DIM), lambda bi: (bi, 0, 0))
    old = pl.BlockSpec((1, FOX_DIM, p), lambda bi: (bi, 0, 0))
    return pl.pallas_call(
        functools.partial(_fox_sample_body, ts=ts, past=p),
        grid=(b,),
        in_specs=[new, pl.BlockSpec((1, CUM_TILE, LANES), lambda bi: (bi, p // CUM_TILE, 0)),
                  pl.BlockSpec((1, FOX_HEADS, length), lambda bi: (bi, 0, 0)), old, old, new, new],
        out_specs=new,
        out_shape=jax.ShapeDtypeStruct((b, ts, FOX_DIM), F32),
        compiler_params=_cparams("parallel"),
        name="fox_sample",
    )(q, cum, cumt, kt_past, vt_past, k_new, v_new)


def _rwkv_prep_core(h, prev_row, param_refs, out_refs):
    mu_ref, w0_ref, w2_ref, a0_ref, a2_ref, g2_ref, kk_ref, ka_ref, rk_ref = param_refs
    r_out, w_out, k_out, v_out, kap_out, bet_out, g_out, bon_out = out_refs
    tm = h.shape[0]
    first = lax.broadcasted_iota(jnp.int32, (tm, 1), 0) == 0
    prev = jnp.where(first, prev_row, pltpu.roll(h, 1, axis=0))
    hx = h + (prev - h) * mu_ref[...]
    r = hx[:, 0:RWKV_DIM]
    k = hx[:, RWKV_DIM:2 * RWKV_DIM]
    v = hx[:, 2 * RWKV_DIM:3 * RWKV_DIM]
    xwa = hx[:, 3 * RWKV_DIM:3 * RWKV_DIM + LANES]
    xg = hx[:, 3 * RWKV_DIM + LANES:]
    w_logit = w0_ref[...] + _dot(jnp.tanh(xwa).astype(BF16), w2_ref[...])
    decay = jnp.exp(-jnp.exp(_log_sigmoid(w_logit) - 0.5))
    a = jax.nn.sigmoid(a0_ref[...] + _dot(xwa.astype(BF16), a2_ref[...]))
    g = _dot(jax.nn.sigmoid(xg).astype(BF16), g2_ref[...])
    ones = _head_ones()
    kk = k * kk_ref[...]
    kk = kk / jnp.maximum(jnp.sqrt(_head_sum(kk * kk, ones)), 1e-12)
    k2 = k * (1.0 + (a - 1.0) * ka_ref[...])
    r_out[...] = r
    w_out[...] = decay
    k_out[...] = k2
    v_out[...] = v
    kap_out[...] = kk
    bet_out[...] = kk * a
    g_out[...] = g
    bon_out[...] = _head_sum(r * k2 * rk_ref[...], ones) * v


def _rwkv_prep_body(h_ref, prev_ref, *refs):
    _rwkv_prep_core(h_ref[...], prev_ref[0, 7:8, :], refs[:9], refs[9:])


def _rwkv_prep(hr, prev8, tm, params):
    rows = hr.shape[0]
    out = jax.ShapeDtypeStruct((rows, RWKV_DIM), F32)
    return pl.pallas_call(
        _rwkv_prep_body,
        grid=(rows // tm,),
        in_specs=[_row_spec(tm, RWKV_COLS), pl.BlockSpec((1, 8, RWKV_COLS), lambda i: (i, 0, 0))]
        + [_full_spec(p.shape) for p in params],
        out_specs=[_row_spec(tm, RWKV_DIM)] * 8,
        out_shape=[out] * 8,
        compiler_params=_cparams("parallel"),
        name="rwkv_prep",
    )(hr, prev8, *params)


def _to_chains_body(x_ref, o_ref, st_ref, *, nb, n_out, offsets):
    for b in range(nb):
        st_ref[b] = x_ref[b].T
    for j in range(n_out):
        groups = [st_ref[b, off * RWKV_HEADS:(off + 1) * RWKV_HEADS, :] for off in offsets(j) for b in range(nb)]
        o_ref[:, j, :] = jnp.concatenate(groups, axis=0).T


def _to_chains_pallas(x, n_out, offsets):
    nb, t, _ = x.shape
    tt = LANES
    return pl.pallas_call(
        functools.partial(_to_chains_body, nb=nb, n_out=n_out, offsets=offsets),
        grid=(t // tt,),
        in_specs=[pl.BlockSpec((nb, tt, RWKV_DIM), lambda i: (0, i, 0))],
        out_specs=pl.BlockSpec((tt, n_out, LANES), lambda i: (i, 0, 0)),
        out_shape=jax.ShapeDtypeStruct((t, n_out, LANES), F32),
        scratch_shapes=[pltpu.VMEM((nb, RWKV_DIM, tt), F32)],
        compiler_params=_cparams("parallel"),
        name="to_chains",
    )(x)


def _from_chains_body(o_ref, x_ref, st_ref, *, nb, nv, dup):
    nh = RWKV_HEADS
    for vp in range(nv):
        tile = o_ref[:, vp, :].T
        for vh in range(dup):
            ch = vh * nv + vp
            for b in range(nb):
                r0 = (vh * nb + b) * nh
                st_ref[b, ch * nh:(ch + 1) * nh, :] = tile[r0:r0 + nh, :]
    for b in range(nb):
        x_ref[b] = st_ref[b].T


def _from_chains_pallas(o, nb):
    t, nv, _ = o.shape
    dup = HEAD_DIM // nv
    tt = LANES
    return pl.pallas_call(
        functools.partial(_from_chains_body, nb=nb, nv=nv, dup=dup),
        grid=(t // tt,),
        in_specs=[pl.BlockSpec((tt, nv, LANES), lambda i: (i, 0, 0))],
        out_specs=pl.BlockSpec((nb, tt, RWKV_DIM), lambda i: (0, i, 0)),
        out_shape=jax.ShapeDtypeStruct((nb, t, RWKV_DIM), F32),
        scratch_shapes=[pltpu.VMEM((nb, RWKV_DIM, tt), F32)],
        compiler_params=_cparams("parallel"),
        name="from_chains",
    )(o)


SUBLANES = 8


def _rwkv_scan_body(w_ref, kap_ref, bet_ref, k2_ref, r_ref, v_ref, s0_ref, o_ref, s_ref, ops_ref, *, tb, nv):
    @pl.when(pl.program_id(0) == 0)
    def _():
        s_ref[...] = s0_ref[...]

    packed = w_ref.shape[1] < HEAD_DIM
    low = lax.broadcasted_iota(jnp.int32, (w_ref.shape[1], LANES), 1) < LANES // 2

    def operand(ref, t):
        x = ref[t]
        if not packed:
            return x
        swapped = pltpu.roll(x, LANES // 2, axis=1)
        return jnp.concatenate([jnp.where(low, x, swapped), jnp.where(low, swapped, x)], axis=0)

    def operands(t):
        return tuple(operand(ref, t) for ref in (w_ref, kap_ref, bet_ref, k2_ref, r_ref))

    groups = nv // SUBLANES
    lanes_of_sum = 2

    def accumulate(partials, g, k, term):
        slot = k % lanes_of_sum
        partials[g][slot] = term if partials[g][slot] is None else partials[g][slot] + term

    def total(partials, g):
        return functools.reduce(lambda a, b: a + b, partials[g])

    def token(t, current):
        following = operands(jnp.minimum(t + 1, tb - 1))
        for i, x in enumerate(current):
            ops_ref[i] = x

        def row(i, k):
            return ops_ref[i, pl.ds(k, SUBLANES, stride=0), :]

        v8 = [v_ref[t, g * SUBLANES:(g + 1) * SUBLANES, :] for g in range(groups)]
        removal = [[None] * lanes_of_sum for _ in range(groups)]
        for k in range(HEAD_DIM):
            kap_k = row(1, k)
            for g in range(groups):
                accumulate(removal, g, k, s_ref[g, k] * kap_k)
        rho = [total(removal, g) for g in range(groups)]
        out = [[None] * lanes_of_sum for _ in range(groups)]
        for k in range(HEAD_DIM):
            w_k, bet_k, k2_k, r_k = row(0, k), row(2, k), row(3, k), row(4, k)
            for g in range(groups):
                sn = s_ref[g, k] * w_k - bet_k * rho[g] + k2_k * v8[g]
                s_ref[g, k] = sn
                accumulate(out, g, k, sn * r_k)
        for g in range(groups):
            o_ref[t, g * SUBLANES:(g + 1) * SUBLANES, :] = total(out, g)
        return following

    lax.fori_loop(0, tb, token, operands(0))


def _rwkv_scan(w, kap, bet, k2, r, v, s0):
    t = w.shape[0]
    nv = v.shape[1]
    tb = min(SCAN_TOKENS, t)
    op_spec = pl.BlockSpec((tb, w.shape[1], LANES), lambda i: (i, 0, 0))
    v_spec = pl.BlockSpec((tb, nv, LANES), lambda i: (i, 0, 0))
    s_shape = (nv // SUBLANES, HEAD_DIM, SUBLANES, LANES)
    s_spec = pl.BlockSpec(s_shape, lambda i: (0, 0, 0, 0))
    return pl.pallas_call(
        functools.partial(_rwkv_scan_body, tb=tb, nv=nv),
        grid=(t // tb,),
        in_specs=[op_spec] * 5 + [v_spec, s_spec],
        out_specs=[v_spec, s_spec],
        out_shape=[jax.ShapeDtypeStruct((t, nv, LANES), F32), jax.ShapeDtypeStruct(s_shape, F32)],
        scratch_shapes=[pltpu.VMEM((5, HEAD_DIM, LANES), F32)],
        compiler_params=_cparams("arbitrary"),
        name="rwkv_scan",
    )(w, kap, bet, k2, r, v, s0)


def _outproj_even_core(x, a, o, bon, gate, lng, lnb, w_ref):
    ones = _head_ones()
    d = o - _head_sum(o, ones) * (1.0 / HEAD_DIM)
    var = _head_sum(d * d, ones) * (1.0 / HEAD_DIM)
    y = d * lax.rsqrt(var + RWKV_GN_EPS) * lng + lnb
    b = (y + bon) * gate
    return x + _dot(a.astype(BF16), w_ref[0:FOX_DIM, :]) + _dot(b.astype(BF16), w_ref[FOX_DIM:, :])


def _rope(x, cos, sin_up, sin_dn):
    return x * cos + pltpu.roll(x, ROPE_DIM // 2, axis=1) * sin_up + pltpu.roll(x, LANES - ROPE_DIM // 2, axis=1) * sin_dn


def _inproj_odd_body(x_ref, g_ref, w_ref, qn_ref, kn_ref, vn_ref, cos_ref, sup_ref, sdn_ref, *refs, fuse_sgu):
    if fuse_sgu:
        ws_ref, sb_ref, q_ref, k_ref, v_ref, d_ref = refs
    else:
        q_ref, k_ref, v_ref, u_ref, gv_ref = refs
    n = _rms(x_ref[...], g_ref[...]).astype(BF16)
    cos, sup, sdn = cos_ref[...], sup_ref[...], sdn_ref[...]
    ones64 = _group_ones(HEAD_DIM)
    qw = SWA_Q

    def head_norm(h, gain):
        return h * lax.rsqrt(_group_sum(h * h, ones64) * (1.0 / HEAD_DIM) + NORM_EPS) * gain

    def q_block(j):
        cols = slice(j * LANES, (j + 1) * LANES)

        def finish(h):
            q_ref[:, cols] = (_rope(head_norm(h, qn_ref[:, cols]), cos, sup, sdn) * ATT_SCALE).astype(BF16)
        return cols, finish

    def k_finish(h):
        k_ref[...] = _rope(head_norm(h, kn_ref[...]), cos, sup, sdn)

    def v_finish(h):
        v_ref[...] = h

    held = {}

    def u_finish(h):
        if fuse_sgu:
            held["u"] = _gelu_tanh(h)
        else:
            u_ref[...] = _gelu_tanh(h)

    def gv_finish(h):
        gv = _rms(_gelu_tanh(h), vn_ref[...])
        if fuse_sgu:
            _sgu_core(held["u"], gv, ws_ref, sb_ref, d_ref, sb_ref.shape[0])
        else:
            gv_ref[...] = gv

    s0 = qw + 2 * SWA_KV
    stages = [q_block(j) for j in range(SWA_Q // LANES)] + [
        (slice(qw, qw + SWA_KV), k_finish), (slice(qw + SWA_KV, s0), v_finish),
        (slice(s0, s0 + SGU_DIM), u_finish), (slice(s0 + SGU_DIM, s0 + 2 * SGU_DIM), gv_finish)]
    ahead = _dot(n, w_ref[:, stages[0][0]])
    for idx, (_, finish) in enumerate(stages):
        h = ahead
        if idx + 1 < len(stages):
            ahead = _dot(n, w_ref[:, stages[idx + 1][0]])
        finish(h)


def _inproj_odd(x, g, w, qn, kn, vn, cos, sup, sdn, table_blocks, sgu=None):
    rows = x.shape[0]
    tm = min(ROW_TILE, rows)
    tab = pl.BlockSpec((tm, LANES), lambda i: (i % table_blocks, 0))
    wide = (jax.ShapeDtypeStruct((rows, SGU_DIM), F32), _row_spec(tm, SGU_DIM))
    narrow = (jax.ShapeDtypeStruct((rows, SWA_KV), F32), _row_spec(tm, SWA_KV))
    outs = [(jax.ShapeDtypeStruct((rows, SWA_Q), BF16), _row_spec(tm, SWA_Q)), narrow, narrow, wide]
    outs += [] if sgu else [wide]
    extra = list(sgu) if sgu else []
    return pl.pallas_call(
        functools.partial(_inproj_odd_body, fuse_sgu=sgu is not None),
        grid=(rows // tm,),
        in_specs=[_row_spec(tm, D_MODEL), _full_spec(g.shape), _full_spec(w.shape), _full_spec(qn.shape),
                  _full_spec(kn.shape), _full_spec(vn.shape), tab, tab, tab] + [_full_spec(a.shape) for a in extra],
        out_specs=[spec for _, spec in outs],
        out_shape=[shape for shape, _ in outs],
        compiler_params=_cparams("parallel"),
        name="inproj_odd",
    )(x, g, w, qn, kn, vn, cos, sup, sdn, *extra)


def _sink_attend(scores, vb, sink, visible):
    s = jnp.where(visible, scores, NEG_BIG)
    m = jnp.maximum(jnp.max(s, axis=1, keepdims=True), sink)
    p = jnp.exp(s - m)
    l = jnp.sum(p, axis=1, keepdims=True) + jnp.exp(sink - m)
    return _dot(p.astype(BF16), vb) / l


def _swa_body(sink_ref, q_ref, kp_ref, kc_ref, vp_ref, vc_ref, mask_ref, o_ref):
    tq = q_ref.shape[1]
    kw = jnp.concatenate([kp_ref[0], kc_ref[0]], axis=0)
    vw = jnp.concatenate([vp_ref[0], vc_ref[0]], axis=0)
    nk = kw.shape[0]
    kw_sw = pltpu.roll(kw, HEAD_DIM, axis=1)
    vw_sw = pltpu.roll(vw, HEAD_DIM, axis=1)
    key_half = _lane_half((nk, LANES))
    ks = [jnp.where(key_half, a, b).astype(BF16) for a, b in ((kw, kw_sw), (kw_sw, kw))]
    vs = [jnp.where(key_half, a, b).astype(BF16) for a, b in ((vw, vw_sw), (vw_sw, vw))]
    visible = jnp.concatenate([mask_ref[0] > 0.5] * 2, axis=0)
    half = _lane_half((tq, LANES))
    first = lax.broadcasted_iota(jnp.int32, (2 * tq, 1), 0) < tq
    n_blocks = SWA_Q // LANES
    scores = []
    for j in range(n_blocks):
        qb = q_ref[0, :, j * LANES:(j + 1) * LANES]
        zero = jnp.zeros_like(qb)
        stacked = jnp.concatenate([jnp.where(half, qb, zero), jnp.where(half, zero, qb)], axis=0)
        scores.append(_dot_nt(stacked, ks[(2 * j) // SWA_GROUP]))
    for j in range(n_blocks):
        sink = jnp.where(first, sink_ref[2 * j], sink_ref[2 * j + 1])
        o = _sink_attend(scores[j], vs[(2 * j) // SWA_GROUP], sink, visible)
        o_ref[0, :, j * LANES:(j + 1) * LANES] = jnp.where(half, o[0:tq], o[tq:]).astype(BF16)


def _swa(sinks, q, k_prev, k_cur, v_prev, v_cur, mask, tq, prev_rows, prev_map, mask_map):
    b, t, qw = q.shape
    nk = prev_rows + tq
    cur = lambda bi, i: (bi, i, 0)
    return pl.pallas_call(
        _swa_body,
        grid=(b, t // tq),
        in_specs=[pl.BlockSpec(memory_space=pltpu.SMEM),
                  pl.BlockSpec((1, tq, qw), cur),
                  pl.BlockSpec((1, prev_rows, SWA_KV), prev_map), pl.BlockSpec((1, tq, SWA_KV), cur),
                  pl.BlockSpec((1, prev_rows, SWA_KV), prev_map), pl.BlockSpec((1, tq, SWA_KV), cur),
                  pl.BlockSpec((1, tq, nk), mask_map)],
        out_specs=pl.BlockSpec((1, tq, qw), cur),
        out_shape=jax.ShapeDtypeStruct((b, t, qw), BF16),
        compiler_params=_cparams("parallel", "parallel"),
        name="swa",
    )(sinks, q, k_prev, k_cur, v_prev, v_cur, mask)


def _sgu_core(u, v, w_ref, b_ref, o_ref, length):
    tril = (lax.broadcasted_iota(jnp.int32, (length, length), 1)
            <= lax.broadcasted_iota(jnp.int32, (length, length), 0))
    ws = [jnp.where(tril, w_ref[g], 0.0).astype(BF16) for g in range(SGU_GROUPS)]
    half = _lane_half((length, LANES))
    for c in range(u.shape[0] // length):
        rows = slice(c * length, (c + 1) * length)
        for j in range(SGU_GROUPS // 2):
            cols = slice(j * LANES, (j + 1) * LANES)
            vb = v[rows, cols].astype(BF16)
            mixed = jnp.where(half, _dot(ws[2 * j], vb), _dot(ws[2 * j + 1], vb)) + b_ref[:, cols]
            o_ref[rows, cols] = u[rows, cols] * mixed


def _sgu_body(u_ref, v_ref, w_ref, b_ref, o_ref, *, length):
    _sgu_core(u_ref[...], v_ref[...], w_ref, b_ref, o_ref, length)


def _sgu(u, v, w, bias, length, n_chunks):
    rows = u.shape[0]
    tm = length * n_chunks
    return pl.pallas_call(
        functools.partial(_sgu_body, length=length),
        grid=(rows // tm,),
        in_specs=[_row_spec(tm, SGU_DIM), _row_spec(tm, SGU_DIM), _full_spec(w.shape), _full_spec(bias.shape)],
        out_specs=_row_spec(tm, SGU_DIM),
        out_shape=jax.ShapeDtypeStruct(u.shape, F32),
        compiler_params=_cparams("parallel"),
        name="sgu",
    )(u, v, w, bias)


def _outproj_odd_core(x, c, d, w_ref):
    return x + _dot(c, w_ref[0:SWA_Q, :]) + _dot(d.astype(BF16), w_ref[SWA_Q:, :])


def _memkv_body(m_ref, g_ref, w_ref, kn_ref, k_ref, v_ref):
    n = _rms(m_ref[0], g_ref[...]).astype(BF16)
    ones = _group_ones(HEAD_DIM)
    hk = _dot(n, w_ref[:, 0:MEM_DIM])
    k = hk * lax.rsqrt(_group_sum(hk * hk, ones) * (1.0 / HEAD_DIM) + NORM_EPS) * kn_ref[...]
    k_ref[0] = k.T
    v_ref[0] = _dot(n, w_ref[:, MEM_DIM:]).T


def _memkv(mem, g, w, kn):
    b, m, _ = mem.shape
    out = jax.ShapeDtypeStruct((b, MEM_DIM, m), F32)
    spec = pl.BlockSpec((1, MEM_DIM, m), lambda i: (i, 0, 0))
    return pl.pallas_call(
        _memkv_body,
        grid=(b,),
        in_specs=[pl.BlockSpec((1, m, D_MODEL), lambda i: (i, 0, 0)), _full_spec(g.shape), _full_spec(w.shape),
                  _full_spec(kn.shape)],
        out_specs=[spec, spec],
        out_shape=[out, out],
        compiler_params=_cparams("parallel"),
        name="memkv",
    )(mem, g, w, kn)


def _xattn_core(x, g, wq_ref, qn, mk_ref, mv_ref, wo_ref):
    n_seq = mk_ref.shape[0]
    tq = x.shape[0] // n_seq
    n = _rms(x, g).astype(BF16)
    hq = _dot(n, wq_ref[...])
    ones = _group_ones(HEAD_DIM)
    q = (hq * lax.rsqrt(_group_sum(hq * hq, ones) * (1.0 / HEAD_DIM) + NORM_EPS) * qn * ATT_SCALE).astype(BF16)
    half = _lane_half((tq, LANES))
    scores, values = [], []
    for sq in range(n_seq):
        for j in range(MEM_HEADS // 2):
            cols = slice(j * LANES, (j + 1) * LANES)
            kb = mk_ref[sq, cols, :].astype(BF16)
            values.append(mv_ref[sq, cols, :].astype(BF16))
            qb = q[sq * tq:(sq + 1) * tq, cols]
            zero = jnp.zeros_like(qb)
            heads = [jnp.where(half, qb, zero), jnp.where(half, zero, qb)]
            scores.append([_dot(qh, kb) for qh in ([jnp.concatenate(heads, axis=0)] if tq <= LANES else heads)])
    blocks = []
    for per_block, vb in zip(scores, values):
        outs = []
        for s in per_block:
            p = jnp.exp(s - jnp.max(s, axis=1, keepdims=True))
            outs.append(_dot_nt(p.astype(BF16), vb) / jnp.sum(p, axis=1, keepdims=True))
        if len(outs) == 1:
            outs = [outs[0][0:tq], outs[0][tq:]]
        blocks.append(jnp.where(half, outs[0], outs[1]))
    per_seq = [jnp.concatenate(blocks[i:i + MEM_HEADS // 2], axis=1) for i in range(0, len(blocks), MEM_HEADS // 2)]
    o = (per_seq[0] if n_seq == 1 else jnp.concatenate(per_seq, axis=0)).astype(BF16)
    return x + _dot(o, wo_ref[...])


def _tail_body(*refs, even):
    n_mix = 8 if even else 4
    mix, rest = refs[:n_mix], refs[n_mix:]
    gx_ref, wq_ref, qn_ref, mk_ref, mv_ref, wo_ref, gf_ref, wg_ref, wu_ref, wd_ref, out_ref, acc_ref = rest
    if even:
        x_ref, a_ref, o_ref, bon_ref, gate_ref, lng_ref, lnb_ref, w_ref = mix
        x = _outproj_even_core(x_ref[0], a_ref[0], o_ref[0], bon_ref[0], gate_ref[0], lng_ref[...], lnb_ref[...], w_ref)
    else:
        x_ref, c_ref, d_ref, w_ref = mix
        x = _outproj_odd_core(x_ref[0], c_ref[0], d_ref[0], w_ref)
    x = _xattn_core(x, gx_ref[...], wq_ref, qn_ref[...], mk_ref, mv_ref, wo_ref)
    out_ref[0] = _ffn_core(x, gf_ref[...], wg_ref, wu_ref, wd_ref, acc_ref)


def _tail(even, layer, x, mix_rows, mix_params, xattn_params, mk, mv, wo, ffn_params, tm, seqs_per_tile=1):
    b, t, _ = x.shape
    m = mk.shape[2]
    assert seqs_per_tile == 1 or b == 1
    rows3 = lambda c: pl.BlockSpec((1, tm, c), lambda bi, i: (bi, i, 0))
    mem = pl.BlockSpec((seqs_per_tile, MEM_DIM, m),
                       (lambda bi, i: (bi, 0, 0)) if seqs_per_tile == 1 else (lambda bi, i: (i, 0, 0)))
    args = [x, *mix_rows, *mix_params, *xattn_params, mk, mv, wo, *ffn_params]
    in_specs = ([rows3(a.shape[2]) for a in (x, *mix_rows)] + [_full_spec(p.shape) for p in mix_params]
                + [_full_spec(p.shape) for p in xattn_params] + [mem, mem, _full_spec(wo.shape)]
                + [_full_spec(ffn_params[0].shape)] + [_layer_spec(w, layer) for w in ffn_params[1:]])
    return pl.pallas_call(
        functools.partial(_tail_body, even=even),
        grid=(b, t // tm),
        in_specs=in_specs,
        out_specs=rows3(D_MODEL),
        out_shape=jax.ShapeDtypeStruct(x.shape, F32),
        scratch_shapes=[pltpu.VMEM((tm, D_MODEL), F32)],
        compiler_params=_cparams("parallel", "parallel"),
        name="tail_even" if even else "tail_odd",
    )(*args)


def _head_minor(x, axis=-1):
    axis %= x.ndim
    shape = x.shape
    y = x.reshape(shape[:axis] + (RWKV_HEADS, HEAD_DIM) + shape[axis + 1:])
    return jnp.swapaxes(y, axis, axis + 1).reshape(shape)


def _head_major(x, axis=-1):
    axis %= x.ndim
    shape = x.shape
    y = x.reshape(shape[:axis] + (HEAD_DIM, RWKV_HEADS) + shape[axis + 1:])
    return jnp.swapaxes(y, axis, axis + 1).reshape(shape)


def _cols_head_minor(x, inverse=False):
    n = 3 * RWKV_DIM
    blocks = x[..., :n].reshape(x.shape[:-1] + (3, RWKV_DIM))
    blocks = (_head_major if inverse else _head_minor)(blocks)
    return jnp.concatenate([blocks.reshape(x.shape[:-1] + (n,)), x[..., n:]], axis=-1)


def _to_chains(x, b, t, dup):
    y = x.reshape(b, t, HEAD_DIM, RWKV_HEADS).transpose(1, 2, 0, 3).reshape(t, HEAD_DIM, b * RWKV_HEADS)
    return jnp.concatenate([y] * dup, axis=-1) if dup > 1 else y


def _rope_tables(pos):
    half = ROPE_DIM // 2
    inv_freq = jnp.power(ROPE_THETA, -jnp.arange(half, dtype=F32) / half)
    ang = pos.astype(F32)[:, None] * inv_freq[None, :]
    cos, sin = jnp.cos(ang), jnp.sin(ang)
    n = pos.shape[0]
    pad = jnp.zeros((n, HEAD_DIM - ROPE_DIM), F32)
    zero = jnp.zeros((n, half), F32)
    cos_t = jnp.concatenate([cos, cos, pad + 1.0], axis=1)
    up_t = jnp.concatenate([zero, sin, pad], axis=1)
    dn_t = jnp.concatenate([-sin, zero, pad], axis=1)
    two = lambda a: jnp.concatenate([a, a], axis=1)
    return two(cos_t), two(up_t), two(dn_t)


def _swa_prompt_mask(tq):
    span = WINDOW + tq
    qc = np.arange(tq)[:, None] // CHUNK
    kc = np.arange(span)[None, :] // CHUNK - WINDOW_CHUNKS
    band = (kc <= qc) & (kc >= qc - WINDOW_CHUNKS)
    first = band & (kc >= 0)
    return jnp.asarray(np.stack([first, band]).astype(np.float32))


def _swa_sample_mask(past, rows, t):
    kc = (past - rows + np.arange(rows + t)) // CHUNK
    qc = (past + np.arange(t)) // CHUNK
    m = (kc[None, :] <= qc[:, None]) & (kc[None, :] >= qc[:, None] - WINDOW_CHUNKS)
    return jnp.asarray(m[None].astype(np.float32))


def kernel(x_prompt, x_sample, cache_fox_k, cache_fox_v, cache_fox_logf, state_rwkv, state_rwkv_shift, cache_swa_k, cache_swa_v, cache_mem_k, cache_mem_v, mem_prompt, ffn1_norm, ffn1_w_gate, ffn1_w_up, ffn1_w_down, mix_norm, ev_w_in, fox_b_f, fox_q_norm, fox_k_norm, rwkv_mu, rwkv_w0, rwkv_w2, rwkv_a0, rwkv_a2, rwkv_g2, rwkv_k_k, rwkv_k_a, rwkv_r_k, rwkv_ln_g, rwkv_ln_b, ev_w_out, od_w_in, swa_q_norm, swa_k_norm, swa_sinks, sgu_v_norm, sgu_w_s, sgu_b, od_w_out, xattn_norm, mem_norm, xattn_wq, xattn_wkv, xattn_q_norm, xattn_k_norm, xattn_wo, ffn2_norm, ffn2_w_gate, ffn2_w_up, ffn2_w_down):
    bp, tp, _ = x_prompt.shape
    bs, ts, _ = x_sample.shape
    depth = ffn1_norm.shape[0]
    past = cache_fox_k.shape[2]
    mem_tokens = mem_prompt.shape[1]
    xp = x_prompt.reshape(bp * tp, D_MODEL)
    xs = x_sample.reshape(bs * ts, D_MODEL)
    row = lambda a: a.reshape(1, -1)
    tile_heads = lambda a, n: jnp.tile(a, n).reshape(1, -1)
    ffn1_w = (ffn1_w_gate.astype(BF16), ffn1_w_up.astype(BF16), ffn1_w_down.astype(BF16))
    ffn2_w = (ffn2_w_gate.astype(BF16), ffn2_w_up.astype(BF16), ffn2_w_down.astype(BF16))

    out = {k: [] for k in ("p_fox_k", "p_fox_v", "p_fox_logf", "p_rwkv_state", "p_rwkv_shift", "p_swa_k", "p_swa_v",
                           "p_mem_k", "p_mem_v", "s_fox_k", "s_fox_v", "s_fox_logf", "s_rwkv_state", "s_rwkv_shift",
                           "s_swa_k", "s_swa_v", "s_sgu_v")}

    for l in range(depth):
        f1 = (row(ffn1_norm[l]),) + ffn1_w
        xp = _ffn(xp, l, *f1)
        xs = _ffn(xs, l, *f1)
        g_mix = row(mix_norm[l])
        if l % 2 == 0:
            e = l // 2
            w_in = ev_w_in[e]
            f0 = 3 * FOX_DIM
            w_cat = jnp.concatenate([w_in[:, :f0], jnp.pad(w_in[:, f0:f0 + FOX_HEADS], ((0, 0), (0, LANES - FOX_HEADS))),
                                     _cols_head_minor(w_in[:, f0 + FOX_HEADS:])], axis=1).astype(BF16)
            bf = jnp.pad(fox_b_f[e], (0, LANES - FOX_HEADS)).reshape(1, LANES)
            qn = tile_heads(fox_q_norm[e], FOX_HEADS)
            kn = tile_heads(fox_k_norm[e], FOX_HEADS)
            hm = _head_minor
            w2p = jnp.pad(hm(rwkv_w2[e]), ((0, LANES - DECAY_LORA), (0, 0))).astype(BF16)
            a2p = jnp.pad(hm(rwkv_a2[e]), ((DECAY_LORA, 0), (0, 0))).astype(BF16)
            params = (row(_cols_head_minor(rwkv_mu[e])), row(hm(rwkv_w0[e])), w2p, row(hm(rwkv_a0[e])), a2p,
                      hm(rwkv_g2[e]).astype(BF16), row(hm(rwkv_k_k[e])), row(hm(rwkv_k_a[e])),
                      row(hm(rwkv_r_k[e].reshape(-1))))
            qp, ktp, vtp, lfp, last_p, *prep_p = _inproj_even(xp, g_mix, w_cat, bf, qn, kn, seq_len=tp,
                                                              prep_params=params)
            qs, ks, vs, lfs, hrs = _inproj_even(xs, g_mix, w_cat, bf, qn, kn)

            cum_p, cumt_p = _cumsum(lfp.reshape(bp, tp, LANES), CUM_TILE)
            a_p = _fox_prompt(qp.reshape(bp, tp, FOX_DIM), ktp, vtp, cum_p, cumt_p, ATT_TILE)
            lfs8 = lfs.reshape(bs, ts, LANES)[:, :, :FOX_HEADS]
            tot = past + ts
            padded = -(-tot // CUM_TILE) * CUM_TILE
            lf_all = jnp.concatenate([cache_fox_logf[e].astype(F32), lfs8], axis=1)
            lf_all = jnp.pad(lf_all, ((0, 0), (0, padded - tot), (0, LANES - FOX_HEADS)))
            cum_s, cumt_s = _cumsum(lf_all, CUM_TILE)
            token_minor = lambda c: jnp.transpose(c, (0, 2, 3, 1)).reshape(bs, FOX_DIM, past)
            a_s = _fox_sample(qs.reshape(bs, ts, FOX_DIM), ks.reshape(bs, ts, FOX_DIM), vs.reshape(bs, ts, FOX_DIM),
                              token_minor(cache_fox_k[e]), token_minor(cache_fox_v[e]), cum_s, cumt_s)

            prev_s = jnp.pad(_cols_head_minor(state_rwkv_shift[e].astype(F32)), ((0, 0), (7, 0), (0, 0)))
            prep_s = _rwkv_prep(hrs, prev_s, ts, params)

            def scan(prep, b, t, state0):
                r, w, k2, v, kap, bet, g, bon = prep
                dup = LANES // (b * RWKV_HEADS)
                nv = HEAD_DIM // dup
                in_kernel = t % LANES == 0
                if in_kernel:
                    halves = lambda j: tuple(half * nv + j for half in range(dup))
                    ops = [_to_chains_pallas(a.reshape(b, t, RWKV_DIM), nv, halves) for a in (w, kap, bet, k2, r)]
                    vt = _to_chains_pallas(v.reshape(b, t, RWKV_DIM), nv, halves)
                else:
                    ops = [_to_chains(a, b, t, dup) for a in (w, kap, bet, k2, r)]
                    vt = v.reshape(b, t, dup, nv, RWKV_HEADS).transpose(1, 3, 2, 0, 4).reshape(t, nv, LANES)
                s0 = state0.reshape(b, RWKV_HEADS, dup, nv // SUBLANES, SUBLANES, HEAD_DIM)
                s0 = s0.transpose(3, 5, 4, 2, 0, 1).reshape(nv // SUBLANES, HEAD_DIM, SUBLANES, LANES)
                o, sT = _rwkv_scan(*ops, vt, s0)
                if in_kernel:
                    o = _from_chains_pallas(o, b).reshape(b * t, RWKV_DIM)
                else:
                    o = o.reshape(t, nv, dup, b, RWKV_HEADS).transpose(3, 0, 2, 1, 4).reshape(b * t, RWKV_DIM)
                sT = sT.reshape(nv // SUBLANES, HEAD_DIM, SUBLANES, dup, b, RWKV_HEADS)
                sT = sT.transpose(4, 5, 3, 0, 2, 1).reshape(b, RWKV_HEADS, HEAD_DIM, HEAD_DIM)
                return o, sT, g, bon

            o_p, st_p, g_p, bon_p = scan(prep_p, bp, tp, jnp.zeros((bp, RWKV_HEADS, HEAD_DIM, HEAD_DIM), F32))
            o_s, st_s, g_s, bon_s = scan(prep_s, bs, ts, state_rwkv[e].astype(F32))

            w_out = jnp.concatenate([ev_w_out[e][:FOX_DIM], hm(ev_w_out[e][FOX_DIM:], axis=0)], axis=0).astype(BF16)
            lng, lnb = row(hm(rwkv_ln_g[e])), row(hm(rwkv_ln_b[e]))
            seq = lambda a: a.reshape(bp, tp, -1)
            mix_p = (True, (a_p, seq(o_p), seq(bon_p), seq(g_p)), (lng, lnb, w_out))
            flat = lambda a: a.reshape(1, bs * ts, -1)
            mix_s = (True, (flat(a_s), flat(o_s), flat(bon_s), flat(g_s)), (lng, lnb, w_out))

            rows_of = lambda a: jnp.transpose(a.reshape(bp, FOX_HEADS, HEAD_DIM, tp), (0, 3, 1, 2))
            out["p_fox_k"].append(rows_of(ktp))
            out["p_fox_v"].append(rows_of(vtp))
            out["p_fox_logf"].append(lfp.reshape(bp, tp, LANES)[:, :, :FOX_HEADS])
            out["p_rwkv_state"].append(st_p)
            final_rows = last_p.reshape(bp, tp // ROW_TILE, SUBLANES, RWKV_COLS)[:, -1, SUBLANES - 1:, :]
            out["p_rwkv_shift"].append(_cols_head_minor(final_rows, inverse=True))
            out["s_fox_k"].append(ks.reshape(bs, ts, FOX_HEADS, HEAD_DIM))
            out["s_fox_v"].append(vs.reshape(bs, ts, FOX_HEADS, HEAD_DIM))
            out["s_fox_logf"].append(lfs8)
            out["s_rwkv_state"].append(st_s)
            out["s_rwkv_shift"].append(_cols_head_minor(hrs.reshape(bs, ts, RWKV_COLS)[:, -1:], inverse=True))
        else:
            j = l // 2
            w_cat = od_w_in[j].astype(BF16)
            qn = tile_heads(swa_q_norm[j], SWA_HEADS)
            kn = tile_heads(swa_k_norm[j], SWA_KV_HEADS)
            vn = row(sgu_v_norm[j])
            tabs_p = _rope_tables(jnp.arange(tp))
            tabs_s = _rope_tables(past + jnp.arange(bs * ts) % ts)
            bias = jnp.repeat(jnp.transpose(sgu_b[j]), HEAD_DIM, axis=1)
            qp, kp, vp, d_p = _inproj_odd(xp, g_mix, w_cat, qn, kn, vn, *tabs_p, tp // ROW_TILE,
                                          sgu=(sgu_w_s[j], bias))
            qs, ks, vs, us, gs = _inproj_odd(xs, g_mix, w_cat, qn, kn, vn, *tabs_s, 1)

            qw = SWA_Q
            kp3, vp3 = kp.reshape(bp, tp, SWA_KV), vp.reshape(bp, tp, SWA_KV)
            ratio = SWA_TILE // WINDOW
            c_p = _swa(swa_sinks[j], qp.reshape(bp, tp, qw), kp3, kp3, vp3, vp3, _swa_prompt_mask(SWA_TILE),
                       SWA_TILE, WINDOW, lambda bi, i: (bi, jnp.maximum(i * ratio - 1, 0), 0),
                       lambda bi, i: (jnp.minimum(i, 1), 0, 0))
            rows_c = cache_swa_k.shape[2]
            ck3 = cache_swa_k[j].reshape(bs, rows_c, SWA_KV)
            cv3 = cache_swa_v[j].reshape(bs, rows_c, SWA_KV)
            ks3, vs3 = ks.reshape(bs, ts, SWA_KV), vs.reshape(bs, ts, SWA_KV)
            c_s = _swa(swa_sinks[j], qs.reshape(bs, ts, qw), ck3, ks3, cv3, vs3, _swa_sample_mask(past, rows_c, ts),
                       ts, rows_c, lambda bi, i: (bi, 0, 0), lambda bi, i: (0, 0, 0))

            d_s = _sgu(us, gs, sgu_w_s[j][:, :ts, :ts], bias[:ts], ts, 1)

            w_out = od_w_out[j].astype(BF16)
            mix_p = (False, (c_p, d_p.reshape(bp, tp, SGU_DIM)), (w_out,))
            mix_s = (False, (c_s.reshape(1, bs * ts, qw), d_s.reshape(1, bs * ts, SGU_DIM)), (w_out,))

            out["p_swa_k"].append(kp3[:, -WINDOW:].reshape(bp, WINDOW, SWA_KV_HEADS, HEAD_DIM))
            out["p_swa_v"].append(vp3[:, -WINDOW:].reshape(bp, WINDOW, SWA_KV_HEADS, HEAD_DIM))
            out["s_swa_k"].append(jnp.concatenate([ck3, ks3], axis=1)[:, -rows_c:].reshape(bs, rows_c, SWA_KV_HEADS, HEAD_DIM))
            out["s_swa_v"].append(jnp.concatenate([cv3, vs3], axis=1)[:, -rows_c:].reshape(bs, rows_c, SWA_KV_HEADS, HEAD_DIM))
            out["s_sgu_v"].append(gs.reshape(bs, ts, SGU_DIM))

        mk, mv = _memkv(mem_prompt, row(mem_norm[l]), xattn_wkv[l].astype(BF16), tile_heads(xattn_k_norm[l], MEM_HEADS))
        xa = (row(xattn_norm[l]), xattn_wq[l].astype(BF16), tile_heads(xattn_q_norm[l], MEM_HEADS))
        wo = xattn_wo[l].astype(BF16)
        f2 = (row(ffn2_norm[l]),) + ffn2_w
        xp = _tail(mix_p[0], l, xp.reshape(bp, tp, D_MODEL), mix_p[1], mix_p[2], xa, mk, mv, wo, f2,
                   ROW_TILE).reshape(bp * tp, D_MODEL)
        mem_minor = lambda c: jnp.transpose(c, (0, 2, 3, 1)).reshape(bs, MEM_DIM, mem_tokens)
        tile_s = min(ROW_TILE, bs * ts)
        xs = _tail(mix_s[0], l, xs.reshape(1, bs * ts, D_MODEL), mix_s[1], mix_s[2], xa, mem_minor(cache_mem_k[l]),
                   mem_minor(cache_mem_v[l]), wo, f2, tile_s, seqs_per_tile=tile_s // ts).reshape(bs * ts, D_MODEL)
        mem_rows_of = lambda a: jnp.transpose(a.reshape(bp, MEM_HEADS, HEAD_DIM, mem_tokens), (0, 3, 1, 2))
        out["p_mem_k"].append(mem_rows_of(mk))
        out["p_mem_v"].append(mem_rows_of(mv))

    order = ("p_fox_k", "p_fox_v", "p_fox_logf", "p_rwkv_state", "p_rwkv_shift", "p_swa_k", "p_swa_v", "p_mem_k",
             "p_mem_v", "s_fox_k", "s_fox_v", "s_fox_logf", "s_rwkv_state", "s_rwkv_shift", "s_swa_k", "s_swa_v",
             "s_sgu_v")
    return (xp.reshape(bp, tp, D_MODEL), xs.reshape(bs, ts, D_MODEL)) + tuple(jnp.stack(out[k]) for k in order)
```
